```python
import jax
import jax.numpy as jnp
from jax import lax
import numpy as np

D_MODEL = 1024
BATCH = 4
SEQ = 4096
DEPTH = 1

GLA_HEADS = 4
GLA_DK = D_MODEL // (2 * GLA_HEADS)
GLA_DV = D_MODEL // GLA_HEADS
GLA_LOWRANK = 16
GLA_TAU = 16.0
GLA_CHUNK = 16

ATTN_GROUPS = ((128, 1), (512, 4), (2048, 16))
ATTN_HEADS_PER_GROUP = 4
ATTN_HEAD_DIM = 64
ATTN_HEADS = ATTN_HEADS_PER_GROUP * len(ATTN_GROUPS)
ROPE_THETA = 10000.0

D_FF = 2816
CONV_WIDTH = 3
EPS = 1e-6

GLA_QK_WIDTH = GLA_HEADS * GLA_DK
GLA_V_WIDTH = GLA_HEADS * GLA_DV
ATTN_WIDTH = ATTN_HEADS * ATTN_HEAD_DIM
ATTN_OUT_WIDTH = ATTN_HEADS_PER_GROUP * ATTN_HEAD_DIM
IN_WIDTHS = (GLA_QK_WIDTH, GLA_QK_WIDTH, GLA_V_WIDTH, GLA_V_WIDTH, GLA_LOWRANK,
             ATTN_WIDTH, ATTN_WIDTH, ATTN_WIDTH, D_MODEL, D_MODEL)
D_IN = sum(IN_WIDTHS)

kernel_name = "hybrid_gla_dilated_attn_convffn_block"


def rmsnorm(x, w):
    xf = x.astype(jnp.float32)
    y = xf * lax.rsqrt(jnp.mean(xf * xf, axis=-1, keepdims=True) + EPS)
    return (y * w.astype(jnp.float32)).astype(x.dtype)


def rotary(t, positions):
    half = t.shape[-1] // 2
    inv_freq = ROPE_THETA ** (-jnp.arange(half, dtype=jnp.float32) / half)
    ang = positions.astype(jnp.float32)[..., None] * inv_freq
    cos = jnp.cos(ang)[:, :, None, :]
    sin = jnp.sin(ang)[:, :, None, :]
    tf = t.astype(jnp.float32)
    t1, t2 = tf[..., :half], tf[..., half:]
    return jnp.concatenate([t1 * cos - t2 * sin, t2 * cos + t1 * sin], axis=-1).astype(t.dtype)


def gla_chunked(q, k, v, log_a):
    B, S, H, dk = q.shape
    dv = v.shape[-1]
    C = GLA_CHUNK
    N = S // C

    def chunks(t):
        return t.astype(jnp.float32).reshape(B, N, C, H, -1).transpose(0, 3, 1, 2, 4)

    qc = chunks(q) * (dk ** -0.5)
    kc, vc, lac = chunks(k), chunks(v), chunks(log_a)
    b = jnp.cumsum(lac, axis=3)
    b_last = b[:, :, :, -1:, :]
    qg = qc * jnp.exp(b)
    kg = kc * jnp.exp(-b)
    kd = kc * jnp.exp(b_last - b)

    causal = jnp.tril(jnp.ones((C, C), dtype=bool))
    attn = jnp.where(causal, jnp.einsum('bhnid,bhnjd->bhnij', qg, kg), 0.0)
    o_intra = jnp.einsum('bhnij,bhnjv->bhniv', attn, vc)

    def step(state, xs):
        qg_n, kd_n, v_n, dec_n = xs
        o_n = jnp.einsum('bhcd,bhdv->bhcv', qg_n, state)
        state = dec_n[..., None] * state + jnp.einsum('bhcd,bhcv->bhdv', kd_n, v_n)
        return state, o_n

    xs = (qg.transpose(2, 0, 1, 3, 4), kd.transpose(2, 0, 1, 3, 4), vc.transpose(2, 0, 1, 3, 4),
          jnp.exp(b_last[:, :, :, 0, :]).transpose(2, 0, 1, 3))
    state0 = jnp.zeros((B, H, dk, dv), jnp.float32)
    _, o_inter = lax.scan(step, state0, xs)
    o = o_intra + o_inter.transpose(1, 2, 0, 3, 4)
    return o.transpose(0, 2, 3, 1, 4).reshape(B, S, H, dv)


def dilated_group_attention(q, k, v, window, dilation):
    B, S, H, Dh = q.shape
    r = dilation
    L = S // r
    blk = window // dilation
    nblk = -(-L // blk)
    Lp = nblk * blk

    def to_sub(t):
        t = t.reshape(B, L, r, H, Dh).transpose(0, 2, 3, 1, 4)
        t = jnp.pad(t, ((0, 0), (0, 0), (0, 0), (0, Lp - L), (0, 0)))
        return t.reshape(B, r, H, nblk, blk, Dh)

    def with_prev(t):
        prev = jnp.pad(t, ((0, 0), (0, 0), (0, 0), (1, 0), (0, 0), (0, 0)))[:, :, :, :-1]
        return jnp.concatenate([prev, t], axis=-2)

    qb = to_sub(q)
    kc = with_prev(to_sub(k))
    vc = with_prev(to_sub(v))
    s = jnp.einsum('brhnid,brhnjd->brhnij', qb, kc).astype(jnp.float32) * (Dh ** -0.5)
    i_idx = jnp.arange(blk)[:, None]
    j_idx = jnp.arange(2 * blk)[None, :]
    dist = i_idx + blk - j_idx
    band = (dist >= 0) & (dist <= blk)
    n_idx = jnp.arange(nblk)[:, None, None]
    valid = band[None] & ((n_idx > 0) | (j_idx >= blk)[None])
    s = jnp.where(valid, s, -jnp.inf)
    m = jnp.max(s, axis=-1, keepdims=True)
    p = jnp.exp(s - m)
    denom = jnp.sum(p, axis=-1, keepdims=True)
    o = jnp.einsum('brhnij,brhnjd->brhnid', p, vc.astype(jnp.float32)) / denom
    lse = (m + jnp.log(denom))[..., 0]
    o = o.reshape(B, r, H, Lp, Dh)[:, :, :, :L].transpose(0, 3, 1, 2, 4).reshape(B, S, H, Dh)
    lse = lse.reshape(B, r, H, Lp)[..., :L].transpose(0, 3, 1, 2).reshape(B, S, H)
    return o, lse


def dilated_attention(aq, ak, av, positions):
    B, S, _ = aq.shape
    q = rotary(aq.reshape(B, S, ATTN_HEADS, ATTN_HEAD_DIM), positions)
    k = rotary(ak.reshape(B, S, ATTN_HEADS, ATTN_HEAD_DIM), positions)
    v = av.reshape(B, S, ATTN_HEADS, ATTN_HEAD_DIM)
    outs, lses = [], []
    for g, (window, dilation) in enumerate(ATTN_GROUPS):
        sl = slice(g * ATTN_HEADS_PER_GROUP, (g + 1) * ATTN_HEADS_PER_GROUP)
        o_g, lse_g = dilated_group_attention(q[:, :, sl], k[:, :, sl], v[:, :, sl], window, dilation)
        outs.append(o_g)
        lses.append(lse_g)
    wts = jax.nn.softmax(jnp.stack(lses, axis=0), axis=0)
    o = jnp.sum(wts[..., None] * jnp.stack(outs, axis=0), axis=0)
    return o.reshape(B, S, ATTN_OUT_WIDTH).astype(aq.dtype)


def causal_conv_ffn_hidden(h, w_up, conv_w, conv_b):
    S = h.shape[1]
    u = h @ w_up
    up = jnp.pad(u, ((0, 0), (CONV_WIDTH - 1, 0), (0, 0)))
    u = conv_b + sum(conv_w[i] * up[:, i:i + S] for i in range(CONV_WIDTH))
    value, gate = jnp.split(u, 2, axis=-1)
    return jax.nn.gelu(gate, approximate=False) * value


def setup_inputs(seed: int = 0) -> dict:
    key = jax.random.key(seed)
    ks = jax.random.split(key, 20)
    f32 = jnp.float32

    def nrm(k, shape, scale):
        return jax.random.normal(k, shape, f32) * scale

    def gain(k, shape):
        return 1.0 + 0.02 * jax.random.normal(k, shape, f32)

    L = DEPTH
    x = nrm(ks[0], (BATCH, SEQ, D_MODEL), 1.0)
    c = nrm(ks[1], (BATCH, D_MODEL), 1.0)
    positions = (jax.random.randint(ks[2], (BATCH, 1), 0, 1024, dtype=jnp.int32)
                 + jnp.arange(SEQ, dtype=jnp.int32)[None, :])
    return {
        'x': x,
        'c': c,
        'positions': positions,
        'ada_w': nrm(ks[3], (L, D_MODEL, 6 * D_MODEL), D_MODEL ** -0.5),
        'ada_b': nrm(ks[4], (L, 6 * D_MODEL), 0.02),
        'norm1_w': gain(ks[5], (L, D_MODEL)),
        'w_in': nrm(ks[6], (L, D_MODEL, D_IN), D_MODEL ** -0.5),
        'gla_gate_w2': nrm(ks[7], (L, GLA_LOWRANK, GLA_QK_WIDTH), GLA_LOWRANK ** -0.5),
        'gla_gate_b': nrm(ks[8], (L, GLA_QK_WIDTH), 0.1),
        'gla_norm_w': gain(ks[9], (L, GLA_DV)),
        'w_gla_branch': nrm(ks[10], (L, GLA_V_WIDTH, D_MODEL), GLA_V_WIDTH ** -0.5),
        'w_attn_branch': nrm(ks[11], (L, ATTN_OUT_WIDTH, D_MODEL), ATTN_OUT_WIDTH ** -0.5),
        'w_out': nrm(ks[12], (L, D_MODEL, D_MODEL), D_MODEL ** -0.5),
        'norm2_w': gain(ks[13], (L, D_MODEL)),
        'w_up': nrm(ks[14], (L, D_MODEL, 2 * D_FF), D_MODEL ** -0.5),
        'conv_w': nrm(ks[15], (L, CONV_WIDTH, 2 * D_FF), CONV_WIDTH ** -0.5),
        'conv_b': nrm(ks[16], (L, 2 * D_FF), 0.01),
        'w_down': nrm(ks[17], (L, D_FF, D_MODEL), D_FF ** -0.5),
        'final_norm_w': gain(ks[18], (D_MODEL,)),
    }


def reference(x, c, positions, ada_w, ada_b, norm1_w, w_in, gla_gate_w2, gla_gate_b, gla_norm_w,
              w_gla_branch, w_attn_branch, w_out, norm2_w, w_up, conv_w, conv_b, w_down, final_norm_w):
    B, S, _ = x.shape
    split_at = np.cumsum(IN_WIDTHS)[:-1].tolist()
    for layer in range(DEPTH):
        mod = jax.nn.silu(c) @ ada_w[layer] + ada_b[layer]
        shift1, scale1, gate1, shift2, scale2, gate2 = jnp.split(mod[:, None, :], 6, axis=-1)

        h = rmsnorm(x, norm1_w[layer]) * (1 + scale1) + shift1
        gq, gk, gv, gr, g_lr, aq, ak, av, merge_a, merge_b = jnp.split(h @ w_in[layer], split_at, axis=-1)

        log_a = jax.nn.log_sigmoid((g_lr @ gla_gate_w2[layer] + gla_gate_b[layer]).astype(jnp.float32)) / GLA_TAU
        o_gla = gla_chunked(gq.reshape(B, S, GLA_HEADS, GLA_DK), gk.reshape(B, S, GLA_HEADS, GLA_DK),
                            gv.reshape(B, S, GLA_HEADS, GLA_DV), log_a.reshape(B, S, GLA_HEADS, GLA_DK))
        o_gla = rmsnorm(o_gla.astype(x.dtype), gla_norm_w[layer]).reshape(B, S, GLA_V_WIDTH) * jax.nn.silu(gr)
        y_gla = o_gla @ w_gla_branch[layer]

        y_attn = dilated_attention(aq, ak, av, positions) @ w_attn_branch[layer]

        mixed = jax.nn.sigmoid(merge_a) * y_gla + jax.nn.sigmoid(merge_b) * y_attn
        x = x + gate1 * (mixed @ w_out[layer])

        h2 = rmsnorm(x, norm2_w[layer]) * (1 + scale2) + shift2
        hidden = causal_conv_ffn_hidden(h2, w_up[layer], conv_w[layer], conv_b[layer])
        x = x + gate2 * (hidden @ w_down[layer])
    return rmsnorm(x, final_norm_w)
```

```python
import functools

import jax
import jax.numpy as jnp
import numpy as np
from jax import lax
from jax.experimental import pallas as pl
from jax.experimental.pallas import tpu as pltpu

F32 = jnp.float32
BF16 = jnp.bfloat16

D_MODEL = 1024
GLA_HEADS = 4
GLA_DK = 128
GLA_DV = 256
GLA_LOWRANK = 16
GLA_TAU = 16.0
GLA_QK_WIDTH = GLA_HEADS * GLA_DK
GLA_V_WIDTH = GLA_HEADS * GLA_DV
ATTN_GROUPS = ((128, 1), (512, 4), (2048, 16))
ATTN_HEADS_PER_GROUP = 4
ATTN_HEAD_DIM = 64
ATTN_WIDTH = ATTN_HEADS_PER_GROUP * len(ATTN_GROUPS) * ATTN_HEAD_DIM
ATTN_OUT_WIDTH = ATTN_HEADS_PER_GROUP * ATTN_HEAD_DIM
ROPE_THETA = 10000.0
D_FF = 2816
CONV_WIDTH = 3
EPS = 1e-6
IN_WIDTHS = (GLA_QK_WIDTH, GLA_QK_WIDTH, GLA_V_WIDTH, GLA_V_WIDTH, GLA_LOWRANK,
             ATTN_WIDTH, ATTN_WIDTH, ATTN_WIDTH, D_MODEL, D_MODEL)

LANES = 128
VMEM_LIMIT_BYTES = 56 * 1024 * 1024

TM_IN = 512
GLA_BLOCK = 512
GLA_CHUNK = 64
ATTN_BLK = 128
TM_MERGE = 512
TM_FFN = 1024
TF_FFN = 256
MASK_VALUE = -1e30


def _params(*sem):
    return pltpu.CompilerParams(dimension_semantics=sem, vmem_limit_bytes=VMEM_LIMIT_BYTES)


def _split_hi_lo(a):
    hi = a.astype(BF16)
    lo = (a - hi.astype(F32)).astype(BF16)
    return hi, lo


def _mod_kernel(c_ref, w_ref, b_ref, o_ref):
    c = c_ref[...]
    s = c * (1.0 / (1.0 + jnp.exp(-c)))
    s_hi, s_lo = _split_hi_lo(s)
    lhs = jnp.concatenate([s_hi, s_lo], axis=0)
    w_hi, w_lo = _split_hi_lo(w_ref[...])
    acc = jnp.dot(lhs, w_hi, preferred_element_type=F32) + jnp.dot(lhs, w_lo, preferred_element_type=F32)
    o_ref[...] = acc[0:8] + acc[8:16] + b_ref[...]


def _modulation(c, ada_w, ada_b):
    B, D = c.shape
    N = ada_w.shape[1]
    tn = 1024
    c8 = jnp.pad(c, ((0, 8 - B), (0, 0)))
    out = pl.pallas_call(
        _mod_kernel,
        grid=(N // tn,),
        in_specs=[pl.BlockSpec((8, D), lambda j: (0, 0)),
                  pl.BlockSpec((D, tn), lambda j: (0, j)),
                  pl.BlockSpec((1, tn), lambda j: (0, j))],
        out_specs=pl.BlockSpec((8, tn), lambda j: (0, j)),
        out_shape=jax.ShapeDtypeStruct((8, N), F32),
        compiler_params=_params("parallel"),
        name="modulation",
    )(c8, ada_w, ada_b.reshape(1, N))
    return out[:B]


def _rope_kernel(pos_ref, invf_ref, cos_ref, sin_ref):
    ang = pos_ref[...].astype(F32) * invf_ref[...]
    cos_ref[...] = jnp.cos(ang)
    sin_ref[...] = jnp.sin(ang)


def _rope_tables(positions):
    T = positions.size
    half = ATTN_HEAD_DIM // 2
    per_row = LANES // half
    inv_freq = ROPE_THETA ** (-jnp.arange(half, dtype=F32) / half)
    invf = jnp.tile(inv_freq, per_row).reshape(1, LANES)
    pos = jnp.repeat(positions.reshape(T // per_row, per_row), half, axis=1)
    rows = T // per_row
    tr = 512
    cos, sin = pl.pallas_call(
        _rope_kernel,
        grid=(rows // tr,),
        in_specs=[pl.BlockSpec((tr, LANES), lambda i: (i, 0)),
                  pl.BlockSpec((1, LANES), lambda i: (0, 0))],
        out_specs=[pl.BlockSpec((tr, LANES), lambda i: (i, 0))] * 2,
        out_shape=[jax.ShapeDtypeStruct((rows, LANES), F32)] * 2,
        compiler_params=_params("parallel"),
        name="rope_tables",
    )(pos, invf)
    cos = jnp.tile(cos.reshape(T, half), (1, per_row))
    sin = jnp.tile(sin.reshape(T, half), (1, per_row))
    return cos, sin


def _rmsnorm_mod(x, w, scale, shift):
    ms = jnp.mean(x * x, axis=-1, keepdims=True)
    return (x * lax.rsqrt(ms + EPS) * w) * (1.0 + scale) + shift


def _rotate_half_pairs(t, cos, sin):
    lane = lax.broadcasted_iota(jnp.int32, t.shape, 1)
    first_half = (lane % ATTN_HEAD_DIM) < (ATTN_HEAD_DIM // 2)
    from_right = pltpu.roll(t, LANES - ATTN_HEAD_DIM // 2, axis=1)
    from_left = pltpu.roll(t, ATTN_HEAD_DIM // 2, axis=1)
    partner = jnp.where(first_half, -from_right, from_left)
    return t * cos + partner * sin


def _in_proj_kernel(x_ref, n1w_ref, scale_ref, shift_ref, cos_ref, sin_ref, w_ref, wlr_ref, w2_ref, b2_ref,
                    gq_ref, gk_ref, gv_ref, gr_ref, la_ref, aq_ref, ak_ref, av_ref, ma_ref, mb_ref):
    h = _rmsnorm_mod(x_ref[...], n1w_ref[...], scale_ref[0], shift_ref[0]).astype(BF16)

    def proj(col, width):
        return jnp.dot(h, w_ref[:, col:col + width], preferred_element_type=F32)

    col = 0
    for ref, width in ((gq_ref, GLA_QK_WIDTH), (gk_ref, GLA_QK_WIDTH)):
        ref[...] = proj(col, width).astype(ref.dtype)
        col += width
    for ref in (gv_ref, gr_ref):
        for s in range(0, GLA_V_WIDTH, 512):
            ref[:, s:s + 512] = proj(col + s, 512).astype(ref.dtype)
        col += GLA_V_WIDTH
    cos = cos_ref[...]
    sin = sin_ref[...]
    q_scale = ATTN_HEAD_DIM ** -0.5
    for ref, scale in ((aq_ref, q_scale), (ak_ref, 1.0)):
        for s in range(0, ATTN_WIDTH, LANES):
            t = proj(col + s, LANES)
            ref[:, s:s + LANES] = (_rotate_half_pairs(t, cos, sin) * scale).astype(ref.dtype)
        col += ATTN_WIDTH
    for s in range(0, ATTN_WIDTH, 256):
        av_ref[:, s:s + 256] = proj(col + s, 256).astype(av_ref.dtype)
    col += ATTN_WIDTH
    for ref in (ma_ref, mb_ref):
        for s in range(0, D_MODEL, 512):
            z = proj(col + s, 512)
            ref[:, s:s + 512] = (1.0 / (1.0 + jnp.exp(-z))).astype(ref.dtype)
        col += D_MODEL

    g_lr = jnp.dot(h, wlr_ref[...], preferred_element_type=F32)
    g_hi, g_lo = _split_hi_lo(g_lr)
    w2_hi, w2_lo = _split_hi_lo(w2_ref[...])
    z = (jnp.dot(g_hi, w2_hi, preferred_element_type=F32) + jnp.dot(g_lo, w2_hi, preferred_element_type=F32)
         + jnp.dot(g_hi, w2_lo, preferred_element_type=F32)) + b2_ref[...]
    log_sig = jnp.minimum(z, 0.0) - jnp.log(1.0 + jnp.exp(-jnp.abs(z)))
    la_ref[...] = log_sig * (1.0 / GLA_TAU)


def _in_proj(x2, n1w, scale1, shift1, cos, sin, w_main, w_lr, w2p, b2):
    T, D = x2.shape
    B = scale1.shape[0]
    tm = TM_IN
    per_b = T // B // tm
    row = lambda i: (i, 0)
    const = lambda i: (0, 0)
    batch = lambda i: (i // per_b, 0, 0)
    widths = (GLA_QK_WIDTH, GLA_QK_WIDTH, GLA_V_WIDTH, GLA_V_WIDTH, GLA_QK_WIDTH,
              ATTN_WIDTH, ATTN_WIDTH, ATTN_WIDTH, D_MODEL, D_MODEL)
    dtypes = (BF16, BF16, BF16, BF16, F32, BF16, BF16, BF16, BF16, BF16)
    single = dict(pipeline_mode=pl.Buffered(1))
    return pl.pallas_call(
        _in_proj_kernel,
        grid=(T // tm,),
        in_specs=[pl.BlockSpec((tm, D), row),
                  pl.BlockSpec((1, D), const),
                  pl.BlockSpec((1, 1, D), batch),
                  pl.BlockSpec((1, 1, D), batch),
                  pl.BlockSpec((tm, LANES), row),
                  pl.BlockSpec((tm, LANES), row),
                  pl.BlockSpec(w_main.shape, const, **single),
                  pl.BlockSpec(w_lr.shape, const, **single),
                  pl.BlockSpec(w2p.shape, const, **single),
                  pl.BlockSpec(b2.shape, const, **single)],
        out_specs=[pl.BlockSpec((tm, w), row) for w in widths],
        out_shape=[jax.ShapeDtypeStruct((T, w), dt) for w, dt in zip(widths, dtypes)],
        compiler_params=_params("parallel"),
        name="in_proj",
    )(x2, n1w, scale1, shift1, cos, sin, w_main, w_lr, w2p, b2)


def _gla_kernel(q_ref, k_ref, v_ref, la_ref, gr_ref, nw_ref, o_ref, state_ref):
    C = GLA_CHUNK

    @pl.when(pl.program_id(2) == 0)
    def _():
        state_ref[...] = jnp.zeros_like(state_ref)

    rows = lax.broadcasted_iota(jnp.int32, (C, C), 0)
    cols = lax.broadcasted_iota(jnp.int32, (C, C), 1)
    causal = cols <= rows
    tri = causal.astype(BF16)
    q_scale = GLA_DK ** -0.5
    nw = nw_ref[...]

    for c in range(GLA_BLOCK // C):
        sl = pl.ds(c * C, C)
        q = q_ref[sl, :].astype(F32) * q_scale
        k = k_ref[sl, :].astype(F32)
        v = v_ref[sl, :]
        la_hi, la_lo = _split_hi_lo(la_ref[sl, :])
        cum = jnp.dot(tri, la_hi, preferred_element_type=F32) + jnp.dot(tri, la_lo, preferred_element_type=F32)
        mid = cum[C // 2 - 1:C // 2, :]
        last = cum[C - 1:C, :]
        qg = (q * jnp.exp(cum - mid)).astype(BF16)
        kg = (k * jnp.exp(mid - cum)).astype(BF16)
        q_in = (q * jnp.exp(cum)).astype(BF16)
        k_out = k * jnp.exp(last - cum)
        attn = lax.dot_general(qg, kg, (((1,), (1,)), ((), ())), preferred_element_type=F32)
        attn = jnp.where(causal, attn, 0.0).astype(BF16)
        state = state_ref[...]
        o = (jnp.dot(attn, v, preferred_element_type=F32)
             + jnp.dot(q_in, state.astype(BF16), preferred_element_type=F32))
        decay = jnp.transpose(jnp.broadcast_to(jnp.exp(last), (8, GLA_DK)))[:, 0:1]
        state_ref[...] = decay * state + jnp.dot(jnp.transpose(k_out).astype(BF16), v,
                                                 preferred_element_type=F32)
        ms = jnp.mean(o * o, axis=-1, keepdims=True)
        g = gr_ref[sl, :].astype(F32)
        silu_g = g * (1.0 / (1.0 + jnp.exp(-g)))
        o_ref[sl, :] = (o * lax.rsqrt(ms + EPS) * nw * silu_g).astype(o_ref.dtype)


def _gla(gq, gk, gv, la, gr, gla_norm_w, B):
    T = gq.shape[0]
    nb = T // B // GLA_BLOCK
    qk_map = lambda b, h, n: (b * nb + n, h)
    return pl.pallas_call(
        _gla_kernel,
        grid=(B, GLA_HEADS, nb),
        in_specs=[pl.BlockSpec((GLA_BLOCK, GLA_DK), qk_map),
                  pl.BlockSpec((GLA_BLOCK, GLA_DK), qk_map),
                  pl.BlockSpec((GLA_BLOCK, GLA_DV), qk_map),
                  pl.BlockSpec((GLA_BLOCK, GLA_DK), qk_map),
                  pl.BlockSpec((GLA_BLOCK, GLA_DV), qk_map),
                  pl.BlockSpec((1, GLA_DV), lambda b, h, n: (0, 0))],
        out_specs=pl.BlockSpec((GLA_BLOCK, GLA_DV), qk_map),
        out_shape=jax.ShapeDtypeStruct((T, GLA_V_WIDTH), BF16),
        scratch_shapes=[pltpu.VMEM((GLA_DK, GLA_DV), F32)],
        compiler_params=_params("parallel", "parallel", "arbitrary"),
        name="gla",
    )(gq, gk, gv, la, gr, gla_norm_w)


def _attn_kernel(q_ref, kc_ref, kp_ref, vc_ref, vp_ref, o_ref, lse_ref):
    blk = ATTN_BLK
    has_prev = pl.program_id(2) > 0
    rows = lax.broadcasted_iota(jnp.int32, (blk, blk), 0)
    cols = lax.broadcasted_iota(jnp.int32, (blk, blk), 1)
    cur_ok = cols <= rows
    prev_ok = (cols >= rows) & has_prev
    lane = lax.broadcasted_iota(jnp.int32, (blk, LANES), 1)
    lse_out = jnp.zeros((blk, LANES), F32)
    nt = (((1,), (1,)), ((), ()))
    for h in range(ATTN_HEADS_PER_GROUP):
        hs = slice(h * ATTN_HEAD_DIM, (h + 1) * ATTN_HEAD_DIM)
        q = q_ref[0, :, hs]
        s_cur = jnp.where(cur_ok, lax.dot_general(q, kc_ref[0, :, hs], nt, preferred_element_type=F32), MASK_VALUE)
        s_prev = jnp.where(prev_ok, lax.dot_general(q, kp_ref[0, :, hs], nt, preferred_element_type=F32), MASK_VALUE)
        m = jnp.maximum(jnp.max(s_cur, axis=-1, keepdims=True), jnp.max(s_prev, axis=-1, keepdims=True))
        p_cur = jnp.exp(s_cur - m)
        p_prev = jnp.exp(s_prev - m)
        denom = jnp.sum(p_cur, axis=-1, keepdims=True) + jnp.sum(p_prev, axis=-1, keepdims=True)
        acc = (jnp.dot(p_cur.astype(BF16), vc_ref[0, :, hs], preferred_element_type=F32)
               + jnp.dot(p_prev.astype(BF16), vp_ref[0, :, hs], preferred_element_type=F32))
        o_ref[0, :, hs] = (acc / denom).astype(o_ref.dtype)
        lse_out = jnp.where(lane == h, m + jnp.log(denom), lse_out)
    lse_ref[0] = lse_out


def _dilated_group(aq, ak, av, B, S, group, dilation):
    r = dilation
    L = S // r
    nblk = L // ATTN_BLK
    G = len(ATTN_GROUPS)
    gw = ATTN_OUT_WIDTH
    view = lambda a: a.reshape(B, L, r * ATTN_WIDTH)
    cur = lambda b, p, n: (b, n, p * G + group)
    prev = lambda b, p, n: (b, jnp.maximum(n - 1, 0), p * G + group)
    out_map = lambda b, p, n: (b, n, p)
    blk = (1, ATTN_BLK, gw)
    o, lse = pl.pallas_call(
        _attn_kernel,
        grid=(B, r, nblk),
        in_specs=[pl.BlockSpec(blk, cur), pl.BlockSpec(blk, cur), pl.BlockSpec(blk, prev),
                  pl.BlockSpec(blk, cur), pl.BlockSpec(blk, prev)],
        out_specs=[pl.BlockSpec(blk, out_map), pl.BlockSpec((1, ATTN_BLK, LANES), out_map)],
        out_shape=[jax.ShapeDtypeStruct((B, L, r * gw), BF16),
                   jax.ShapeDtypeStruct((B, L, r * LANES), F32)],
        compiler_params=_params("parallel", "parallel", "arbitrary"),
        name=f"dilated_attn_r{r}",
    )(view(aq), view(ak), view(ak), view(av), view(av))
    return o.reshape(B * S, gw), lse.reshape(B * S, LANES)


def _merge_kernel(x_ref, og_ref, o1_ref, o2_ref, o3_ref, l1_ref, l2_ref, l3_ref, ma_ref, mb_ref,
                  gate_ref, scale_ref, shift_ref, n2w_ref, wg_ref, wa_ref, wo_ref, x1_ref, h2_ref):
    l1, l2, l3 = l1_ref[...], l2_ref[...], l3_ref[...]
    m = jnp.maximum(jnp.maximum(l1, l2), l3)
    e1, e2, e3 = jnp.exp(l1 - m), jnp.exp(l2 - m), jnp.exp(l3 - m)
    den = e1 + e2 + e3
    tm = l1.shape[0]
    lane = lax.broadcasted_iota(jnp.int32, (tm, LANES), 1)
    left = lane < ATTN_HEAD_DIM
    pieces = []
    for j in range(ATTN_OUT_WIDTH // LANES):
        cs = slice(j * LANES, (j + 1) * LANES)
        acc = jnp.zeros((tm, LANES), F32)
        for e, o_ref in ((e1, o1_ref), (e2, o2_ref), (e3, o3_ref)):
            w = e / den
            wfull = jnp.where(left, w[:, 2 * j:2 * j + 1], w[:, 2 * j + 1:2 * j + 2])
            acc = acc + wfull * o_ref[:, cs].astype(F32)
        pieces.append(acc.astype(BF16))
    o_attn = jnp.concatenate(pieces, axis=1)
    y_attn = jnp.dot(o_attn, wa_ref[...], preferred_element_type=F32)
    y_gla = jnp.dot(og_ref[...], wg_ref[...], preferred_element_type=F32)
    mixed = (ma_ref[...].astype(F32) * y_gla + mb_ref[...].astype(F32) * y_attn).astype(BF16)
    x1 = x_ref[...] + gate_ref[0] * jnp.dot(mixed, wo_ref[...], preferred_element_type=F32)
    x1_ref[...] = x1
    h2_ref[...] = _rmsnorm_mod(x1, n2w_ref[...], scale_ref[0], shift_ref[0]).astype(h2_ref.dtype)


def _merge(x2, og, os, lses, ma, mb, gate1, scale2, shift2, n2w, wg, wa, wo):
    T, D = x2.shape
    B = gate1.shape[0]
    tm = TM_MERGE
    per_b = T // B // tm
    row = lambda i: (i, 0)
    const = lambda i: (0, 0)
    batch = lambda i: (i // per_b, 0, 0)
    return pl.pallas_call(
        _merge_kernel,
        grid=(T // tm,),
        in_specs=[pl.BlockSpec((tm, D), row), pl.BlockSpec((tm, GLA_V_WIDTH), row)]
                 + [pl.BlockSpec((tm, ATTN_OUT_WIDTH), row)] * 3
                 + [pl.BlockSpec((tm, LANES), row)] * 3
                 + [pl.BlockSpec((tm, D), row)] * 2
                 + [pl.BlockSpec((1, 1, D), batch)] * 3
                 + [pl.BlockSpec((1, D), const),
                    pl.BlockSpec(wg.shape, const), pl.BlockSpec(wa.shape, const), pl.BlockSpec(wo.shape, const)],
        out_specs=[pl.BlockSpec((tm, D), row)] * 2,
        out_shape=[jax.ShapeDtypeStruct((T, D), F32), jax.ShapeDtypeStruct((T, D), BF16)],
        compiler_params=_params("parallel"),
        name="merge",
    )(x2, og, *os, *lses, ma, mb, gate1, scale2, shift2, n2w, wg, wa, wo)


def _gelu_exact(g):
    return 0.5 * g * (1.0 + lax.erf(g * (2.0 ** -0.5)))


def _ffn_kernel(h_ref, hprev_ref, x1_ref, wv_ref, wgate_ref, cwv_ref, cwg_ref, cbv_ref, cbg_ref, wd_ref,
                gate_ref, fw_ref, o_ref, acc_ref, *, tiles_per_seq):
    i = pl.program_id(0)
    j = pl.program_id(1)

    @pl.when(j == 0)
    def _():
        acc_ref[...] = jnp.zeros_like(acc_ref)

    h = h_ref[...]
    hp = jnp.where(i % tiles_per_seq == 0, jnp.zeros_like(hprev_ref[...]), hprev_ref[...])
    tm = h.shape[0]
    halo = hp.shape[0]

    def conv_branch(w_ref, cw_ref, cb_ref):
        u = jnp.dot(h, w_ref[...], preferred_element_type=F32)
        up = jnp.dot(hp, w_ref[...], preferred_element_type=F32)
        ext = jnp.concatenate([up, u], axis=0)
        cw = cw_ref[...]
        return (cb_ref[...] + cw[0:1] * ext[halo - 2:halo - 2 + tm] + cw[1:2] * ext[halo - 1:halo - 1 + tm]
                + cw[2:3] * u)

    value = conv_branch(wv_ref, cwv_ref, cbv_ref)
    gate = conv_branch(wgate_ref, cwg_ref, cbg_ref)
    hidden = (_gelu_exact(gate) * value).astype(BF16)
    acc_ref[...] += jnp.dot(hidden, wd_ref[...], preferred_element_type=F32)

    @pl.when(j == pl.num_programs(1) - 1)
    def _():
        x2 = x1_ref[...] + gate_ref[0] * acc_ref[...]
        ms = jnp.mean(x2 * x2, axis=-1, keepdims=True)
        o_ref[...] = x2 * lax.rsqrt(ms + EPS) * fw_ref[...]


def _ffn(h2, x1, w_up, conv_w, conv_b, w_down, gate2, final_w, B):
    T, D = x1.shape
    tm, tf = TM_FFN, TF_FFN
    halo = 16
    per_b = T // B // tm
    nj = D_FF // tf
    row = lambda i, j: (i, 0)
    return pl.pallas_call(
        functools.partial(_ffn_kernel, tiles_per_seq=per_b),
        grid=(T // tm, nj),
        in_specs=[pl.BlockSpec((tm, D), row),
                  pl.BlockSpec((halo, D), lambda i, j: (jnp.maximum(i * (tm // halo) - 1, 0), 0)),
                  pl.BlockSpec((tm, D), row),
                  pl.BlockSpec((D, tf), lambda i, j: (0, j)),
                  pl.BlockSpec((D, tf), lambda i, j: (0, nj + j)),
                  pl.BlockSpec((CONV_WIDTH, tf), lambda i, j: (0, j)),
                  pl.BlockSpec((CONV_WIDTH, tf), lambda i, j: (0, nj + j)),
                  pl.BlockSpec((1, tf), lambda i, j: (0, j)),
                  pl.BlockSpec((1, tf), lambda i, j: (0, nj + j)),
                  pl.BlockSpec((tf, D), lambda i, j: (j, 0)),
                  pl.BlockSpec((1, 1, D), lambda i, j: (i // per_b, 0, 0)),
                  pl.BlockSpec((1, D), lambda i, j: (0, 0))],
        out_specs=pl.BlockSpec((tm, D), row),
        out_shape=jax.ShapeDtypeStruct((T, D), F32),
        scratch_shapes=[pltpu.VMEM((tm, D), F32)],
        compiler_params=_params("parallel", "arbitrary"),
        name="ffn",
    )(h2, h2, x1, w_up, w_up, conv_w, conv_w, conv_b, conv_b, w_down, gate2, final_w)


def kernel(x, c, positions, ada_w, ada_b, norm1_w, w_in, gla_gate_w2, gla_gate_b, gla_norm_w, w_gla_branch,
           w_attn_branch, w_out, norm2_w, w_up, conv_w, conv_b, w_down, final_norm_w):
    B, S, D = x.shape
    T = B * S
    depth = ada_w.shape[0]
    split_at = np.cumsum(IN_WIDTHS)[:-1].tolist()
    cos, sin = _rope_tables(positions)
    x2 = x.reshape(T, D)
    for layer in range(depth):
        mod = _modulation(c, ada_w[layer], ada_b[layer])
        shift1, scale1, gate1, shift2, scale2, gate2 = jnp.split(mod[:, None, :], 6, axis=-1)

        cols = jnp.split(w_in[layer], split_at, axis=1)
        w_main = jnp.concatenate([cols[i] for i in (0, 1, 2, 3, 5, 6, 7, 8, 9)], axis=1).astype(BF16)
        w_lr = jnp.pad(cols[4], ((0, 0), (0, LANES - GLA_LOWRANK))).astype(BF16)
        w2p = jnp.pad(gla_gate_w2[layer], ((0, LANES - GLA_LOWRANK), (0, 0)))

        gq, gk, gv, gr, la, aq, ak, av, ma, mb = _in_proj(
            x2, norm1_w[layer].reshape(1, D), scale1, shift1, cos, sin, w_main, w_lr, w2p,
            gla_gate_b[layer].reshape(1, -1))

        og = _gla(gq, gk, gv, la, gr, gla_norm_w[layer].reshape(1, -1), B)

        os, lses = [], []
        for g, (window, dilation) in enumerate(ATTN_GROUPS):
            assert window // dilation == ATTN_BLK
            o_g, lse_g = _dilated_group(aq, ak, av, B, S, g, dilation)
            os.append(o_g)
            lses.append(lse_g)

        x1, h2 = _merge(x2, og, os, lses, ma, mb, gate1, scale2, shift2, norm2_w[layer].reshape(1, D),
                        w_gla_branch[layer].astype(BF16), w_attn_branch[layer].astype(BF16),
                        w_out[layer].astype(BF16))

        last = layer == depth - 1
        assert last, "final norm is fused into the last layer's ffn"
        x2 = _ffn(h2, x1, w_up[layer].astype(BF16), conv_w[layer], conv_b[layer].reshape(1, -1),
                  w_down[layer].astype(BF16), gate2, final_norm_w.reshape(1, D), B)
    return x2.reshape(B, S, D)
```

```python
import functools

import jax
import jax.numpy as jnp
import numpy as np
from jax import lax
from jax.experimental import pallas as pl
from jax.experimental.pallas import tpu as pltpu

F32 = jnp.float32
BF16 = jnp.bfloat16

D_MODEL = 1024
GLA_HEADS = 4
GLA_DK = 128
GLA_DV = 256
GLA_LOWRANK = 16
GLA_TAU = 16.0
GLA_QK_WIDTH = GLA_HEADS * GLA_DK
GLA_V_WIDTH = GLA_HEADS * GLA_DV
ATTN_GROUPS = ((128, 1), (512, 4), (2048, 16))
ATTN_HEADS_PER_GROUP = 4
ATTN_HEAD_DIM = 64
ATTN_WIDTH = ATTN_HEADS_PER_GROUP * len(ATTN_GROUPS) * ATTN_HEAD_DIM
ATTN_OUT_WIDTH = ATTN_HEADS_PER_GROUP * ATTN_HEAD_DIM
ROPE_THETA = 10000.0
D_FF = 2816
CONV_WIDTH = 3
EPS = 1e-6
IN_WIDTHS = (GLA_QK_WIDTH, GLA_QK_WIDTH, GLA_V_WIDTH, GLA_V_WIDTH, GLA_LOWRANK,
             ATTN_WIDTH, ATTN_WIDTH, ATTN_WIDTH, D_MODEL, D_MODEL)

LANES = 128
VMEM_LIMIT_BYTES = 56 * 1024 * 1024

TM_IN = 512
GLA_BLOCK = 512
GLA_CHUNK = 64
ATTN_BLK = 128
ATTN_SPAN = 2048
TM_MERGE = 512
TM_FFN = 1024
TF_FFN = 256
MASK_VALUE = -1e30


def _params(*sem):
    return pltpu.CompilerParams(dimension_semantics=sem, vmem_limit_bytes=VMEM_LIMIT_BYTES)


def _split_hi_lo(a):
    hi = a.astype(BF16)
    lo = (a - hi.astype(F32)).astype(BF16)
    return hi, lo


def _mod_kernel(c_ref, w_ref, b_ref, o_ref):
    c = c_ref[...]
    s = c * (1.0 / (1.0 + jnp.exp(-c)))
    s_hi, s_lo = _split_hi_lo(s)
    lhs = jnp.concatenate([s_hi, s_lo], axis=0)
    w_hi, w_lo = _split_hi_lo(w_ref[...])
    acc = jnp.dot(lhs, w_hi, preferred_element_type=F32) + jnp.dot(lhs, w_lo, preferred_element_type=F32)
    o_ref[...] = acc[0:8] + acc[8:16] + b_ref[...]


def _modulation(c, ada_w, ada_b):
    B, D = c.shape
    N = ada_w.shape[1]
    tn = 1024
    c8 = jnp.pad(c, ((0, 8 - B), (0, 0)))
    out = pl.pallas_call(
        _mod_kernel,
        grid=(N // tn,),
        in_specs=[pl.BlockSpec((8, D), lambda j: (0, 0)),
                  pl.BlockSpec((D, tn), lambda j: (0, j)),
                  pl.BlockSpec((1, tn), lambda j: (0, j))],
        out_specs=pl.BlockSpec((8, tn), lambda j: (0, j)),
        out_shape=jax.ShapeDtypeStruct((8, N), F32),
        compiler_params=_params("parallel"),
        name="modulation",
    )(c8, ada_w, ada_b.reshape(1, N))
    return out[:B]


def _rope_kernel(pos_ref, invf_ref, cos_ref, sin_ref):
    ang = pos_ref[...].astype(F32) * invf_ref[...]
    cos_ref[...] = jnp.cos(ang)
    sin_ref[...] = jnp.sin(ang)


def _rope_tables(positions):
    T = positions.size
    half = ATTN_HEAD_DIM // 2
    per_row = LANES // half
    inv_freq = ROPE_THETA ** (-jnp.arange(half, dtype=F32) / half)
    invf = jnp.tile(inv_freq, per_row).reshape(1, LANES)
    pos = jnp.repeat(positions.reshape(T // per_row, per_row), half, axis=1)
    rows = T // per_row
    tr = 512
    cos, sin = pl.pallas_call(
        _rope_kernel,
        grid=(rows // tr,),
        in_specs=[pl.BlockSpec((tr, LANES), lambda i: (i, 0)),
                  pl.BlockSpec((1, LANES), lambda i: (0, 0))],
        out_specs=[pl.BlockSpec((tr, LANES), lambda i: (i, 0))] * 2,
        out_shape=[jax.ShapeDtypeStruct((rows, LANES), F32)] * 2,
        compiler_params=_params("parallel"),
        name="rope_tables",
    )(pos, invf)
    cos = jnp.tile(cos.reshape(T, half), (1, per_row))
    sin = jnp.tile(sin.reshape(T, half), (1, per_row))
    return cos, sin


def _rmsnorm_mod(x, w, scale, shift):
    ms = jnp.mean(x * x, axis=-1, keepdims=True)
    return (x * lax.rsqrt(ms + EPS) * w) * (1.0 + scale) + shift


def _rotate_half_pairs(t, cos, sin):
    lane = lax.broadcasted_iota(jnp.int32, t.shape, 1)
    first_half = (lane % ATTN_HEAD_DIM) < (ATTN_HEAD_DIM // 2)
    from_right = pltpu.roll(t, LANES - ATTN_HEAD_DIM // 2, axis=1)
    from_left = pltpu.roll(t, ATTN_HEAD_DIM // 2, axis=1)
    partner = jnp.where(first_half, -from_right, from_left)
    return t * cos + partner * sin


def _store_residue_major(ref, perm_ref, slab, col, dilation):
    if dilation == 1:
        ref[:, col:col + LANES] = slab.astype(ref.dtype)
        return
    perm_ref[...] = slab
    n = slab.shape[0] // dilation
    for p in range(dilation):
        ref[p * n:(p + 1) * n, col:col + LANES] = perm_ref[pl.ds(p, n, stride=dilation), :].astype(ref.dtype)


def _in_proj_kernel(x_ref, n1w_ref, scale_ref, shift_ref, cos_ref, sin_ref, w_ref, wlr_ref, w2_ref, b2_ref,
                    gq_ref, gk_ref, gv_ref, gr_ref, la_ref, q1_ref, q2_ref, q3_ref, k1_ref, k2_ref, k3_ref,
                    v1_ref, v2_ref, v3_ref, ma_ref, mb_ref, perm_ref):
    h = _rmsnorm_mod(x_ref[...], n1w_ref[...], scale_ref[0], shift_ref[0]).astype(BF16)

    def proj(col, width):
        return jnp.dot(h, w_ref[:, col:col + width], preferred_element_type=F32)

    col = 0
    for ref, width in ((gq_ref, GLA_QK_WIDTH), (gk_ref, GLA_QK_WIDTH)):
        ref[...] = proj(col, width).astype(ref.dtype)
        col += width
    for ref in (gv_ref, gr_ref):
        for s in range(0, GLA_V_WIDTH, 512):
            ref[:, s:s + 512] = proj(col + s, 512).astype(ref.dtype)
        col += GLA_V_WIDTH
    cos = cos_ref[...]
    sin = sin_ref[...]
    q_scale = ATTN_HEAD_DIM ** -0.5
    dilations = [d for _, d in ATTN_GROUPS]
    for refs, scale in (((q1_ref, q2_ref, q3_ref), q_scale), ((k1_ref, k2_ref, k3_ref), 1.0)):
        for s in range(0, ATTN_WIDTH, LANES):
            g, gcol = divmod(s, ATTN_OUT_WIDTH)
            t = _rotate_half_pairs(proj(col + s, LANES), cos, sin) * scale
            _store_residue_major(refs[g], perm_ref, t, gcol, dilations[g])
        col += ATTN_WIDTH
    for s in range(0, ATTN_WIDTH, LANES):
        g, gcol = divmod(s, ATTN_OUT_WIDTH)
        _store_residue_major((v1_ref, v2_ref, v3_ref)[g], perm_ref, proj(col + s, LANES), gcol, dilations[g])
    col += ATTN_WIDTH
    for ref in (ma_ref, mb_ref):
        for s in range(0, D_MODEL, 512):
            z = proj(col + s, 512)
            ref[:, s:s + 512] = (1.0 / (1.0 + jnp.exp(-z))).astype(ref.dtype)
        col += D_MODEL

    g_lr = jnp.dot(h, wlr_ref[...], preferred_element_type=F32)
    g_hi, g_lo = _split_hi_lo(g_lr)
    w2_hi, w2_lo = _split_hi_lo(w2_ref[...])
    z = (jnp.dot(g_hi, w2_hi, preferred_element_type=F32) + jnp.dot(g_lo, w2_hi, preferred_element_type=F32)
         + jnp.dot(g_hi, w2_lo, preferred_element_type=F32)) + b2_ref[...]
    log_sig = jnp.minimum(z, 0.0) - jnp.log(1.0 + jnp.exp(-jnp.abs(z)))
    la_ref[...] = log_sig * (1.0 / GLA_TAU)


def _in_proj(x2, n1w, scale1, shift1, cos, sin, w_main, w_lr, w2p, b2):
    T, D = x2.shape
    B = scale1.shape[0]
    tm = TM_IN
    per_b = T // B // tm
    row = lambda i: (i, 0)
    const = lambda i: (0, 0)
    batch = lambda i: (i // per_b, 0, 0)
    widths = ((GLA_QK_WIDTH, GLA_QK_WIDTH, GLA_V_WIDTH, GLA_V_WIDTH, GLA_QK_WIDTH)
              + (ATTN_OUT_WIDTH,) * 9 + (D_MODEL, D_MODEL))
    dtypes = (BF16, BF16, BF16, BF16, F32) + (BF16,) * 11
    single = dict(pipeline_mode=pl.Buffered(1))
    return pl.pallas_call(
        _in_proj_kernel,
        grid=(T // tm,),
        in_specs=[pl.BlockSpec((tm, D), row),
                  pl.BlockSpec((1, D), const),
                  pl.BlockSpec((1, 1, D), batch),
                  pl.BlockSpec((1, 1, D), batch),
                  pl.BlockSpec((tm, LANES), row),
                  pl.BlockSpec((tm, LANES), row),
                  pl.BlockSpec(w_main.shape, const, **single),
                  pl.BlockSpec(w_lr.shape, const, **single),
                  pl.BlockSpec(w2p.shape, const, **single),
                  pl.BlockSpec(b2.shape, const, **single)],
        out_specs=[pl.BlockSpec((tm, w), row) for w in widths],
        out_shape=[jax.ShapeDtypeStruct((T, w), dt) for w, dt in zip(widths, dtypes)],
        scratch_shapes=[pltpu.VMEM((tm, LANES), F32)],
        compiler_params=_params("parallel"),
        name="in_proj",
    )(x2, n1w, scale1, shift1, cos, sin, w_main, w_lr, w2p, b2)


def _gla_kernel(q_ref, k_ref, v_ref, la_ref, gr_ref, nw_ref, o_ref, state_ref):
    C = GLA_CHUNK

    @pl.when(pl.program_id(2) == 0)
    def _():
        state_ref[...] = jnp.zeros_like(state_ref)

    rows = lax.broadcasted_iota(jnp.int32, (C, C), 0)
    cols = lax.broadcasted_iota(jnp.int32, (C, C), 1)
    causal = cols <= rows
    tri = causal.astype(BF16)
    q_scale = GLA_DK ** -0.5
    nw = nw_ref[...]

    for c in range(GLA_BLOCK // C):
        sl = pl.ds(c * C, C)
        q = q_ref[sl, :].astype(F32) * q_scale
        k = k_ref[sl, :].astype(F32)
        v = v_ref[sl, :]
        la_hi, la_lo = _split_hi_lo(la_ref[sl, :])
        cum = jnp.dot(tri, la_hi, preferred_element_type=F32) + jnp.dot(tri, la_lo, preferred_element_type=F32)
        mid = cum[C // 2 - 1:C // 2, :]
        last = cum[C - 1:C, :]
        qg = (q * jnp.exp(cum - mid)).astype(BF16)
        kg = (k * jnp.exp(mid - cum)).astype(BF16)
        q_in = (q * jnp.exp(cum)).astype(BF16)
        k_out = k * jnp.exp(last - cum)
        attn = lax.dot_general(qg, kg, (((1,), (1,)), ((), ())), preferred_element_type=F32)
        attn = jnp.where(causal, attn, 0.0).astype(BF16)
        state = state_ref[...]
        o = (jnp.dot(attn, v, preferred_element_type=F32)
             + jnp.dot(q_in, state.astype(BF16), preferred_element_type=F32))
        decay = jnp.transpose(jnp.broadcast_to(jnp.exp(last), (8, GLA_DK)))[:, 0:1]
        state_ref[...] = decay * state + jnp.dot(jnp.transpose(k_out).astype(BF16), v,
                                                 preferred_element_type=F32)
        ms = jnp.mean(o * o, axis=-1, keepdims=True)
        g = gr_ref[sl, :].astype(F32)
        silu_g = g * (1.0 / (1.0 + jnp.exp(-g)))
        o_ref[sl, :] = (o * lax.rsqrt(ms + EPS) * nw * silu_g).astype(o_ref.dtype)


def _gla(gq, gk, gv, la, gr, gla_norm_w, B):
    T = gq.shape[0]
    nb = T // B // GLA_BLOCK
    qk_map = lambda b, h, n: (b * nb + n, h)
    return pl.pallas_call(
        _gla_kernel,
        grid=(B, GLA_HEADS, nb),
        in_specs=[pl.BlockSpec((GLA_BLOCK, GLA_DK), qk_map),
                  pl.BlockSpec((GLA_BLOCK, GLA_DK), qk_map),
                  pl.BlockSpec((GLA_BLOCK, GLA_DV), qk_map),
                  pl.BlockSpec((GLA_BLOCK, GLA_DK), qk_map),
                  pl.BlockSpec((GLA_BLOCK, GLA_DV), qk_map),
                  pl.BlockSpec((1, GLA_DV), lambda b, h, n: (0, 0))],
        out_specs=pl.BlockSpec((GLA_BLOCK, GLA_DV), qk_map),
        out_shape=jax.ShapeDtypeStruct((T, GLA_V_WIDTH), BF16),
        scratch_shapes=[pltpu.VMEM((GLA_DK, GLA_DV), F32)],
        compiler_params=_params("parallel", "parallel", "arbitrary"),
        name="gla",
    )(gq, gk, gv, la, gr, gla_norm_w)


def _attn_kernel(q1_ref, q2_ref, q3_ref, k1c_ref, k1p_ref, k2c_ref, k2p_ref, k3c_ref, k3p_ref,
                 v1c_ref, v1p_ref, v2c_ref, v2p_ref, v3c_ref, v3p_ref, o_ref,
                 k1cat, v1cat, k2cat, v2cat, *slabs):
    blk = ATTN_BLK
    tile = TM_IN
    span = ATTN_SPAN
    tiles = span // tile
    r2, r3 = ATTN_GROUPS[1][1], ATTN_GROUPS[2][1]
    per3 = tile // r3
    not_first_span = pl.program_id(1) > 0
    npair = ATTN_OUT_WIDTH // LANES
    o1_s, o2_s, o3_s, l1_s, l2_s, l3_s, out_s = [slabs[n * npair:(n + 1) * npair] for n in range(7)]

    for cat, prev, cur in ((k1cat, k1p_ref, k1c_ref), (v1cat, v1p_ref, v1c_ref),
                           (k2cat, k2p_ref, k2c_ref), (v2cat, v2p_ref, v2c_ref)):
        halo = prev.shape[0]
        cat[0:halo] = prev[...]
        cat[halo:] = cur[...]

    rows = lax.broadcasted_iota(jnp.int32, (blk, 2 * blk), 0)
    cols = lax.broadcasted_iota(jnp.int32, (blk, 2 * blk), 1)
    band = (cols >= rows) & (cols <= rows + blk)
    in_cur = cols >= blk
    left = lax.broadcasted_iota(jnp.int32, (blk, LANES), 1) < ATTN_HEAD_DIM
    ones = jnp.ones((2 * blk, LANES), BF16)
    nt = (((1,), (1,)), ((), ()))

    def pair_attend(q_pair, k_cat, v_cat, has_prev):
        zero = jnp.zeros_like(q_pair)
        q2 = jnp.concatenate([jnp.where(left, q_pair, zero), jnp.where(left, zero, q_pair)], axis=0)
        s = lax.dot_general(q2, k_cat, nt, preferred_element_type=F32)
        valid = band & (in_cur | has_prev)
        s = jnp.where(jnp.concatenate([valid, valid], axis=0), s, MASK_VALUE)
        m = jnp.max(s, axis=-1, keepdims=True)
        p = jnp.exp(s - m).astype(BF16)
        r = jnp.dot(p, jnp.concatenate([v_cat, ones], axis=1), preferred_element_type=F32)
        acc = jnp.where(left, r[:blk, :LANES], r[blk:, :LANES])
        den = jnp.where(left, r[:blk, LANES:], r[blk:, LANES:])
        m_pair = jnp.where(left, jnp.broadcast_to(m[:blk], (blk, LANES)), jnp.broadcast_to(m[blk:], (blk, LANES)))
        return acc / den, m_pair + jnp.log(den)

    def tile_body(t, carry):
        for p in range(tiles):
            i = t * tiles + p
            r0 = pl.multiple_of(i * blk, blk)
            for j in range(ATTN_OUT_WIDTH // LANES):
                cs = slice(j * LANES, (j + 1) * LANES)
                o, lse = pair_attend(q1_ref[pl.ds(r0, blk), cs], k1cat[pl.ds(r0, 2 * blk), cs],
                                     v1cat[pl.ds(r0, 2 * blk), cs], not_first_span | (i > 0))
                o1_s[j][pl.ds(r0, blk), :] = o
                l1_s[j][pl.ds(r0, blk), :] = lse
                r1 = pl.multiple_of(r0 + tile, blk)
                k_cat = jnp.concatenate([k2cat[pl.ds(r0, blk), cs], k2cat[pl.ds(r1, blk), cs]], axis=0)
                v_cat = jnp.concatenate([v2cat[pl.ds(r0, blk), cs], v2cat[pl.ds(r1, blk), cs]], axis=0)
                o, lse = pair_attend(q2_ref[pl.ds(r0, blk), cs], k_cat, v_cat, not_first_span | (t > 0))
                tok = pl.ds(pl.multiple_of(t * tile, tile) + p, blk, stride=r2)
                o2_s[j][tok, :] = o
                l2_s[j][tok, :] = lse
                rr = pl.multiple_of(i * per3, per3)
                gather = lambda ref: jnp.concatenate(
                    [ref[pl.ds(rr + u * tile, per3), cs] for u in range(tiles)], axis=0)
                k_cat = jnp.concatenate([gather(k3p_ref), gather(k3c_ref)], axis=0)
                v_cat = jnp.concatenate([gather(v3p_ref), gather(v3c_ref)], axis=0)
                o, lse = pair_attend(gather(q3_ref), k_cat, v_cat, not_first_span)
                o3_s[j][pl.ds(r0, blk), :] = o
                l3_s[j][pl.ds(r0, blk), :] = lse
        return carry

    lax.fori_loop(0, tiles, tile_body, 0)

    for j in range(npair):
        for p in range(r3):
            tok = pl.ds(p, blk, stride=r3)
            res = slice(p * blk, (p + 1) * blk)
            l1, l2, l3 = l1_s[j][tok, :], l2_s[j][tok, :], l3_s[j][res, :]
            m = jnp.maximum(jnp.maximum(l1, l2), l3)
            e1, e2, e3 = jnp.exp(l1 - m), jnp.exp(l2 - m), jnp.exp(l3 - m)
            num = e1 * o1_s[j][tok, :] + e2 * o2_s[j][tok, :] + e3 * o3_s[j][res, :]
            out_s[j][tok, :] = num / (e1 + e2 + e3)
        o_ref[:, j * LANES:(j + 1) * LANES] = out_s[j][...].astype(o_ref.dtype)


def _dilated_attention(qs, ks, vs, B, S):
    span = ATTN_SPAN
    nsp = S // span
    gw = ATTN_OUT_WIDTH
    T = B * S
    cur = lambda b, s: (b * nsp + s, 0)

    def prev(rows):
        per = span // rows
        return pl.BlockSpec((rows, gw), lambda b, s: (jnp.maximum((b * nsp + s) * per - 1, 0), 0))

    halos = (ATTN_BLK, TM_IN, span)
    full = pl.BlockSpec((span, gw), cur)
    kv_specs = []
    for h in halos:
        kv_specs += [full, prev(h)]
    kv_args = lambda arrs: [a for arr in arrs for a in (arr, arr)]
    return pl.pallas_call(
        _attn_kernel,
        grid=(B, nsp),
        in_specs=[full] * 3 + kv_specs + kv_specs,
        out_specs=full,
        out_shape=jax.ShapeDtypeStruct((T, gw), BF16),
        scratch_shapes=[pltpu.VMEM((halos[0] + span, gw), BF16), pltpu.VMEM((halos[0] + span, gw), BF16),
                        pltpu.VMEM((halos[1] + span, gw), BF16), pltpu.VMEM((halos[1] + span, gw), BF16)]
                       + [pltpu.VMEM((span, LANES), F32)] * (7 * (gw // LANES)),
        compiler_params=_params("parallel", "parallel"),
        name="dilated_attn",
    )(*qs, *kv_args(ks), *kv_args(vs))


def _merge_kernel(x_ref, og_ref, oa_ref, ma_ref, mb_ref,
                  gate_ref, scale_ref, shift_ref, n2w_ref, wg_ref, wa_ref, wo_ref, x1_ref, h2_ref):
    y_attn = jnp.dot(oa_ref[...], wa_ref[...], preferred_element_type=F32)
    y_gla = jnp.dot(og_ref[...], wg_ref[...], preferred_element_type=F32)
    mixed = (ma_ref[...].astype(F32) * y_gla + mb_ref[...].astype(F32) * y_attn).astype(BF16)
    x1 = x_ref[...] + gate_ref[0] * jnp.dot(mixed, wo_ref[...], preferred_element_type=F32)
    x1_ref[...] = x1
    h2_ref[...] = _rmsnorm_mod(x1, n2w_ref[...], scale_ref[0], shift_ref[0]).astype(h2_ref.dtype)


def _merge(x2, og, oa, ma, mb, gate1, scale2, shift2, n2w, wg, wa, wo):
    T, D = x2.shape
    B = gate1.shape[0]
    tm = TM_MERGE
    per_b = T // B // tm
    row = lambda i: (i, 0)
    const = lambda i: (0, 0)
    batch = lambda i: (i // per_b, 0, 0)
    return pl.pallas_call(
        _merge_kernel,
        grid=(T // tm,),
        in_specs=[pl.BlockSpec((tm, D), row), pl.BlockSpec((tm, GLA_V_WIDTH), row),
                  pl.BlockSpec((tm, ATTN_OUT_WIDTH), row)]
                 + [pl.BlockSpec((tm, D), row)] * 2
                 + [pl.BlockSpec((1, 1, D), batch)] * 3
                 + [pl.BlockSpec((1, D), const),
                    pl.BlockSpec(wg.shape, const), pl.BlockSpec(wa.shape, const), pl.BlockSpec(wo.shape, const)],
        out_specs=[pl.BlockSpec((tm, D), row)] * 2,
        out_shape=[jax.ShapeDtypeStruct((T, D), F32), jax.ShapeDtypeStruct((T, D), BF16)],
        compiler_params=_params("parallel"),
        name="merge",
    )(x2, og, oa, ma, mb, gate1, scale2, shift2, n2w, wg, wa, wo)


def _gelu_exact(g):
    return 0.5 * g * (1.0 + lax.erf(g * (2.0 ** -0.5)))


def _ffn_kernel(h_ref, hprev_ref, x1_ref, wv_ref, wgate_ref, cwv_ref, cwg_ref, cbv_ref, cbg_ref, wd_ref,
                gate_ref, fw_ref, o_ref, acc_ref, *, tiles_per_seq):
    i = pl.program_id(0)
    j = pl.program_id(1)

    @pl.when(j == 0)
    def _():
        acc_ref[...] = jnp.zeros_like(acc_ref)

    h = h_ref[...]
    hp = jnp.where(i % tiles_per_seq == 0, jnp.zeros_like(hprev_ref[...]), hprev_ref[...])
    tm = h.shape[0]
    halo = hp.shape[0]

    def conv_branch(w_ref, cw_ref, cb_ref):
        u = jnp.dot(h, w_ref[...], preferred_element_type=F32)
        up = jnp.dot(hp, w_ref[...], preferred_element_type=F32)
        ext = jnp.concatenate([up, u], axis=0)
        cw = cw_ref[...]
        return (cb_ref[...] + cw[0:1] * ext[halo - 2:halo - 2 + tm] + cw[1:2] * ext[halo - 1:halo - 1 + tm]
                + cw[2:3] * u)

    value = conv_branch(wv_ref, cwv_ref, cbv_ref)
    gate = conv_branch(wgate_ref, cwg_ref, cbg_ref)
    hidden = (_gelu_exact(gate) * value).astype(BF16)
    acc_ref[...] += jnp.dot(hidden, wd_ref[...], preferred_element_type=F32)

    @pl.when(j == pl.num_programs(1) - 1)
    def _():
        x2 = x1_ref[...] + gate_ref[0] * acc_ref[...]
        ms = jnp.mean(x2 * x2, axis=-1, keepdims=True)
        o_ref[...] = x2 * lax.rsqrt(ms + EPS) * fw_ref[...]


def _ffn(h2, x1, w_up, conv_w, conv_b, w_down, gate2, final_w, B):
    T, D = x1.shape
    tm, tf = TM_FFN, TF_FFN
    halo = 16
    per_b = T // B // tm
    nj = D_FF // tf
    row = lambda i, j: (i, 0)
    return pl.pallas_call(
        functools.partial(_ffn_kernel, tiles_per_seq=per_b),
        grid=(T // tm, nj),
        in_specs=[pl.BlockSpec((tm, D), row),
                  pl.BlockSpec((halo, D), lambda i, j: (jnp.maximum(i * (tm // halo) - 1, 0), 0)),
                  pl.BlockSpec((tm, D), row),
                  pl.BlockSpec((D, tf), lambda i, j: (0, j)),
                  pl.BlockSpec((D, tf), lambda i, j: (0, nj + j)),
                  pl.BlockSpec((CONV_WIDTH, tf), lambda i, j: (0, j)),
                  pl.BlockSpec((CONV_WIDTH, tf), lambda i, j: (0, nj + j)),
                  pl.BlockSpec((1, tf), lambda i, j: (0, j)),
                  pl.BlockSpec((1, tf), lambda i, j: (0, nj + j)),
                  pl.BlockSpec((tf, D), lambda i, j: (j, 0)),
                  pl.BlockSpec((1, 1, D), lambda i, j: (i // per_b, 0, 0)),
                  pl.BlockSpec((1, D), lambda i, j: (0, 0))],
        out_specs=pl.BlockSpec((tm, D), row),
        out_shape=jax.ShapeDtypeStruct((T, D), F32),
        scratch_shapes=[pltpu.VMEM((tm, D), F32)],
        compiler_params=_params("parallel", "arbitrary"),
        name="ffn",
    )(h2, h2, x1, w_up, w_up, conv_w, conv_w, conv_b, conv_b, w_down, gate2, final_w)


def kernel(x, c, positions, ada_w, ada_b, norm1_w, w_in, gla_gate_w2, gla_gate_b, gla_norm_w, w_gla_branch,
           w_attn_branch, w_out, norm2_w, w_up, conv_w, conv_b, w_down, final_norm_w):
    B, S, D = x.shape
    T = B * S
    depth = ada_w.shape[0]
    split_at = np.cumsum(IN_WIDTHS)[:-1].tolist()
    cos, sin = _rope_tables(positions)
    x2 = x.reshape(T, D)
    for layer in range(depth):
        mod = _modulation(c, ada_w[layer], ada_b[layer])
        shift1, scale1, gate1, shift2, scale2, gate2 = jnp.split(mod[:, None, :], 6, axis=-1)

        cols = jnp.split(w_in[layer], split_at, axis=1)
        w_main = jnp.concatenate([cols[i] for i in (0, 1, 2, 3, 5, 6, 7, 8, 9)], axis=1).astype(BF16)
        w_lr = jnp.pad(cols[4], ((0, 0), (0, LANES - GLA_LOWRANK))).astype(BF16)
        w2p = jnp.pad(gla_gate_w2[layer], ((0, LANES - GLA_LOWRANK), (0, 0)))

        gq, gk, gv, gr, la, q1, q2, q3, k1, k2, k3, v1, v2, v3, ma, mb = _in_proj(
            x2, norm1_w[layer].reshape(1, D), scale1, shift1, cos, sin, w_main, w_lr, w2p,
            gla_gate_b[layer].reshape(1, -1))

        og = _gla(gq, gk, gv, la, gr, gla_norm_w[layer].reshape(1, -1), B)
        oa = _dilated_attention((q1, q2, q3), (k1, k2, k3), (v1, v2, v3), B, S)

        x1, h2 = _merge(x2, og, oa, ma, mb, gate1, scale2, shift2, norm2_w[layer].reshape(1, D),
                        w_gla_branch[layer].astype(BF16), w_attn_branch[layer].astype(BF16),
                        w_out[layer].astype(BF16))

        last = layer == depth - 1
        assert last, "final norm is fused into the last layer's ffn"
        x2 = _ffn(h2, x1, w_up[layer].astype(BF16), conv_w[layer], conv_b[layer].reshape(1, -1),
                  w_down[layer].astype(BF16), gate2, final_norm_w.reshape(1, D), B)
    return x2.reshape(B, S, D)
```

```python
import functools

import jax
import jax.numpy as jnp
import numpy as np
from jax import lax
from jax.experimental import pallas as pl
from jax.experimental.pallas import tpu as pltpu

F32 = jnp.float32
BF16 = jnp.bfloat16

D_MODEL = 1024
GLA_HEADS = 4
GLA_DK = 128
GLA_DV = 256
GLA_LOWRANK = 16
GLA_TAU = 16.0
GLA_QK_WIDTH = GLA_HEADS * GLA_DK
GLA_V_WIDTH = GLA_HEADS * GLA_DV
ATTN_GROUPS = ((128, 1), (512, 4), (2048, 16))
ATTN_HEADS_PER_GROUP = 4
ATTN_HEAD_DIM = 64
ATTN_WIDTH = ATTN_HEADS_PER_GROUP * len(ATTN_GROUPS) * ATTN_HEAD_DIM
ATTN_OUT_WIDTH = ATTN_HEADS_PER_GROUP * ATTN_HEAD_DIM
ROPE_THETA = 10000.0
D_FF = 2816
CONV_WIDTH = 3
EPS = 1e-6
IN_WIDTHS = (GLA_QK_WIDTH, GLA_QK_WIDTH, GLA_V_WIDTH, GLA_V_WIDTH, GLA_LOWRANK,
             ATTN_WIDTH, ATTN_WIDTH, ATTN_WIDTH, D_MODEL, D_MODEL)

LANES = 128
VMEM_LIMIT_BYTES = 56 * 1024 * 1024

TM_IN = 512
GLA_BLOCK = 512
GLA_CHUNK = 64
ATTN_BLK = 128
ATTN_SPAN = 2048
TM_MERGE = 512
TM_FFN = 512
FFN_CHUNK = 256
FFN_HALO = 16
MASK_VALUE = -1e30


def _params(*sem):
    return pltpu.CompilerParams(dimension_semantics=sem, vmem_limit_bytes=VMEM_LIMIT_BYTES)


def _split_hi_lo(a):
    hi = a.astype(BF16)
    lo = (a - hi.astype(F32)).astype(BF16)
    return hi, lo


def _mod_kernel(c_ref, w_ref, b_ref, o_ref):
    c = c_ref[...]
    s = c * (1.0 / (1.0 + jnp.exp(-c)))
    s_hi, s_lo = _split_hi_lo(s)
    lhs = jnp.concatenate([s_hi, s_lo], axis=0)
    w_hi, w_lo = _split_hi_lo(w_ref[...])
    acc = jnp.dot(lhs, w_hi, preferred_element_type=F32) + jnp.dot(lhs, w_lo, preferred_element_type=F32)
    o_ref[...] = acc[0:8] + acc[8:16] + b_ref[...]


def _modulation(c, ada_w, ada_b):
    B, D = c.shape
    N = ada_w.shape[1]
    tn = 1024
    c8 = jnp.pad(c, ((0, 8 - B), (0, 0)))
    out = pl.pallas_call(
        _mod_kernel,
        grid=(N // tn,),
        in_specs=[pl.BlockSpec((8, D), lambda j: (0, 0)),
                  pl.BlockSpec((D, tn), lambda j: (0, j)),
                  pl.BlockSpec((1, tn), lambda j: (0, j))],
        out_specs=pl.BlockSpec((8, tn), lambda j: (0, j)),
        out_shape=jax.ShapeDtypeStruct((8, N), F32),
        compiler_params=_params("parallel"),
        name="modulation",
    )(c8, ada_w, ada_b.reshape(1, N))
    return out[:B]


def _rope_kernel(pos_ref, invf_ref, cos_ref, sin_ref):
    ang = pos_ref[...].astype(F32) * invf_ref[...]
    cos_ref[...] = jnp.cos(ang)
    sin_ref[...] = jnp.sin(ang)


def _rope_tables(positions):
    T = positions.size
    half = ATTN_HEAD_DIM // 2
    per_row = LANES // half
    inv_freq = ROPE_THETA ** (-jnp.arange(half, dtype=F32) / half)
    invf = jnp.tile(inv_freq, per_row).reshape(1, LANES)
    pos = jnp.repeat(positions.reshape(T // per_row, per_row), half, axis=1)
    rows = T // per_row
    tr = 512
    cos, sin = pl.pallas_call(
        _rope_kernel,
        grid=(rows // tr,),
        in_specs=[pl.BlockSpec((tr, LANES), lambda i: (i, 0)),
                  pl.BlockSpec((1, LANES), lambda i: (0, 0))],
        out_specs=[pl.BlockSpec((tr, LANES), lambda i: (i, 0))] * 2,
        out_shape=[jax.ShapeDtypeStruct((rows, LANES), F32)] * 2,
        compiler_params=_params("parallel"),
        name="rope_tables",
    )(pos, invf)
    cos = jnp.tile(cos.reshape(T, half), (1, per_row))
    sin = jnp.tile(sin.reshape(T, half), (1, per_row))
    return cos, sin


def _rmsnorm_mod(x, w, scale, shift):
    ms = jnp.mean(x * x, axis=-1, keepdims=True)
    return (x * lax.rsqrt(ms + EPS) * w) * (1.0 + scale) + shift


def _rotate_half_pairs(t, cos, sin):
    lane = lax.broadcasted_iota(jnp.int32, t.shape, 1)
    first_half = (lane % ATTN_HEAD_DIM) < (ATTN_HEAD_DIM // 2)
    from_right = pltpu.roll(t, LANES - ATTN_HEAD_DIM // 2, axis=1)
    from_left = pltpu.roll(t, ATTN_HEAD_DIM // 2, axis=1)
    partner = jnp.where(first_half, -from_right, from_left)
    return t * cos + partner * sin


def _store_residue_major(ref, perm_ref, slab, col, dilation):
    if dilation == 1:
        ref[:, col:col + LANES] = slab.astype(ref.dtype)
        return
    perm_ref[...] = slab
    n = slab.shape[0] // dilation
    for p in range(dilation):
        ref[p * n:(p + 1) * n, col:col + LANES] = perm_ref[pl.ds(p, n, stride=dilation), :].astype(ref.dtype)


def _in_proj_kernel(x_ref, n1w_ref, scale_ref, shift_ref, cos_ref, sin_ref, w_ref, wlr_ref, w2_ref, b2_ref,
                    gq_ref, gk_ref, gv_ref, gr_ref, la_ref, q1_ref, q2_ref, q3_ref, k1_ref, k2_ref, k3_ref,
                    v1_ref, v2_ref, v3_ref, ma_ref, mb_ref, perm_ref):
    h = _rmsnorm_mod(x_ref[...], n1w_ref[...], scale_ref[0], shift_ref[0]).astype(BF16)

    def proj(col, width):
        return jnp.dot(h, w_ref[:, col:col + width], preferred_element_type=F32)

    col = 0
    for ref, width in ((gq_ref, GLA_QK_WIDTH), (gk_ref, GLA_QK_WIDTH)):
        ref[...] = proj(col, width).astype(ref.dtype)
        col += width
    for ref in (gv_ref, gr_ref):
        for s in range(0, GLA_V_WIDTH, 512):
            ref[:, s:s + 512] = proj(col + s, 512).astype(ref.dtype)
        col += GLA_V_WIDTH
    cos = cos_ref[...]
    sin = sin_ref[...]
    q_scale = ATTN_HEAD_DIM ** -0.5
    dilations = [d for _, d in ATTN_GROUPS]
    for refs, scale in (((q1_ref, q2_ref, q3_ref), q_scale), ((k1_ref, k2_ref, k3_ref), 1.0)):
        for g, ref in enumerate(refs):
            t = proj(col, ATTN_OUT_WIDTH)
            for s in range(0, ATTN_OUT_WIDTH, LANES):
                rot = _rotate_half_pairs(t[:, s:s + LANES], cos, sin) * scale
                _store_residue_major(ref, perm_ref, rot, s, dilations[g])
            col += ATTN_OUT_WIDTH
    for g, ref in enumerate((v1_ref, v2_ref, v3_ref)):
        t = proj(col, ATTN_OUT_WIDTH)
        for s in range(0, ATTN_OUT_WIDTH, LANES):
            _store_residue_major(ref, perm_ref, t[:, s:s + LANES], s, dilations[g])
        col += ATTN_OUT_WIDTH
    for ref in (ma_ref, mb_ref):
        for s in range(0, D_MODEL, 512):
            z = proj(col + s, 512)
            ref[:, s:s + 512] = (1.0 / (1.0 + jnp.exp(-z))).astype(ref.dtype)
        col += D_MODEL

    g_lr = jnp.dot(h, wlr_ref[...], preferred_element_type=F32)
    g_hi, g_lo = _split_hi_lo(g_lr)
    w2_hi, w2_lo = _split_hi_lo(w2_ref[...])
    z = (jnp.dot(g_hi, w2_hi, preferred_element_type=F32) + jnp.dot(g_lo, w2_hi, preferred_element_type=F32)
         + jnp.dot(g_hi, w2_lo, preferred_element_type=F32)) + b2_ref[...]
    log_sig = jnp.minimum(z, 0.0) - jnp.log(1.0 + jnp.exp(-jnp.abs(z)))
    la_ref[...] = log_sig * (1.0 / GLA_TAU)


def _in_proj(x2, n1w, scale1, shift1, cos, sin, w_main, w_lr, w2p, b2):
    T, D = x2.shape
    B = scale1.shape[0]
    tm = TM_IN
    per_b = T // B // tm
    row = lambda i: (i, 0)
    const = lambda i: (0, 0)
    batch = lambda i: (i // per_b, 0, 0)
    widths = ((GLA_QK_WIDTH, GLA_QK_WIDTH, GLA_V_WIDTH, GLA_V_WIDTH, GLA_QK_WIDTH)
              + (ATTN_OUT_WIDTH,) * 9 + (D_MODEL, D_MODEL))
    dtypes = (BF16, BF16, BF16, BF16, F32) + (BF16,) * 11
    single = dict(pipeline_mode=pl.Buffered(1))
    return pl.pallas_call(
        _in_proj_kernel,
        grid=(T // tm,),
        in_specs=[pl.BlockSpec((tm, D), row),
                  pl.BlockSpec((1, D), const),
                  pl.BlockSpec((1, 1, D), batch),
                  pl.BlockSpec((1, 1, D), batch),
                  pl.BlockSpec((tm, LANES), row),
                  pl.BlockSpec((tm, LANES), row),
                  pl.BlockSpec(w_main.shape, const, **single),
                  pl.BlockSpec(w_lr.shape, const, **single),
                  pl.BlockSpec(w2p.shape, const, **single),
                  pl.BlockSpec(b2.shape, const, **single)],
        out_specs=[pl.BlockSpec((tm, w), row) for w in widths],
        out_shape=[jax.ShapeDtypeStruct((T, w), dt) for w, dt in zip(widths, dtypes)],
        scratch_shapes=[pltpu.VMEM((tm, LANES), F32)],
        compiler_params=_params("parallel"),
        name="in_proj",
    )(x2, n1w, scale1, shift1, cos, sin, w_main, w_lr, w2p, b2)


def _gla_kernel(q_ref, k_ref, v_ref, la_ref, gr_ref, nw_ref, o_ref, state_ref):
    C = GLA_CHUNK
    heads = range(GLA_HEADS)

    @pl.when(pl.program_id(1) == 0)
    def _():
        state_ref[...] = jnp.zeros_like(state_ref)

    rows = lax.broadcasted_iota(jnp.int32, (C, C), 0)
    cols = lax.broadcasted_iota(jnp.int32, (C, C), 1)
    causal = cols <= rows
    tri = causal.astype(BF16)
    q_scale = GLA_DK ** -0.5
    nw = jnp.concatenate([nw_ref[...]] * GLA_HEADS, axis=1)
    nt = (((1,), (1,)), ((), ()))
    ks = lambda a, h: a[:, h * GLA_DK:(h + 1) * GLA_DK]
    vs = lambda a, h: a[:, h * GLA_DV:(h + 1) * GLA_DV]

    for c in range(GLA_BLOCK // C):
        sl = pl.ds(c * C, C)
        q = q_ref[sl, :].astype(F32) * q_scale
        k = k_ref[sl, :].astype(F32)
        v = v_ref[sl, :]
        la_hi, la_lo = _split_hi_lo(la_ref[sl, :])
        cum = jnp.dot(tri, la_hi, preferred_element_type=F32) + jnp.dot(tri, la_lo, preferred_element_type=F32)
        mid = cum[C // 2 - 1:C // 2, :]
        last = cum[C - 1:C, :]
        qg = (q * jnp.exp(cum - mid)).astype(BF16)
        kg = (k * jnp.exp(mid - cum)).astype(BF16)
        q_in = (q * jnp.exp(cum)).astype(BF16)
        k_out = k * jnp.exp(last - cum)
        decay_row = jnp.broadcast_to(jnp.exp(last), (8, GLA_QK_WIDTH))
        g = gr_ref[sl, :].astype(F32)
        gate = nw * (g * (1.0 / (1.0 + jnp.exp(-g))))

        attn = [lax.dot_general(ks(qg, h), ks(kg, h), nt, preferred_element_type=F32) for h in heads]
        attn = [jnp.where(causal, a, 0.0).astype(BF16) for a in attn]
        state = [state_ref[h] for h in heads]
        o = [jnp.dot(attn[h], vs(v, h), preferred_element_type=F32)
             + jnp.dot(ks(q_in, h), state[h].astype(BF16), preferred_element_type=F32) for h in heads]
        k_out_t = [jnp.transpose(ks(k_out, h)).astype(BF16) for h in heads]
        decay = [jnp.transpose(ks(decay_row, h))[:, 0:1] for h in heads]
        for h in heads:
            state_ref[h] = decay[h] * state[h] + jnp.dot(k_out_t[h], vs(v, h), preferred_element_type=F32)
        ms = [jnp.mean(o[h] * o[h], axis=-1, keepdims=True) for h in heads]
        for h in heads:
            o_ref[sl, h * GLA_DV:(h + 1) * GLA_DV] = (o[h] * lax.rsqrt(ms[h] + EPS) * vs(gate, h)).astype(o_ref.dtype)


def _gla(gq, gk, gv, la, gr, gla_norm_w, B):
    T = gq.shape[0]
    nb = T // B // GLA_BLOCK
    row = lambda b, n: (b * nb + n, 0)
    return pl.pallas_call(
        _gla_kernel,
        grid=(B, nb),
        in_specs=[pl.BlockSpec((GLA_BLOCK, GLA_QK_WIDTH), row),
                  pl.BlockSpec((GLA_BLOCK, GLA_QK_WIDTH), row),
                  pl.BlockSpec((GLA_BLOCK, GLA_V_WIDTH), row),
                  pl.BlockSpec((GLA_BLOCK, GLA_QK_WIDTH), row),
                  pl.BlockSpec((GLA_BLOCK, GLA_V_WIDTH), row),
                  pl.BlockSpec((1, GLA_DV), lambda b, n: (0, 0))],
        out_specs=pl.BlockSpec((GLA_BLOCK, GLA_V_WIDTH), row),
        out_shape=jax.ShapeDtypeStruct((T, GLA_V_WIDTH), BF16),
        scratch_shapes=[pltpu.VMEM((GLA_HEADS, GLA_DK, GLA_DV), F32)],
        compiler_params=_params("parallel", "arbitrary"),
        name="gla",
    )(gq, gk, gv, la, gr, gla_norm_w)


def _attn_kernel(q1_ref, q2_ref, q3_ref, k1c_ref, k1p_ref, k2c_ref, k2p_ref, k3c_ref, k3p_ref,
                 v1c_ref, v1p_ref, v2c_ref, v2p_ref, v3c_ref, v3p_ref, o_ref,
                 k1cat, v1cat, k2cat, v2cat, *slabs):
    blk = ATTN_BLK
    tile = TM_IN
    span = ATTN_SPAN
    tiles = span // tile
    r2, r3 = ATTN_GROUPS[1][1], ATTN_GROUPS[2][1]
    per3 = tile // r3
    not_first_span = pl.program_id(1) > 0
    npair = ATTN_OUT_WIDTH // LANES
    o1_s, o2_s, o3_s, l1_s, l2_s, l3_s, out_s = [slabs[n * npair:(n + 1) * npair] for n in range(7)]

    for cat, prev, cur in ((k1cat, k1p_ref, k1c_ref), (v1cat, v1p_ref, v1c_ref),
                           (k2cat, k2p_ref, k2c_ref), (v2cat, v2p_ref, v2c_ref)):
        halo = prev.shape[0]
        cat[0:halo] = prev[...]
        cat[halo:] = cur[...]

    rows = lax.broadcasted_iota(jnp.int32, (blk, 2 * blk), 0)
    cols = lax.broadcasted_iota(jnp.int32, (blk, 2 * blk), 1)
    band = (cols >= rows) & (cols <= rows + blk)
    in_cur = cols >= blk
    left = lax.broadcasted_iota(jnp.int32, (blk, LANES), 1) < ATTN_HEAD_DIM
    ones = jnp.ones((2 * blk, LANES), BF16)
    nt = (((1,), (1,)), ((), ()))

    def pair_attend(q_pair, k_cat, v_cat, has_prev):
        zero = jnp.zeros_like(q_pair)
        q2 = jnp.concatenate([jnp.where(left, q_pair, zero), jnp.where(left, zero, q_pair)], axis=0)
        s = lax.dot_general(q2, k_cat, nt, preferred_element_type=F32)
        valid = band & (in_cur | has_prev)
        s = jnp.where(jnp.concatenate([valid, valid], axis=0), s, MASK_VALUE)
        m = jnp.max(s, axis=-1, keepdims=True)
        p = jnp.exp(s - m).astype(BF16)
        r = jnp.dot(p, jnp.concatenate([v_cat, ones], axis=1), preferred_element_type=F32)
        acc = jnp.where(left, r[:blk, :LANES], r[blk:, :LANES])
        den = jnp.where(left, r[:blk, LANES:], r[blk:, LANES:])
        m_pair = jnp.where(left, jnp.broadcast_to(m[:blk], (blk, LANES)), jnp.broadcast_to(m[blk:], (blk, LANES)))
        return acc / den, m_pair + jnp.log(den)

    def tile_body(t, carry):
        for p in range(tiles):
            i = t * tiles + p
            r0 = pl.multiple_of(i * blk, blk)
            for j in range(ATTN_OUT_WIDTH // LANES):
                cs = slice(j * LANES, (j + 1) * LANES)
                o, lse = pair_attend(q1_ref[pl.ds(r0, blk), cs], k1cat[pl.ds(r0, 2 * blk), cs],
                                     v1cat[pl.ds(r0, 2 * blk), cs], not_first_span | (i > 0))
                o1_s[j][pl.ds(r0, blk), :] = o
                l1_s[j][pl.ds(r0, blk), :] = lse
                r1 = pl.multiple_of(r0 + tile, blk)
                k_cat = jnp.concatenate([k2cat[pl.ds(r0, blk), cs], k2cat[pl.ds(r1, blk), cs]], axis=0)
                v_cat = jnp.concatenate([v2cat[pl.ds(r0, blk), cs], v2cat[pl.ds(r1, blk), cs]], axis=0)
                o, lse = pair_attend(q2_ref[pl.ds(r0, blk), cs], k_cat, v_cat, not_first_span | (t > 0))
                tok = pl.ds(pl.multiple_of(t * tile, tile) + p, blk, stride=r2)
                o2_s[j][tok, :] = o
                l2_s[j][tok, :] = lse
                rr = pl.multiple_of(i * per3, per3)
                gather = lambda ref: jnp.concatenate(
                    [ref[pl.ds(rr + u * tile, per3), cs] for u in range(tiles)], axis=0)
                k_cat = jnp.concatenate([gather(k3p_ref), gather(k3c_ref)], axis=0)
                v_cat = jnp.concatenate([gather(v3p_ref), gather(v3c_ref)], axis=0)
                o, lse = pair_attend(gather(q3_ref), k_cat, v_cat, not_first_span)
                o3_s[j][pl.ds(r0, blk), :] = o
                l3_s[j][pl.ds(r0, blk), :] = lse
        return carry

    lax.fori_loop(0, tiles, tile_body, 0)

    for j in range(npair):
        for p in range(r3):
            tok = pl.ds(p, blk, stride=r3)
            res = slice(p * blk, (p + 1) * blk)
            l1, l2, l3 = l1_s[j][tok, :], l2_s[j][tok, :], l3_s[j][res, :]
            m = jnp.maximum(jnp.maximum(l1, l2), l3)
            e1, e2, e3 = jnp.exp(l1 - m), jnp.exp(l2 - m), jnp.exp(l3 - m)
            num = e1 * o1_s[j][tok, :] + e2 * o2_s[j][tok, :] + e3 * o3_s[j][res, :]
            out_s[j][tok, :] = num / (e1 + e2 + e3)
        o_ref[:, j * LANES:(j + 1) * LANES] = out_s[j][...].astype(o_ref.dtype)


def _dilated_attention(qs, ks, vs, B, S):
    span = ATTN_SPAN
    nsp = S // span
    gw = ATTN_OUT_WIDTH
    T = B * S
    cur = lambda b, s: (b * nsp + s, 0)

    def prev(rows):
        per = span // rows
        return pl.BlockSpec((rows, gw), lambda b, s: (jnp.maximum((b * nsp + s) * per - 1, 0), 0))

    halos = (ATTN_BLK, TM_IN, span)
    full = pl.BlockSpec((span, gw), cur)
    kv_specs = []
    for h in halos:
        kv_specs += [full, prev(h)]
    kv_args = lambda arrs: [a for arr in arrs for a in (arr, arr)]
    return pl.pallas_call(
        _attn_kernel,
        grid=(B, nsp),
        in_specs=[full] * 3 + kv_specs + kv_specs,
        out_specs=full,
        out_shape=jax.ShapeDtypeStruct((T, gw), BF16),
        scratch_shapes=[pltpu.VMEM((halos[0] + span, gw), BF16), pltpu.VMEM((halos[0] + span, gw), BF16),
                        pltpu.VMEM((halos[1] + span, gw), BF16), pltpu.VMEM((halos[1] + span, gw), BF16)]
                       + [pltpu.VMEM((span, LANES), F32)] * (7 * (gw // LANES)),
        compiler_params=_params("parallel", "parallel"),
        name="dilated_attn",
    )(*qs, *kv_args(ks), *kv_args(vs))


def _merge_kernel(x_ref, og_ref, oa_ref, ma_ref, mb_ref,
                  gate_ref, scale_ref, shift_ref, n2w_ref, wg_ref, wa_ref, wo_ref, x1_ref, h2_ref):
    y_attn = jnp.dot(oa_ref[...], wa_ref[...], preferred_element_type=F32)
    y_gla = jnp.dot(og_ref[...], wg_ref[...], preferred_element_type=F32)
    mixed = (ma_ref[...].astype(F32) * y_gla + mb_ref[...].astype(F32) * y_attn).astype(BF16)
    x1 = x_ref[...] + gate_ref[0] * jnp.dot(mixed, wo_ref[...], preferred_element_type=F32)
    x1_ref[...] = x1
    h2_ref[...] = _rmsnorm_mod(x1, n2w_ref[...], scale_ref[0], shift_ref[0]).astype(h2_ref.dtype)


def _merge(x2, og, oa, ma, mb, gate1, scale2, shift2, n2w, wg, wa, wo):
    T, D = x2.shape
    B = gate1.shape[0]
    tm = TM_MERGE
    per_b = T // B // tm
    row = lambda i: (i, 0)
    const = lambda i: (0, 0)
    batch = lambda i: (i // per_b, 0, 0)
    return pl.pallas_call(
        _merge_kernel,
        grid=(T // tm,),
        in_specs=[pl.BlockSpec((tm, D), row), pl.BlockSpec((tm, GLA_V_WIDTH), row),
                  pl.BlockSpec((tm, ATTN_OUT_WIDTH), row)]
                 + [pl.BlockSpec((tm, D), row)] * 2
                 + [pl.BlockSpec((1, 1, D), batch)] * 3
                 + [pl.BlockSpec((1, D), const),
                    pl.BlockSpec(wg.shape, const), pl.BlockSpec(wa.shape, const), pl.BlockSpec(wo.shape, const)],
        out_specs=[pl.BlockSpec((tm, D), row)] * 2,
        out_shape=[jax.ShapeDtypeStruct((T, D), F32), jax.ShapeDtypeStruct((T, D), BF16)],
        compiler_params=_params("parallel"),
        name="merge",
    )(x2, og, oa, ma, mb, gate1, scale2, shift2, n2w, wg, wa, wo)


def _gelu_exact(g):
    return 0.5 * g * (1.0 + lax.erf(g * (2.0 ** -0.5)))


def _ffn_kernel(h_ref, hprev_ref, x1_ref, wup_ref, cw_ref, cb_ref, wd_ref, gate_ref, fw_ref, o_ref,
                hcat_s, u_s, hid_s, *, tiles_per_seq):
    i = pl.program_id(0)
    tm = h_ref.shape[0]
    halo = hprev_ref.shape[0]
    tf = FFN_CHUNK
    hcat_s[0:halo] = jnp.where(i % tiles_per_seq == 0, jnp.zeros_like(hprev_ref[...]), hprev_ref[...])
    hcat_s[halo:] = h_ref[...]
    nchunk = wup_ref.shape[0]

    def up_project(j):
        u_s[j % 2] = jnp.dot(hcat_s[...], wup_ref[j], preferred_element_type=F32)

    up_project(0)
    for j in range(nchunk):
        if j + 1 < nchunk:
            up_project(j + 1)
        u = u_s.at[j % 2]
        cw = cw_ref[j]
        conv = (cb_ref[j] + cw[0:1] * u[halo - 2:halo - 2 + tm] + cw[1:2] * u[halo - 1:halo - 1 + tm]
                + cw[2:3] * u[halo:halo + tm])
        hid_s[:, j * tf:(j + 1) * tf] = (_gelu_exact(conv[:, tf:]) * conv[:, :tf]).astype(hid_s.dtype)

    x2 = x1_ref[...] + gate_ref[0] * jnp.dot(hid_s[...], wd_ref[...], preferred_element_type=F32)
    ms = jnp.mean(x2 * x2, axis=-1, keepdims=True)
    o_ref[...] = x2 * lax.rsqrt(ms + EPS) * fw_ref[...]


def _chunk_value_gate(a, tf):
    lead = a.shape[:-1]
    n = D_FF // tf
    both = jnp.concatenate([a[..., :D_FF].reshape(*lead, n, tf), a[..., D_FF:].reshape(*lead, n, tf)], axis=-1)
    return jnp.moveaxis(both, -2, 0)


def _ffn(h2, x1, w_up, conv_w, conv_b, w_down, gate2, final_w, B):
    T, D = x1.shape
    tm, tf, halo = TM_FFN, FFN_CHUNK, FFN_HALO
    per_b = T // B // tm
    row = lambda i: (i, 0)
    const2 = lambda i: (0, 0)
    const3 = lambda i: (0, 0, 0)
    single = dict(pipeline_mode=pl.Buffered(1))
    wup3 = _chunk_value_gate(w_up, tf).astype(BF16)
    cw3 = _chunk_value_gate(conv_w, tf)
    cb3 = _chunk_value_gate(conv_b, tf)
    return pl.pallas_call(
        functools.partial(_ffn_kernel, tiles_per_seq=per_b),
        grid=(T // tm,),
        in_specs=[pl.BlockSpec((tm, D), row),
                  pl.BlockSpec((halo, D), lambda i: (jnp.maximum(i * (tm // halo) - 1, 0), 0)),
                  pl.BlockSpec((tm, D), row),
                  pl.BlockSpec(wup3.shape, const3, **single),
                  pl.BlockSpec(cw3.shape, const3, **single),
                  pl.BlockSpec(cb3.shape, const3, **single),
                  pl.BlockSpec(w_down.shape, const2, **single),
                  pl.BlockSpec((1, 1, D), lambda i: (i // per_b, 0, 0)),
                  pl.BlockSpec((1, D), const2)],
        out_specs=pl.BlockSpec((tm, D), row),
        out_shape=jax.ShapeDtypeStruct((T, D), F32),
        scratch_shapes=[pltpu.VMEM((halo + tm, D), BF16),
                        pltpu.VMEM((2, halo + tm, 2 * tf), F32),
                        pltpu.VMEM((tm, D_FF), BF16)],
        compiler_params=_params("parallel"),
        name="ffn",
    )(h2, h2, x1, wup3, cw3, cb3, w_down, gate2, final_w)


def kernel(x, c, positions, ada_w, ada_b, norm1_w, w_in, gla_gate_w2, gla_gate_b, gla_norm_w, w_gla_branch,
           w_attn_branch, w_out, norm2_w, w_up, conv_w, conv_b, w_down, final_norm_w):
    B, S, D = x.shape
    T = B * S
    depth = ada_w.shape[0]
    split_at = np.cumsum(IN_WIDTHS)[:-1].tolist()
    cos, sin = _rope_tables(positions)
    x2 = x.reshape(T, D)
    for layer in range(depth):
        mod = _modulation(c, ada_w[layer], ada_b[layer])
        shift1, scale1, gate1, shift2, scale2, gate2 = jnp.split(mod[:, None, :], 6, axis=-1)

        cols = jnp.split(w_in[layer], split_at, axis=1)
        w_main = jnp.concatenate([cols[i] for i in (0, 1, 2, 3, 5, 6, 7, 8, 9)], axis=1).astype(BF16)
        w_lr = jnp.pad(cols[4], ((0, 0), (0, LANES - GLA_LOWRANK))).astype(BF16)
        w2p = jnp.pad(gla_gate_w2[layer], ((0, LANES - GLA_LOWRANK), (0, 0)))

        gq, gk, gv, gr, la, q1, q2, q3, k1, k2, k3, v1, v2, v3, ma, mb = _in_proj(
            x2, norm1_w[layer].reshape(1, D), scale1, shift1, cos, sin, w_main, w_lr, w2p,
            gla_gate_b[layer].reshape(1, -1))

        og = _gla(gq, gk, gv, la, gr, gla_norm_w[layer].reshape(1, -1), B)
        oa = _dilated_attention((q1, q2, q3), (k1, k2, k3), (v1, v2, v3), B, S)

        x1, h2 = _merge(x2, og, oa, ma, mb, gate1, scale2, shift2, norm2_w[layer].reshape(1, D),
                        w_gla_branch[layer].astype(BF16), w_attn_branch[layer].astype(BF16),
                        w_out[layer].astype(BF16))

        last = layer == depth - 1
        assert last, "final norm is fused into the last layer's ffn"
        x2 = _ffn(h2, x1, w_up[layer], conv_w[layer], conv_b[layer].reshape(1, -1),
                  w_down[layer].astype(BF16), gate2, final_norm_w.reshape(1, D), B)
    return x2.reshape(B, S, D)
```

```python
import functools

import jax
import jax.numpy as jnp
import numpy as np
from jax import lax
from jax.experimental import pallas as pl
from jax.experimental.pallas import tpu as pltpu

F32 = jnp.float32
BF16 = jnp.bfloat16

D_MODEL = 1024
GLA_HEADS = 4
GLA_DK = 128
GLA_DV = 256
GLA_LOWRANK = 16
GLA_TAU = 16.0
GLA_QK_WIDTH = GLA_HEADS * GLA_DK
GLA_V_WIDTH = GLA_HEADS * GLA_DV
ATTN_GROUPS = ((128, 1), (512, 4), (2048, 16))
ATTN_HEADS_PER_GROUP = 4
ATTN_HEAD_DIM = 64
ATTN_WIDTH = ATTN_HEADS_PER_GROUP * len(ATTN_GROUPS) * ATTN_HEAD_DIM
ATTN_OUT_WIDTH = ATTN_HEADS_PER_GROUP * ATTN_HEAD_DIM
ROPE_THETA = 10000.0
D_FF = 2816
CONV_WIDTH = 3
EPS = 1e-6
IN_WIDTHS = (GLA_QK_WIDTH, GLA_QK_WIDTH, GLA_V_WIDTH, GLA_V_WIDTH, GLA_LOWRANK,
             ATTN_WIDTH, ATTN_WIDTH, ATTN_WIDTH, D_MODEL, D_MODEL)

LANES = 128
VMEM_LIMIT_BYTES = 56 * 1024 * 1024

TM_IN = 512
GLA_BLOCK = 512
GLA_CHUNK = 64
ATTN_BLK = 128
ATTN_SPAN = 2048
TM_MERGE = 512
TM_FFN = 512
FFN_CHUNK = 256
FFN_HALO = 16
MASK_VALUE = -1e30


def _params(*sem):
    return pltpu.CompilerParams(dimension_semantics=sem, vmem_limit_bytes=VMEM_LIMIT_BYTES)


def _split_hi_lo(a):
    hi = a.astype(BF16)
    lo = (a - hi.astype(F32)).astype(BF16)
    return hi, lo


def _mod_kernel(c_ref, w_ref, b_ref, o_ref):
    c = c_ref[...]
    s = c * (1.0 / (1.0 + jnp.exp(-c)))
    s_hi, s_lo = _split_hi_lo(s)
    lhs = jnp.concatenate([s_hi, s_lo], axis=0)
    w_hi, w_lo = _split_hi_lo(w_ref[...])
    acc = jnp.dot(lhs, w_hi, preferred_element_type=F32) + jnp.dot(lhs, w_lo, preferred_element_type=F32)
    o_ref[...] = acc[0:8] + acc[8:16] + b_ref[...]


def _modulation(c, ada_w, ada_b):
    B, D = c.shape
    N = ada_w.shape[1]
    tn = 1024
    c8 = jnp.pad(c, ((0, 8 - B), (0, 0)))
    out = pl.pallas_call(
        _mod_kernel,
        grid=(N // tn,),
        in_specs=[pl.BlockSpec((8, D), lambda j: (0, 0)),
                  pl.BlockSpec((D, tn), lambda j: (0, j)),
                  pl.BlockSpec((1, tn), lambda j: (0, j))],
        out_specs=pl.BlockSpec((8, tn), lambda j: (0, j)),
        out_shape=jax.ShapeDtypeStruct((8, N), F32),
        compiler_params=_params("parallel"),
        name="modulation",
    )(c8, ada_w, ada_b.reshape(1, N))
    return out


def _mod_spec(which, D):
    return pl.BlockSpec((8, D), lambda i: (0, which))


MOD_SHIFT1, MOD_SCALE1, MOD_GATE1, MOD_SHIFT2, MOD_SCALE2, MOD_GATE2 = range(6)


def _rope_kernel(pos_ref, invf_ref, cos_ref, sin_ref):
    half = ATTN_HEAD_DIM // 2
    groups = LANES // half
    tr = pos_ref.shape[1]
    pos = jnp.concatenate([pos_ref[...].astype(F32), jnp.zeros((8 - groups, tr), F32)], axis=0)
    pos_t = jnp.transpose(pos)
    lane = lax.broadcasted_iota(jnp.int32, (tr, LANES), 1)
    group = lane // half
    pos_dense = jnp.zeros((tr, LANES), F32)
    for q in range(groups):
        pos_dense = jnp.where(group == q, pos_t[:, q:q + 1], pos_dense)
    ang = pos_dense * invf_ref[...]
    first_half = (lane % ATTN_HEAD_DIM) < half
    for table, out_ref, signed in ((jnp.cos(ang), cos_ref, False), (jnp.sin(ang), sin_ref, True)):
        for q in range(groups):
            only = jnp.where(group == q, table, 0.0)
            spread = only
            for s in range(1, groups):
                spread = spread + pltpu.roll(only, s * half, axis=1)
            out_ref[q] = jnp.where(first_half, -spread, spread) if signed else spread


def _rope_tables(positions):
    T = positions.size
    half = ATTN_HEAD_DIM // 2
    groups = LANES // half
    inv_freq = ROPE_THETA ** (-jnp.arange(half, dtype=F32) / half)
    invf = jnp.tile(inv_freq, groups).reshape(1, LANES)
    per = T // groups
    tr = 512
    cos, sin = pl.pallas_call(
        _rope_kernel,
        grid=(per // tr,),
        in_specs=[pl.BlockSpec((groups, tr), lambda i: (0, i)),
                  pl.BlockSpec((1, LANES), lambda i: (0, 0))],
        out_specs=[pl.BlockSpec((groups, tr, LANES), lambda i: (0, i, 0))] * 2,
        out_shape=[jax.ShapeDtypeStruct((groups, per, LANES), F32)] * 2,
        compiler_params=_params("parallel"),
        name="rope_tables",
    )(positions.reshape(groups, per), invf)
    return cos.reshape(T, LANES), sin.reshape(T, LANES)


def _rmsnorm_mod(x, w, scale, shift):
    ms = jnp.mean(x * x, axis=-1, keepdims=True)
    return (x * lax.rsqrt(ms + EPS) * w) * (1.0 + scale) + shift


def _rotate_half_pairs(t, cos, sin_signed):
    lane = lax.broadcasted_iota(jnp.int32, t.shape, 1)
    first_half = (lane % ATTN_HEAD_DIM) < (ATTN_HEAD_DIM // 2)
    from_right = pltpu.roll(t, LANES - ATTN_HEAD_DIM // 2, axis=1)
    from_left = pltpu.roll(t, ATTN_HEAD_DIM // 2, axis=1)
    return t * cos + jnp.where(first_half, from_right, from_left) * sin_signed


def _store_residue_major(ref, perm_ref, slab, col, dilation):
    if dilation == 1:
        ref[:, col:col + LANES] = slab.astype(ref.dtype)
        return
    perm_ref[...] = slab
    n = slab.shape[0] // dilation
    for p in range(dilation):
        ref[p * n:(p + 1) * n, col:col + LANES] = perm_ref[pl.ds(p, n, stride=dilation), :].astype(ref.dtype)


IN_ALIGNED = sum(IN_WIDTHS[:4])
IN_SHIFTED = sum(IN_WIDTHS[5:])


def _in_proj_kernel(x_ref, n1w_ref, scale_ref, shift_ref, cos_ref, sin_ref, w_ref, w2_ref, b2_ref,
                    gq_ref, gk_ref, gv_ref, gr_ref, la_ref, q1_ref, q2_ref, q3_ref, k1_ref, k2_ref, k3_ref,
                    v1_ref, v2_ref, v3_ref, ma_ref, mb_ref, perm_ref, wsh_ref, wlr_ref, *, tiles_per_seq):
    i = pl.program_id(0)

    @pl.when(i == 0)
    def _():
        D = w_ref.shape[0]
        lane = lax.broadcasted_iota(jnp.int32, (D, LANES), 1)
        nblk = IN_SHIFTED // LANES
        first = w_ref[:, IN_ALIGNED:IN_ALIGNED + LANES]
        wlr_ref[...] = jnp.where(lane < GLA_LOWRANK, first, jnp.zeros_like(first))
        rolled = pltpu.roll(first.astype(F32), LANES - GLA_LOWRANK, axis=1)
        for c in range(nblk):
            nxt = w_ref[:, IN_ALIGNED + (c + 1) * LANES:IN_ALIGNED + (c + 2) * LANES].astype(F32)
            rolled_nxt = pltpu.roll(nxt, LANES - GLA_LOWRANK, axis=1)
            wsh_ref[:, c * LANES:(c + 1) * LANES] = jnp.where(
                lane < LANES - GLA_LOWRANK, rolled, rolled_nxt).astype(wsh_ref.dtype)
            rolled = rolled_nxt

    b = i // tiles_per_seq
    h = _rmsnorm_mod(x_ref[...], n1w_ref[...], scale_ref[pl.ds(b, 1), :], shift_ref[pl.ds(b, 1), :]).astype(BF16)

    col = 0
    for ref, width in ((gq_ref, GLA_QK_WIDTH), (gk_ref, GLA_QK_WIDTH)):
        ref[...] = jnp.dot(h, w_ref[:, col:col + width], preferred_element_type=F32).astype(ref.dtype)
        col += width
    for ref in (gv_ref, gr_ref):
        for s in range(0, GLA_V_WIDTH, 512):
            ref[:, s:s + 512] = jnp.dot(h, w_ref[:, col + s:col + s + 512],
                                        preferred_element_type=F32).astype(ref.dtype)
        col += GLA_V_WIDTH

    def proj(col, width):
        return jnp.dot(h, wsh_ref[:, col:col + width], preferred_element_type=F32)

    col = 0
    cos = cos_ref[...]
    sin = sin_ref[...]
    q_scale = ATTN_HEAD_DIM ** -0.5
    dilations = [d for _, d in ATTN_GROUPS]
    for refs, scale in (((q1_ref, q2_ref, q3_ref), q_scale), ((k1_ref, k2_ref, k3_ref), 1.0)):
        for g, ref in enumerate(refs):
            t = proj(col, ATTN_OUT_WIDTH)
            for s in range(0, ATTN_OUT_WIDTH, LANES):
                rot = _rotate_half_pairs(t[:, s:s + LANES], cos, sin) * scale
                _store_residue_major(ref, perm_ref, rot, s, dilations[g])
            col += ATTN_OUT_WIDTH
    for g, ref in enumerate((v1_ref, v2_ref, v3_ref)):
        t = proj(col, ATTN_OUT_WIDTH)
        for s in range(0, ATTN_OUT_WIDTH, LANES):
            _store_residue_major(ref, perm_ref, t[:, s:s + LANES], s, dilations[g])
        col += ATTN_OUT_WIDTH
    for ref in (ma_ref, mb_ref):
        for s in range(0, D_MODEL, 512):
            z = proj(col + s, 512)
            ref[:, s:s + 512] = (1.0 / (1.0 + jnp.exp(-z))).astype(ref.dtype)
        col += D_MODEL

    g_lr = jnp.dot(h, wlr_ref[...], preferred_element_type=F32)
    g_hi, g_lo = _split_hi_lo(g_lr)
    w2_hi, w2_lo = _split_hi_lo(w2_ref[...])
    z = (jnp.dot(g_hi, w2_hi, preferred_element_type=F32) + jnp.dot(g_lo, w2_hi, preferred_element_type=F32)
         + jnp.dot(g_hi, w2_lo, preferred_element_type=F32)) + b2_ref[...]
    log_sig = jnp.minimum(z, 0.0) - jnp.log(1.0 + jnp.exp(-jnp.abs(z)))
    la_ref[...] = log_sig * (1.0 / GLA_TAU)


def _in_proj(x2, n1w, mod, cos, sin, w_in, w2, b2, B):
    T, D = x2.shape
    tm = TM_IN
    row = lambda i: (i, 0)
    const = lambda i: (0, 0)
    widths = ((GLA_QK_WIDTH, GLA_QK_WIDTH, GLA_V_WIDTH, GLA_V_WIDTH, GLA_QK_WIDTH)
              + (ATTN_OUT_WIDTH,) * 9 + (D_MODEL, D_MODEL))
    dtypes = (BF16, BF16, BF16, BF16, F32) + (BF16,) * 11
    single = dict(pipeline_mode=pl.Buffered(1))
    pad_to = IN_ALIGNED + IN_SHIFTED + LANES
    w_pad = jnp.pad(w_in.astype(BF16), ((0, 0), (0, pad_to - w_in.shape[1])))
    w2p = jnp.pad(w2, ((0, LANES - GLA_LOWRANK), (0, 0)))
    return pl.pallas_call(
        functools.partial(_in_proj_kernel, tiles_per_seq=T // B // tm),
        grid=(T // tm,),
        in_specs=[pl.BlockSpec((tm, D), row),
                  pl.BlockSpec((1, D), const),
                  _mod_spec(MOD_SCALE1, D),
                  _mod_spec(MOD_SHIFT1, D),
                  pl.BlockSpec((tm, LANES), row),
                  pl.BlockSpec((tm, LANES), row),
                  pl.BlockSpec(w_pad.shape, const, **single),
                  pl.BlockSpec(w2p.shape, const, **single),
                  pl.BlockSpec(b2.shape, const, **single)],
        out_specs=[pl.BlockSpec((tm, w), row) for w in widths],
        out_shape=[jax.ShapeDtypeStruct((T, w), dt) for w, dt in zip(widths, dtypes)],
        scratch_shapes=[pltpu.VMEM((tm, LANES), F32),
                        pltpu.VMEM((D, IN_SHIFTED), BF16),
                        pltpu.VMEM((D, LANES), BF16)],
        compiler_params=_params("arbitrary"),
        name="in_proj",
    )(x2, n1w, mod, mod, cos, sin, w_pad, w2p, b2)


def _gla_kernel(q_ref, k_ref, v_ref, la_ref, gr_ref, nw_ref, o_ref, state_ref):
    C = GLA_CHUNK
    heads = range(GLA_HEADS)

    @pl.when(pl.program_id(1) == 0)
    def _():
        state_ref[...] = jnp.zeros_like(state_ref)

    rows = lax.broadcasted_iota(jnp.int32, (C, C), 0)
    cols = lax.broadcasted_iota(jnp.int32, (C, C), 1)
    causal = cols <= rows
    tri = causal.astype(BF16)
    q_scale = GLA_DK ** -0.5
    nw = jnp.concatenate([nw_ref[...]] * GLA_HEADS, axis=1)
    nt = (((1,), (1,)), ((), ()))
    ks = lambda a, h: a[:, h * GLA_DK:(h + 1) * GLA_DK]
    vs = lambda a, h: a[:, h * GLA_DV:(h + 1) * GLA_DV]

    for c in range(GLA_BLOCK // C):
        sl = pl.ds(c * C, C)
        q = q_ref[sl, :].astype(F32) * q_scale
        k = k_ref[sl, :].astype(F32)
        v = v_ref[sl, :]
        la_hi, la_lo = _split_hi_lo(la_ref[sl, :])
        cum = jnp.dot(tri, la_hi, preferred_element_type=F32) + jnp.dot(tri, la_lo, preferred_element_type=F32)
        mid = cum[C // 2 - 1:C // 2, :]
        last = cum[C - 1:C, :]
        qg = (q * jnp.exp(cum - mid)).astype(BF16)
        kg = (k * jnp.exp(mid - cum)).astype(BF16)
        q_in = (q * jnp.exp(cum)).astype(BF16)
        k_out = k * jnp.exp(last - cum)
        decay_row = jnp.broadcast_to(jnp.exp(last), (8, GLA_QK_WIDTH))
        g = gr_ref[sl, :].astype(F32)
        gate = nw * (g * (1.0 / (1.0 + jnp.exp(-g))))

        attn = [lax.dot_general(ks(qg, h), ks(kg, h), nt, preferred_element_type=F32) for h in heads]
        attn = [jnp.where(causal, a, 0.0).astype(BF16) for a in attn]
        state = [state_ref[h] for h in heads]
        o = [jnp.dot(attn[h], vs(v, h), preferred_element_type=F32)
             + jnp.dot(ks(q_in, h), state[h].astype(BF16), preferred_element_type=F32) for h in heads]
        k_out_t = [jnp.transpose(ks(k_out, h)).astype(BF16) for h in heads]
        decay = [jnp.transpose(ks(decay_row, h))[:, 0:1] for h in heads]
        for h in heads:
            state_ref[h] = decay[h] * state[h] + jnp.dot(k_out_t[h], vs(v, h), preferred_element_type=F32)
        ms = [jnp.mean(o[h] * o[h], axis=-1, keepdims=True) for h in heads]
        for h in heads:
            o_ref[sl, h * GLA_DV:(h + 1) * GLA_DV] = (o[h] * lax.rsqrt(ms[h] + EPS) * vs(gate, h)).astype(o_ref.dtype)


def _gla(gq, gk, gv, la, gr, gla_norm_w, B):
    T = gq.shape[0]
    nb = T // B // GLA_BLOCK
    row = lambda b, n: (b * nb + n, 0)
    return pl.pallas_call(
        _gla_kernel,
        grid=(B, nb),
        in_specs=[pl.BlockSpec((GLA_BLOCK, GLA_QK_WIDTH), row),
                  pl.BlockSpec((GLA_BLOCK, GLA_QK_WIDTH), row),
                  pl.BlockSpec((GLA_BLOCK, GLA_V_WIDTH), row),
                  pl.BlockSpec((GLA_BLOCK, GLA_QK_WIDTH), row),
                  pl.BlockSpec((GLA_BLOCK, GLA_V_WIDTH), row),
                  pl.BlockSpec((1, GLA_DV), lambda b, n: (0, 0))],
        out_specs=pl.BlockSpec((GLA_BLOCK, GLA_V_WIDTH), row),
        out_shape=jax.ShapeDtypeStruct((T, GLA_V_WIDTH), BF16),
        scratch_shapes=[pltpu.VMEM((GLA_HEADS, GLA_DK, GLA_DV), F32)],
        compiler_params=_params("parallel", "arbitrary"),
        name="gla",
    )(gq, gk, gv, la, gr, gla_norm_w)


def _attn_kernel(q1_ref, q2_ref, q3_ref, k1c_ref, k1p_ref, k2c_ref, k2p_ref, k3c_ref, k3p_ref,
                 v1c_ref, v1p_ref, v2c_ref, v2p_ref, v3c_ref, v3p_ref, o_ref,
                 k1cat, v1cat, k2cat, v2cat, *slabs):
    blk = ATTN_BLK
    tile = TM_IN
    span = ATTN_SPAN
    tiles = span // tile
    r2, r3 = ATTN_GROUPS[1][1], ATTN_GROUPS[2][1]
    per3 = tile // r3
    not_first_span = pl.program_id(1) > 0
    npair = ATTN_OUT_WIDTH // LANES
    o1_s, o2_s, o3_s, l1_s, l2_s, l3_s, out_s = [slabs[n * npair:(n + 1) * npair] for n in range(7)]

    for cat, prev, cur in ((k1cat, k1p_ref, k1c_ref), (v1cat, v1p_ref, v1c_ref),
                           (k2cat, k2p_ref, k2c_ref), (v2cat, v2p_ref, v2c_ref)):
        halo = prev.shape[0]
        cat[0:halo] = prev[...]
        cat[halo:] = cur[...]

    rows = lax.broadcasted_iota(jnp.int32, (blk, 2 * blk), 0)
    cols = lax.broadcasted_iota(jnp.int32, (blk, 2 * blk), 1)
    band = (cols >= rows) & (cols <= rows + blk)
    in_cur = cols >= blk
    left = lax.broadcasted_iota(jnp.int32, (blk, LANES), 1) < ATTN_HEAD_DIM
    ones = jnp.ones((2 * blk, LANES), BF16)
    nt = (((1,), (1,)), ((), ()))

    def pair_attend(q_pair, k_cat, v_cat, has_prev):
        zero = jnp.zeros_like(q_pair)
        q2 = jnp.concatenate([jnp.where(left, q_pair, zero), jnp.where(left, zero, q_pair)], axis=0)
        s = lax.dot_general(q2, k_cat, nt, preferred_element_type=F32)
        valid = band & (in_cur | has_prev)
        s = jnp.where(jnp.concatenate([valid, valid], axis=0), s, MASK_VALUE)
        m = jnp.max(s, axis=-1, keepdims=True)
        p = jnp.exp(s - m).astype(BF16)
        r = jnp.dot(p, jnp.concatenate([v_cat, ones], axis=1), preferred_element_type=F32)
        acc = jnp.where(left, r[:blk, :LANES], r[blk:, :LANES])
        den = jnp.where(left, r[:blk, LANES:], r[blk:, LANES:])
        m_pair = jnp.where(left, jnp.broadcast_to(m[:blk], (blk, LANES)), jnp.broadcast_to(m[blk:], (blk, LANES)))
        return acc / den, m_pair + jnp.log(den)

    def tile_body(t, carry):
        for p in range(tiles):
            i = t * tiles + p
            r0 = pl.multiple_of(i * blk, blk)
            for j in range(ATTN_OUT_WIDTH // LANES):
                cs = slice(j * LANES, (j + 1) * LANES)
                o, lse = pair_attend(q1_ref[pl.ds(r0, blk), cs], k1cat[pl.ds(r0, 2 * blk), cs],
                                     v1cat[pl.ds(r0, 2 * blk), cs], not_first_span | (i > 0))
                o1_s[j][pl.ds(r0, blk), :] = o
                l1_s[j][pl.ds(r0, blk), :] = lse
                r1 = pl.multiple_of(r0 + tile, blk)
                k_cat = jnp.concatenate([k2cat[pl.ds(r0, blk), cs], k2cat[pl.ds(r1, blk), cs]], axis=0)
                v_cat = jnp.concatenate([v2cat[pl.ds(r0, blk), cs], v2cat[pl.ds(r1, blk), cs]], axis=0)
                o, lse = pair_attend(q2_ref[pl.ds(r0, blk), cs], k_cat, v_cat, not_first_span | (t > 0))
                tok = pl.ds(pl.multiple_of(t * tile, tile) + p, blk, stride=r2)
                o2_s[j][tok, :] = o
                l2_s[j][tok, :] = lse
                rr = pl.multiple_of(i * per3, per3)
                gather = lambda ref: jnp.concatenate(
                    [ref[pl.ds(rr + u * tile, per3), cs] for u in range(tiles)], axis=0)
                k_cat = jnp.concatenate([gather(k3p_ref), gather(k3c_ref)], axis=0)
                v_cat = jnp.concatenate([gather(v3p_ref), gather(v3c_ref)], axis=0)
                o, lse = pair_attend(gather(q3_ref), k_cat, v_cat, not_first_span)
                o3_s[j][pl.ds(r0, blk), :] = o
                l3_s[j][pl.ds(r0, blk), :] = lse
        return carry

    lax.fori_loop(0, tiles, tile_body, 0)

    for j in range(npair):
        for p in range(r3):
            tok = pl.ds(p, blk, stride=r3)
            res = slice(p * blk, (p + 1) * blk)
            l1, l2, l3 = l1_s[j][tok, :], l2_s[j][tok, :], l3_s[j][res, :]
            m = jnp.maximum(jnp.maximum(l1, l2), l3)
            e1, e2, e3 = jnp.exp(l1 - m), jnp.exp(l2 - m), jnp.exp(l3 - m)
            num = e1 * o1_s[j][tok, :] + e2 * o2_s[j][tok, :] + e3 * o3_s[j][res, :]
            out_s[j][tok, :] = num / (e1 + e2 + e3)
        o_ref[:, j * LANES:(j + 1) * LANES] = out_s[j][...].astype(o_ref.dtype)


def _dilated_attention(qs, ks, vs, B, S):
    span = ATTN_SPAN
    nsp = S // span
    gw = ATTN_OUT_WIDTH
    T = B * S
    cur = lambda b, s: (b * nsp + s, 0)

    def prev(rows):
        per = span // rows
        return pl.BlockSpec((rows, gw), lambda b, s: (jnp.maximum((b * nsp + s) * per - 1, 0), 0))

    halos = (ATTN_BLK, TM_IN, span)
    full = pl.BlockSpec((span, gw), cur)
    kv_specs = []
    for h in halos:
        kv_specs += [full, prev(h)]
    kv_args = lambda arrs: [a for arr in arrs for a in (arr, arr)]
    return pl.pallas_call(
        _attn_kernel,
        grid=(B, nsp),
        in_specs=[full] * 3 + kv_specs + kv_specs,
        out_specs=full,
        out_shape=jax.ShapeDtypeStruct((T, gw), BF16),
        scratch_shapes=[pltpu.VMEM((halos[0] + span, gw), BF16), pltpu.VMEM((halos[0] + span, gw), BF16),
                        pltpu.VMEM((halos[1] + span, gw), BF16), pltpu.VMEM((halos[1] + span, gw), BF16)]
                       + [pltpu.VMEM((span, LANES), F32)] * (7 * (gw // LANES)),
        compiler_params=_params("parallel", "parallel"),
        name="dilated_attn",
    )(*qs, *kv_args(ks), *kv_args(vs))


def _merge_kernel(x_ref, og_ref, oa_ref, ma_ref, mb_ref,
                  gate_ref, scale_ref, shift_ref, n2w_ref, wg_ref, wa_ref, wo_ref, x1_ref, h2_ref, *, tiles_per_seq):
    b = pl.ds(pl.program_id(0) // tiles_per_seq, 1)
    y_attn = jnp.dot(oa_ref[...], wa_ref[...], preferred_element_type=F32)
    y_gla = jnp.dot(og_ref[...], wg_ref[...], preferred_element_type=F32)
    mixed = (ma_ref[...].astype(F32) * y_gla + mb_ref[...].astype(F32) * y_attn).astype(BF16)
    x1 = x_ref[...] + gate_ref[b, :] * jnp.dot(mixed, wo_ref[...], preferred_element_type=F32)
    x1_ref[...] = x1
    h2_ref[...] = _rmsnorm_mod(x1, n2w_ref[...], scale_ref[b, :], shift_ref[b, :]).astype(h2_ref.dtype)


def _merge(x2, og, oa, ma, mb, mod, n2w, wg, wa, wo, B):
    T, D = x2.shape
    tm = TM_MERGE
    row = lambda i: (i, 0)
    const = lambda i: (0, 0)
    return pl.pallas_call(
        functools.partial(_merge_kernel, tiles_per_seq=T // B // tm),
        grid=(T // tm,),
        in_specs=[pl.BlockSpec((tm, D), row), pl.BlockSpec((tm, GLA_V_WIDTH), row),
                  pl.BlockSpec((tm, ATTN_OUT_WIDTH), row)]
                 + [pl.BlockSpec((tm, D), row)] * 2
                 + [_mod_spec(MOD_GATE1, D), _mod_spec(MOD_SCALE2, D), _mod_spec(MOD_SHIFT2, D)]
                 + [pl.BlockSpec((1, D), const),
                    pl.BlockSpec(wg.shape, const), pl.BlockSpec(wa.shape, const), pl.BlockSpec(wo.shape, const)],
        out_specs=[pl.BlockSpec((tm, D), row)] * 2,
        out_shape=[jax.ShapeDtypeStruct((T, D), F32), jax.ShapeDtypeStruct((T, D), BF16)],
        compiler_params=_params("parallel"),
        name="merge",
    )(x2, og, oa, ma, mb, mod, mod, mod, n2w, wg, wa, wo)


def _gelu_exact(g):
    return 0.5 * g * (1.0 + lax.erf(g * (2.0 ** -0.5)))


def _ffn_kernel(h_ref, hprev_ref, x1_ref, wup_ref, cw_ref, cb_ref, wd_ref, gate_ref, fw_ref, o_ref,
                hcat_s, u_s, hid_s, *, tiles_per_seq):
    i = pl.program_id(0)
    tm = h_ref.shape[0]
    halo = hprev_ref.shape[0]
    tf = FFN_CHUNK
    hcat_s[0:halo] = jnp.where(i % tiles_per_seq == 0, jnp.zeros_like(hprev_ref[...]), hprev_ref[...])
    hcat_s[halo:] = h_ref[...]
    nchunk = D_FF // tf
    branches = (0, D_FF)

    def up_project(j):
        for n, off in enumerate(branches):
            u_s[j % 2, n] = jnp.dot(hcat_s[...], wup_ref[:, off + j * tf:off + (j + 1) * tf],
                                    preferred_element_type=F32)

    def conv(j, n):
        u = u_s.at[j % 2, n]
        cs = slice(branches[n] + j * tf, branches[n] + (j + 1) * tf)
        return (cb_ref[:, cs] + cw_ref[0:1, cs] * u[halo - 2:halo - 2 + tm]
                + cw_ref[1:2, cs] * u[halo - 1:halo - 1 + tm] + cw_ref[2:3, cs] * u[halo:halo + tm])

    up_project(0)
    for j in range(nchunk):
        if j + 1 < nchunk:
            up_project(j + 1)
        hid_s[:, j * tf:(j + 1) * tf] = (_gelu_exact(conv(j, 1)) * conv(j, 0)).astype(hid_s.dtype)

    b = pl.ds(i // tiles_per_seq, 1)
    x2 = x1_ref[...] + gate_ref[b, :] * jnp.dot(hid_s[...], wd_ref[...], preferred_element_type=F32)
    ms = jnp.mean(x2 * x2, axis=-1, keepdims=True)
    o_ref[...] = x2 * lax.rsqrt(ms + EPS) * fw_ref[...]


def _ffn(h2, x1, w_up, conv_w, conv_b, w_down, mod, final_w, B):
    T, D = x1.shape
    tm, tf, halo = TM_FFN, FFN_CHUNK, FFN_HALO
    row = lambda i: (i, 0)
    const = lambda i: (0, 0)
    single = dict(pipeline_mode=pl.Buffered(1))
    return pl.pallas_call(
        functools.partial(_ffn_kernel, tiles_per_seq=T // B // tm),
        grid=(T // tm,),
        in_specs=[pl.BlockSpec((tm, D), row),
                  pl.BlockSpec((halo, D), lambda i: (jnp.maximum(i * (tm // halo) - 1, 0), 0)),
                  pl.BlockSpec((tm, D), row),
                  pl.BlockSpec(w_up.shape, const, **single),
                  pl.BlockSpec(conv_w.shape, const, **single),
                  pl.BlockSpec(conv_b.shape, const, **single),
                  pl.BlockSpec(w_down.shape, const, **single),
                  _mod_spec(MOD_GATE2, D),
                  pl.BlockSpec((1, D), const)],
        out_specs=pl.BlockSpec((tm, D), row),
        out_shape=jax.ShapeDtypeStruct((T, D), F32),
        scratch_shapes=[pltpu.VMEM((halo + tm, D), BF16),
                        pltpu.VMEM((2, 2, halo + tm, tf), F32),
                        pltpu.VMEM((tm, D_FF), BF16)],
        compiler_params=_params("parallel"),
        name="ffn",
    )(h2, h2, x1, w_up, conv_w, conv_b, w_down, mod, final_w)


def kernel(x, c, positions, ada_w, ada_b, norm1_w, w_in, gla_gate_w2, gla_gate_b, gla_norm_w, w_gla_branch,
           w_attn_branch, w_out, norm2_w, w_up, conv_w, conv_b, w_down, final_norm_w):
    B, S, D = x.shape
    T = B * S
    depth = ada_w.shape[0]
    assert depth == 1, "the final norm is fused into the (single) layer's ffn"
    assert all(window // dilation == ATTN_BLK for window, dilation in ATTN_GROUPS)
    cos, sin = _rope_tables(positions)
    x2 = x.reshape(T, D)
    for layer in range(depth):
        mod = _modulation(c, ada_w[layer], ada_b[layer])

        gq, gk, gv, gr, la, q1, q2, q3, k1, k2, k3, v1, v2, v3, ma, mb = _in_proj(
            x2, norm1_w[layer].reshape(1, D), mod, cos, sin, w_in[layer], gla_gate_w2[layer],
            gla_gate_b[layer].reshape(1, -1), B)

        og = _gla(gq, gk, gv, la, gr, gla_norm_w[layer].reshape(1, -1), B)
        oa = _dilated_attention((q1, q2, q3), (k1, k2, k3), (v1, v2, v3), B, S)

        x1, h2 = _merge(x2, og, oa, ma, mb, mod, norm2_w[layer].reshape(1, D),
                        w_gla_branch[layer].astype(BF16), w_attn_branch[layer].astype(BF16),
                        w_out[layer].astype(BF16), B)

        x2 = _ffn(h2, x1, w_up[layer].astype(BF16), conv_w[layer], conv_b[layer].reshape(1, -1),
                  w_down[layer].astype(BF16), mod, final_norm_w.reshape(1, D), B)
    return x2.reshape(B, S, D)
```

```python
import functools

import jax
import jax.numpy as jnp
import numpy as np
from jax import lax
from jax.experimental import pallas as pl
from jax.experimental.pallas import tpu as pltpu

F32 = jnp.float32
BF16 = jnp.bfloat16

D_MODEL = 1024
GLA_HEADS = 4
GLA_DK = 128
GLA_DV = 256
GLA_LOWRANK = 16
GLA_TAU = 16.0
GLA_QK_WIDTH = GLA_HEADS * GLA_DK
GLA_V_WIDTH = GLA_HEADS * GLA_DV
ATTN_GROUPS = ((128, 1), (512, 4), (2048, 16))
ATTN_HEADS_PER_GROUP = 4
ATTN_HEAD_DIM = 64
ATTN_WIDTH = ATTN_HEADS_PER_GROUP * len(ATTN_GROUPS) * ATTN_HEAD_DIM
ATTN_OUT_WIDTH = ATTN_HEADS_PER_GROUP * ATTN_HEAD_DIM
ROPE_THETA = 10000.0
D_FF = 2816
CONV_WIDTH = 3
EPS = 1e-6
IN_WIDTHS = (GLA_QK_WIDTH, GLA_QK_WIDTH, GLA_V_WIDTH, GLA_V_WIDTH, GLA_LOWRANK,
             ATTN_WIDTH, ATTN_WIDTH, ATTN_WIDTH, D_MODEL, D_MODEL)

LANES = 128
VMEM_LIMIT_BYTES = 56 * 1024 * 1024

TM_IN = 512
GLA_BLOCK = 512
GLA_CHUNK = 64
ATTN_BLK = 128
ATTN_SPAN = 2048
TM_MERGE = 1024
MERGE_SPLIT = 2
TM_FFN = 512
FFN_CHUNK = 256
FFN_HALO = 16
FFN_DOWN_SLAB = 256
MASK_VALUE = -1e30


def _params(*sem):
    return pltpu.CompilerParams(dimension_semantics=sem, vmem_limit_bytes=VMEM_LIMIT_BYTES)


def _split_hi_lo(a):
    hi = a.astype(BF16)
    lo = (a - hi.astype(F32)).astype(BF16)
    return hi, lo


def _mod_kernel(c_ref, w_ref, b_ref, o_ref):
    c = c_ref[...]
    s = c * (1.0 / (1.0 + jnp.exp(-c)))
    s_hi, s_lo = _split_hi_lo(s)
    lhs = jnp.concatenate([s_hi, s_lo], axis=0)
    w_hi, w_lo = _split_hi_lo(w_ref[...])
    acc = jnp.dot(lhs, w_hi, preferred_element_type=F32) + jnp.dot(lhs, w_lo, preferred_element_type=F32)
    o_ref[...] = acc[0:8] + acc[8:16] + b_ref[...]


def _modulation(c, ada_w, ada_b):
    B, D = c.shape
    N = ada_w.shape[1]
    tn = 1024
    c8 = jnp.pad(c, ((0, 8 - B), (0, 0)))
    out = pl.pallas_call(
        _mod_kernel,
        grid=(N // tn,),
        in_specs=[pl.BlockSpec((8, D), lambda j: (0, 0)),
                  pl.BlockSpec((D, tn), lambda j: (0, j)),
                  pl.BlockSpec((1, tn), lambda j: (0, j))],
        out_specs=pl.BlockSpec((8, tn), lambda j: (0, j)),
        out_shape=jax.ShapeDtypeStruct((8, N), F32),
        compiler_params=_params("parallel"),
        name="modulation",
    )(c8, ada_w, ada_b.reshape(1, N))
    return out


def _mod_spec(which, D):
    return pl.BlockSpec((8, D), lambda i: (0, which))


MOD_SHIFT1, MOD_SCALE1, MOD_GATE1, MOD_SHIFT2, MOD_SCALE2, MOD_GATE2 = range(6)


def _rope_kernel(pos_ref, invf_ref, cos_ref, sin_ref):
    half = ATTN_HEAD_DIM // 2
    groups = LANES // half
    tr = pos_ref.shape[1]
    pos = jnp.concatenate([pos_ref[...].astype(F32), jnp.zeros((8 - groups, tr), F32)], axis=0)
    pos_t = jnp.transpose(pos)
    lane = lax.broadcasted_iota(jnp.int32, (tr, LANES), 1)
    group = lane // half
    pos_dense = jnp.zeros((tr, LANES), F32)
    for q in range(groups):
        pos_dense = jnp.where(group == q, pos_t[:, q:q + 1], pos_dense)
    ang = pos_dense * invf_ref[...]
    first_half = (lane % ATTN_HEAD_DIM) < half
    for table, out_ref, signed in ((jnp.cos(ang), cos_ref, False), (jnp.sin(ang), sin_ref, True)):
        for q in range(groups):
            only = jnp.where(group == q, table, 0.0)
            spread = only
            for s in range(1, groups):
                spread = spread + pltpu.roll(only, s * half, axis=1)
            out_ref[q] = jnp.where(first_half, -spread, spread) if signed else spread


def _rope_tables(positions):
    T = positions.size
    half = ATTN_HEAD_DIM // 2
    groups = LANES // half
    inv_freq = ROPE_THETA ** (-jnp.arange(half, dtype=F32) / half)
    invf = jnp.tile(inv_freq, groups).reshape(1, LANES)
    per = T // groups
    tr = 512
    cos, sin = pl.pallas_call(
        _rope_kernel,
        grid=(per // tr,),
        in_specs=[pl.BlockSpec((groups, tr), lambda i: (0, i)),
                  pl.BlockSpec((1, LANES), lambda i: (0, 0))],
        out_specs=[pl.BlockSpec((groups, tr, LANES), lambda i: (0, i, 0))] * 2,
        out_shape=[jax.ShapeDtypeStruct((groups, per, LANES), F32)] * 2,
        compiler_params=_params("parallel"),
        name="rope_tables",
    )(positions.reshape(groups, per), invf)
    return cos.reshape(T, LANES), sin.reshape(T, LANES)


def _rmsnorm_mod(x, w, scale, shift):
    ms = jnp.mean(x * x, axis=-1, keepdims=True)
    return (x * lax.rsqrt(ms + EPS) * w) * (1.0 + scale) + shift


def _rotate_half_pairs(t, cos, sin_signed):
    lane = lax.broadcasted_iota(jnp.int32, t.shape, 1)
    first_half = (lane % ATTN_HEAD_DIM) < (ATTN_HEAD_DIM // 2)
    from_right = pltpu.roll(t, LANES - ATTN_HEAD_DIM // 2, axis=1)
    from_left = pltpu.roll(t, ATTN_HEAD_DIM // 2, axis=1)
    return t * cos + jnp.where(first_half, from_right, from_left) * sin_signed


def _store_residue_major(ref, perm_ref, slab, col, dilation):
    if dilation == 1:
        ref[:, col:col + LANES] = slab.astype(ref.dtype)
        return
    perm_ref[...] = slab
    n = slab.shape[0] // dilation
    for p in range(dilation):
        ref[p * n:(p + 1) * n, col:col + LANES] = perm_ref[pl.ds(p, n, stride=dilation), :].astype(ref.dtype)


IN_ALIGNED = sum(IN_WIDTHS[:4])
IN_SHIFTED = sum(IN_WIDTHS[5:])


def _in_proj_kernel(x_ref, n1w_ref, scale_ref, shift_ref, cos_ref, sin_ref, w_ref, w2_ref, b2_ref,
                    gq_ref, gk_ref, gv_ref, gr_ref, la_ref, q1_ref, q2_ref, q3_ref, k1_ref, k2_ref, k3_ref,
                    v1_ref, v2_ref, v3_ref, ma_ref, mb_ref, perm_ref, wsh_ref, wlr_ref, *, tiles_per_seq):
    i = pl.program_id(0)

    @pl.when(i == 0)
    def _():
        D = w_ref.shape[0]
        lane = lax.broadcasted_iota(jnp.int32, (D, LANES), 1)
        nblk = IN_SHIFTED // LANES
        first = w_ref[:, IN_ALIGNED:IN_ALIGNED + LANES]
        wlr_ref[...] = jnp.where(lane < GLA_LOWRANK, first, jnp.zeros_like(first))
        rolled = pltpu.roll(first.astype(F32), LANES - GLA_LOWRANK, axis=1)
        for c in range(nblk):
            nxt = w_ref[:, IN_ALIGNED + (c + 1) * LANES:IN_ALIGNED + (c + 2) * LANES].astype(F32)
            rolled_nxt = pltpu.roll(nxt, LANES - GLA_LOWRANK, axis=1)
            wsh_ref[:, c * LANES:(c + 1) * LANES] = jnp.where(
                lane < LANES - GLA_LOWRANK, rolled, rolled_nxt).astype(wsh_ref.dtype)
            rolled = rolled_nxt

    b = i // tiles_per_seq
    h = _rmsnorm_mod(x_ref[...], n1w_ref[...], scale_ref[pl.ds(b, 1), :], shift_ref[pl.ds(b, 1), :]).astype(BF16)

    col = 0
    for ref, width in ((gq_ref, GLA_QK_WIDTH), (gk_ref, GLA_QK_WIDTH)):
        ref[...] = jnp.dot(h, w_ref[:, col:col + width], preferred_element_type=F32).astype(ref.dtype)
        col += width
    for ref in (gv_ref, gr_ref):
        for s in range(0, GLA_V_WIDTH, 512):
            ref[:, s:s + 512] = jnp.dot(h, w_ref[:, col + s:col + s + 512],
                                        preferred_element_type=F32).astype(ref.dtype)
        col += GLA_V_WIDTH

    def proj(col, width):
        return jnp.dot(h, wsh_ref[:, col:col + width], preferred_element_type=F32)

    col = 0
    cos = cos_ref[...]
    sin = sin_ref[...]
    q_scale = ATTN_HEAD_DIM ** -0.5
    dilations = [d for _, d in ATTN_GROUPS]
    for refs, scale in (((q1_ref, q2_ref, q3_ref), q_scale), ((k1_ref, k2_ref, k3_ref), 1.0)):
        for g, ref in enumerate(refs):
            t = proj(col, ATTN_OUT_WIDTH)
            for s in range(0, ATTN_OUT_WIDTH, LANES):
                rot = _rotate_half_pairs(t[:, s:s + LANES], cos, sin) * scale
                _store_residue_major(ref, perm_ref, rot, s, dilations[g])
            col += ATTN_OUT_WIDTH
    for g, ref in enumerate((v1_ref, v2_ref, v3_ref)):
        t = proj(col, ATTN_OUT_WIDTH)
        for s in range(0, ATTN_OUT_WIDTH, LANES):
            _store_residue_major(ref, perm_ref, t[:, s:s + LANES], s, dilations[g])
        col += ATTN_OUT_WIDTH
    for ref in (ma_ref, mb_ref):
        for s in range(0, D_MODEL, 512):
            z = proj(col + s, 512)
            ref[:, s:s + 512] = (1.0 / (1.0 + jnp.exp(-z))).astype(ref.dtype)
        col += D_MODEL

    g_lr = jnp.dot(h, wlr_ref[...], preferred_element_type=F32)
    g_hi, g_lo = _split_hi_lo(g_lr)
    w2_hi, w2_lo = _split_hi_lo(w2_ref[...])
    z = (jnp.dot(g_hi, w2_hi, preferred_element_type=F32) + jnp.dot(g_lo, w2_hi, preferred_element_type=F32)
         + jnp.dot(g_hi, w2_lo, preferred_element_type=F32)) + b2_ref[...]
    log_sig = jnp.minimum(z, 0.0) - jnp.log(1.0 + jnp.exp(-jnp.abs(z)))
    la_ref[...] = log_sig * (1.0 / GLA_TAU)


def _in_proj(x2, n1w, mod, cos, sin, w_in, w2, b2, B):
    T, D = x2.shape
    tm = TM_IN
    row = lambda i: (i, 0)
    const = lambda i: (0, 0)
    widths = ((GLA_QK_WIDTH, GLA_QK_WIDTH, GLA_V_WIDTH, GLA_V_WIDTH, GLA_QK_WIDTH)
              + (ATTN_OUT_WIDTH,) * 9 + (D_MODEL, D_MODEL))
    dtypes = (BF16, BF16, BF16, BF16, F32) + (BF16,) * 11
    single = dict(pipeline_mode=pl.Buffered(1))
    pad_to = IN_ALIGNED + IN_SHIFTED + LANES
    w_pad = w_in.astype(BF16)
    w2p = jnp.pad(w2, ((0, LANES - GLA_LOWRANK), (0, 0)))
    return pl.pallas_call(
        functools.partial(_in_proj_kernel, tiles_per_seq=T // B // tm),
        grid=(T // tm,),
        in_specs=[pl.BlockSpec((tm, D), row),
                  pl.BlockSpec((1, D), const),
                  _mod_spec(MOD_SCALE1, D),
                  _mod_spec(MOD_SHIFT1, D),
                  pl.BlockSpec((tm, LANES), row),
                  pl.BlockSpec((tm, LANES), row),
                  pl.BlockSpec((D, pad_to), const, **single),
                  pl.BlockSpec(w2p.shape, const, **single),
                  pl.BlockSpec(b2.shape, const, **single)],
        out_specs=[pl.BlockSpec((tm, w), row) for w in widths],
        out_shape=[jax.ShapeDtypeStruct((T, w), dt) for w, dt in zip(widths, dtypes)],
        scratch_shapes=[pltpu.VMEM((tm, LANES), F32),
                        pltpu.VMEM((D, IN_SHIFTED), BF16),
                        pltpu.VMEM((D, LANES), BF16)],
        compiler_params=_params("arbitrary"),
        name="in_proj",
    )(x2, n1w, mod, mod, cos, sin, w_pad, w2p, b2)


def _gla_kernel(q_ref, k_ref, v_ref, la_ref, gr_ref, nw_ref, o_ref, state_ref):
    C = GLA_CHUNK
    heads = range(GLA_HEADS)

    @pl.when(pl.program_id(1) == 0)
    def _():
        state_ref[...] = jnp.zeros_like(state_ref)

    rows = lax.broadcasted_iota(jnp.int32, (C, C), 0)
    cols = lax.broadcasted_iota(jnp.int32, (C, C), 1)
    causal = cols <= rows
    tri = causal.astype(BF16)
    q_scale = GLA_DK ** -0.5
    nw = jnp.concatenate([nw_ref[...]] * GLA_HEADS, axis=1)
    nt = (((1,), (1,)), ((), ()))
    ks = lambda a, h: a[:, h * GLA_DK:(h + 1) * GLA_DK]
    vs = lambda a, h: a[:, h * GLA_DV:(h + 1) * GLA_DV]

    for c in range(GLA_BLOCK // C):
        sl = pl.ds(c * C, C)
        q = q_ref[sl, :].astype(F32) * q_scale
        k = k_ref[sl, :].astype(F32)
        v = v_ref[sl, :]
        la_hi, la_lo = _split_hi_lo(la_ref[sl, :])
        cum = jnp.dot(tri, la_hi, preferred_element_type=F32) + jnp.dot(tri, la_lo, preferred_element_type=F32)
        mid = cum[C // 2 - 1:C // 2, :]
        last = cum[C - 1:C, :]
        qg = (q * jnp.exp(cum - mid)).astype(BF16)
        kg = (k * jnp.exp(mid - cum)).astype(BF16)
        q_in = (q * jnp.exp(cum)).astype(BF16)
        k_out = k * jnp.exp(last - cum)
        decay_row = jnp.broadcast_to(jnp.exp(last), (8, GLA_QK_WIDTH))
        g = gr_ref[sl, :].astype(F32)
        gate = nw * (g * (1.0 / (1.0 + jnp.exp(-g))))

        attn = [lax.dot_general(ks(qg, h), ks(kg, h), nt, preferred_element_type=F32) for h in heads]
        attn = [jnp.where(causal, a, 0.0).astype(BF16) for a in attn]
        state = [state_ref[h] for h in heads]
        o = [jnp.dot(attn[h], vs(v, h), preferred_element_type=F32)
             + jnp.dot(ks(q_in, h), state[h].astype(BF16), preferred_element_type=F32) for h in heads]
        k_out_t = [jnp.transpose(ks(k_out, h)).astype(BF16) for h in heads]
        decay = [jnp.transpose(ks(decay_row, h))[:, 0:1] for h in heads]
        for h in heads:
            state_ref[h] = decay[h] * state[h] + jnp.dot(k_out_t[h], vs(v, h), preferred_element_type=F32)
        ms = [jnp.mean(o[h] * o[h], axis=-1, keepdims=True) for h in heads]
        for h in heads:
            o_ref[sl, h * GLA_DV:(h + 1) * GLA_DV] = (o[h] * lax.rsqrt(ms[h] + EPS) * vs(gate, h)).astype(o_ref.dtype)


def _gla(gq, gk, gv, la, gr, gla_norm_w, B):
    T = gq.shape[0]
    nb = T // B // GLA_BLOCK
    row = lambda b, n: (b * nb + n, 0)
    return pl.pallas_call(
        _gla_kernel,
        grid=(B, nb),
        in_specs=[pl.BlockSpec((GLA_BLOCK, GLA_QK_WIDTH), row),
                  pl.BlockSpec((GLA_BLOCK, GLA_QK_WIDTH), row),
                  pl.BlockSpec((GLA_BLOCK, GLA_V_WIDTH), row),
                  pl.BlockSpec((GLA_BLOCK, GLA_QK_WIDTH), row),
                  pl.BlockSpec((GLA_BLOCK, GLA_V_WIDTH), row),
                  pl.BlockSpec((1, GLA_DV), lambda b, n: (0, 0))],
        out_specs=pl.BlockSpec((GLA_BLOCK, GLA_V_WIDTH), row),
        out_shape=jax.ShapeDtypeStruct((T, GLA_V_WIDTH), BF16),
        scratch_shapes=[pltpu.VMEM((GLA_HEADS, GLA_DK, GLA_DV), F32)],
        compiler_params=_params("parallel", "arbitrary"),
        name="gla",
    )(gq, gk, gv, la, gr, gla_norm_w)


def _attn_kernel(q1_ref, q2_ref, q3_ref, k1c_ref, k1p_ref, k2c_ref, k2p_ref, k3c_ref, k3p_ref,
                 v1c_ref, v1p_ref, v2c_ref, v2p_ref, v3c_ref, v3p_ref, o_ref,
                 k1cat, v1cat, k2cat, v2cat, *slabs):
    blk = ATTN_BLK
    tile = TM_IN
    span = ATTN_SPAN
    tiles = span // tile
    r2, r3 = ATTN_GROUPS[1][1], ATTN_GROUPS[2][1]
    per3 = tile // r3
    not_first_span = pl.program_id(1) > 0
    npair = ATTN_OUT_WIDTH // LANES
    o1_s, o2_s, o3_s, l1_s, l2_s, l3_s, out_s = [slabs[n * npair:(n + 1) * npair] for n in range(7)]

    for cat, prev, cur in ((k1cat, k1p_ref, k1c_ref), (v1cat, v1p_ref, v1c_ref),
                           (k2cat, k2p_ref, k2c_ref), (v2cat, v2p_ref, v2c_ref)):
        halo = prev.shape[0]
        cat[0:halo] = prev[...]
        cat[halo:] = cur[...]

    rows = lax.broadcasted_iota(jnp.int32, (blk, 2 * blk), 0)
    cols = lax.broadcasted_iota(jnp.int32, (blk, 2 * blk), 1)
    band = (cols >= rows) & (cols <= rows + blk)
    in_cur = cols >= blk
    left = lax.broadcasted_iota(jnp.int32, (blk, LANES), 1) < ATTN_HEAD_DIM
    ones = jnp.ones((2 * blk, LANES), BF16)
    nt = (((1,), (1,)), ((), ()))

    def pair_attend(q_pair, k_cat, v_cat, has_prev):
        zero = jnp.zeros_like(q_pair)
        q2 = jnp.concatenate([jnp.where(left, q_pair, zero), jnp.where(left, zero, q_pair)], axis=0)
        s = lax.dot_general(q2, k_cat, nt, preferred_element_type=F32)
        valid = band & (in_cur | has_prev)
        s = jnp.where(jnp.concatenate([valid, valid], axis=0), s, MASK_VALUE)
        m = jnp.max(s, axis=-1, keepdims=True)
        p = jnp.exp(s - m).astype(BF16)
        r = jnp.dot(p, jnp.concatenate([v_cat, ones], axis=1), preferred_element_type=F32)
        acc = jnp.where(left, r[:blk, :LANES], r[blk:, :LANES])
        den = jnp.where(left, r[:blk, LANES:], r[blk:, LANES:])
        m_pair = jnp.where(left, jnp.broadcast_to(m[:blk], (blk, LANES)), jnp.broadcast_to(m[blk:], (blk, LANES)))
        return acc / den, m_pair + jnp.log(den)

    def tile_body(t, carry):
        for p in range(tiles):
            i = t * tiles + p
            r0 = pl.multiple_of(i * blk, blk)
            for j in range(ATTN_OUT_WIDTH // LANES):
                cs = slice(j * LANES, (j + 1) * LANES)
                o, lse = pair_attend(q1_ref[pl.ds(r0, blk), cs], k1cat[pl.ds(r0, 2 * blk), cs],
                                     v1cat[pl.ds(r0, 2 * blk), cs], not_first_span | (i > 0))
                o1_s[j][pl.ds(r0, blk), :] = o
                l1_s[j][pl.ds(r0, blk), :] = lse
                r1 = pl.multiple_of(r0 + tile, blk)
                k_cat = jnp.concatenate([k2cat[pl.ds(r0, blk), cs], k2cat[pl.ds(r1, blk), cs]], axis=0)
                v_cat = jnp.concatenate([v2cat[pl.ds(r0, blk), cs], v2cat[pl.ds(r1, blk), cs]], axis=0)
                o, lse = pair_attend(q2_ref[pl.ds(r0, blk), cs], k_cat, v_cat, not_first_span | (t > 0))
                tok = pl.ds(pl.multiple_of(t * tile, tile) + p, blk, stride=r2)
                o2_s[j][tok, :] = o
                l2_s[j][tok, :] = lse
                rr = pl.multiple_of(i * per3, per3)
                gather = lambda ref: jnp.concatenate(
                    [ref[pl.ds(rr + u * tile, per3), cs] for u in range(tiles)], axis=0)
                k_cat = jnp.concatenate([gather(k3p_ref), gather(k3c_ref)], axis=0)
                v_cat = jnp.concatenate([gather(v3p_ref), gather(v3c_ref)], axis=0)
                o, lse = pair_attend(gather(q3_ref), k_cat, v_cat, not_first_span)
                o3_s[j][pl.ds(r0, blk), :] = o
                l3_s[j][pl.ds(r0, blk), :] = lse
        return carry

    lax.fori_loop(0, tiles, tile_body, 0)

    for j in range(npair):
        for p in range(r3):
            tok = pl.ds(p, blk, stride=r3)
            res = slice(p * blk, (p + 1) * blk)
            l1, l2, l3 = l1_s[j][tok, :], l2_s[j][tok, :], l3_s[j][res, :]
            m = jnp.maximum(jnp.maximum(l1, l2), l3)
            e1, e2, e3 = jnp.exp(l1 - m), jnp.exp(l2 - m), jnp.exp(l3 - m)
            num = e1 * o1_s[j][tok, :] + e2 * o2_s[j][tok, :] + e3 * o3_s[j][res, :]
            out_s[j][tok, :] = num / (e1 + e2 + e3)
        o_ref[:, j * LANES:(j + 1) * LANES] = out_s[j][...].astype(o_ref.dtype)


def _dilated_attention(qs, ks, vs, B, S):
    span = ATTN_SPAN
    nsp = S // span
    gw = ATTN_OUT_WIDTH
    T = B * S
    cur = lambda b, s: (b * nsp + s, 0)

    def prev(rows):
        per = span // rows
        return pl.BlockSpec((rows, gw), lambda b, s: (jnp.maximum((b * nsp + s) * per - 1, 0), 0))

    halos = (ATTN_BLK, TM_IN, span)
    full = pl.BlockSpec((span, gw), cur)
    kv_specs = []
    for h in halos:
        kv_specs += [full, prev(h)]
    kv_args = lambda arrs: [a for arr in arrs for a in (arr, arr)]
    return pl.pallas_call(
        _attn_kernel,
        grid=(B, nsp),
        in_specs=[full] * 3 + kv_specs + kv_specs,
        out_specs=full,
        out_shape=jax.ShapeDtypeStruct((T, gw), BF16),
        scratch_shapes=[pltpu.VMEM((halos[0] + span, gw), BF16), pltpu.VMEM((halos[0] + span, gw), BF16),
                        pltpu.VMEM((halos[1] + span, gw), BF16), pltpu.VMEM((halos[1] + span, gw), BF16)]
                       + [pltpu.VMEM((span, LANES), F32)] * (7 * (gw // LANES)),
        compiler_params=_params("parallel", "parallel"),
        name="dilated_attn",
    )(*qs, *kv_args(ks), *kv_args(vs))


def _merge_kernel(x_ref, og_ref, oa_ref, ma_ref, mb_ref,
                  gate_ref, scale_ref, shift_ref, n2w_ref, wg_ref, wa_ref, wo_ref, x1_ref, h2_ref, *, tiles_per_seq):
    b = pl.ds(pl.program_id(0) // tiles_per_seq, 1)
    gate, scale, shift = gate_ref[b, :], scale_ref[b, :], shift_ref[b, :]
    sub = x_ref.shape[0] // MERGE_SPLIT
    for r in range(MERGE_SPLIT):
        rs = slice(r * sub, (r + 1) * sub)
        y_attn = jnp.dot(oa_ref[rs, :], wa_ref[...], preferred_element_type=F32)
        y_gla = jnp.dot(og_ref[rs, :], wg_ref[...], preferred_element_type=F32)
        mixed = (ma_ref[rs, :].astype(F32) * y_gla + mb_ref[rs, :].astype(F32) * y_attn).astype(BF16)
        x1 = x_ref[rs, :] + gate * jnp.dot(mixed, wo_ref[...], preferred_element_type=F32)
        x1_ref[rs, :] = x1
        h2_ref[rs, :] = _rmsnorm_mod(x1, n2w_ref[...], scale, shift).astype(h2_ref.dtype)


def _merge(x2, og, oa, ma, mb, mod, n2w, wg, wa, wo, B):
    T, D = x2.shape
    tm = TM_MERGE
    row = lambda i: (i, 0)
    const = lambda i: (0, 0)
    return pl.pallas_call(
        functools.partial(_merge_kernel, tiles_per_seq=T // B // tm),
        grid=(T // tm,),
        in_specs=[pl.BlockSpec((tm, D), row), pl.BlockSpec((tm, GLA_V_WIDTH), row),
                  pl.BlockSpec((tm, ATTN_OUT_WIDTH), row)]
                 + [pl.BlockSpec((tm, D), row)] * 2
                 + [_mod_spec(MOD_GATE1, D), _mod_spec(MOD_SCALE2, D), _mod_spec(MOD_SHIFT2, D)]
                 + [pl.BlockSpec((1, D), const),
                    pl.BlockSpec(wg.shape, const), pl.BlockSpec(wa.shape, const), pl.BlockSpec(wo.shape, const)],
        out_specs=[pl.BlockSpec((tm, D), row)] * 2,
        out_shape=[jax.ShapeDtypeStruct((T, D), F32), jax.ShapeDtypeStruct((T, D), BF16)],
        compiler_params=_params("parallel"),
        name="merge",
    )(x2, og, oa, ma, mb, mod, mod, mod, n2w, wg, wa, wo)


def _gelu_exact(g):
    return 0.5 * g * (1.0 + lax.erf(g * (2.0 ** -0.5)))


def _ffn_kernel(h_ref, hprev_ref, x1_ref, wup_ref, cw_ref, cb_ref, wd_ref, gate_ref, fw_ref, o_ref,
                hcat_s, u_s, hid_s, order_s, *, tiles_per_seq):
    i = pl.program_id(0)
    tm = h_ref.shape[0]
    halo = hprev_ref.shape[0]
    tf = FFN_CHUNK
    half = tm // 2
    hcat_s[0:halo] = jnp.where(i % tiles_per_seq == 0, jnp.zeros_like(hprev_ref[...]), hprev_ref[...])
    hcat_s[halo:] = h_ref[...]
    nchunk = D_FF // tf
    branches = (0, D_FF)

    def up_project(j):
        for n, off in enumerate(branches):
            u = jnp.dot(hcat_s[...], wup_ref[:, off + j * tf:off + (j + 1) * tf], preferred_element_type=F32)
            for c in range(tf // LANES):
                u_s[j % 2, n, c] = u[:, c * LANES:(c + 1) * LANES]

    def conv(j, n, c, parity):
        u = u_s.at[j % 2, n, c]
        cs = slice(branches[n] + j * tf + c * LANES, branches[n] + j * tf + (c + 1) * LANES)
        tap = lambda back: u[pl.ds(halo + parity - back, half, stride=2), :]
        return cb_ref[:, cs] + cw_ref[0:1, cs] * tap(2) + cw_ref[1:2, cs] * tap(1) + cw_ref[2:3, cs] * tap(0)

    up_project(0)
    for j in range(nchunk):
        if j + 1 < nchunk:
            up_project(j + 1)
        for c in range(tf // LANES):
            for parity in range(2):
                hidden = _gelu_exact(conv(j, 1, c, parity)) * conv(j, 0, c, parity)
                hid_s[parity * half:(parity + 1) * half, j * tf + c * LANES:j * tf + (c + 1) * LANES] = (
                    hidden.astype(hid_s.dtype))

    gate = gate_ref[pl.ds(i // tiles_per_seq, 1), :]
    d_model = o_ref.shape[1]
    sq = jnp.zeros((tm, 1), F32)
    for n in range(d_model // FFN_DOWN_SLAB):
        ns = slice(n * FFN_DOWN_SLAB, (n + 1) * FFN_DOWN_SLAB)
        down = jnp.dot(hid_s[...], wd_ref[:, ns], preferred_element_type=F32)
        for c in range(FFN_DOWN_SLAB // LANES):
            g = n * (FFN_DOWN_SLAB // LANES) + c
            cs = slice(g * LANES, (g + 1) * LANES)
            for parity in range(2):
                order_s[g, pl.ds(parity, half, stride=2), :] = (
                    down[parity * half:(parity + 1) * half, c * LANES:(c + 1) * LANES])
            x2 = x1_ref[:, cs] + gate[:, cs] * order_s[g]
            sq = sq + jnp.sum(x2 * x2, axis=-1, keepdims=True)
            o_ref[:, cs] = x2
    o_ref[...] = o_ref[...] * lax.rsqrt(sq * (1.0 / d_model) + EPS) * fw_ref[...]


def _ffn(h2, x1, w_up, conv_w, conv_b, w_down, mod, final_w, B):
    T, D = x1.shape
    tm, tf, halo = TM_FFN, FFN_CHUNK, FFN_HALO
    row = lambda i: (i, 0)
    const = lambda i: (0, 0)
    single = dict(pipeline_mode=pl.Buffered(1))
    return pl.pallas_call(
        functools.partial(_ffn_kernel, tiles_per_seq=T // B // tm),
        grid=(T // tm,),
        in_specs=[pl.BlockSpec((tm, D), row),
                  pl.BlockSpec((halo, D), lambda i: (jnp.maximum(i * (tm // halo) - 1, 0), 0)),
                  pl.BlockSpec((tm, D), row),
                  pl.BlockSpec(w_up.shape, const, **single),
                  pl.BlockSpec(conv_w.shape, const, **single),
                  pl.BlockSpec(conv_b.shape, const, **single),
                  pl.BlockSpec(w_down.shape, const, **single),
                  _mod_spec(MOD_GATE2, D),
                  pl.BlockSpec((1, D), const)],
        out_specs=pl.BlockSpec((tm, D), row),
        out_shape=jax.ShapeDtypeStruct((T, D), F32),
        scratch_shapes=[pltpu.VMEM((halo + tm, D), BF16),
                        pltpu.VMEM((2, 2, tf // LANES, halo + tm, LANES), F32),
                        pltpu.VMEM((tm, D_FF), BF16),
                        pltpu.VMEM((D // LANES, tm, LANES), F32)],
        compiler_params=_params("parallel"),
        name="ffn",
    )(h2, h2, x1, w_up, conv_w, conv_b, w_down, mod, final_w)


def kernel(x, c, positions, ada_w, ada_b, norm1_w, w_in, gla_gate_w2, gla_gate_b, gla_norm_w, w_gla_branch,
           w_attn_branch, w_out, norm2_w, w_up, conv_w, conv_b, w_down, final_norm_w):
    B, S, D = x.shape
    T = B * S
    depth = ada_w.shape[0]
    assert depth == 1, "the final norm is fused into the (single) layer's ffn"
    assert all(window // dilation == ATTN_BLK for window, dilation in ATTN_GROUPS)
    cos, sin = _rope_tables(positions)
    x2 = x.reshape(T, D)
    for layer in range(depth):
        mod = _modulation(c, ada_w[layer], ada_b[layer])

        gq, gk, gv, gr, la, q1, q2, q3, k1, k2, k3, v1, v2, v3, ma, mb = _in_proj(
            x2, norm1_w[layer].reshape(1, D), mod, cos, sin, w_in[layer], gla_gate_w2[layer],
            gla_gate_b[layer].reshape(1, -1), B)

        og = _gla(gq, gk, gv, la, gr, gla_norm_w[layer].reshape(1, -1), B)
        oa = _dilated_attention((q1, q2, q3), (k1, k2, k3), (v1, v2, v3), B, S)

        x1, h2 = _merge(x2, og, oa, ma, mb, mod, norm2_w[layer].reshape(1, D),
                        w_gla_branch[layer].astype(BF16), w_attn_branch[layer].astype(BF16),
                        w_out[layer].astype(BF16), B)

        x2 = _ffn(h2, x1, w_up[layer].astype(BF16), conv_w[layer], conv_b[layer].reshape(1, -1),
                  w_down[layer].astype(BF16), mod, final_norm_w.reshape(1, D), B)
    return x2.reshape(B, S, D)
```

```python
import functools

import jax
import jax.numpy as jnp
from jax import lax
from jax.experimental import pallas as pl
from jax.experimental.pallas import tpu as pltpu

F32 = jnp.float32
BF16 = jnp.bfloat16

D_MODEL = 1024
GLA_HEADS = 4
GLA_DK = 128
GLA_DV = 256
GLA_LOWRANK = 16
GLA_TAU = 16.0
GLA_QK_WIDTH = GLA_HEADS * GLA_DK
GLA_V_WIDTH = GLA_HEADS * GLA_DV
ATTN_GROUPS = ((128, 1), (512, 4), (2048, 16))
ATTN_HEADS_PER_GROUP = 4
ATTN_HEAD_DIM = 64
ATTN_WIDTH = ATTN_HEADS_PER_GROUP * len(ATTN_GROUPS) * ATTN_HEAD_DIM
ATTN_OUT_WIDTH = ATTN_HEADS_PER_GROUP * ATTN_HEAD_DIM
ROPE_THETA = 10000.0
D_FF = 2816
CONV_WIDTH = 3
EPS = 1e-6
IN_WIDTHS = (GLA_QK_WIDTH, GLA_QK_WIDTH, GLA_V_WIDTH, GLA_V_WIDTH, GLA_LOWRANK,
             ATTN_WIDTH, ATTN_WIDTH, ATTN_WIDTH, D_MODEL, D_MODEL)

LANES = 128
VMEM_LIMIT_BYTES = 56 * 1024 * 1024

TM_IN = 512
GLA_CHUNK = 64
ATTN_BLK = 128
ATTN_SPAN = 2048
TM_MERGE = 1024
MERGE_SPLIT = 2
TM_FFN = 512
FFN_CHUNK = 256
FFN_HALO = 16
FFN_DOWN_SLAB = 256
MASK_VALUE = -1e30


def _params(*sem):
    return pltpu.CompilerParams(dimension_semantics=sem, vmem_limit_bytes=VMEM_LIMIT_BYTES)


def _split_hi_lo(a):
    hi = a.astype(BF16)
    lo = (a - hi.astype(F32)).astype(BF16)
    return hi, lo


def _mod_kernel(c_ref, w_ref, b_ref, o_ref):
    c = c_ref[...]
    s = c * (1.0 / (1.0 + jnp.exp(-c)))
    s_hi, s_lo = _split_hi_lo(s)
    lhs = jnp.concatenate([s_hi, s_lo], axis=0)
    w_hi, w_lo = _split_hi_lo(w_ref[...])
    acc = jnp.dot(lhs, w_hi, preferred_element_type=F32) + jnp.dot(lhs, w_lo, preferred_element_type=F32)
    o_ref[...] = acc[0:8] + acc[8:16] + b_ref[...]


def _modulation(c, ada_w, ada_b):
    B, D = c.shape
    N = ada_w.shape[1]
    tn = 1024
    c8 = jnp.pad(c, ((0, 8 - B), (0, 0)))
    out = pl.pallas_call(
        _mod_kernel,
        grid=(N // tn,),
        in_specs=[pl.BlockSpec((8, D), lambda j: (0, 0)),
                  pl.BlockSpec((D, tn), lambda j: (0, j)),
                  pl.BlockSpec((1, tn), lambda j: (0, j))],
        out_specs=pl.BlockSpec((8, tn), lambda j: (0, j)),
        out_shape=jax.ShapeDtypeStruct((8, N), F32),
        compiler_params=_params("parallel"),
        name="modulation",
    )(c8, ada_w, ada_b.reshape(1, N))
    return out


def _mod_spec(which, D):
    return pl.BlockSpec((8, D), lambda i: (0, which))


MOD_SHIFT1, MOD_SCALE1, MOD_GATE1, MOD_SHIFT2, MOD_SCALE2, MOD_GATE2 = range(6)


def _rope_kernel(pos_ref, invf_ref, cos_ref, sin_ref):
    half = ATTN_HEAD_DIM // 2
    groups = LANES // half
    tr = pos_ref.shape[1]
    pos = jnp.concatenate([pos_ref[...].astype(F32), jnp.zeros((8 - groups, tr), F32)], axis=0)
    pos_t = jnp.transpose(pos)
    lane = lax.broadcasted_iota(jnp.int32, (tr, LANES), 1)
    group = lane // half
    pos_dense = jnp.zeros((tr, LANES), F32)
    for q in range(groups):
        pos_dense = jnp.where(group == q, pos_t[:, q:q + 1], pos_dense)
    ang = pos_dense * invf_ref[...]
    first_half = (lane % ATTN_HEAD_DIM) < half
    for table, out_ref, signed in ((jnp.cos(ang), cos_ref, False), (jnp.sin(ang), sin_ref, True)):
        for q in range(groups):
            only = jnp.where(group == q, table, 0.0)
            spread = only
            for s in range(1, groups):
                spread = spread + pltpu.roll(only, s * half, axis=1)
            out_ref[q] = jnp.where(first_half, -spread, spread) if signed else spread


def _rope_tables(positions):
    T = positions.size
    half = ATTN_HEAD_DIM // 2
    groups = LANES // half
    inv_freq = ROPE_THETA ** (-jnp.arange(half, dtype=F32) / half)
    invf = jnp.tile(inv_freq, groups).reshape(1, LANES)
    per = T // groups
    tr = 512
    cos, sin = pl.pallas_call(
        _rope_kernel,
        grid=(per // tr,),
        in_specs=[pl.BlockSpec((groups, tr), lambda i: (0, i)),
                  pl.BlockSpec((1, LANES), lambda i: (0, 0))],
        out_specs=[pl.BlockSpec((groups, tr, LANES), lambda i: (0, i, 0))] * 2,
        out_shape=[jax.ShapeDtypeStruct((groups, per, LANES), F32)] * 2,
        compiler_params=_params("parallel"),
        name="rope_tables",
    )(positions.reshape(groups, per), invf)
    return cos.reshape(T, LANES), sin.reshape(T, LANES)


def _rmsnorm_mod(x, w, scale, shift):
    ms = jnp.mean(x * x, axis=-1, keepdims=True)
    return (x * lax.rsqrt(ms + EPS) * w) * (1.0 + scale) + shift


def _rotate_half_pairs(t, cos, sin_signed):
    lane = lax.broadcasted_iota(jnp.int32, t.shape, 1)
    first_half = (lane % ATTN_HEAD_DIM) < (ATTN_HEAD_DIM // 2)
    from_right = pltpu.roll(t, LANES - ATTN_HEAD_DIM // 2, axis=1)
    from_left = pltpu.roll(t, ATTN_HEAD_DIM // 2, axis=1)
    return t * cos + jnp.where(first_half, from_right, from_left) * sin_signed


def _store_residue_major(ref, perm_ref, slab, col, dilation):
    if dilation == 1:
        ref[:, col:col + LANES] = slab.astype(ref.dtype)
        return
    perm_ref[...] = slab
    n = slab.shape[0] // dilation
    for p in range(dilation):
        ref[p * n:(p + 1) * n, col:col + LANES] = perm_ref[pl.ds(p, n, stride=dilation), :].astype(ref.dtype)


IN_ALIGNED = sum(IN_WIDTHS[:4])
IN_SHIFTED = sum(IN_WIDTHS[5:])


def _in_proj_gla_kernel(x_ref, n1w_ref, scale_ref, shift_ref, cos_ref, sin_ref, w_ref, w2_ref, b2_ref, nw_ref,
                        og_ref, q1_ref, q2_ref, q3_ref, k1_ref, k2_ref, k3_ref, v1_ref, v2_ref, v3_ref,
                        ma_ref, mb_ref,
                        perm_ref, wsh_ref, wlr_ref, gq_s, gk_s, gv_s, gr_s, la_s, state_ref,
                        *, tiles_per_seq, n_tiles):
    i = pl.program_id(0)

    @pl.when(i == 0)
    def _():
        D = w_ref.shape[0]
        lane = lax.broadcasted_iota(jnp.int32, (D, LANES), 1)
        nblk = IN_SHIFTED // LANES
        first = w_ref[:, IN_ALIGNED:IN_ALIGNED + LANES]
        wlr_ref[...] = jnp.where(lane < GLA_LOWRANK, first, jnp.zeros_like(first))
        rolled = pltpu.roll(first.astype(F32), LANES - GLA_LOWRANK, axis=1)
        for c in range(nblk):
            nxt = w_ref[:, IN_ALIGNED + (c + 1) * LANES:IN_ALIGNED + (c + 2) * LANES].astype(F32)
            rolled_nxt = pltpu.roll(nxt, LANES - GLA_LOWRANK, axis=1)
            wsh_ref[:, c * LANES:(c + 1) * LANES] = jnp.where(
                lane < LANES - GLA_LOWRANK, rolled, rolled_nxt).astype(wsh_ref.dtype)
            rolled = rolled_nxt
        for ref in (gq_s, gk_s, gv_s, gr_s, la_s, state_ref):
            ref[...] = jnp.zeros_like(ref)

    b = jnp.minimum(i, n_tiles - 1) // tiles_per_seq
    h = _rmsnorm_mod(x_ref[...], n1w_ref[...], scale_ref[pl.ds(b, 1), :], shift_ref[pl.ds(b, 1), :]).astype(BF16)

    def proj(col, width):
        return jnp.dot(h, wsh_ref[:, col:col + width], preferred_element_type=F32)

    cos = cos_ref[...]
    sin = sin_ref[...]
    q_scale = ATTN_HEAD_DIM ** -0.5
    dilations = [d for _, d in ATTN_GROUPS]

    def rope_piece(ref, col, scale, dilation):
        def run():
            t = proj(col, ATTN_OUT_WIDTH)
            for s in range(0, ATTN_OUT_WIDTH, LANES):
                rot = _rotate_half_pairs(t[:, s:s + LANES], cos, sin) * scale
                _store_residue_major(ref, perm_ref, rot, s, dilation)
        return run

    def value_piece(ref, col, dilation):
        def run():
            t = proj(col, ATTN_OUT_WIDTH)
            for s in range(0, ATTN_OUT_WIDTH, LANES):
                _store_residue_major(ref, perm_ref, t[:, s:s + LANES], s, dilation)
        return run

    def gate_piece(ref, col, s):
        def run():
            z = proj(col + s, 512)
            ref[:, s:s + 512] = (1.0 / (1.0 + jnp.exp(-z))).astype(ref.dtype)
        return run

    pieces = []
    col = 0
    for refs, scale in (((q1_ref, q2_ref, q3_ref), q_scale), ((k1_ref, k2_ref, k3_ref), 1.0)):
        for g, ref in enumerate(refs):
            pieces.append(rope_piece(ref, col, scale, dilations[g]))
            col += ATTN_OUT_WIDTH
    for g, ref in enumerate((v1_ref, v2_ref, v3_ref)):
        pieces.append(value_piece(ref, col, dilations[g]))
        col += ATTN_OUT_WIDTH
    for ref in (ma_ref, mb_ref):
        for s in range(0, D_MODEL, 512):
            pieces.append(gate_piece(ref, col, s))
        col += D_MODEL

    first_of_seq = (i + tiles_per_seq - 1) % tiles_per_seq == 0
    chunks = _gla_chunks(gq_s, gk_s, gv_s, la_s, gr_s, nw_ref, og_ref, state_ref, first_of_seq)
    for n in range(max(len(pieces), len(chunks))):
        if n < len(pieces):
            pieces[n]()
        if n < len(chunks):
            chunks[n]()

    col = 0
    for ref, width in ((gq_s, GLA_QK_WIDTH), (gk_s, GLA_QK_WIDTH)):
        ref[...] = jnp.dot(h, w_ref[:, col:col + width], preferred_element_type=F32).astype(ref.dtype)
        col += width
    for ref in (gv_s, gr_s):
        for s in range(0, GLA_V_WIDTH, 512):
            ref[:, s:s + 512] = jnp.dot(h, w_ref[:, col + s:col + s + 512],
                                        preferred_element_type=F32).astype(ref.dtype)
        col += GLA_V_WIDTH
    g_lr = jnp.dot(h, wlr_ref[...], preferred_element_type=F32)
    g_hi, g_lo = _split_hi_lo(g_lr)
    w2_hi, w2_lo = _split_hi_lo(w2_ref[...])
    z = (jnp.dot(g_hi, w2_hi, preferred_element_type=F32) + jnp.dot(g_lo, w2_hi, preferred_element_type=F32)
         + jnp.dot(g_hi, w2_lo, preferred_element_type=F32)) + b2_ref[...]
    log_sig = jnp.minimum(z, 0.0) - jnp.log(1.0 + jnp.exp(-jnp.abs(z)))
    la_s[...] = log_sig * (1.0 / GLA_TAU)


def _gla_chunks(q_ref, k_ref, v_ref, la_ref, gr_ref, nw_ref, o_ref, state_ref, first_of_seq):
    C = GLA_CHUNK
    heads = range(GLA_HEADS)
    rows = lax.broadcasted_iota(jnp.int32, (C, C), 0)
    cols = lax.broadcasted_iota(jnp.int32, (C, C), 1)
    causal = cols <= rows
    tri = causal.astype(BF16)
    q_scale = GLA_DK ** -0.5
    nw = jnp.concatenate([nw_ref[...]] * GLA_HEADS, axis=1)
    nt = (((1,), (1,)), ((), ()))
    ks = lambda a, h: a[:, h * GLA_DK:(h + 1) * GLA_DK]
    vs = lambda a, h: a[:, h * GLA_DV:(h + 1) * GLA_DV]

    def run_chunk(c):
        sl = pl.ds(c * C, C)
        q = q_ref[sl, :].astype(F32) * q_scale
        k = k_ref[sl, :].astype(F32)
        v = v_ref[sl, :]
        la_hi, la_lo = _split_hi_lo(la_ref[sl, :])
        cum = jnp.dot(tri, la_hi, preferred_element_type=F32) + jnp.dot(tri, la_lo, preferred_element_type=F32)
        mid = cum[C // 2 - 1:C // 2, :]
        last = cum[C - 1:C, :]
        qg = (q * jnp.exp(cum - mid)).astype(BF16)
        kg = (k * jnp.exp(mid - cum)).astype(BF16)
        q_in = (q * jnp.exp(cum)).astype(BF16)
        k_out = k * jnp.exp(last - cum)
        decay_row = jnp.broadcast_to(jnp.exp(last), (8, GLA_QK_WIDTH))
        g = gr_ref[sl, :].astype(F32)
        gate = nw * (g * (1.0 / (1.0 + jnp.exp(-g))))

        attn = [lax.dot_general(ks(qg, h), ks(kg, h), nt, preferred_element_type=F32) for h in heads]
        attn = [jnp.where(causal, a, 0.0).astype(BF16) for a in attn]
        if c == 0:
            state = [jnp.where(first_of_seq, 0.0, state_ref[h]) for h in heads]
        else:
            state = [state_ref[h] for h in heads]
        o = [jnp.dot(attn[h], vs(v, h), preferred_element_type=F32)
             + jnp.dot(ks(q_in, h), state[h].astype(BF16), preferred_element_type=F32) for h in heads]
        k_out_t = [jnp.transpose(ks(k_out, h)).astype(BF16) for h in heads]
        decay = [jnp.transpose(ks(decay_row, h))[:, 0:1] for h in heads]
        for h in heads:
            state_ref[h] = decay[h] * state[h] + jnp.dot(k_out_t[h], vs(v, h), preferred_element_type=F32)
        ms = [jnp.mean(o[h] * o[h], axis=-1, keepdims=True) for h in heads]
        for h in heads:
            o_ref[sl, h * GLA_DV:(h + 1) * GLA_DV] = (o[h] * lax.rsqrt(ms[h] + EPS) * vs(gate, h)).astype(o_ref.dtype)

    return [functools.partial(run_chunk, c) for c in range(q_ref.shape[0] // C)]


def _in_proj_gla(x2, n1w, mod, cos, sin, w_in, w2, b2, gla_norm_w, B):
    T, D = x2.shape
    tm = TM_IN
    n_tiles = T // tm
    cur = lambda i: (jnp.minimum(i, n_tiles - 1), 0)
    lag = lambda i: (jnp.maximum(i - 1, 0), 0)
    const = lambda i: (0, 0)
    widths = (ATTN_OUT_WIDTH,) * 9 + (D_MODEL, D_MODEL)
    single = dict(pipeline_mode=pl.Buffered(1))
    pad_to = IN_ALIGNED + IN_SHIFTED + LANES
    w2p = jnp.pad(w2, ((0, LANES - GLA_LOWRANK), (0, 0)))
    outs = pl.pallas_call(
        functools.partial(_in_proj_gla_kernel, tiles_per_seq=T // B // tm, n_tiles=n_tiles),
        grid=(n_tiles + 1,),
        in_specs=[pl.BlockSpec((tm, D), cur),
                  pl.BlockSpec((1, D), const),
                  _mod_spec(MOD_SCALE1, D),
                  _mod_spec(MOD_SHIFT1, D),
                  pl.BlockSpec((tm, LANES), cur),
                  pl.BlockSpec((tm, LANES), cur),
                  pl.BlockSpec((D, pad_to), const, **single),
                  pl.BlockSpec(w2p.shape, const, **single),
                  pl.BlockSpec(b2.shape, const, **single),
                  pl.BlockSpec(gla_norm_w.shape, const, **single)],
        out_specs=[pl.BlockSpec((tm, GLA_V_WIDTH), lag)] + [pl.BlockSpec((tm, w), cur) for w in widths],
        out_shape=[jax.ShapeDtypeStruct((T, GLA_V_WIDTH), BF16)]
                  + [jax.ShapeDtypeStruct((T, w), BF16) for w in widths],
        scratch_shapes=[pltpu.VMEM((tm, LANES), F32),
                        pltpu.VMEM((D, IN_SHIFTED), BF16),
                        pltpu.VMEM((D, LANES), BF16),
                        pltpu.VMEM((tm, GLA_QK_WIDTH), BF16),
                        pltpu.VMEM((tm, GLA_QK_WIDTH), BF16),
                        pltpu.VMEM((tm, GLA_V_WIDTH), BF16),
                        pltpu.VMEM((tm, GLA_V_WIDTH), BF16),
                        pltpu.VMEM((tm, GLA_QK_WIDTH), F32),
                        pltpu.VMEM((GLA_HEADS, GLA_DK, GLA_DV), F32)],
        compiler_params=_params("arbitrary"),
        name="in_proj_gla",
    )(x2, n1w, mod, mod, cos, sin, w_in.astype(BF16), w2p, b2, gla_norm_w)
    return outs[0], outs[1:]


def _attn_kernel(q1_ref, q2_ref, q3_ref, k1c_ref, k1p_ref, k2c_ref, k2p_ref, k3c_ref, k3p_ref,
                 v1c_ref, v1p_ref, v2c_ref, v2p_ref, v3c_ref, v3p_ref, o_ref,
                 k1cat, v1cat, k2cat, v2cat, *slabs):
    blk = ATTN_BLK
    tile = TM_IN
    span = ATTN_SPAN
    tiles = span // tile
    r2, r3 = ATTN_GROUPS[1][1], ATTN_GROUPS[2][1]
    per3 = tile // r3
    not_first_span = pl.program_id(1) > 0
    npair = ATTN_OUT_WIDTH // LANES
    o1_s, o2_s, o3_s, l1_s, l2_s, l3_s, out_s = [slabs[n * npair:(n + 1) * npair] for n in range(7)]

    for cat, prev, cur in ((k1cat, k1p_ref, k1c_ref), (v1cat, v1p_ref, v1c_ref),
                           (k2cat, k2p_ref, k2c_ref), (v2cat, v2p_ref, v2c_ref)):
        halo = prev.shape[0]
        cat[0:halo] = prev[...]
        cat[halo:] = cur[...]

    rows = lax.broadcasted_iota(jnp.int32, (blk, 2 * blk), 0)
    cols = lax.broadcasted_iota(jnp.int32, (blk, 2 * blk), 1)
    band = (cols >= rows) & (cols <= rows + blk)
    in_cur = cols >= blk
    left = lax.broadcasted_iota(jnp.int32, (blk, LANES), 1) < ATTN_HEAD_DIM
    ones = jnp.ones((2 * blk, LANES), BF16)
    nt = (((1,), (1,)), ((), ()))

    def pair_attend(q_pair, k_cat, v_cat, has_prev):
        zero = jnp.zeros_like(q_pair)
        q2 = jnp.concatenate([jnp.where(left, q_pair, zero), jnp.where(left, zero, q_pair)], axis=0)
        s = lax.dot_general(q2, k_cat, nt, preferred_element_type=F32)
        valid = band & (in_cur | has_prev)
        s = jnp.where(jnp.concatenate([valid, valid], axis=0), s, MASK_VALUE)
        m = jnp.max(s, axis=-1, keepdims=True)
        p = jnp.exp(s - m).astype(BF16)
        r = jnp.dot(p, jnp.concatenate([v_cat, ones], axis=1), preferred_element_type=F32)
        acc = jnp.where(left, r[:blk, :LANES], r[blk:, :LANES])
        den = jnp.where(left, r[:blk, LANES:], r[blk:, LANES:])
        m_pair = jnp.where(left, jnp.broadcast_to(m[:blk], (blk, LANES)), jnp.broadcast_to(m[blk:], (blk, LANES)))
        return acc / den, m_pair + jnp.log(den)

    def tile_body(t, carry):
        for p in range(tiles):
            i = t * tiles + p
            r0 = pl.multiple_of(i * blk, blk)
            for j in range(ATTN_OUT_WIDTH // LANES):
                cs = slice(j * LANES, (j + 1) * LANES)
                o, lse = pair_attend(q1_ref[pl.ds(r0, blk), cs], k1cat[pl.ds(r0, 2 * blk), cs],
                                     v1cat[pl.ds(r0, 2 * blk), cs], not_first_span | (i > 0))
                o1_s[j][pl.ds(r0, blk), :] = o
                l1_s[j][pl.ds(r0, blk), :] = lse
                r1 = pl.multiple_of(r0 + tile, blk)
                k_cat = jnp.concatenate([k2cat[pl.ds(r0, blk), cs], k2cat[pl.ds(r1, blk), cs]], axis=0)
                v_cat = jnp.concatenate([v2cat[pl.ds(r0, blk), cs], v2cat[pl.ds(r1, blk), cs]], axis=0)
                o, lse = pair_attend(q2_ref[pl.ds(r0, blk), cs], k_cat, v_cat, not_first_span | (t > 0))
                tok = pl.ds(pl.multiple_of(t * tile, tile) + p, blk, stride=r2)
                o2_s[j][tok, :] = o
                l2_s[j][tok, :] = lse
                rr = pl.multiple_of(i * per3, per3)
                gather = lambda ref: jnp.concatenate(
                    [ref[pl.ds(rr + u * tile, per3), cs] for u in range(tiles)], axis=0)
                k_cat = jnp.concatenate([gather(k3p_ref), gather(k3c_ref)], axis=0)
                v_cat = jnp.concatenate([gather(v3p_ref), gather(v3c_ref)], axis=0)
                o, lse = pair_attend(gather(q3_ref), k_cat, v_cat, not_first_span)
                o3_s[j][pl.ds(r0, blk), :] = o
                l3_s[j][pl.ds(r0, blk), :] = lse
        return carry

    lax.fori_loop(0, tiles, tile_body, 0)

    for j in range(npair):
        for p in range(r3):
            tok = pl.ds(p, blk, stride=r3)
            res = slice(p * blk, (p + 1) * blk)
            l1, l2, l3 = l1_s[j][tok, :], l2_s[j][tok, :], l3_s[j][res, :]
            m = jnp.maximum(jnp.maximum(l1, l2), l3)
            e1, e2, e3 = jnp.exp(l1 - m), jnp.exp(l2 - m), jnp.exp(l3 - m)
            num = e1 * o1_s[j][tok, :] + e2 * o2_s[j][tok, :] + e3 * o3_s[j][res, :]
            out_s[j][tok, :] = num / (e1 + e2 + e3)
        o_ref[:, j * LANES:(j + 1) * LANES] = out_s[j][...].astype(o_ref.dtype)


def _dilated_attention(qs, ks, vs, B, S):
    span = ATTN_SPAN
    nsp = S // span
    gw = ATTN_OUT_WIDTH
    T = B * S
    cur = lambda b, s: (b * nsp + s, 0)

    def prev(rows):
        per = span // rows
        return pl.BlockSpec((rows, gw), lambda b, s: (jnp.maximum((b * nsp + s) * per - 1, 0), 0))

    halos = (ATTN_BLK, TM_IN, span)
    full = pl.BlockSpec((span, gw), cur)
    kv_specs = []
    for h in halos:
        kv_specs += [full, prev(h)]
    kv_args = lambda arrs: [a for arr in arrs for a in (arr, arr)]
    return pl.pallas_call(
        _attn_kernel,
        grid=(B, nsp),
        in_specs=[full] * 3 + kv_specs + kv_specs,
        out_specs=full,
        out_shape=jax.ShapeDtypeStruct((T, gw), BF16),
        scratch_shapes=[pltpu.VMEM((halos[0] + span, gw), BF16), pltpu.VMEM((halos[0] + span, gw), BF16),
                        pltpu.VMEM((halos[1] + span, gw), BF16), pltpu.VMEM((halos[1] + span, gw), BF16)]
                       + [pltpu.VMEM((span, LANES), F32)] * (7 * (gw // LANES)),
        compiler_params=_params("parallel", "parallel"),
        name="dilated_attn",
    )(*qs, *kv_args(ks), *kv_args(vs))


def _merge_kernel(x_ref, og_ref, oa_ref, ma_ref, mb_ref,
                  gate_ref, scale_ref, shift_ref, n2w_ref, wg_ref, wa_ref, wo_ref, x1_ref, h2_ref, *, tiles_per_seq):
    b = pl.ds(pl.program_id(0) // tiles_per_seq, 1)
    gate, scale, shift = gate_ref[b, :], scale_ref[b, :], shift_ref[b, :]
    sub = x_ref.shape[0] // MERGE_SPLIT
    for r in range(MERGE_SPLIT):
        rs = slice(r * sub, (r + 1) * sub)
        y_attn = jnp.dot(oa_ref[rs, :], wa_ref[...], preferred_element_type=F32)
        y_gla = jnp.dot(og_ref[rs, :], wg_ref[...], preferred_element_type=F32)
        mixed = (ma_ref[rs, :].astype(F32) * y_gla + mb_ref[rs, :].astype(F32) * y_attn).astype(BF16)
        x1 = x_ref[rs, :] + gate * jnp.dot(mixed, wo_ref[...], preferred_element_type=F32)
        x1_ref[rs, :] = x1
        h2_ref[rs, :] = _rmsnorm_mod(x1, n2w_ref[...], scale, shift).astype(h2_ref.dtype)


def _merge(x2, og, oa, ma, mb, mod, n2w, wg, wa, wo, B):
    T, D = x2.shape
    tm = TM_MERGE
    row = lambda i: (i, 0)
    const = lambda i: (0, 0)
    return pl.pallas_call(
        functools.partial(_merge_kernel, tiles_per_seq=T // B // tm),
        grid=(T // tm,),
        in_specs=[pl.BlockSpec((tm, D), row), pl.BlockSpec((tm, GLA_V_WIDTH), row),
                  pl.BlockSpec((tm, ATTN_OUT_WIDTH), row)]
                 + [pl.BlockSpec((tm, D), row)] * 2
                 + [_mod_spec(MOD_GATE1, D), _mod_spec(MOD_SCALE2, D), _mod_spec(MOD_SHIFT2, D)]
                 + [pl.BlockSpec((1, D), const),
                    pl.BlockSpec(wg.shape, const), pl.BlockSpec(wa.shape, const), pl.BlockSpec(wo.shape, const)],
        out_specs=[pl.BlockSpec((tm, D), row)] * 2,
        out_shape=[jax.ShapeDtypeStruct((T, D), F32), jax.ShapeDtypeStruct((T, D), BF16)],
        compiler_params=_params("parallel"),
        name="merge",
    )(x2, og, oa, ma, mb, mod, mod, mod, n2w, wg, wa, wo)


def _gelu_exact(g):
    return 0.5 * g * (1.0 + lax.erf(g * (2.0 ** -0.5)))


def _ffn_kernel(h_ref, hprev_ref, x1_ref, wup_ref, cw_ref, cb_ref, wd_ref, gate_ref, fw_ref, o_ref,
                hcat_s, u_s, hid_s, order_s, *, tiles_per_seq):
    i = pl.program_id(0)
    tm = h_ref.shape[0]
    halo = hprev_ref.shape[0]
    tf = FFN_CHUNK
    half = tm // 2
    hcat_s[0:halo] = jnp.where(i % tiles_per_seq == 0, jnp.zeros_like(hprev_ref[...]), hprev_ref[...])
    hcat_s[halo:] = h_ref[...]
    nchunk = D_FF // tf
    branches = (0, D_FF)

    def up_project(j):
        for n, off in enumerate(branches):
            u = jnp.dot(hcat_s[...], wup_ref[:, off + j * tf:off + (j + 1) * tf], preferred_element_type=F32)
            for c in range(tf // LANES):
                u_s[j % 2, n, c] = u[:, c * LANES:(c + 1) * LANES]

    def conv(j, n, c, parity):
        u = u_s.at[j % 2, n, c]
        cs = slice(branches[n] + j * tf + c * LANES, branches[n] + j * tf + (c + 1) * LANES)
        tap = lambda back: u[pl.ds(halo + parity - back, half, stride=2), :]
        return cb_ref[:, cs] + cw_ref[0:1, cs] * tap(2) + cw_ref[1:2, cs] * tap(1) + cw_ref[2:3, cs] * tap(0)

    up_project(0)
    for j in range(nchunk):
        if j + 1 < nchunk:
            up_project(j + 1)
        for c in range(tf // LANES):
            for parity in range(2):
                hidden = _gelu_exact(conv(j, 1, c, parity)) * conv(j, 0, c, parity)
                hid_s[parity * half:(parity + 1) * half, j * tf + c * LANES:j * tf + (c + 1) * LANES] = (
                    hidden.astype(hid_s.dtype))

    gate = gate_ref[pl.ds(i // tiles_per_seq, 1), :]
    d_model = o_ref.shape[1]
    sq = jnp.zeros((tm, 1), F32)
    for n in range(d_model // FFN_DOWN_SLAB):
        ns = slice(n * FFN_DOWN_SLAB, (n + 1) * FFN_DOWN_SLAB)
        down = jnp.dot(hid_s[...], wd_ref[:, ns], preferred_element_type=F32)
        for c in range(FFN_DOWN_SLAB // LANES):
            g = n * (FFN_DOWN_SLAB // LANES) + c
            cs = slice(g * LANES, (g + 1) * LANES)
            for parity in range(2):
                order_s[g, pl.ds(parity, half, stride=2), :] = (
                    down[parity * half:(parity + 1) * half, c * LANES:(c + 1) * LANES])
            x2 = x1_ref[:, cs] + gate[:, cs] * order_s[g]
            sq = sq + jnp.sum(x2 * x2, axis=-1, keepdims=True)
            o_ref[:, cs] = x2
    o_ref[...] = o_ref[...] * lax.rsqrt(sq * (1.0 / d_model) + EPS) * fw_ref[...]


def _ffn(h2, x1, w_up, conv_w, conv_b, w_down, mod, final_w, B):
    T, D = x1.shape
    tm, tf, halo = TM_FFN, FFN_CHUNK, FFN_HALO
    row = lambda i: (i, 0)
    const = lambda i: (0, 0)
    single = dict(pipeline_mode=pl.Buffered(1))
    return pl.pallas_call(
        functools.partial(_ffn_kernel, tiles_per_seq=T // B // tm),
        grid=(T // tm,),
        in_specs=[pl.BlockSpec((tm, D), row),
                  pl.BlockSpec((halo, D), lambda i: (jnp.maximum(i * (tm // halo) - 1, 0), 0)),
                  pl.BlockSpec((tm, D), row),
                  pl.BlockSpec(w_up.shape, const, **single),
                  pl.BlockSpec(conv_w.shape, const, **single),
                  pl.BlockSpec(conv_b.shape, const, **single),
                  pl.BlockSpec(w_down.shape, const, **single),
                  _mod_spec(MOD_GATE2, D),
                  pl.BlockSpec((1, D), const)],
        out_specs=pl.BlockSpec((tm, D), row),
        out_shape=jax.ShapeDtypeStruct((T, D), F32),
        scratch_shapes=[pltpu.VMEM((halo + tm, D), BF16),
                        pltpu.VMEM((2, 2, tf // LANES, halo + tm, LANES), F32),
                        pltpu.VMEM((tm, D_FF), BF16),
                        pltpu.VMEM((D // LANES, tm, LANES), F32)],
        compiler_params=_params("parallel"),
        name="ffn",
    )(h2, h2, x1, w_up, conv_w, conv_b, w_down, mod, final_w)


def kernel(x, c, positions, ada_w, ada_b, norm1_w, w_in, gla_gate_w2, gla_gate_b, gla_norm_w, w_gla_branch,
           w_attn_branch, w_out, norm2_w, w_up, conv_w, conv_b, w_down, final_norm_w):
    B, S, D = x.shape
    T = B * S
    depth = ada_w.shape[0]
    assert depth == 1, "the final norm is fused into the (single) layer's ffn"
    assert all(window // dilation == ATTN_BLK for window, dilation in ATTN_GROUPS)
    cos, sin = _rope_tables(positions)
    x2 = x.reshape(T, D)
    for layer in range(depth):
        mod = _modulation(c, ada_w[layer], ada_b[layer])

        og, (q1, q2, q3, k1, k2, k3, v1, v2, v3, ma, mb) = _in_proj_gla(
            x2, norm1_w[layer].reshape(1, D), mod, cos, sin, w_in[layer], gla_gate_w2[layer],
            gla_gate_b[layer].reshape(1, -1), gla_norm_w[layer].reshape(1, -1), B)
        oa = _dilated_attention((q1, q2, q3), (k1, k2, k3), (v1, v2, v3), B, S)

        x1, h2 = _merge(x2, og, oa, ma, mb, mod, norm2_w[layer].reshape(1, D),
                        w_gla_branch[layer].astype(BF16), w_attn_branch[layer].astype(BF16),
                        w_out[layer].astype(BF16), B)

        x2 = _ffn(h2, x1, w_up[layer].astype(BF16), conv_w[layer], conv_b[layer].reshape(1, -1),
                  w_down[layer].astype(BF16), mod, final_norm_w.reshape(1, D), B)
    return x2.reshape(B, S, D)
```

```python
import functools

import jax
import jax.numpy as jnp
from jax import lax
from jax.experimental import pallas as pl
from jax.experimental.pallas import tpu as pltpu

F32 = jnp.float32
BF16 = jnp.bfloat16

D_MODEL = 1024
GLA_HEADS = 4
GLA_DK = 128
GLA_DV = 256
GLA_LOWRANK = 16
GLA_TAU = 16.0
GLA_QK_WIDTH = GLA_HEADS * GLA_DK
GLA_V_WIDTH = GLA_HEADS * GLA_DV
ATTN_GROUPS = ((128, 1), (512, 4), (2048, 16))
ATTN_HEADS_PER_GROUP = 4
ATTN_HEAD_DIM = 64
ATTN_WIDTH = ATTN_HEADS_PER_GROUP * len(ATTN_GROUPS) * ATTN_HEAD_DIM
ATTN_OUT_WIDTH = ATTN_HEADS_PER_GROUP * ATTN_HEAD_DIM
ROPE_THETA = 10000.0
D_FF = 2816
CONV_WIDTH = 3
EPS = 1e-6
IN_WIDTHS = (GLA_QK_WIDTH, GLA_QK_WIDTH, GLA_V_WIDTH, GLA_V_WIDTH, GLA_LOWRANK,
             ATTN_WIDTH, ATTN_WIDTH, ATTN_WIDTH, D_MODEL, D_MODEL)

LANES = 128
VMEM_LIMIT_BYTES = 56 * 1024 * 1024

TM_IN = 512
GLA_CHUNK = 64
ATTN_BLK = 128
ATTN_SPAN = 2048
ATTN_MERGE_ROWS = 256
TM_MERGE = 1024
MERGE_SPLIT = 2
TM_FFN = 512
FFN_CHUNK = 256
FFN_HALO = 16
FFN_DOWN_SLAB = 256
MASK_VALUE = -1e30


def _params(*sem):
    return pltpu.CompilerParams(dimension_semantics=sem, vmem_limit_bytes=VMEM_LIMIT_BYTES)


def _split_hi_lo(a):
    hi = a.astype(BF16)
    lo = (a - hi.astype(F32)).astype(BF16)
    return hi, lo


def _mod_kernel(c_ref, w_ref, b_ref, o_ref):
    c = c_ref[...]
    s = c * (1.0 / (1.0 + jnp.exp(-c)))
    s_hi, s_lo = _split_hi_lo(s)
    lhs = jnp.concatenate([s_hi, s_lo], axis=0)
    w_hi, w_lo = _split_hi_lo(w_ref[...])
    acc = jnp.dot(lhs, w_hi, preferred_element_type=F32) + jnp.dot(lhs, w_lo, preferred_element_type=F32)
    o_ref[...] = acc[0:8] + acc[8:16] + b_ref[...]


def _modulation(c, ada_w, ada_b):
    B, D = c.shape
    N = ada_w.shape[1]
    tn = 1024
    c8 = jnp.pad(c, ((0, 8 - B), (0, 0)))
    out = pl.pallas_call(
        _mod_kernel,
        grid=(N // tn,),
        in_specs=[pl.BlockSpec((8, D), lambda j: (0, 0)),
                  pl.BlockSpec((D, tn), lambda j: (0, j)),
                  pl.BlockSpec((1, tn), lambda j: (0, j))],
        out_specs=pl.BlockSpec((8, tn), lambda j: (0, j)),
        out_shape=jax.ShapeDtypeStruct((8, N), F32),
        compiler_params=_params("parallel"),
        name="modulation",
    )(c8, ada_w, ada_b.reshape(1, N))
    return out


def _mod_spec(which, D):
    return pl.BlockSpec((8, D), lambda i: (0, which))


MOD_SHIFT1, MOD_SCALE1, MOD_GATE1, MOD_SHIFT2, MOD_SCALE2, MOD_GATE2 = range(6)


def _rope_kernel(pos_ref, invf_ref, cos_ref, sin_ref):
    half = ATTN_HEAD_DIM // 2
    groups = LANES // half
    tr = pos_ref.shape[1]
    pos = jnp.concatenate([pos_ref[...].astype(F32), jnp.zeros((8 - groups, tr), F32)], axis=0)
    pos_t = jnp.transpose(pos)
    lane = lax.broadcasted_iota(jnp.int32, (tr, LANES), 1)
    group = lane // half
    pos_dense = jnp.zeros((tr, LANES), F32)
    for q in range(groups):
        pos_dense = jnp.where(group == q, pos_t[:, q:q + 1], pos_dense)
    ang = pos_dense * invf_ref[...]
    first_half = (lane % ATTN_HEAD_DIM) < half
    for table, out_ref, signed in ((jnp.cos(ang), cos_ref, False), (jnp.sin(ang), sin_ref, True)):
        for q in range(groups):
            only = jnp.where(group == q, table, 0.0)
            spread = only
            for s in range(1, groups):
                spread = spread + pltpu.roll(only, s * half, axis=1)
            out_ref[q] = jnp.where(first_half, -spread, spread) if signed else spread


def _rope_tables(positions):
    T = positions.size
    half = ATTN_HEAD_DIM // 2
    groups = LANES // half
    inv_freq = ROPE_THETA ** (-jnp.arange(half, dtype=F32) / half)
    invf = jnp.tile(inv_freq, groups).reshape(1, LANES)
    per = T // groups
    tr = 512
    cos, sin = pl.pallas_call(
        _rope_kernel,
        grid=(per // tr,),
        in_specs=[pl.BlockSpec((groups, tr), lambda i: (0, i)),
                  pl.BlockSpec((1, LANES), lambda i: (0, 0))],
        out_specs=[pl.BlockSpec((groups, tr, LANES), lambda i: (0, i, 0))] * 2,
        out_shape=[jax.ShapeDtypeStruct((groups, per, LANES), F32)] * 2,
        compiler_params=_params("parallel"),
        name="rope_tables",
    )(positions.reshape(groups, per), invf)
    return cos.reshape(T, LANES), sin.reshape(T, LANES)


def _rmsnorm_mod(x, w, scale, shift):
    ms = jnp.mean(x * x, axis=-1, keepdims=True)
    return (x * lax.rsqrt(ms + EPS) * w) * (1.0 + scale) + shift


def _rotate_half_pairs(t, cos, sin_signed):
    lane = lax.broadcasted_iota(jnp.int32, t.shape, 1)
    first_half = (lane % ATTN_HEAD_DIM) < (ATTN_HEAD_DIM // 2)
    from_right = pltpu.roll(t, LANES - ATTN_HEAD_DIM // 2, axis=1)
    from_left = pltpu.roll(t, ATTN_HEAD_DIM // 2, axis=1)
    return t * cos + jnp.where(first_half, from_right, from_left) * sin_signed


def _store_residue_major(ref, perm_ref, slab, col, dilation):
    if dilation == 1:
        ref[:, col:col + LANES] = slab.astype(ref.dtype)
        return
    perm_ref[...] = slab
    n = slab.shape[0] // dilation
    for p in range(dilation):
        ref[p * n:(p + 1) * n, col:col + LANES] = perm_ref[pl.ds(p, n, stride=dilation), :].astype(ref.dtype)


IN_ALIGNED = sum(IN_WIDTHS[:4])
IN_SHIFTED = sum(IN_WIDTHS[5:])


def _in_proj_gla_kernel(x_ref, n1w_ref, scale_ref, shift_ref, cos_ref, sin_ref, w_ref, w2_ref, b2_ref, nw_ref,
                        og_ref, q1_ref, q2_ref, q3_ref, k1_ref, k2_ref, k3_ref, v1_ref, v2_ref, v3_ref,
                        ma_ref, mb_ref,
                        perm_ref, wsh_ref, wlr_ref, gq_s, gk_s, gv_s, gr_s, la_s, state_ref,
                        *, tiles_per_seq, n_tiles):
    i = pl.program_id(0)

    @pl.when(i == 0)
    def _():
        D = w_ref.shape[0]
        lane = lax.broadcasted_iota(jnp.int32, (D, LANES), 1)
        nblk = IN_SHIFTED // LANES
        first = w_ref[:, IN_ALIGNED:IN_ALIGNED + LANES]
        wlr_ref[...] = jnp.where(lane < GLA_LOWRANK, first, jnp.zeros_like(first))
        rolled = pltpu.roll(first.astype(F32), LANES - GLA_LOWRANK, axis=1)
        for c in range(nblk):
            nxt = w_ref[:, IN_ALIGNED + (c + 1) * LANES:IN_ALIGNED + (c + 2) * LANES].astype(F32)
            rolled_nxt = pltpu.roll(nxt, LANES - GLA_LOWRANK, axis=1)
            wsh_ref[:, c * LANES:(c + 1) * LANES] = jnp.where(
                lane < LANES - GLA_LOWRANK, rolled, rolled_nxt).astype(wsh_ref.dtype)
            rolled = rolled_nxt
        for ref in (gq_s, gk_s, gv_s, gr_s, la_s, state_ref):
            ref[...] = jnp.zeros_like(ref)

    b = jnp.minimum(i, n_tiles - 1) // tiles_per_seq
    h = _rmsnorm_mod(x_ref[...], n1w_ref[...], scale_ref[pl.ds(b, 1), :], shift_ref[pl.ds(b, 1), :]).astype(BF16)

    def proj(col, width):
        return jnp.dot(h, wsh_ref[:, col:col + width], preferred_element_type=F32)

    cos = cos_ref[...]
    sin = sin_ref[...]
    q_scale = ATTN_HEAD_DIM ** -0.5
    dilations = [d for _, d in ATTN_GROUPS]

    def rope_piece(ref, col, scale, dilation):
        def run():
            t = proj(col, ATTN_OUT_WIDTH)
            for s in range(0, ATTN_OUT_WIDTH, LANES):
                rot = _rotate_half_pairs(t[:, s:s + LANES], cos, sin) * scale
                _store_residue_major(ref, perm_ref, rot, s, dilation)
        return run

    def value_piece(ref, col, dilation):
        def run():
            t = proj(col, ATTN_OUT_WIDTH)
            for s in range(0, ATTN_OUT_WIDTH, LANES):
                _store_residue_major(ref, perm_ref, t[:, s:s + LANES], s, dilation)
        return run

    def gate_piece(ref, col, s):
        def run():
            z = proj(col + s, 512)
            ref[:, s:s + 512] = (1.0 / (1.0 + jnp.exp(-z))).astype(ref.dtype)
        return run

    pieces = []
    col = 0
    for refs, scale in (((q1_ref, q2_ref, q3_ref), q_scale), ((k1_ref, k2_ref, k3_ref), 1.0)):
        for g, ref in enumerate(refs):
            pieces.append(rope_piece(ref, col, scale, dilations[g]))
            col += ATTN_OUT_WIDTH
    for g, ref in enumerate((v1_ref, v2_ref, v3_ref)):
        pieces.append(value_piece(ref, col, dilations[g]))
        col += ATTN_OUT_WIDTH
    for ref in (ma_ref, mb_ref):
        for s in range(0, D_MODEL, 512):
            pieces.append(gate_piece(ref, col, s))
        col += D_MODEL

    first_of_seq = (i + tiles_per_seq - 1) % tiles_per_seq == 0
    chunks = _gla_chunks(gq_s, gk_s, gv_s, la_s, gr_s, nw_ref, og_ref, state_ref, first_of_seq)
    for n in range(max(len(pieces), len(chunks))):
        if n < len(pieces):
            pieces[n]()
        if n < len(chunks):
            chunks[n]()

    col = 0
    for ref, width in ((gq_s, GLA_QK_WIDTH), (gk_s, GLA_QK_WIDTH)):
        ref[...] = jnp.dot(h, w_ref[:, col:col + width], preferred_element_type=F32).astype(ref.dtype)
        col += width
    for ref in (gv_s, gr_s):
        for s in range(0, GLA_V_WIDTH, 512):
            ref[:, s:s + 512] = jnp.dot(h, w_ref[:, col + s:col + s + 512],
                                        preferred_element_type=F32).astype(ref.dtype)
        col += GLA_V_WIDTH
    g_lr = jnp.dot(h, wlr_ref[...], preferred_element_type=F32)
    g_hi, g_lo = _split_hi_lo(g_lr)
    w2_hi, w2_lo = _split_hi_lo(w2_ref[...])
    z = (jnp.dot(g_hi, w2_hi, preferred_element_type=F32) + jnp.dot(g_lo, w2_hi, preferred_element_type=F32)
         + jnp.dot(g_hi, w2_lo, preferred_element_type=F32)) + b2_ref[...]
    log_sig = jnp.minimum(z, 0.0) - jnp.log(1.0 + jnp.exp(-jnp.abs(z)))
    la_s[...] = log_sig * (1.0 / GLA_TAU)


def _gla_chunks(q_ref, k_ref, v_ref, la_ref, gr_ref, nw_ref, o_ref, state_ref, first_of_seq):
    C = GLA_CHUNK
    heads = range(GLA_HEADS)
    rows = lax.broadcasted_iota(jnp.int32, (C, C), 0)
    cols = lax.broadcasted_iota(jnp.int32, (C, C), 1)
    causal = cols <= rows
    tri = causal.astype(BF16)
    q_scale = GLA_DK ** -0.5
    nw = jnp.concatenate([nw_ref[...]] * GLA_HEADS, axis=1)
    nt = (((1,), (1,)), ((), ()))
    ks = lambda a, h: a[:, h * GLA_DK:(h + 1) * GLA_DK]
    vs = lambda a, h: a[:, h * GLA_DV:(h + 1) * GLA_DV]

    def run_chunk(c):
        sl = pl.ds(c * C, C)
        q = q_ref[sl, :].astype(F32) * q_scale
        k = k_ref[sl, :].astype(F32)
        v = v_ref[sl, :]
        la_hi, la_lo = _split_hi_lo(la_ref[sl, :])
        cum = jnp.dot(tri, la_hi, preferred_element_type=F32) + jnp.dot(tri, la_lo, preferred_element_type=F32)
        mid = cum[C // 2 - 1:C // 2, :]
        last = cum[C - 1:C, :]
        qg = (q * jnp.exp(cum - mid)).astype(BF16)
        kg = (k * jnp.exp(mid - cum)).astype(BF16)
        q_in = (q * jnp.exp(cum)).astype(BF16)
        k_out = k * jnp.exp(last - cum)
        decay_row = jnp.broadcast_to(jnp.exp(last), (8, GLA_QK_WIDTH))
        g = gr_ref[sl, :].astype(F32)
        gate = nw * (g * (1.0 / (1.0 + jnp.exp(-g))))

        attn = [lax.dot_general(ks(qg, h), ks(kg, h), nt, preferred_element_type=F32) for h in heads]
        attn = [jnp.where(causal, a, 0.0).astype(BF16) for a in attn]
        if c == 0:
            state = [jnp.where(first_of_seq, 0.0, state_ref[h]) for h in heads]
        else:
            state = [state_ref[h] for h in heads]
        o = [jnp.dot(attn[h], vs(v, h), preferred_element_type=F32)
             + jnp.dot(ks(q_in, h), state[h].astype(BF16), preferred_element_type=F32) for h in heads]
        k_out_t = [jnp.transpose(ks(k_out, h)).astype(BF16) for h in heads]
        decay = [jnp.transpose(ks(decay_row, h))[:, 0:1] for h in heads]
        for h in heads:
            state_ref[h] = decay[h] * state[h] + jnp.dot(k_out_t[h], vs(v, h), preferred_element_type=F32)
        ms = [jnp.mean(o[h] * o[h], axis=-1, keepdims=True) for h in heads]
        for h in heads:
            o_ref[sl, h * GLA_DV:(h + 1) * GLA_DV] = (o[h] * lax.rsqrt(ms[h] + EPS) * vs(gate, h)).astype(o_ref.dtype)

    return [functools.partial(run_chunk, c) for c in range(q_ref.shape[0] // C)]


def _in_proj_gla(x2, n1w, mod, cos, sin, w_in, w2, b2, gla_norm_w, B):
    T, D = x2.shape
    tm = TM_IN
    n_tiles = T // tm
    cur = lambda i: (jnp.minimum(i, n_tiles - 1), 0)
    lag = lambda i: (jnp.maximum(i - 1, 0), 0)
    const = lambda i: (0, 0)
    widths = (ATTN_OUT_WIDTH,) * 9 + (D_MODEL, D_MODEL)
    single = dict(pipeline_mode=pl.Buffered(1))
    pad_to = IN_ALIGNED + IN_SHIFTED + LANES
    w2p = jnp.pad(w2, ((0, LANES - GLA_LOWRANK), (0, 0)))
    outs = pl.pallas_call(
        functools.partial(_in_proj_gla_kernel, tiles_per_seq=T // B // tm, n_tiles=n_tiles),
        grid=(n_tiles + 1,),
        in_specs=[pl.BlockSpec((tm, D), cur),
                  pl.BlockSpec((1, D), const),
                  _mod_spec(MOD_SCALE1, D),
                  _mod_spec(MOD_SHIFT1, D),
                  pl.BlockSpec((tm, LANES), cur),
                  pl.BlockSpec((tm, LANES), cur),
                  pl.BlockSpec((D, pad_to), const, **single),
                  pl.BlockSpec(w2p.shape, const, **single),
                  pl.BlockSpec(b2.shape, const, **single),
                  pl.BlockSpec(gla_norm_w.shape, const, **single)],
        out_specs=[pl.BlockSpec((tm, GLA_V_WIDTH), lag)] + [pl.BlockSpec((tm, w), cur) for w in widths],
        out_shape=[jax.ShapeDtypeStruct((T, GLA_V_WIDTH), BF16)]
                  + [jax.ShapeDtypeStruct((T, w), BF16) for w in widths],
        scratch_shapes=[pltpu.VMEM((tm, LANES), F32),
                        pltpu.VMEM((D, IN_SHIFTED), BF16),
                        pltpu.VMEM((D, LANES), BF16),
                        pltpu.VMEM((tm, GLA_QK_WIDTH), BF16),
                        pltpu.VMEM((tm, GLA_QK_WIDTH), BF16),
                        pltpu.VMEM((tm, GLA_V_WIDTH), BF16),
                        pltpu.VMEM((tm, GLA_V_WIDTH), BF16),
                        pltpu.VMEM((tm, GLA_QK_WIDTH), F32),
                        pltpu.VMEM((GLA_HEADS, GLA_DK, GLA_DV), F32)],
        compiler_params=_params("arbitrary"),
        name="in_proj_gla",
    )(x2, n1w, mod, mod, cos, sin, w_in.astype(BF16), w2p, b2, gla_norm_w)
    return outs[0], outs[1:]


def _attn_kernel(q1_ref, q2_ref, q3_ref, k1c_ref, k1p_ref, k2c_ref, k2p_ref, k3c_ref, k3p_ref,
                 v1c_ref, v1p_ref, v2c_ref, v2p_ref, v3c_ref, v3p_ref, o_ref, *slabs):
    blk = ATTN_BLK
    tile = TM_IN
    span = ATTN_SPAN
    tiles = span // tile
    r2, r3 = ATTN_GROUPS[1][1], ATTN_GROUPS[2][1]
    per3 = tile // r3
    not_first_span = pl.program_id(1) > 0
    npair = ATTN_OUT_WIDTH // LANES
    o1_s, o2_s, o3_s, l1_s, l2_s, l3_s = [slabs[n * npair:(n + 1) * npair] for n in range(6)]

    rows = lax.broadcasted_iota(jnp.int32, (blk, 2 * blk), 0)
    cols = lax.broadcasted_iota(jnp.int32, (blk, 2 * blk), 1)
    band = (cols >= rows) & (cols <= rows + blk)
    in_cur = cols >= blk
    left = lax.broadcasted_iota(jnp.int32, (blk, LANES), 1) < ATTN_HEAD_DIM
    ones = jnp.ones((2 * blk, LANES), BF16)
    nt = (((1,), (1,)), ((), ()))

    def pair_attend(q_pair, k_cat, v_cat, has_prev):
        zero = jnp.zeros_like(q_pair)
        q2 = jnp.concatenate([jnp.where(left, q_pair, zero), jnp.where(left, zero, q_pair)], axis=0)
        s = lax.dot_general(q2, k_cat, nt, preferred_element_type=F32)
        valid = band & (in_cur | has_prev)
        s = jnp.where(jnp.concatenate([valid, valid], axis=0), s, MASK_VALUE)
        m = jnp.max(s, axis=-1, keepdims=True)
        p = jnp.exp(s - m).astype(BF16)
        r = jnp.dot(p, jnp.concatenate([v_cat, ones], axis=1), preferred_element_type=F32)
        acc = jnp.where(left, r[:blk, :LANES], r[blk:, :LANES])
        den = jnp.where(left, r[:blk, LANES:], r[blk:, LANES:])
        m_pair = jnp.where(left, jnp.broadcast_to(m[:blk], (blk, LANES)), jnp.broadcast_to(m[blk:], (blk, LANES)))
        return acc / den, m_pair + jnp.log(den)

    def with_prev(cur_ref, prev_ref, r0, back, prev_rows, first, cs):
        before = cur_ref[pl.ds(pl.multiple_of(jnp.maximum(r0 - back, 0), blk), blk), cs]
        if prev_rows is not None:
            before = jnp.where(first, prev_ref[prev_rows, cs], before)
        return jnp.concatenate([before, cur_ref[pl.ds(r0, blk), cs]], axis=0)

    def tile_body(t, carry):
        for p in range(tiles):
            i = t * tiles + p
            r0 = pl.multiple_of(i * blk, blk)
            for j in range(npair):
                cs = slice(j * LANES, (j + 1) * LANES)
                rows1 = slice(0, blk) if p == 0 else None
                o, lse = pair_attend(q1_ref[pl.ds(r0, blk), cs],
                                     with_prev(k1c_ref, k1p_ref, r0, blk, rows1, t == 0, cs),
                                     with_prev(v1c_ref, v1p_ref, r0, blk, rows1, t == 0, cs),
                                     not_first_span | (i > 0))
                o1_s[j][pl.ds(r0, blk), :] = o
                l1_s[j][pl.ds(r0, blk), :] = lse
                rows2 = slice(p * blk, (p + 1) * blk)
                o, lse = pair_attend(q2_ref[pl.ds(r0, blk), cs],
                                     with_prev(k2c_ref, k2p_ref, r0, tile, rows2, t == 0, cs),
                                     with_prev(v2c_ref, v2p_ref, r0, tile, rows2, t == 0, cs),
                                     not_first_span | (t > 0))
                tok = pl.ds(pl.multiple_of(t * tile, tile) + p, blk, stride=r2)
                o2_s[j][tok, :] = o
                l2_s[j][tok, :] = lse
                rr = pl.multiple_of(i * per3, per3)
                gather = lambda ref: jnp.concatenate(
                    [ref[pl.ds(rr + u * tile, per3), cs] for u in range(tiles)], axis=0)
                k_cat = jnp.concatenate([gather(k3p_ref), gather(k3c_ref)], axis=0)
                v_cat = jnp.concatenate([gather(v3p_ref), gather(v3c_ref)], axis=0)
                o, lse = pair_attend(gather(q3_ref), k_cat, v_cat, not_first_span)
                tok = pl.ds(i, blk, stride=r3)
                o3_s[j][tok, :] = o
                l3_s[j][tok, :] = lse
        return carry

    lax.fori_loop(0, tiles, tile_body, 0)

    for j in range(npair):
        for n in range(span // ATTN_MERGE_ROWS):
            rs = slice(n * ATTN_MERGE_ROWS, (n + 1) * ATTN_MERGE_ROWS)
            l1, l2, l3 = l1_s[j][rs, :], l2_s[j][rs, :], l3_s[j][rs, :]
            m = jnp.maximum(jnp.maximum(l1, l2), l3)
            e1, e2, e3 = jnp.exp(l1 - m), jnp.exp(l2 - m), jnp.exp(l3 - m)
            num = e1 * o1_s[j][rs, :] + e2 * o2_s[j][rs, :] + e3 * o3_s[j][rs, :]
            o_ref[rs, j * LANES:(j + 1) * LANES] = (num / (e1 + e2 + e3)).astype(o_ref.dtype)


def _dilated_attention(qs, ks, vs, B, S):
    span = ATTN_SPAN
    nsp = S // span
    gw = ATTN_OUT_WIDTH
    T = B * S
    cur = lambda b, s: (b * nsp + s, 0)

    def prev(rows):
        per = span // rows
        return pl.BlockSpec((rows, gw), lambda b, s: (jnp.maximum((b * nsp + s) * per - 1, 0), 0))

    halos = (ATTN_BLK, TM_IN, span)
    full = pl.BlockSpec((span, gw), cur)
    kv_specs = []
    for h in halos:
        kv_specs += [full, prev(h)]
    kv_args = lambda arrs: [a for arr in arrs for a in (arr, arr)]
    return pl.pallas_call(
        _attn_kernel,
        grid=(B, nsp),
        in_specs=[full] * 3 + kv_specs + kv_specs,
        out_specs=full,
        out_shape=jax.ShapeDtypeStruct((T, gw), BF16),
        scratch_shapes=[pltpu.VMEM((span, LANES), F32)] * (6 * (gw // LANES)),
        compiler_params=_params("parallel", "parallel"),
        name="dilated_attn",
    )(*qs, *kv_args(ks), *kv_args(vs))


def _merge_kernel(x_ref, og_ref, oa_ref, ma_ref, mb_ref,
                  gate_ref, scale_ref, shift_ref, n2w_ref, wg_ref, wa_ref, wo_ref, x1_ref, h2_ref, *, tiles_per_seq):
    b = pl.ds(pl.program_id(0) // tiles_per_seq, 1)
    gate, scale, shift = gate_ref[b, :], scale_ref[b, :], shift_ref[b, :]
    sub = x_ref.shape[0] // MERGE_SPLIT
    for r in range(MERGE_SPLIT):
        rs = slice(r * sub, (r + 1) * sub)
        y_attn = jnp.dot(oa_ref[rs, :], wa_ref[...], preferred_element_type=F32)
        y_gla = jnp.dot(og_ref[rs, :], wg_ref[...], preferred_element_type=F32)
        mixed = (ma_ref[rs, :].astype(F32) * y_gla + mb_ref[rs, :].astype(F32) * y_attn).astype(BF16)
        x1 = x_ref[rs, :] + gate * jnp.dot(mixed, wo_ref[...], preferred_element_type=F32)
        x1_ref[rs, :] = x1
        h2_ref[rs, :] = _rmsnorm_mod(x1, n2w_ref[...], scale, shift).astype(h2_ref.dtype)


def _merge(x2, og, oa, ma, mb, mod, n2w, wg, wa, wo, B):
    T, D = x2.shape
    tm = TM_MERGE
    row = lambda i: (i, 0)
    const = lambda i: (0, 0)
    return pl.pallas_call(
        functools.partial(_merge_kernel, tiles_per_seq=T // B // tm),
        grid=(T // tm,),
        in_specs=[pl.BlockSpec((tm, D), row), pl.BlockSpec((tm, GLA_V_WIDTH), row),
                  pl.BlockSpec((tm, ATTN_OUT_WIDTH), row)]
                 + [pl.BlockSpec((tm, D), row)] * 2
                 + [_mod_spec(MOD_GATE1, D), _mod_spec(MOD_SCALE2, D), _mod_spec(MOD_SHIFT2, D)]
                 + [pl.BlockSpec((1, D), const),
                    pl.BlockSpec(wg.shape, const), pl.BlockSpec(wa.shape, const), pl.BlockSpec(wo.shape, const)],
        out_specs=[pl.BlockSpec((tm, D), row)] * 2,
        out_shape=[jax.ShapeDtypeStruct((T, D), F32), jax.ShapeDtypeStruct((T, D), BF16)],
        compiler_params=_params("parallel"),
        name="merge",
    )(x2, og, oa, ma, mb, mod, mod, mod, n2w, wg, wa, wo)


def _gelu_exact(g):
    return 0.5 * g * (1.0 + lax.erf(g * (2.0 ** -0.5)))


def _ffn_kernel(h_ref, hprev_ref, x1_ref, wup_ref, cw_ref, cb_ref, wd_ref, gate_ref, fw_ref, o_ref,
                hcat_s, u_s, hid_s, order_s, *, tiles_per_seq):
    i = pl.program_id(0)
    tm = h_ref.shape[0]
    halo = hprev_ref.shape[0]
    tf = FFN_CHUNK
    half = tm // 2
    hcat_s[0:halo] = jnp.where(i % tiles_per_seq == 0, jnp.zeros_like(hprev_ref[...]), hprev_ref[...])
    hcat_s[halo:] = h_ref[...]
    nchunk = D_FF // tf
    branches = (0, D_FF)

    def up_project(j):
        for n, off in enumerate(branches):
            u = jnp.dot(hcat_s[...], wup_ref[:, off + j * tf:off + (j + 1) * tf], preferred_element_type=F32)
            for c in range(tf // LANES):
                u_s[j % 2, n, c] = u[:, c * LANES:(c + 1) * LANES]

    def conv(j, n, c, parity):
        u = u_s.at[j % 2, n, c]
        cs = slice(branches[n] + j * tf + c * LANES, branches[n] + j * tf + (c + 1) * LANES)
        tap = lambda back: u[pl.ds(halo + parity - back, half, stride=2), :]
        return cb_ref[:, cs] + cw_ref[0:1, cs] * tap(2) + cw_ref[1:2, cs] * tap(1) + cw_ref[2:3, cs] * tap(0)

    up_project(0)
    for j in range(nchunk):
        if j + 1 < nchunk:
            up_project(j + 1)
        for c in range(tf // LANES):
            for parity in range(2):
                hidden = _gelu_exact(conv(j, 1, c, parity)) * conv(j, 0, c, parity)
                hid_s[parity * half:(parity + 1) * half, j * tf + c * LANES:j * tf + (c + 1) * LANES] = (
                    hidden.astype(hid_s.dtype))

    gate = gate_ref[pl.ds(i // tiles_per_seq, 1), :]
    d_model = o_ref.shape[1]
    sq = jnp.zeros((tm, 1), F32)
    for n in range(d_model // FFN_DOWN_SLAB):
        ns = slice(n * FFN_DOWN_SLAB, (n + 1) * FFN_DOWN_SLAB)
        down = jnp.dot(hid_s[...], wd_ref[:, ns], preferred_element_type=F32)
        for c in range(FFN_DOWN_SLAB // LANES):
            g = n * (FFN_DOWN_SLAB // LANES) + c
            cs = slice(g * LANES, (g + 1) * LANES)
            for parity in range(2):
                order_s[g, pl.ds(parity, half, stride=2), :] = (
                    down[parity * half:(parity + 1) * half, c * LANES:(c + 1) * LANES])
            x2 = x1_ref[:, cs] + gate[:, cs] * order_s[g]
            sq = sq + jnp.sum(x2 * x2, axis=-1, keepdims=True)
            o_ref[:, cs] = x2
    o_ref[...] = o_ref[...] * lax.rsqrt(sq * (1.0 / d_model) + EPS) * fw_ref[...]


def _ffn(h2, x1, w_up, conv_w, conv_b, w_down, mod, final_w, B):
    T, D = x1.shape
    tm, tf, halo = TM_FFN, FFN_CHUNK, FFN_HALO
    row = lambda i: (i, 0)
    const = lambda i: (0, 0)
    single = dict(pipeline_mode=pl.Buffered(1))
    return pl.pallas_call(
        functools.partial(_ffn_kernel, tiles_per_seq=T // B // tm),
        grid=(T // tm,),
        in_specs=[pl.BlockSpec((tm, D), row),
                  pl.BlockSpec((halo, D), lambda i: (jnp.maximum(i * (tm // halo) - 1, 0), 0)),
                  pl.BlockSpec((tm, D), row),
                  pl.BlockSpec(w_up.shape, const, **single),
                  pl.BlockSpec(conv_w.shape, const, **single),
                  pl.BlockSpec(conv_b.shape, const, **single),
                  pl.BlockSpec(w_down.shape, const, **single),
                  _mod_spec(MOD_GATE2, D),
                  pl.BlockSpec((1, D), const)],
        out_specs=pl.BlockSpec((tm, D), row),
        out_shape=jax.ShapeDtypeStruct((T, D), F32),
        scratch_shapes=[pltpu.VMEM((halo + tm, D), BF16),
                        pltpu.VMEM((2, 2, tf // LANES, halo + tm, LANES), F32),
                        pltpu.VMEM((tm, D_FF), BF16),
                        pltpu.VMEM((D // LANES, tm, LANES), F32)],
        compiler_params=_params("parallel"),
        name="ffn",
    )(h2, h2, x1, w_up, conv_w, conv_b, w_down, mod, final_w)


def kernel(x, c, positions, ada_w, ada_b, norm1_w, w_in, gla_gate_w2, gla_gate_b, gla_norm_w, w_gla_branch,
           w_attn_branch, w_out, norm2_w, w_up, conv_w, conv_b, w_down, final_norm_w):
    B, S, D = x.shape
    T = B * S
    depth = ada_w.shape[0]
    assert depth == 1, "the final norm is fused into the (single) layer's ffn"
    assert all(window // dilation == ATTN_BLK for window, dilation in ATTN_GROUPS)
    cos, sin = _rope_tables(positions)
    x2 = x.reshape(T, D)
    for layer in range(depth):
        mod = _modulation(c, ada_w[layer], ada_b[layer])

        og, (q1, q2, q3, k1, k2, k3, v1, v2, v3, ma, mb) = _in_proj_gla(
            x2, norm1_w[layer].reshape(1, D), mod, cos, sin, w_in[layer], gla_gate_w2[layer],
            gla_gate_b[layer].reshape(1, -1), gla_norm_w[layer].reshape(1, -1), B)
        oa = _dilated_attention((q1, q2, q3), (k1, k2, k3), (v1, v2, v3), B, S)

        x1, h2 = _merge(x2, og, oa, ma, mb, mod, norm2_w[layer].reshape(1, D),
                        w_gla_branch[layer].astype(BF16), w_attn_branch[layer].astype(BF16),
                        w_out[layer].astype(BF16), B)

        x2 = _ffn(h2, x1, w_up[layer].astype(BF16), conv_w[layer], conv_b[layer].reshape(1, -1),
                  w_down[layer].astype(BF16), mod, final_norm_w.reshape(1, D), B)
    return x2.reshape(B, S, D)
```

```python
import functools

import jax
import jax.numpy as jnp
from jax import lax
from jax.experimental import pallas as pl
from jax.experimental.pallas import tpu as pltpu

F32 = jnp.float32
BF16 = jnp.bfloat16

D_MODEL = 1024
GLA_HEADS = 4
GLA_DK = 128
GLA_DV = 256
GLA_LOWRANK = 16
GLA_TAU = 16.0
GLA_QK_WIDTH = GLA_HEADS * GLA_DK
GLA_V_WIDTH = GLA_HEADS * GLA_DV
ATTN_GROUPS = ((128, 1), (512, 4), (2048, 16))
ATTN_HEADS_PER_GROUP = 4
ATTN_HEAD_DIM = 64
ATTN_WIDTH = ATTN_HEADS_PER_GROUP * len(ATTN_GROUPS) * ATTN_HEAD_DIM
ATTN_OUT_WIDTH = ATTN_HEADS_PER_GROUP * ATTN_HEAD_DIM
ROPE_THETA = 10000.0
D_FF = 2816
CONV_WIDTH = 3
EPS = 1e-6
IN_WIDTHS = (GLA_QK_WIDTH, GLA_QK_WIDTH, GLA_V_WIDTH, GLA_V_WIDTH, GLA_LOWRANK,
             ATTN_WIDTH, ATTN_WIDTH, ATTN_WIDTH, D_MODEL, D_MODEL)

LANES = 128
VMEM_LIMIT_BYTES = 56 * 1024 * 1024

TM_IN = 512
GLA_CHUNK = 64
ATTN_BLK = 128
ATTN_SPAN = 2048
ATTN_MERGE_ROWS = 256
TM_MERGE = 1024
MERGE_SPLIT = 2
TM_FFN = 512
FFN_CHUNK = 256
FFN_HALO = 16
FFN_DOWN_EVERY = 2
MASK_VALUE = -1e30


def _params(*sem):
    return pltpu.CompilerParams(dimension_semantics=sem, vmem_limit_bytes=VMEM_LIMIT_BYTES)


def _split_hi_lo(a):
    hi = a.astype(BF16)
    lo = (a - hi.astype(F32)).astype(BF16)
    return hi, lo


def _mod_kernel(c_ref, w_ref, b_ref, o_ref):
    c = c_ref[...]
    s = c * (1.0 / (1.0 + jnp.exp(-c)))
    s_hi, s_lo = _split_hi_lo(s)
    lhs = jnp.concatenate([s_hi, s_lo], axis=0)
    w_hi, w_lo = _split_hi_lo(w_ref[...])
    acc = jnp.dot(lhs, w_hi, preferred_element_type=F32) + jnp.dot(lhs, w_lo, preferred_element_type=F32)
    o_ref[...] = acc[0:8] + acc[8:16] + b_ref[...]


def _modulation(c, ada_w, ada_b):
    B, D = c.shape
    N = ada_w.shape[1]
    tn = 1024
    c8 = jnp.pad(c, ((0, 8 - B), (0, 0)))
    out = pl.pallas_call(
        _mod_kernel,
        grid=(N // tn,),
        in_specs=[pl.BlockSpec((8, D), lambda j: (0, 0)),
                  pl.BlockSpec((D, tn), lambda j: (0, j)),
                  pl.BlockSpec((1, tn), lambda j: (0, j))],
        out_specs=pl.BlockSpec((8, tn), lambda j: (0, j)),
        out_shape=jax.ShapeDtypeStruct((8, N), F32),
        compiler_params=_params("parallel"),
        name="modulation",
    )(c8, ada_w, ada_b.reshape(1, N))
    return out


def _mod_spec(which, D):
    return pl.BlockSpec((8, D), lambda i: (0, which))


MOD_SHIFT1, MOD_SCALE1, MOD_GATE1, MOD_SHIFT2, MOD_SCALE2, MOD_GATE2 = range(6)


def _rope_kernel(pos_ref, invf_ref, cos_ref, sin_ref):
    half = ATTN_HEAD_DIM // 2
    groups = LANES // half
    tr = pos_ref.shape[1]
    pos = jnp.concatenate([pos_ref[...].astype(F32), jnp.zeros((8 - groups, tr), F32)], axis=0)
    pos_t = jnp.transpose(pos)
    lane = lax.broadcasted_iota(jnp.int32, (tr, LANES), 1)
    group = lane // half
    pos_dense = jnp.zeros((tr, LANES), F32)
    for q in range(groups):
        pos_dense = jnp.where(group == q, pos_t[:, q:q + 1], pos_dense)
    ang = pos_dense * invf_ref[...]
    first_half = (lane % ATTN_HEAD_DIM) < half
    for table, out_ref, signed in ((jnp.cos(ang), cos_ref, False), (jnp.sin(ang), sin_ref, True)):
        for q in range(groups):
            only = jnp.where(group == q, table, 0.0)
            spread = only
            for s in range(1, groups):
                spread = spread + pltpu.roll(only, s * half, axis=1)
            out_ref[q] = jnp.where(first_half, -spread, spread) if signed else spread


def _rope_tables(positions):
    T = positions.size
    half = ATTN_HEAD_DIM // 2
    groups = LANES // half
    inv_freq = ROPE_THETA ** (-jnp.arange(half, dtype=F32) / half)
    invf = jnp.tile(inv_freq, groups).reshape(1, LANES)
    per = T // groups
    tr = 512
    cos, sin = pl.pallas_call(
        _rope_kernel,
        grid=(per // tr,),
        in_specs=[pl.BlockSpec((groups, tr), lambda i: (0, i)),
                  pl.BlockSpec((1, LANES), lambda i: (0, 0))],
        out_specs=[pl.BlockSpec((groups, tr, LANES), lambda i: (0, i, 0))] * 2,
        out_shape=[jax.ShapeDtypeStruct((groups, per, LANES), F32)] * 2,
        compiler_params=_params("parallel"),
        name="rope_tables",
    )(positions.reshape(groups, per), invf)
    return cos.reshape(T, LANES), sin.reshape(T, LANES)


def _rmsnorm_mod(x, w, scale, shift):
    ms = jnp.mean(x * x, axis=-1, keepdims=True)
    return (x * lax.rsqrt(ms + EPS) * w) * (1.0 + scale) + shift


def _rotate_half_pairs(t, cos, sin_signed):
    lane = lax.broadcasted_iota(jnp.int32, t.shape, 1)
    first_half = (lane % ATTN_HEAD_DIM) < (ATTN_HEAD_DIM // 2)
    from_right = pltpu.roll(t, LANES - ATTN_HEAD_DIM // 2, axis=1)
    from_left = pltpu.roll(t, ATTN_HEAD_DIM // 2, axis=1)
    return t * cos + jnp.where(first_half, from_right, from_left) * sin_signed


def _store_residue_major(ref, perm_ref, slab, col, dilation):
    if dilation == 1:
        ref[:, col:col + LANES] = slab.astype(ref.dtype)
        return
    perm_ref[...] = slab
    n = slab.shape[0] // dilation
    for p in range(dilation):
        ref[p * n:(p + 1) * n, col:col + LANES] = perm_ref[pl.ds(p, n, stride=dilation), :].astype(ref.dtype)


IN_ALIGNED = sum(IN_WIDTHS[:4])
IN_SHIFTED = sum(IN_WIDTHS[5:])


def _in_proj_gla_kernel(x_ref, n1w_ref, scale_ref, shift_ref, cos_ref, sin_ref, w_ref, w2_ref, b2_ref, nw_ref,
                        og_ref, q1_ref, q2_ref, q3_ref, k1_ref, k2_ref, k3_ref, v1_ref, v2_ref, v3_ref,
                        ma_ref, mb_ref,
                        perm_ref, wsh_ref, wlr_ref, gq_s, gk_s, gv_s, gr_s, la_s, state_ref,
                        *, tiles_per_seq, n_tiles):
    i = pl.program_id(0)

    @pl.when(i == 0)
    def _():
        D = w_ref.shape[0]
        lane = lax.broadcasted_iota(jnp.int32, (D, LANES), 1)
        nblk = IN_SHIFTED // LANES
        first = w_ref[:, IN_ALIGNED:IN_ALIGNED + LANES]
        wlr_ref[...] = jnp.where(lane < GLA_LOWRANK, first, jnp.zeros_like(first))
        rolled = pltpu.roll(first.astype(F32), LANES - GLA_LOWRANK, axis=1)
        for c in range(nblk):
            nxt = w_ref[:, IN_ALIGNED + (c + 1) * LANES:IN_ALIGNED + (c + 2) * LANES].astype(F32)
            rolled_nxt = pltpu.roll(nxt, LANES - GLA_LOWRANK, axis=1)
            wsh_ref[:, c * LANES:(c + 1) * LANES] = jnp.where(
                lane < LANES - GLA_LOWRANK, rolled, rolled_nxt).astype(wsh_ref.dtype)
            rolled = rolled_nxt
        for ref in (gq_s, gk_s, gv_s, gr_s, la_s, state_ref):
            ref[...] = jnp.zeros_like(ref)

    b = jnp.minimum(i, n_tiles - 1) // tiles_per_seq
    h = _rmsnorm_mod(x_ref[...], n1w_ref[...], scale_ref[pl.ds(b, 1), :], shift_ref[pl.ds(b, 1), :]).astype(BF16)

    def proj(col, width):
        return jnp.dot(h, wsh_ref[:, col:col + width], preferred_element_type=F32)

    cos = cos_ref[...]
    sin = sin_ref[...]
    q_scale = ATTN_HEAD_DIM ** -0.5
    dilations = [d for _, d in ATTN_GROUPS]

    def rope_piece(ref, col, scale, dilation):
        def run():
            t = proj(col, ATTN_OUT_WIDTH)
            for s in range(0, ATTN_OUT_WIDTH, LANES):
                rot = _rotate_half_pairs(t[:, s:s + LANES], cos, sin) * scale
                _store_residue_major(ref, perm_ref, rot, s, dilation)
        return run

    def value_piece(ref, col, dilation):
        def run():
            t = proj(col, ATTN_OUT_WIDTH)
            for s in range(0, ATTN_OUT_WIDTH, LANES):
                _store_residue_major(ref, perm_ref, t[:, s:s + LANES], s, dilation)
        return run

    def gate_piece(ref, col, s):
        def run():
            z = proj(col + s, 512)
            ref[:, s:s + 512] = (1.0 / (1.0 + jnp.exp(-z))).astype(ref.dtype)
        return run

    pieces = []
    col = 0
    for refs, scale in (((q1_ref, q2_ref, q3_ref), q_scale), ((k1_ref, k2_ref, k3_ref), 1.0)):
        for g, ref in enumerate(refs):
            pieces.append(rope_piece(ref, col, scale, dilations[g]))
            col += ATTN_OUT_WIDTH
    for g, ref in enumerate((v1_ref, v2_ref, v3_ref)):
        pieces.append(value_piece(ref, col, dilations[g]))
        col += ATTN_OUT_WIDTH
    for ref in (ma_ref, mb_ref):
        for s in range(0, D_MODEL, 512):
            pieces.append(gate_piece(ref, col, s))
        col += D_MODEL

    first_of_seq = (i + tiles_per_seq - 1) % tiles_per_seq == 0
    chunks = _gla_chunks(gq_s, gk_s, gv_s, la_s, gr_s, nw_ref, og_ref, state_ref, first_of_seq)
    for n in range(max(len(pieces), len(chunks))):
        if n < len(pieces):
            pieces[n]()
        if n < len(chunks):
            chunks[n]()

    col = 0
    for ref, width in ((gq_s, GLA_QK_WIDTH), (gk_s, GLA_QK_WIDTH)):
        ref[...] = jnp.dot(h, w_ref[:, col:col + width], preferred_element_type=F32).astype(ref.dtype)
        col += width
    for ref in (gv_s, gr_s):
        for s in range(0, GLA_V_WIDTH, 512):
            ref[:, s:s + 512] = jnp.dot(h, w_ref[:, col + s:col + s + 512],
                                        preferred_element_type=F32).astype(ref.dtype)
        col += GLA_V_WIDTH
    g_lr = jnp.dot(h, wlr_ref[...], preferred_element_type=F32)
    g_hi, g_lo = _split_hi_lo(g_lr)
    w2_hi, w2_lo = _split_hi_lo(w2_ref[...])
    z = (jnp.dot(g_hi, w2_hi, preferred_element_type=F32) + jnp.dot(g_lo, w2_hi, preferred_element_type=F32)
         + jnp.dot(g_hi, w2_lo, preferred_element_type=F32)) + b2_ref[...]
    log_sig = jnp.minimum(z, 0.0) - jnp.log(1.0 + jnp.exp(-jnp.abs(z)))
    la_s[...] = log_sig * (1.0 / GLA_TAU)


def _gla_chunks(q_ref, k_ref, v_ref, la_ref, gr_ref, nw_ref, o_ref, state_ref, first_of_seq):
    C = GLA_CHUNK
    heads = range(GLA_HEADS)
    rows = lax.broadcasted_iota(jnp.int32, (C, C), 0)
    cols = lax.broadcasted_iota(jnp.int32, (C, C), 1)
    causal = cols <= rows
    tri = causal.astype(BF16)
    q_scale = GLA_DK ** -0.5
    nw = jnp.concatenate([nw_ref[...]] * GLA_HEADS, axis=1)
    nt = (((1,), (1,)), ((), ()))
    ks = lambda a, h: a[:, h * GLA_DK:(h + 1) * GLA_DK]
    vs = lambda a, h: a[:, h * GLA_DV:(h + 1) * GLA_DV]

    def run_chunk(c):
        sl = pl.ds(c * C, C)
        q = q_ref[sl, :].astype(F32) * q_scale
        k = k_ref[sl, :].astype(F32)
        v = v_ref[sl, :]
        la_hi, la_lo = _split_hi_lo(la_ref[sl, :])
        cum = jnp.dot(tri, la_hi, preferred_element_type=F32) + jnp.dot(tri, la_lo, preferred_element_type=F32)
        mid = cum[C // 2 - 1:C // 2, :]
        last = cum[C - 1:C, :]
        qg = (q * jnp.exp(cum - mid)).astype(BF16)
        kg = (k * jnp.exp(mid - cum)).astype(BF16)
        q_in = (q * jnp.exp(cum)).astype(BF16)
        k_out = k * jnp.exp(last - cum)
        decay_row = jnp.broadcast_to(jnp.exp(last), (8, GLA_QK_WIDTH))
        g = gr_ref[sl, :].astype(F32)
        gate = nw * (g * (1.0 / (1.0 + jnp.exp(-g))))

        attn = [lax.dot_general(ks(qg, h), ks(kg, h), nt, preferred_element_type=F32) for h in heads]
        attn = [jnp.where(causal, a, 0.0).astype(BF16) for a in attn]
        if c == 0:
            state = [jnp.where(first_of_seq, 0.0, state_ref[h]) for h in heads]
        else:
            state = [state_ref[h] for h in heads]
        o = [jnp.dot(attn[h], vs(v, h), preferred_element_type=F32)
             + jnp.dot(ks(q_in, h), state[h].astype(BF16), preferred_element_type=F32) for h in heads]
        k_out_t = [jnp.transpose(ks(k_out, h)).astype(BF16) for h in heads]
        decay = [jnp.transpose(ks(decay_row, h))[:, 0:1] for h in heads]
        for h in heads:
            state_ref[h] = decay[h] * state[h] + jnp.dot(k_out_t[h], vs(v, h), preferred_element_type=F32)
        ms = [jnp.mean(o[h] * o[h], axis=-1, keepdims=True) for h in heads]
        for h in heads:
            o_ref[sl, h * GLA_DV:(h + 1) * GLA_DV] = (o[h] * lax.rsqrt(ms[h] + EPS) * vs(gate, h)).astype(o_ref.dtype)

    return [functools.partial(run_chunk, c) for c in range(q_ref.shape[0] // C)]


def _in_proj_gla(x2, n1w, mod, cos, sin, w_in, w2, b2, gla_norm_w, B):
    T, D = x2.shape
    tm = TM_IN
    n_tiles = T // tm
    cur = lambda i: (jnp.minimum(i, n_tiles - 1), 0)
    lag = lambda i: (jnp.maximum(i - 1, 0), 0)
    const = lambda i: (0, 0)
    widths = (ATTN_OUT_WIDTH,) * 9 + (D_MODEL, D_MODEL)
    single = dict(pipeline_mode=pl.Buffered(1))
    pad_to = IN_ALIGNED + IN_SHIFTED + LANES
    w2p = jnp.pad(w2, ((0, LANES - GLA_LOWRANK), (0, 0)))
    outs = pl.pallas_call(
        functools.partial(_in_proj_gla_kernel, tiles_per_seq=T // B // tm, n_tiles=n_tiles),
        grid=(n_tiles + 1,),
        in_specs=[pl.BlockSpec((tm, D), cur),
                  pl.BlockSpec((1, D), const),
                  _mod_spec(MOD_SCALE1, D),
                  _mod_spec(MOD_SHIFT1, D),
                  pl.BlockSpec((tm, LANES), cur),
                  pl.BlockSpec((tm, LANES), cur),
                  pl.BlockSpec((D, pad_to), const, **single),
                  pl.BlockSpec(w2p.shape, const, **single),
                  pl.BlockSpec(b2.shape, const, **single),
                  pl.BlockSpec(gla_norm_w.shape, const, **single)],
        out_specs=[pl.BlockSpec((tm, GLA_V_WIDTH), lag)] + [pl.BlockSpec((tm, w), cur) for w in widths],
        out_shape=[jax.ShapeDtypeStruct((T, GLA_V_WIDTH), BF16)]
                  + [jax.ShapeDtypeStruct((T, w), BF16) for w in widths],
        scratch_shapes=[pltpu.VMEM((tm, LANES), F32),
                        pltpu.VMEM((D, IN_SHIFTED), BF16),
                        pltpu.VMEM((D, LANES), BF16),
                        pltpu.VMEM((tm, GLA_QK_WIDTH), BF16),
                        pltpu.VMEM((tm, GLA_QK_WIDTH), BF16),
                        pltpu.VMEM((tm, GLA_V_WIDTH), BF16),
                        pltpu.VMEM((tm, GLA_V_WIDTH), BF16),
                        pltpu.VMEM((tm, GLA_QK_WIDTH), F32),
                        pltpu.VMEM((GLA_HEADS, GLA_DK, GLA_DV), F32)],
        compiler_params=_params("arbitrary"),
        name="in_proj_gla",
    )(x2, n1w, mod, mod, cos, sin, w_in.astype(BF16), w2p, b2, gla_norm_w)
    return outs[0], outs[1:]


def _attn_kernel(q1_ref, q2_ref, q3_ref, k1c_ref, k1p_ref, k2c_ref, k2p_ref, k3c_ref, k3p_ref,
                 v1c_ref, v1p_ref, v2c_ref, v2p_ref, v3c_ref, v3p_ref, o_ref, *slabs):
    blk = ATTN_BLK
    tile = TM_IN
    span = ATTN_SPAN
    tiles = span // tile
    r2, r3 = ATTN_GROUPS[1][1], ATTN_GROUPS[2][1]
    per3 = tile // r3
    not_first_span = pl.program_id(1) > 0
    npair = ATTN_OUT_WIDTH // LANES
    o1_s, o2_s, o3_s, l1_s, l2_s, l3_s = [slabs[n * npair:(n + 1) * npair] for n in range(6)]

    rows = lax.broadcasted_iota(jnp.int32, (blk, 2 * blk), 0)
    cols = lax.broadcasted_iota(jnp.int32, (blk, 2 * blk), 1)
    band = (cols >= rows) & (cols <= rows + blk)
    in_cur = cols >= blk
    left = lax.broadcasted_iota(jnp.int32, (blk, LANES), 1) < ATTN_HEAD_DIM
    ones = jnp.ones((2 * blk, LANES), BF16)
    nt = (((1,), (1,)), ((), ()))

    def pair_attend(q_pair, k_cat, v_cat, has_prev):
        zero = jnp.zeros_like(q_pair)
        q2 = jnp.concatenate([jnp.where(left, q_pair, zero), jnp.where(left, zero, q_pair)], axis=0)
        s = lax.dot_general(q2, k_cat, nt, preferred_element_type=F32)
        valid = band & (in_cur | has_prev)
        s = jnp.where(jnp.concatenate([valid, valid], axis=0), s, MASK_VALUE)
        m = jnp.max(s, axis=-1, keepdims=True)
        p = jnp.exp(s - m).astype(BF16)
        r = jnp.dot(p, jnp.concatenate([v_cat, ones], axis=1), preferred_element_type=F32)
        acc = jnp.where(left, r[:blk, :LANES], r[blk:, :LANES])
        den = jnp.where(left, r[:blk, LANES:], r[blk:, LANES:])
        m_pair = jnp.where(left, jnp.broadcast_to(m[:blk], (blk, LANES)), jnp.broadcast_to(m[blk:], (blk, LANES)))
        return acc / den, m_pair + jnp.log(den)

    def with_prev(cur_ref, prev_ref, r0, back, prev_rows, first, cs):
        before = cur_ref[pl.ds(pl.multiple_of(jnp.maximum(r0 - back, 0), blk), blk), cs]
        if prev_rows is not None:
            before = jnp.where(first, prev_ref[prev_rows, cs], before)
        return jnp.concatenate([before, cur_ref[pl.ds(r0, blk), cs]], axis=0)

    def tile_body(t, carry):
        for p in range(tiles):
            i = t * tiles + p
            r0 = pl.multiple_of(i * blk, blk)
            for j in range(npair):
                cs = slice(j * LANES, (j + 1) * LANES)
                rows1 = slice(0, blk) if p == 0 else None
                o, lse = pair_attend(q1_ref[pl.ds(r0, blk), cs],
                                     with_prev(k1c_ref, k1p_ref, r0, blk, rows1, t == 0, cs),
                                     with_prev(v1c_ref, v1p_ref, r0, blk, rows1, t == 0, cs),
                                     not_first_span | (i > 0))
                o1_s[j][pl.ds(r0, blk), :] = o
                l1_s[j][pl.ds(r0, blk), :] = lse
                rows2 = slice(p * blk, (p + 1) * blk)
                o, lse = pair_attend(q2_ref[pl.ds(r0, blk), cs],
                                     with_prev(k2c_ref, k2p_ref, r0, tile, rows2, t == 0, cs),
                                     with_prev(v2c_ref, v2p_ref, r0, tile, rows2, t == 0, cs),
                                     not_first_span | (t > 0))
                tok = pl.ds(pl.multiple_of(t * tile, tile) + p, blk, stride=r2)
                o2_s[j][tok, :] = o
                l2_s[j][tok, :] = lse
                rr = pl.multiple_of(i * per3, per3)
                gather = lambda ref: jnp.concatenate(
                    [ref[pl.ds(rr + u * tile, per3), cs] for u in range(tiles)], axis=0)
                k_cat = jnp.concatenate([gather(k3p_ref), gather(k3c_ref)], axis=0)
                v_cat = jnp.concatenate([gather(v3p_ref), gather(v3c_ref)], axis=0)
                o, lse = pair_attend(gather(q3_ref), k_cat, v_cat, not_first_span)
                tok = pl.ds(i, blk, stride=r3)
                o3_s[j][tok, :] = o
                l3_s[j][tok, :] = lse
        return carry

    lax.fori_loop(0, tiles, tile_body, 0)

    for j in range(npair):
        for n in range(span // ATTN_MERGE_ROWS):
            rs = slice(n * ATTN_MERGE_ROWS, (n + 1) * ATTN_MERGE_ROWS)
            l1, l2, l3 = l1_s[j][rs, :], l2_s[j][rs, :], l3_s[j][rs, :]
            m = jnp.maximum(jnp.maximum(l1, l2), l3)
            e1, e2, e3 = jnp.exp(l1 - m), jnp.exp(l2 - m), jnp.exp(l3 - m)
            num = e1 * o1_s[j][rs, :] + e2 * o2_s[j][rs, :] + e3 * o3_s[j][rs, :]
            o_ref[rs, j * LANES:(j + 1) * LANES] = (num / (e1 + e2 + e3)).astype(o_ref.dtype)


def _dilated_attention(qs, ks, vs, B, S):
    span = ATTN_SPAN
    nsp = S // span
    gw = ATTN_OUT_WIDTH
    T = B * S
    cur = lambda b, s: (b * nsp + s, 0)

    def prev(rows):
        per = span // rows
        return pl.BlockSpec((rows, gw), lambda b, s: (jnp.maximum((b * nsp + s) * per - 1, 0), 0))

    halos = (ATTN_BLK, TM_IN, span)
    full = pl.BlockSpec((span, gw), cur)
    kv_specs = []
    for h in halos:
        kv_specs += [full, prev(h)]
    kv_args = lambda arrs: [a for arr in arrs for a in (arr, arr)]
    return pl.pallas_call(
        _attn_kernel,
        grid=(B, nsp),
        in_specs=[full] * 3 + kv_specs + kv_specs,
        out_specs=full,
        out_shape=jax.ShapeDtypeStruct((T, gw), BF16),
        scratch_shapes=[pltpu.VMEM((span, LANES), F32)] * (6 * (gw // LANES)),
        compiler_params=_params("parallel", "parallel"),
        name="dilated_attn",
    )(*qs, *kv_args(ks), *kv_args(vs))


def _merge_kernel(x_ref, og_ref, oa_ref, ma_ref, mb_ref,
                  gate_ref, scale_ref, shift_ref, n2w_ref, wg_ref, wa_ref, wo_ref, x1_ref, h2_ref, *, tiles_per_seq):
    b = pl.ds(pl.program_id(0) // tiles_per_seq, 1)
    gate, scale, shift = gate_ref[b, :], scale_ref[b, :], shift_ref[b, :]
    sub = x_ref.shape[0] // MERGE_SPLIT
    for r in range(MERGE_SPLIT):
        rs = slice(r * sub, (r + 1) * sub)
        y_attn = jnp.dot(oa_ref[rs, :], wa_ref[...], preferred_element_type=F32)
        y_gla = jnp.dot(og_ref[rs, :], wg_ref[...], preferred_element_type=F32)
        mixed = (ma_ref[rs, :].astype(F32) * y_gla + mb_ref[rs, :].astype(F32) * y_attn).astype(BF16)
        x1 = x_ref[rs, :] + gate * jnp.dot(mixed, wo_ref[...], preferred_element_type=F32)
        x1_ref[rs, :] = x1
        h2_ref[rs, :] = _rmsnorm_mod(x1, n2w_ref[...], scale, shift).astype(h2_ref.dtype)


def _merge(x2, og, oa, ma, mb, mod, n2w, wg, wa, wo, B):
    T, D = x2.shape
    tm = TM_MERGE
    row = lambda i: (i, 0)
    const = lambda i: (0, 0)
    return pl.pallas_call(
        functools.partial(_merge_kernel, tiles_per_seq=T // B // tm),
        grid=(T // tm,),
        in_specs=[pl.BlockSpec((tm, D), row), pl.BlockSpec((tm, GLA_V_WIDTH), row),
                  pl.BlockSpec((tm, ATTN_OUT_WIDTH), row)]
                 + [pl.BlockSpec((tm, D), row)] * 2
                 + [_mod_spec(MOD_GATE1, D), _mod_spec(MOD_SCALE2, D), _mod_spec(MOD_SHIFT2, D)]
                 + [pl.BlockSpec((1, D), const),
                    pl.BlockSpec(wg.shape, const), pl.BlockSpec(wa.shape, const), pl.BlockSpec(wo.shape, const)],
        out_specs=[pl.BlockSpec((tm, D), row)] * 2,
        out_shape=[jax.ShapeDtypeStruct((T, D), F32), jax.ShapeDtypeStruct((T, D), BF16)],
        compiler_params=_params("parallel"),
        name="merge",
    )(x2, og, oa, ma, mb, mod, mod, mod, n2w, wg, wa, wo)


def _gelu_exact(g):
    return 0.5 * g * (1.0 + lax.erf(g * (2.0 ** -0.5)))


def _ffn_kernel(h_ref, hprev_ref, x1_ref, wup_ref, cw_ref, cb_ref, wd_ref, gate_ref, fw_ref, o_ref,
                hcat_s, u_s, hid_s, order_s, *, tiles_per_seq):
    i = pl.program_id(0)
    tm = h_ref.shape[0]
    halo = hprev_ref.shape[0]
    tf = FFN_CHUNK
    half = tm // 2
    hcat_s[0:halo] = jnp.where(i % tiles_per_seq == 0, jnp.zeros_like(hprev_ref[...]), hprev_ref[...])
    hcat_s[halo:] = h_ref[...]
    nchunk = D_FF // tf
    branches = (0, D_FF)

    def up_project(j):
        for n, off in enumerate(branches):
            u = jnp.dot(hcat_s[...], wup_ref[:, off + j * tf:off + (j + 1) * tf], preferred_element_type=F32)
            for c in range(tf // LANES):
                u_s[j % 2, n, c] = u[:, c * LANES:(c + 1) * LANES]

    def conv(j, n, c, parity):
        u = u_s.at[j % 2, n, c]
        cs = slice(branches[n] + j * tf + c * LANES, branches[n] + j * tf + (c + 1) * LANES)
        tap = lambda back: u[pl.ds(halo + parity - back, half, stride=2), :]
        return cb_ref[:, cs] + cw_ref[0:1, cs] * tap(2) + cw_ref[1:2, cs] * tap(1) + cw_ref[2:3, cs] * tap(0)

    down = None
    group_start = 0
    up_project(0)
    for j in range(nchunk):
        if j + 1 < nchunk:
            up_project(j + 1)
        for c in range(tf // LANES):
            for parity in range(2):
                hidden = _gelu_exact(conv(j, 1, c, parity)) * conv(j, 0, c, parity)
                hid_s[parity * half:(parity + 1) * half, j * tf + c * LANES:j * tf + (c + 1) * LANES] = (
                    hidden.astype(hid_s.dtype))
        if (j + 1) % FFN_DOWN_EVERY == 0 or j + 1 == nchunk:
            ks = slice(group_start * tf, (j + 1) * tf)
            part = jnp.dot(hid_s[:, ks], wd_ref[ks, :], preferred_element_type=F32)
            down = part if down is None else down + part
            group_start = j + 1

    gate = gate_ref[pl.ds(i // tiles_per_seq, 1), :]
    d_model = o_ref.shape[1]
    sq = jnp.zeros((tm, 1), F32)
    for g in range(d_model // LANES):
        cs = slice(g * LANES, (g + 1) * LANES)
        for parity in range(2):
            order_s[g, pl.ds(parity, half, stride=2), :] = down[parity * half:(parity + 1) * half, cs]
        x2 = x1_ref[:, cs] + gate[:, cs] * order_s[g]
        sq = sq + jnp.sum(x2 * x2, axis=-1, keepdims=True)
        o_ref[:, cs] = x2
    o_ref[...] = o_ref[...] * lax.rsqrt(sq * (1.0 / d_model) + EPS) * fw_ref[...]


def _ffn(h2, x1, w_up, conv_w, conv_b, w_down, mod, final_w, B):
    T, D = x1.shape
    tm, tf, halo = TM_FFN, FFN_CHUNK, FFN_HALO
    row = lambda i: (i, 0)
    const = lambda i: (0, 0)
    single = dict(pipeline_mode=pl.Buffered(1))
    return pl.pallas_call(
        functools.partial(_ffn_kernel, tiles_per_seq=T // B // tm),
        grid=(T // tm,),
        in_specs=[pl.BlockSpec((tm, D), row),
                  pl.BlockSpec((halo, D), lambda i: (jnp.maximum(i * (tm // halo) - 1, 0), 0)),
                  pl.BlockSpec((tm, D), row),
                  pl.BlockSpec(w_up.shape, const, **single),
                  pl.BlockSpec(conv_w.shape, const, **single),
                  pl.BlockSpec(conv_b.shape, const, **single),
                  pl.BlockSpec(w_down.shape, const, **single),
                  _mod_spec(MOD_GATE2, D),
                  pl.BlockSpec((1, D), const)],
        out_specs=pl.BlockSpec((tm, D), row),
        out_shape=jax.ShapeDtypeStruct((T, D), F32),
        scratch_shapes=[pltpu.VMEM((halo + tm, D), BF16),
                        pltpu.VMEM((2, 2, tf // LANES, halo + tm, LANES), F32),
                        pltpu.VMEM((tm, D_FF), BF16),
                        pltpu.VMEM((D // LANES, tm, LANES), F32)],
        compiler_params=_params("parallel"),
        name="ffn",
    )(h2, h2, x1, w_up, conv_w, conv_b, w_down, mod, final_w)


def kernel(x, c, positions, ada_w, ada_b, norm1_w, w_in, gla_gate_w2, gla_gate_b, gla_norm_w, w_gla_branch,
           w_attn_branch, w_out, norm2_w, w_up, conv_w, conv_b, w_down, final_norm_w):
    B, S, D = x.shape
    T = B * S
    depth = ada_w.shape[0]
    assert depth == 1, "the final norm is fused into the (single) layer's ffn"
    assert all(window // dilation == ATTN_BLK for window, dilation in ATTN_GROUPS)
    cos, sin = _rope_tables(positions)
    x2 = x.reshape(T, D)
    for layer in range(depth):
        mod = _modulation(c, ada_w[layer], ada_b[layer])

        og, (q1, q2, q3, k1, k2, k3, v1, v2, v3, ma, mb) = _in_proj_gla(
            x2, norm1_w[layer].reshape(1, D), mod, cos, sin, w_in[layer], gla_gate_w2[layer],
            gla_gate_b[layer].reshape(1, -1), gla_norm_w[layer].reshape(1, -1), B)
        oa = _dilated_attention((q1, q2, q3), (k1, k2, k3), (v1, v2, v3), B, S)

        x1, h2 = _merge(x2, og, oa, ma, mb, mod, norm2_w[layer].reshape(1, D),
                        w_gla_branch[layer].astype(BF16), w_attn_branch[layer].astype(BF16),
                        w_out[layer].astype(BF16), B)

        x2 = _ffn(h2, x1, w_up[layer].astype(BF16), conv_w[layer], conv_b[layer].reshape(1, -1),
                  w_down[layer].astype(BF16), mod, final_norm_w.reshape(1, D), B)
    return x2.reshape(B, S, D)
```

```python
import functools

import jax
import jax.numpy as jnp
from jax import lax
from jax.experimental import pallas as pl
from jax.experimental.pallas import tpu as pltpu

F32 = jnp.float32
BF16 = jnp.bfloat16

D_MODEL = 1024
GLA_HEADS = 4
GLA_DK = 128
GLA_DV = 256
GLA_LOWRANK = 16
GLA_TAU = 16.0
GLA_QK_WIDTH = GLA_HEADS * GLA_DK
GLA_V_WIDTH = GLA_HEADS * GLA_DV
ATTN_GROUPS = ((128, 1), (512, 4), (2048, 16))
ATTN_HEADS_PER_GROUP = 4
ATTN_HEAD_DIM = 64
ATTN_WIDTH = ATTN_HEADS_PER_GROUP * len(ATTN_GROUPS) * ATTN_HEAD_DIM
ATTN_OUT_WIDTH = ATTN_HEADS_PER_GROUP * ATTN_HEAD_DIM
ROPE_THETA = 10000.0
D_FF = 2816
CONV_WIDTH = 3
EPS = 1e-6
IN_WIDTHS = (GLA_QK_WIDTH, GLA_QK_WIDTH, GLA_V_WIDTH, GLA_V_WIDTH, GLA_LOWRANK,
             ATTN_WIDTH, ATTN_WIDTH, ATTN_WIDTH, D_MODEL, D_MODEL)

LANES = 128
VMEM_LIMIT_BYTES = 56 * 1024 * 1024

TM_IN = 512
GLA_CHUNK = 64
ATTN_BLK = 128
ATTN_SPAN = 2048
ATTN_MERGE_ROWS = 256
TM_MERGE = 1024
MERGE_SPLIT = 2
TM_FFN = 512
FFN_CHUNK = 256
FFN_HALO = 16
FFN_DOWN_SLAB = 256
MASK_VALUE = -1e30


def _params(*sem):
    return pltpu.CompilerParams(dimension_semantics=sem, vmem_limit_bytes=VMEM_LIMIT_BYTES)


def _split_hi_lo(a):
    hi = a.astype(BF16)
    lo = (a - hi.astype(F32)).astype(BF16)
    return hi, lo


def _mod_kernel(c_ref, w_ref, b_ref, o_ref):
    c = c_ref[...]
    s = c * (1.0 / (1.0 + jnp.exp(-c)))
    s_hi, s_lo = _split_hi_lo(s)
    lhs = jnp.concatenate([s_hi, s_lo], axis=0)
    w_hi, w_lo = _split_hi_lo(w_ref[...])
    acc = jnp.dot(lhs, w_hi, preferred_element_type=F32) + jnp.dot(lhs, w_lo, preferred_element_type=F32)
    o_ref[...] = acc[0:8] + acc[8:16] + b_ref[...]


def _modulation(c, ada_w, ada_b):
    B, D = c.shape
    N = ada_w.shape[1]
    tn = 1024
    c8 = jnp.pad(c, ((0, 8 - B), (0, 0)))
    out = pl.pallas_call(
        _mod_kernel,
        grid=(N // tn,),
        in_specs=[pl.BlockSpec((8, D), lambda j: (0, 0)),
                  pl.BlockSpec((D, tn), lambda j: (0, j)),
                  pl.BlockSpec((1, tn), lambda j: (0, j))],
        out_specs=pl.BlockSpec((8, tn), lambda j: (0, j)),
        out_shape=jax.ShapeDtypeStruct((8, N), F32),
        compiler_params=_params("parallel"),
        name="modulation",
    )(c8, ada_w, ada_b.reshape(1, N))
    return out


def _mod_spec(which, D):
    return pl.BlockSpec((8, D), lambda i: (0, which))


MOD_SHIFT1, MOD_SCALE1, MOD_GATE1, MOD_SHIFT2, MOD_SCALE2, MOD_GATE2 = range(6)


def _rope_kernel(pos_ref, invf_ref, cos_ref, sin_ref):
    half = ATTN_HEAD_DIM // 2
    groups = LANES // half
    tr = pos_ref.shape[1]
    pos = jnp.concatenate([pos_ref[...].astype(F32), jnp.zeros((8 - groups, tr), F32)], axis=0)
    pos_t = jnp.transpose(pos)
    lane = lax.broadcasted_iota(jnp.int32, (tr, LANES), 1)
    group = lane // half
    pos_dense = jnp.zeros((tr, LANES), F32)
    for q in range(groups):
        pos_dense = jnp.where(group == q, pos_t[:, q:q + 1], pos_dense)
    ang = pos_dense * invf_ref[...]
    first_half = (lane % ATTN_HEAD_DIM) < half
    for table, out_ref, signed in ((jnp.cos(ang), cos_ref, False), (jnp.sin(ang), sin_ref, True)):
        for q in range(groups):
            only = jnp.where(group == q, table, 0.0)
            spread = only
            for s in range(1, groups):
                spread = spread + pltpu.roll(only, s * half, axis=1)
            out_ref[q] = jnp.where(first_half, -spread, spread) if signed else spread


def _rope_tables(positions):
    T = positions.size
    half = ATTN_HEAD_DIM // 2
    groups = LANES // half
    inv_freq = ROPE_THETA ** (-jnp.arange(half, dtype=F32) / half)
    invf = jnp.tile(inv_freq, groups).reshape(1, LANES)
    per = T // groups
    tr = 512
    cos, sin = pl.pallas_call(
        _rope_kernel,
        grid=(per // tr,),
        in_specs=[pl.BlockSpec((groups, tr), lambda i: (0, i)),
                  pl.BlockSpec((1, LANES), lambda i: (0, 0))],
        out_specs=[pl.BlockSpec((groups, tr, LANES), lambda i: (0, i, 0))] * 2,
        out_shape=[jax.ShapeDtypeStruct((groups, per, LANES), F32)] * 2,
        compiler_params=_params("parallel"),
        name="rope_tables",
    )(positions.reshape(groups, per), invf)
    return cos.reshape(T, LANES), sin.reshape(T, LANES)


def _rmsnorm_mod(x, w, scale, shift):
    ms = jnp.mean(x * x, axis=-1, keepdims=True)
    return (x * lax.rsqrt(ms + EPS) * w) * (1.0 + scale) + shift


def _rotate_half_pairs(t, cos, sin_signed):
    lane = lax.broadcasted_iota(jnp.int32, t.shape, 1)
    first_half = (lane % ATTN_HEAD_DIM) < (ATTN_HEAD_DIM // 2)
    from_right = pltpu.roll(t, LANES - ATTN_HEAD_DIM // 2, axis=1)
    from_left = pltpu.roll(t, ATTN_HEAD_DIM // 2, axis=1)
    return t * cos + jnp.where(first_half, from_right, from_left) * sin_signed


def _store_residue_major(ref, perm_ref, slab, col, dilation):
    if dilation == 1:
        ref[:, col:col + LANES] = slab.astype(ref.dtype)
        return
    perm_ref[...] = slab
    n = slab.shape[0] // dilation
    for p in range(dilation):
        ref[p * n:(p + 1) * n, col:col + LANES] = perm_ref[pl.ds(p, n, stride=dilation), :].astype(ref.dtype)


IN_ALIGNED = sum(IN_WIDTHS[:4])
IN_SHIFTED = sum(IN_WIDTHS[5:])


def _in_proj_gla_kernel(x_ref, n1w_ref, scale_ref, shift_ref, cos_ref, sin_ref, w_ref, w2_ref, b2_ref, nw_ref,
                        og_ref, q1_ref, q2_ref, q3_ref, k1_ref, k2_ref, k3_ref, v1_ref, v2_ref, v3_ref,
                        ma_ref, mb_ref,
                        perm_ref, wsh_ref, wlr_ref, gq_s, gk_s, gv_s, gr_s, la_s, state_ref,
                        *, tiles_per_seq, n_tiles):
    i = pl.program_id(0)

    @pl.when(i == 0)
    def _():
        D = w_ref.shape[0]
        lane = lax.broadcasted_iota(jnp.int32, (D, LANES), 1)
        nblk = IN_SHIFTED // LANES
        first = w_ref[:, IN_ALIGNED:IN_ALIGNED + LANES]
        wlr_ref[...] = jnp.where(lane < GLA_LOWRANK, first, jnp.zeros_like(first))
        rolled = pltpu.roll(first.astype(F32), LANES - GLA_LOWRANK, axis=1)
        for c in range(nblk):
            nxt = w_ref[:, IN_ALIGNED + (c + 1) * LANES:IN_ALIGNED + (c + 2) * LANES].astype(F32)
            rolled_nxt = pltpu.roll(nxt, LANES - GLA_LOWRANK, axis=1)
            wsh_ref[:, c * LANES:(c + 1) * LANES] = jnp.where(
                lane < LANES - GLA_LOWRANK, rolled, rolled_nxt).astype(wsh_ref.dtype)
            rolled = rolled_nxt
        for ref in (gq_s, gk_s, gv_s, gr_s, la_s, state_ref):
            ref[...] = jnp.zeros_like(ref)

    first_of_seq = (i + tiles_per_seq - 1) % tiles_per_seq == 0
    chunks = _gla_chunks(gq_s, gk_s, gv_s, la_s, gr_s, nw_ref, og_ref, state_ref, first_of_seq)

    @pl.when(i == n_tiles)
    def _():
        for chunk in chunks:
            chunk()

    @pl.when(i < n_tiles)
    def _():
        b = i // tiles_per_seq
        h = _rmsnorm_mod(x_ref[...], n1w_ref[...], scale_ref[pl.ds(b, 1), :], shift_ref[pl.ds(b, 1), :]).astype(BF16)

        def proj(col, width):
            return jnp.dot(h, wsh_ref[:, col:col + width], preferred_element_type=F32)

        cos = cos_ref[...]
        sin = sin_ref[...]
        q_scale = ATTN_HEAD_DIM ** -0.5
        dilations = [d for _, d in ATTN_GROUPS]

        def rope_piece(ref, col, scale, dilation):
            def run():
                t = proj(col, ATTN_OUT_WIDTH)
                for s in range(0, ATTN_OUT_WIDTH, LANES):
                    rot = _rotate_half_pairs(t[:, s:s + LANES], cos, sin) * scale
                    _store_residue_major(ref, perm_ref, rot, s, dilation)
            return run

        def value_piece(ref, col, dilation):
            def run():
                t = proj(col, ATTN_OUT_WIDTH)
                for s in range(0, ATTN_OUT_WIDTH, LANES):
                    _store_residue_major(ref, perm_ref, t[:, s:s + LANES], s, dilation)
            return run

        def gate_piece(ref, col, s):
            def run():
                z = proj(col + s, 512)
                ref[:, s:s + 512] = (1.0 / (1.0 + jnp.exp(-z))).astype(ref.dtype)
            return run

        pieces = []
        col = 0
        for refs, scale in (((q1_ref, q2_ref, q3_ref), q_scale), ((k1_ref, k2_ref, k3_ref), 1.0)):
            for g, ref in enumerate(refs):
                pieces.append(rope_piece(ref, col, scale, dilations[g]))
                col += ATTN_OUT_WIDTH
        for g, ref in enumerate((v1_ref, v2_ref, v3_ref)):
            pieces.append(value_piece(ref, col, dilations[g]))
            col += ATTN_OUT_WIDTH
        for ref in (ma_ref, mb_ref):
            for s in range(0, D_MODEL, 512):
                pieces.append(gate_piece(ref, col, s))
            col += D_MODEL

        for n in range(max(len(pieces), len(chunks))):
            if n < len(pieces):
                pieces[n]()
            if n < len(chunks):
                chunks[n]()

        col = 0
        for ref, width in ((gq_s, GLA_QK_WIDTH), (gk_s, GLA_QK_WIDTH)):
            ref[...] = jnp.dot(h, w_ref[:, col:col + width], preferred_element_type=F32).astype(ref.dtype)
            col += width
        for ref in (gv_s, gr_s):
            for s in range(0, GLA_V_WIDTH, 512):
                ref[:, s:s + 512] = jnp.dot(h, w_ref[:, col + s:col + s + 512],
                                            preferred_element_type=F32).astype(ref.dtype)
            col += GLA_V_WIDTH
        g_lr = jnp.dot(h, wlr_ref[...], preferred_element_type=F32)
        g_hi, g_lo = _split_hi_lo(g_lr)
        w2_hi, w2_lo = _split_hi_lo(w2_ref[...])
        z = (jnp.dot(g_hi, w2_hi, preferred_element_type=F32) + jnp.dot(g_lo, w2_hi, preferred_element_type=F32)
             + jnp.dot(g_hi, w2_lo, preferred_element_type=F32)) + b2_ref[...]
        log_sig = jnp.minimum(z, 0.0) - jnp.log(1.0 + jnp.exp(-jnp.abs(z)))
        la_s[...] = log_sig * (1.0 / GLA_TAU)


def _gla_chunks(q_ref, k_ref, v_ref, la_ref, gr_ref, nw_ref, o_ref, state_ref, first_of_seq):
    C = GLA_CHUNK
    heads = range(GLA_HEADS)
    rows = lax.broadcasted_iota(jnp.int32, (C, C), 0)
    cols = lax.broadcasted_iota(jnp.int32, (C, C), 1)
    causal = cols <= rows
    tri = causal.astype(BF16)
    q_scale = GLA_DK ** -0.5
    nw = jnp.concatenate([nw_ref[...]] * GLA_HEADS, axis=1)
    nt = (((1,), (1,)), ((), ()))
    ks = lambda a, h: a[:, h * GLA_DK:(h + 1) * GLA_DK]
    vs = lambda a, h: a[:, h * GLA_DV:(h + 1) * GLA_DV]

    def run_chunk(c):
        sl = pl.ds(c * C, C)
        q = q_ref[sl, :].astype(F32) * q_scale
        k = k_ref[sl, :].astype(F32)
        v = v_ref[sl, :]
        la_hi, la_lo = _split_hi_lo(la_ref[sl, :])
        cum = jnp.dot(tri, la_hi, preferred_element_type=F32) + jnp.dot(tri, la_lo, preferred_element_type=F32)
        mid = cum[C // 2 - 1:C // 2, :]
        last = cum[C - 1:C, :]
        qg = (q * jnp.exp(cum - mid)).astype(BF16)
        kg = (k * jnp.exp(mid - cum)).astype(BF16)
        q_in = (q * jnp.exp(cum)).astype(BF16)
        k_out = k * jnp.exp(last - cum)
        decay_row = jnp.broadcast_to(jnp.exp(last), (8, GLA_QK_WIDTH))
        g = gr_ref[sl, :].astype(F32)
        gate = nw * (g * (1.0 / (1.0 + jnp.exp(-g))))

        attn = [lax.dot_general(ks(qg, h), ks(kg, h), nt, preferred_element_type=F32) for h in heads]
        attn = [jnp.where(causal, a, 0.0).astype(BF16) for a in attn]
        if c == 0:
            state = [jnp.where(first_of_seq, 0.0, state_ref[h]) for h in heads]
        else:
            state = [state_ref[h] for h in heads]
        o = [jnp.dot(attn[h], vs(v, h), preferred_element_type=F32)
             + jnp.dot(ks(q_in, h), state[h].astype(BF16), preferred_element_type=F32) for h in heads]
        k_out_t = [jnp.transpose(ks(k_out, h)).astype(BF16) for h in heads]
        decay = [jnp.transpose(ks(decay_row, h))[:, 0:1] for h in heads]
        for h in heads:
            state_ref[h] = decay[h] * state[h] + jnp.dot(k_out_t[h], vs(v, h), preferred_element_type=F32)
        ms = [jnp.mean(o[h] * o[h], axis=-1, keepdims=True) for h in heads]
        for h in heads:
            o_ref[sl, h * GLA_DV:(h + 1) * GLA_DV] = (o[h] * lax.rsqrt(ms[h] + EPS) * vs(gate, h)).astype(o_ref.dtype)

    return [functools.partial(run_chunk, c) for c in range(q_ref.shape[0] // C)]


def _in_proj_gla(x2, n1w, mod, cos, sin, w_in, w2, b2, gla_norm_w, B):
    T, D = x2.shape
    tm = TM_IN
    n_tiles = T // tm
    cur = lambda i: (jnp.minimum(i, n_tiles - 1), 0)
    lag = lambda i: (jnp.maximum(i - 1, 0), 0)
    const = lambda i: (0, 0)
    widths = (ATTN_OUT_WIDTH,) * 9 + (D_MODEL, D_MODEL)
    single = dict(pipeline_mode=pl.Buffered(1))
    pad_to = IN_ALIGNED + IN_SHIFTED + LANES
    w2p = jnp.pad(w2, ((0, LANES - GLA_LOWRANK), (0, 0)))
    outs = pl.pallas_call(
        functools.partial(_in_proj_gla_kernel, tiles_per_seq=T // B // tm, n_tiles=n_tiles),
        grid=(n_tiles + 1,),
        in_specs=[pl.BlockSpec((tm, D), cur),
                  pl.BlockSpec((1, D), const),
                  _mod_spec(MOD_SCALE1, D),
                  _mod_spec(MOD_SHIFT1, D),
                  pl.BlockSpec((tm, LANES), cur),
                  pl.BlockSpec((tm, LANES), cur),
                  pl.BlockSpec((D, pad_to), const, **single),
                  pl.BlockSpec(w2p.shape, const, **single),
                  pl.BlockSpec(b2.shape, const, **single),
                  pl.BlockSpec(gla_norm_w.shape, const, **single)],
        out_specs=[pl.BlockSpec((tm, GLA_V_WIDTH), lag)] + [pl.BlockSpec((tm, w), cur) for w in widths],
        out_shape=[jax.ShapeDtypeStruct((T, GLA_V_WIDTH), BF16)]
                  + [jax.ShapeDtypeStruct((T, w), BF16) for w in widths],
        scratch_shapes=[pltpu.VMEM((tm, LANES), F32),
                        pltpu.VMEM((D, IN_SHIFTED), BF16),
                        pltpu.VMEM((D, LANES), BF16),
                        pltpu.VMEM((tm, GLA_QK_WIDTH), BF16),
                        pltpu.VMEM((tm, GLA_QK_WIDTH), BF16),
                        pltpu.VMEM((tm, GLA_V_WIDTH), BF16),
                        pltpu.VMEM((tm, GLA_V_WIDTH), BF16),
                        pltpu.VMEM((tm, GLA_QK_WIDTH), F32),
                        pltpu.VMEM((GLA_HEADS, GLA_DK, GLA_DV), F32)],
        compiler_params=_params("arbitrary"),
        name="in_proj_gla",
    )(x2, n1w, mod, mod, cos, sin, w_in.astype(BF16), w2p, b2, gla_norm_w)
    return outs[0], outs[1:]


def _attn_kernel(q1_ref, q2_ref, q3_ref, k1c_ref, k1p_ref, k2c_ref, k2p_ref, k3c_ref, k3p_ref,
                 v1c_ref, v1p_ref, v2c_ref, v2p_ref, v3c_ref, v3p_ref, o_ref, *slabs):
    blk = ATTN_BLK
    tile = TM_IN
    span = ATTN_SPAN
    tiles = span // tile
    r2, r3 = ATTN_GROUPS[1][1], ATTN_GROUPS[2][1]
    per3 = tile // r3
    not_first_span = pl.program_id(1) > 0
    npair = ATTN_OUT_WIDTH // LANES
    o1_s, o2_s, o3_s, l1_s, l2_s, l3_s = [slabs[n * npair:(n + 1) * npair] for n in range(6)]

    rows = lax.broadcasted_iota(jnp.int32, (blk, 2 * blk), 0)
    cols = lax.broadcasted_iota(jnp.int32, (blk, 2 * blk), 1)
    band = (cols >= rows) & (cols <= rows + blk)
    in_cur = cols >= blk
    left = lax.broadcasted_iota(jnp.int32, (blk, LANES), 1) < ATTN_HEAD_DIM
    ones = jnp.ones((2 * blk, LANES), BF16)
    nt = (((1,), (1,)), ((), ()))

    def pair_attend(q_pair, k_cat, v_cat, has_prev):
        zero = jnp.zeros_like(q_pair)
        q2 = jnp.concatenate([jnp.where(left, q_pair, zero), jnp.where(left, zero, q_pair)], axis=0)
        s = lax.dot_general(q2, k_cat, nt, preferred_element_type=F32)
        valid = band & (in_cur | has_prev)
        s = jnp.where(jnp.concatenate([valid, valid], axis=0), s, MASK_VALUE)
        m = jnp.max(s, axis=-1, keepdims=True)
        p = jnp.exp(s - m).astype(BF16)
        r = jnp.dot(p, jnp.concatenate([v_cat, ones], axis=1), preferred_element_type=F32)
        acc = jnp.where(left, r[:blk, :LANES], r[blk:, :LANES])
        den = jnp.where(left, r[:blk, LANES:], r[blk:, LANES:])
        m_pair = jnp.where(left, jnp.broadcast_to(m[:blk], (blk, LANES)), jnp.broadcast_to(m[blk:], (blk, LANES)))
        return acc / den, m_pair + jnp.log(den)

    def with_prev(cur_ref, prev_ref, r0, back, prev_rows, first, cs):
        before = cur_ref[pl.ds(pl.multiple_of(jnp.maximum(r0 - back, 0), blk), blk), cs]
        if prev_rows is not None:
            before = jnp.where(first, prev_ref[prev_rows, cs], before)
        return jnp.concatenate([before, cur_ref[pl.ds(r0, blk), cs]], axis=0)

    def tile_body(t, carry):
        for p in range(tiles):
            i = t * tiles + p
            r0 = pl.multiple_of(i * blk, blk)
            for j in range(npair):
                cs = slice(j * LANES, (j + 1) * LANES)
                rows1 = slice(0, blk) if p == 0 else None
                o, lse = pair_attend(q1_ref[pl.ds(r0, blk), cs],
                                     with_prev(k1c_ref, k1p_ref, r0, blk, rows1, t == 0, cs),
                                     with_prev(v1c_ref, v1p_ref, r0, blk, rows1, t == 0, cs),
                                     not_first_span | (i > 0))
                o1_s[j][pl.ds(r0, blk), :] = o
                l1_s[j][pl.ds(r0, blk), :] = lse
                rows2 = slice(p * blk, (p + 1) * blk)
                o, lse = pair_attend(q2_ref[pl.ds(r0, blk), cs],
                                     with_prev(k2c_ref, k2p_ref, r0, tile, rows2, t == 0, cs),
                                     with_prev(v2c_ref, v2p_ref, r0, tile, rows2, t == 0, cs),
                                     not_first_span | (t > 0))
                tok = pl.ds(pl.multiple_of(t * tile, tile) + p, blk, stride=r2)
                o2_s[j][tok, :] = o
                l2_s[j][tok, :] = lse
                rr = pl.multiple_of(i * per3, per3)
                gather = lambda ref: jnp.concatenate(
                    [ref[pl.ds(rr + u * tile, per3), cs] for u in range(tiles)], axis=0)
                k_cat = jnp.concatenate([gather(k3p_ref), gather(k3c_ref)], axis=0)
                v_cat = jnp.concatenate([gather(v3p_ref), gather(v3c_ref)], axis=0)
                o, lse = pair_attend(gather(q3_ref), k_cat, v_cat, not_first_span)
                tok = pl.ds(i, blk, stride=r3)
                o3_s[j][tok, :] = o
                l3_s[j][tok, :] = lse
        return carry

    lax.fori_loop(0, tiles, tile_body, 0)

    for j in range(npair):
        for n in range(span // ATTN_MERGE_ROWS):
            rs = slice(n * ATTN_MERGE_ROWS, (n + 1) * ATTN_MERGE_ROWS)
            l1, l2, l3 = l1_s[j][rs, :], l2_s[j][rs, :], l3_s[j][rs, :]
            m = jnp.maximum(jnp.maximum(l1, l2), l3)
            e1, e2, e3 = jnp.exp(l1 - m), jnp.exp(l2 - m), jnp.exp(l3 - m)
            num = e1 * o1_s[j][rs, :] + e2 * o2_s[j][rs, :] + e3 * o3_s[j][rs, :]
            o_ref[rs, j * LANES:(j + 1) * LANES] = (num / (e1 + e2 + e3)).astype(o_ref.dtype)


def _dilated_attention(qs, ks, vs, B, S):
    span = ATTN_SPAN
    nsp = S // span
    gw = ATTN_OUT_WIDTH
    T = B * S
    cur = lambda b, s: (b * nsp + s, 0)

    def prev(rows):
        per = span // rows
        return pl.BlockSpec((rows, gw), lambda b, s: (jnp.maximum((b * nsp + s) * per - 1, 0), 0))

    halos = (ATTN_BLK, TM_IN, span)
    full = pl.BlockSpec((span, gw), cur)
    kv_specs = []
    for h in halos:
        kv_specs += [full, prev(h)]
    kv_args = lambda arrs: [a for arr in arrs for a in (arr, arr)]
    return pl.pallas_call(
        _attn_kernel,
        grid=(B, nsp),
        in_specs=[full] * 3 + kv_specs + kv_specs,
        out_specs=full,
        out_shape=jax.ShapeDtypeStruct((T, gw), BF16),
        scratch_shapes=[pltpu.VMEM((span, LANES), F32)] * (6 * (gw // LANES)),
        compiler_params=_params("parallel", "parallel"),
        name="dilated_attn",
    )(*qs, *kv_args(ks), *kv_args(vs))


def _merge_kernel(x_ref, og_ref, oa_ref, ma_ref, mb_ref,
                  gate_ref, scale_ref, shift_ref, n2w_ref, wg_ref, wa_ref, wo_ref, x1_ref, h2_ref, *, tiles_per_seq):
    b = pl.ds(pl.program_id(0) // tiles_per_seq, 1)
    gate, scale, shift = gate_ref[b, :], scale_ref[b, :], shift_ref[b, :]
    sub = x_ref.shape[0] // MERGE_SPLIT
    for r in range(MERGE_SPLIT):
        rs = slice(r * sub, (r + 1) * sub)
        y_attn = jnp.dot(oa_ref[rs, :], wa_ref[...], preferred_element_type=F32)
        y_gla = jnp.dot(og_ref[rs, :], wg_ref[...], preferred_element_type=F32)
        mixed = (ma_ref[rs, :].astype(F32) * y_gla + mb_ref[rs, :].astype(F32) * y_attn).astype(BF16)
        x1 = x_ref[rs, :] + gate * jnp.dot(mixed, wo_ref[...], preferred_element_type=F32)
        x1_ref[rs, :] = x1
        h2_ref[rs, :] = _rmsnorm_mod(x1, n2w_ref[...], scale, shift).astype(h2_ref.dtype)


def _merge(x2, og, oa, ma, mb, mod, n2w, wg, wa, wo, B):
    T, D = x2.shape
    tm = TM_MERGE
    row = lambda i: (i, 0)
    const = lambda i: (0, 0)
    return pl.pallas_call(
        functools.partial(_merge_kernel, tiles_per_seq=T // B // tm),
        grid=(T // tm,),
        in_specs=[pl.BlockSpec((tm, D), row), pl.BlockSpec((tm, GLA_V_WIDTH), row),
                  pl.BlockSpec((tm, ATTN_OUT_WIDTH), row)]
                 + [pl.BlockSpec((tm, D), row)] * 2
                 + [_mod_spec(MOD_GATE1, D), _mod_spec(MOD_SCALE2, D), _mod_spec(MOD_SHIFT2, D)]
                 + [pl.BlockSpec((1, D), const),
                    pl.BlockSpec(wg.shape, const), pl.BlockSpec(wa.shape, const), pl.BlockSpec(wo.shape, const)],
        out_specs=[pl.BlockSpec((tm, D), row)] * 2,
        out_shape=[jax.ShapeDtypeStruct((T, D), F32), jax.ShapeDtypeStruct((T, D), BF16)],
        compiler_params=_params("parallel"),
        name="merge",
    )(x2, og, oa, ma, mb, mod, mod, mod, n2w, wg, wa, wo)


def _gelu_exact(g):
    return 0.5 * g * (1.0 + lax.erf(g * (2.0 ** -0.5)))


def _ffn_kernel(h_ref, hprev_ref, x1_ref, wup_ref, cw_ref, cb_ref, wd_ref, gate_ref, fw_ref, o_ref,
                hcat_s, u_s, hid_s, order_s, *, tiles_per_seq):
    i = pl.program_id(0)
    tm = h_ref.shape[0]
    halo = hprev_ref.shape[0]
    tf = FFN_CHUNK
    half = tm // 2
    hcat_s[0:halo] = jnp.where(i % tiles_per_seq == 0, jnp.zeros_like(hprev_ref[...]), hprev_ref[...])
    hcat_s[halo:] = h_ref[...]
    nchunk = D_FF // tf
    branches = (0, D_FF)

    def up_project(j):
        for n, off in enumerate(branches):
            u = jnp.dot(hcat_s[...], wup_ref[:, off + j * tf:off + (j + 1) * tf], preferred_element_type=F32)
            for c in range(tf // LANES):
                u_s[j % 2, n, c] = u[:, c * LANES:(c + 1) * LANES]

    def conv(j, n, c, parity):
        u = u_s.at[j % 2, n, c]
        cs = slice(branches[n] + j * tf + c * LANES, branches[n] + j * tf + (c + 1) * LANES)
        tap = lambda back: u[pl.ds(halo + parity - back, half, stride=2), :]
        return cb_ref[:, cs] + cw_ref[0:1, cs] * tap(2) + cw_ref[1:2, cs] * tap(1) + cw_ref[2:3, cs] * tap(0)

    up_project(0)
    for j in range(nchunk):
        if j + 1 < nchunk:
            up_project(j + 1)
        for c in range(tf // LANES):
            for parity in range(2):
                hidden = _gelu_exact(conv(j, 1, c, parity)) * conv(j, 0, c, parity)
                hid_s[parity * half:(parity + 1) * half, j * tf + c * LANES:j * tf + (c + 1) * LANES] = (
                    hidden.astype(hid_s.dtype))

    gate = gate_ref[pl.ds(i // tiles_per_seq, 1), :]
    d_model = o_ref.shape[1]
    sq = jnp.zeros((tm, 1), F32)
    for n in range(d_model // FFN_DOWN_SLAB):
        ns = slice(n * FFN_DOWN_SLAB, (n + 1) * FFN_DOWN_SLAB)
        down = jnp.dot(hid_s[...], wd_ref[:, ns], preferred_element_type=F32)
        for c in range(FFN_DOWN_SLAB // LANES):
            g = n * (FFN_DOWN_SLAB // LANES) + c
            cs = slice(g * LANES, (g + 1) * LANES)
            for parity in range(2):
                order_s[g, pl.ds(parity, half, stride=2), :] = (
                    down[parity * half:(parity + 1) * half, c * LANES:(c + 1) * LANES])
            x2 = x1_ref[:, cs] + gate[:, cs] * order_s[g]
            sq = sq + jnp.sum(x2 * x2, axis=-1, keepdims=True)
            o_ref[:, cs] = x2
    o_ref[...] = o_ref[...] * lax.rsqrt(sq * (1.0 / d_model) + EPS) * fw_ref[...]


def _ffn(h2, x1, w_up, conv_w, conv_b, w_down, mod, final_w, B):
    T, D = x1.shape
    tm, tf, halo = TM_FFN, FFN_CHUNK, FFN_HALO
    row = lambda i: (i, 0)
    const = lambda i: (0, 0)
    single = dict(pipeline_mode=pl.Buffered(1))
    return pl.pallas_call(
        functools.partial(_ffn_kernel, tiles_per_seq=T // B // tm),
        grid=(T // tm,),
        in_specs=[pl.BlockSpec((tm, D), row),
                  pl.BlockSpec((halo, D), lambda i: (jnp.maximum(i * (tm // halo) - 1, 0), 0)),
                  pl.BlockSpec((tm, D), row),
                  pl.BlockSpec(w_up.shape, const, **single),
                  pl.BlockSpec(conv_w.shape, const, **single),
                  pl.BlockSpec(conv_b.shape, const, **single),
                  pl.BlockSpec(w_down.shape, const, **single),
                  _mod_spec(MOD_GATE2, D),
                  pl.BlockSpec((1, D), const)],
        out_specs=pl.BlockSpec((tm, D), row),
        out_shape=jax.ShapeDtypeStruct((T, D), F32),
        scratch_shapes=[pltpu.VMEM((halo + tm, D), BF16),
                        pltpu.VMEM((2, 2, tf // LANES, halo + tm, LANES), F32),
                        pltpu.VMEM((tm, D_FF), BF16),
                        pltpu.VMEM((D // LANES, tm, LANES), F32)],
        compiler_params=_params("parallel"),
        name="ffn",
    )(h2, h2, x1, w_up, conv_w, conv_b, w_down, mod, final_w)


def kernel(x, c, positions, ada_w, ada_b, norm1_w, w_in, gla_gate_w2, gla_gate_b, gla_norm_w, w_gla_branch,
           w_attn_branch, w_out, norm2_w, w_up, conv_w, conv_b, w_down, final_norm_w):
    B, S, D = x.shape
    T = B * S
    depth = ada_w.shape[0]
    assert depth == 1, "the final norm is fused into the (single) layer's ffn"
    assert all(window // dilation == ATTN_BLK for window, dilation in ATTN_GROUPS)
    cos, sin = _rope_tables(positions)
    x2 = x.reshape(T, D)
    for layer in range(depth):
        mod = _modulation(c, ada_w[layer], ada_b[layer])

        og, (q1, q2, q3, k1, k2, k3, v1, v2, v3, ma, mb) = _in_proj_gla(
            x2, norm1_w[layer].reshape(1, D), mod, cos, sin, w_in[layer], gla_gate_w2[layer],
            gla_gate_b[layer].reshape(1, -1), gla_norm_w[layer].reshape(1, -1), B)
        oa = _dilated_attention((q1, q2, q3), (k1, k2, k3), (v1, v2, v3), B, S)

        x1, h2 = _merge(x2, og, oa, ma, mb, mod, norm2_w[layer].reshape(1, D),
                        w_gla_branch[layer].astype(BF16), w_attn_branch[layer].astype(BF16),
                        w_out[layer].astype(BF16), B)

        x2 = _ffn(h2, x1, w_up[layer].astype(BF16), conv_w[layer], conv_b[layer].reshape(1, -1),
                  w_down[layer].astype(BF16), mod, final_norm_w.reshape(1, D), B)
    return x2.reshape(B, S, D)
```

```python
import functools

import jax
import jax.numpy as jnp
from jax import lax
from jax.experimental import pallas as pl
from jax.experimental.pallas import tpu as pltpu

F32 = jnp.float32
BF16 = jnp.bfloat16

D_MODEL = 1024
GLA_HEADS = 4
GLA_DK = 128
GLA_DV = 256
GLA_LOWRANK = 16
GLA_TAU = 16.0
GLA_QK_WIDTH = GLA_HEADS * GLA_DK
GLA_V_WIDTH = GLA_HEADS * GLA_DV
ATTN_GROUPS = ((128, 1), (512, 4), (2048, 16))
ATTN_HEADS_PER_GROUP = 4
ATTN_HEAD_DIM = 64
ATTN_WIDTH = ATTN_HEADS_PER_GROUP * len(ATTN_GROUPS) * ATTN_HEAD_DIM
ATTN_OUT_WIDTH = ATTN_HEADS_PER_GROUP * ATTN_HEAD_DIM
ROPE_THETA = 10000.0
D_FF = 2816
CONV_WIDTH = 3
EPS = 1e-6
IN_WIDTHS = (GLA_QK_WIDTH, GLA_QK_WIDTH, GLA_V_WIDTH, GLA_V_WIDTH, GLA_LOWRANK,
             ATTN_WIDTH, ATTN_WIDTH, ATTN_WIDTH, D_MODEL, D_MODEL)

LANES = 128
VMEM_LIMIT_BYTES = 56 * 1024 * 1024

TM_IN = 512
GLA_CHUNK = 64
ATTN_BLK = 128
ATTN_SPAN = 2048
ATTN_MERGE_ROWS = 256
TM_MERGE = 1024
MERGE_SPLIT = 2
TM_FFN = 512
FFN_CHUNK = 256
FFN_HALO = 16
FFN_DOWN_SLAB = 256
MASK_VALUE = -1e30


def _params(*sem):
    return pltpu.CompilerParams(dimension_semantics=sem, vmem_limit_bytes=VMEM_LIMIT_BYTES)


def _split_hi_lo(a):
    hi = a.astype(BF16)
    lo = (a - hi.astype(F32)).astype(BF16)
    return hi, lo


def _mod_kernel(c_ref, w_ref, b_ref, o_ref):
    c = c_ref[...]
    s = c * (1.0 / (1.0 + jnp.exp(-c)))
    s_hi, s_lo = _split_hi_lo(s)
    lhs = jnp.concatenate([s_hi, s_lo], axis=0)
    w_hi, w_lo = _split_hi_lo(w_ref[...])
    acc = jnp.dot(lhs, w_hi, preferred_element_type=F32) + jnp.dot(lhs, w_lo, preferred_element_type=F32)
    o_ref[...] = acc[0:8] + acc[8:16] + b_ref[...]


def _modulation(c, ada_w, ada_b):
    B, D = c.shape
    N = ada_w.shape[1]
    tn = 1024
    c8 = jnp.pad(c, ((0, 8 - B), (0, 0)))
    out = pl.pallas_call(
        _mod_kernel,
        grid=(N // tn,),
        in_specs=[pl.BlockSpec((8, D), lambda j: (0, 0)),
                  pl.BlockSpec((D, tn), lambda j: (0, j)),
                  pl.BlockSpec((1, tn), lambda j: (0, j))],
        out_specs=pl.BlockSpec((8, tn), lambda j: (0, j)),
        out_shape=jax.ShapeDtypeStruct((8, N), F32),
        compiler_params=_params("parallel"),
        name="modulation",
    )(c8, ada_w, ada_b.reshape(1, N))
    return out


def _mod_spec(which, D):
    return pl.BlockSpec((8, D), lambda i: (0, which))


MOD_SHIFT1, MOD_SCALE1, MOD_GATE1, MOD_SHIFT2, MOD_SCALE2, MOD_GATE2 = range(6)


def _rope_kernel(pos_ref, invf_ref, cos_ref, sin_ref):
    half = ATTN_HEAD_DIM // 2
    groups = LANES // half
    tr = pos_ref.shape[1]
    pos = jnp.concatenate([pos_ref[...].astype(F32), jnp.zeros((8 - groups, tr), F32)], axis=0)
    pos_t = jnp.transpose(pos)
    lane = lax.broadcasted_iota(jnp.int32, (tr, LANES), 1)
    group = lane // half
    pos_dense = jnp.zeros((tr, LANES), F32)
    for q in range(groups):
        pos_dense = jnp.where(group == q, pos_t[:, q:q + 1], pos_dense)
    ang = pos_dense * invf_ref[...]
    first_half = (lane % ATTN_HEAD_DIM) < half
    for table, out_ref, signed in ((jnp.cos(ang), cos_ref, False), (jnp.sin(ang), sin_ref, True)):
        for q in range(groups):
            spread = jnp.where(group == q, table, 0.0)
            shift = groups // 2
            while shift >= 1:
                spread = spread + pltpu.roll(spread, shift * half, axis=1)
                shift //= 2
            out_ref[q] = jnp.where(first_half, -spread, spread) if signed else spread


def _rope_tables(positions):
    T = positions.size
    half = ATTN_HEAD_DIM // 2
    groups = LANES // half
    inv_freq = ROPE_THETA ** (-jnp.arange(half, dtype=F32) / half)
    invf = jnp.tile(inv_freq, groups).reshape(1, LANES)
    per = T // groups
    tr = 512
    cos, sin = pl.pallas_call(
        _rope_kernel,
        grid=(per // tr,),
        in_specs=[pl.BlockSpec((groups, tr), lambda i: (0, i)),
                  pl.BlockSpec((1, LANES), lambda i: (0, 0))],
        out_specs=[pl.BlockSpec((groups, tr, LANES), lambda i: (0, i, 0))] * 2,
        out_shape=[jax.ShapeDtypeStruct((groups, per, LANES), F32)] * 2,
        compiler_params=_params("parallel"),
        name="rope_tables",
    )(positions.reshape(groups, per), invf)
    return cos.reshape(T, LANES), sin.reshape(T, LANES)


def _rmsnorm_mod(x, w, scale, shift):
    ms = jnp.mean(x * x, axis=-1, keepdims=True)
    return (x * lax.rsqrt(ms + EPS) * w) * (1.0 + scale) + shift


def _rotate_half_pairs(t, cos, sin_signed):
    lane = lax.broadcasted_iota(jnp.int32, t.shape, 1)
    first_half = (lane % ATTN_HEAD_DIM) < (ATTN_HEAD_DIM // 2)
    from_right = pltpu.roll(t, LANES - ATTN_HEAD_DIM // 2, axis=1)
    from_left = pltpu.roll(t, ATTN_HEAD_DIM // 2, axis=1)
    return t * cos + jnp.where(first_half, from_right, from_left) * sin_signed


def _store_residue_major(ref, perm_ref, slab, col, dilation):
    if dilation == 1:
        ref[:, col:col + LANES] = slab.astype(ref.dtype)
        return
    perm_ref[...] = slab
    n = slab.shape[0] // dilation
    for p in range(dilation):
        ref[p * n:(p + 1) * n, col:col + LANES] = perm_ref[pl.ds(p, n, stride=dilation), :].astype(ref.dtype)


IN_ALIGNED = sum(IN_WIDTHS[:4])
IN_SHIFTED = sum(IN_WIDTHS[5:])


def _in_proj_gla_kernel(x_ref, n1w_ref, scale_ref, shift_ref, cos_ref, sin_ref, w_ref, w2_ref, b2_ref, nw_ref,
                        og_ref, q1_ref, q2_ref, q3_ref, k1_ref, k2_ref, k3_ref, v1_ref, v2_ref, v3_ref,
                        ma_ref, mb_ref,
                        perm_ref, wsh_ref, wlr_ref, gq_s, gk_s, gv_s, gr_s, la_s, state_ref,
                        *, tiles_per_seq, n_tiles):
    i = pl.program_id(0)

    @pl.when(i == 0)
    def _():
        D = w_ref.shape[0]
        lane = lax.broadcasted_iota(jnp.int32, (D, LANES), 1)
        nblk = IN_SHIFTED // LANES
        first = w_ref[:, IN_ALIGNED:IN_ALIGNED + LANES]
        wlr_ref[...] = jnp.where(lane < GLA_LOWRANK, first, jnp.zeros_like(first))
        rolled = pltpu.roll(first.astype(F32), LANES - GLA_LOWRANK, axis=1)
        for c in range(nblk):
            nxt = w_ref[:, IN_ALIGNED + (c + 1) * LANES:IN_ALIGNED + (c + 2) * LANES].astype(F32)
            rolled_nxt = pltpu.roll(nxt, LANES - GLA_LOWRANK, axis=1)
            wsh_ref[:, c * LANES:(c + 1) * LANES] = jnp.where(
                lane < LANES - GLA_LOWRANK, rolled, rolled_nxt).astype(wsh_ref.dtype)
            rolled = rolled_nxt
        for ref in (gq_s, gk_s, gv_s, gr_s, la_s, state_ref):
            ref[...] = jnp.zeros_like(ref)

    first_of_seq = (i + tiles_per_seq - 1) % tiles_per_seq == 0
    chunks = _gla_chunks(gq_s, gk_s, gv_s, la_s, gr_s, nw_ref, og_ref, state_ref, first_of_seq)

    @pl.when(i == n_tiles)
    def _():
        for chunk in chunks:
            chunk()

    @pl.when(i < n_tiles)
    def _():
        b = i // tiles_per_seq
        h = _rmsnorm_mod(x_ref[...], n1w_ref[...], scale_ref[pl.ds(b, 1), :], shift_ref[pl.ds(b, 1), :]).astype(BF16)

        def proj(col, width):
            return jnp.dot(h, wsh_ref[:, col:col + width], preferred_element_type=F32)

        cos = cos_ref[...]
        sin = sin_ref[...]
        q_scale = ATTN_HEAD_DIM ** -0.5
        dilations = [d for _, d in ATTN_GROUPS]

        def rope_piece(ref, col, scale, dilation):
            def run():
                t = proj(col, ATTN_OUT_WIDTH)
                for s in range(0, ATTN_OUT_WIDTH, LANES):
                    rot = _rotate_half_pairs(t[:, s:s + LANES], cos, sin) * scale
                    _store_residue_major(ref, perm_ref, rot, s, dilation)
            return run

        def value_piece(ref, col, dilation):
            def run():
                t = proj(col, ATTN_OUT_WIDTH)
                for s in range(0, ATTN_OUT_WIDTH, LANES):
                    _store_residue_major(ref, perm_ref, t[:, s:s + LANES], s, dilation)
            return run

        def gate_piece(ref, col, s):
            def run():
                z = proj(col + s, 512)
                ref[:, s:s + 512] = (1.0 / (1.0 + jnp.exp(-z))).astype(ref.dtype)
            return run

        pieces = []
        col = 0
        for refs, scale in (((q1_ref, q2_ref, q3_ref), q_scale), ((k1_ref, k2_ref, k3_ref), 1.0)):
            for g, ref in enumerate(refs):
                pieces.append(rope_piece(ref, col, scale, dilations[g]))
                col += ATTN_OUT_WIDTH
        for g, ref in enumerate((v1_ref, v2_ref, v3_ref)):
            pieces.append(value_piece(ref, col, dilations[g]))
            col += ATTN_OUT_WIDTH
        for ref in (ma_ref, mb_ref):
            for s in range(0, D_MODEL, 512):
                pieces.append(gate_piece(ref, col, s))
            col += D_MODEL

        for n in range(max(len(pieces), len(chunks))):
            if n < len(pieces):
                pieces[n]()
            if n < len(chunks):
                chunks[n]()

        col = 0
        for ref, width in ((gq_s, GLA_QK_WIDTH), (gk_s, GLA_QK_WIDTH)):
            ref[...] = jnp.dot(h, w_ref[:, col:col + width], preferred_element_type=F32).astype(ref.dtype)
            col += width
        for ref in (gv_s, gr_s):
            for s in range(0, GLA_V_WIDTH, 512):
                ref[:, s:s + 512] = jnp.dot(h, w_ref[:, col + s:col + s + 512],
                                            preferred_element_type=F32).astype(ref.dtype)
            col += GLA_V_WIDTH
        g_lr = jnp.dot(h, wlr_ref[...], preferred_element_type=F32)
        g_hi, g_lo = _split_hi_lo(g_lr)
        w2_hi, w2_lo = _split_hi_lo(w2_ref[...])
        z = (jnp.dot(g_hi, w2_hi, preferred_element_type=F32) + jnp.dot(g_lo, w2_hi, preferred_element_type=F32)
             + jnp.dot(g_hi, w2_lo, preferred_element_type=F32)) + b2_ref[...]
        log_sig = jnp.minimum(z, 0.0) - jnp.log(1.0 + jnp.exp(-jnp.abs(z)))
        la_s[...] = log_sig * (1.0 / GLA_TAU)


def _gla_chunks(q_ref, k_ref, v_ref, la_ref, gr_ref, nw_ref, o_ref, state_ref, first_of_seq):
    C = GLA_CHUNK
    heads = range(GLA_HEADS)
    rows = lax.broadcasted_iota(jnp.int32, (C, C), 0)
    cols = lax.broadcasted_iota(jnp.int32, (C, C), 1)
    causal = cols <= rows
    tri = causal.astype(BF16)
    q_scale = GLA_DK ** -0.5
    nw = jnp.concatenate([nw_ref[...]] * GLA_HEADS, axis=1)
    nt = (((1,), (1,)), ((), ()))
    ks = lambda a, h: a[:, h * GLA_DK:(h + 1) * GLA_DK]
    vs = lambda a, h: a[:, h * GLA_DV:(h + 1) * GLA_DV]

    def run_chunk(c):
        sl = pl.ds(c * C, C)
        q = q_ref[sl, :].astype(F32) * q_scale
        k = k_ref[sl, :].astype(F32)
        v = v_ref[sl, :]
        la_hi, la_lo = _split_hi_lo(la_ref[sl, :])
        cum = jnp.dot(tri, la_hi, preferred_element_type=F32) + jnp.dot(tri, la_lo, preferred_element_type=F32)
        mid = cum[C // 2 - 1:C // 2, :]
        last = cum[C - 1:C, :]
        qg = (q * jnp.exp(cum - mid)).astype(BF16)
        kg = (k * jnp.exp(mid - cum)).astype(BF16)
        q_in = (q * jnp.exp(cum)).astype(BF16)
        k_out = k * jnp.exp(last - cum)
        decay_row = jnp.broadcast_to(jnp.exp(last), (8, GLA_QK_WIDTH))
        g = gr_ref[sl, :].astype(F32)
        gate = nw * (g * (1.0 / (1.0 + jnp.exp(-g))))

        attn = [lax.dot_general(ks(qg, h), ks(kg, h), nt, preferred_element_type=F32) for h in heads]
        attn = [jnp.where(causal, a, 0.0).astype(BF16) for a in attn]
        if c == 0:
            state = [jnp.where(first_of_seq, 0.0, state_ref[h]) for h in heads]
        else:
            state = [state_ref[h] for h in heads]
        o = [jnp.dot(attn[h], vs(v, h), preferred_element_type=F32)
             + jnp.dot(ks(q_in, h), state[h].astype(BF16), preferred_element_type=F32) for h in heads]
        k_out_t = [jnp.transpose(ks(k_out, h)).astype(BF16) for h in heads]
        decay = [jnp.transpose(ks(decay_row, h))[:, 0:1] for h in heads]
        for h in heads:
            state_ref[h] = decay[h] * state[h] + jnp.dot(k_out_t[h], vs(v, h), preferred_element_type=F32)
        ms = [jnp.mean(o[h] * o[h], axis=-1, keepdims=True) for h in heads]
        for h in heads:
            o_ref[sl, h * GLA_DV:(h + 1) * GLA_DV] = (o[h] * lax.rsqrt(ms[h] + EPS) * vs(gate, h)).astype(o_ref.dtype)

    return [functools.partial(run_chunk, c) for c in range(q_ref.shape[0] // C)]


def _in_proj_gla(x2, n1w, mod, cos, sin, w_in, w2, b2, gla_norm_w, B):
    T, D = x2.shape
    tm = TM_IN
    n_tiles = T // tm
    cur = lambda i: (jnp.minimum(i, n_tiles - 1), 0)
    lag = lambda i: (jnp.maximum(i - 1, 0), 0)
    const = lambda i: (0, 0)
    widths = (ATTN_OUT_WIDTH,) * 9 + (D_MODEL, D_MODEL)
    single = dict(pipeline_mode=pl.Buffered(1))
    pad_to = IN_ALIGNED + IN_SHIFTED + LANES
    w2p = jnp.pad(w2, ((0, LANES - GLA_LOWRANK), (0, 0)))
    outs = pl.pallas_call(
        functools.partial(_in_proj_gla_kernel, tiles_per_seq=T // B // tm, n_tiles=n_tiles),
        grid=(n_tiles + 1,),
        in_specs=[pl.BlockSpec((tm, D), cur),
                  pl.BlockSpec((1, D), const),
                  _mod_spec(MOD_SCALE1, D),
                  _mod_spec(MOD_SHIFT1, D),
                  pl.BlockSpec((tm, LANES), cur),
                  pl.BlockSpec((tm, LANES), cur),
                  pl.BlockSpec((D, pad_to), const, **single),
                  pl.BlockSpec(w2p.shape, const, **single),
                  pl.BlockSpec(b2.shape, const, **single),
                  pl.BlockSpec(gla_norm_w.shape, const, **single)],
        out_specs=[pl.BlockSpec((tm, GLA_V_WIDTH), lag)] + [pl.BlockSpec((tm, w), cur) for w in widths],
        out_shape=[jax.ShapeDtypeStruct((T, GLA_V_WIDTH), BF16)]
                  + [jax.ShapeDtypeStruct((T, w), BF16) for w in widths],
        scratch_shapes=[pltpu.VMEM((tm, LANES), F32),
                        pltpu.VMEM((D, IN_SHIFTED), BF16),
                        pltpu.VMEM((D, LANES), BF16),
                        pltpu.VMEM((tm, GLA_QK_WIDTH), BF16),
                        pltpu.VMEM((tm, GLA_QK_WIDTH), BF16),
                        pltpu.VMEM((tm, GLA_V_WIDTH), BF16),
                        pltpu.VMEM((tm, GLA_V_WIDTH), BF16),
                        pltpu.VMEM((tm, GLA_QK_WIDTH), F32),
                        pltpu.VMEM((GLA_HEADS, GLA_DK, GLA_DV), F32)],
        compiler_params=_params("arbitrary"),
        name="in_proj_gla",
    )(x2, n1w, mod, mod, cos, sin, w_in.astype(BF16), w2p, b2, gla_norm_w)
    return outs[0], outs[1:]


def _attn_kernel(q1_ref, q2_ref, q3_ref, k1c_ref, k1p_ref, k2c_ref, k2p_ref, k3c_ref, k3p_ref,
                 v1c_ref, v1p_ref, v2c_ref, v2p_ref, v3c_ref, v3p_ref, o_ref, *slabs):
    blk = ATTN_BLK
    tile = TM_IN
    span = ATTN_SPAN
    tiles = span // tile
    r2, r3 = ATTN_GROUPS[1][1], ATTN_GROUPS[2][1]
    per3 = tile // r3
    not_first_span = pl.program_id(1) > 0
    npair = ATTN_OUT_WIDTH // LANES
    o1_s, o2_s, o3_s, l1_s, l2_s, l3_s = [slabs[n * npair:(n + 1) * npair] for n in range(6)]

    rows = lax.broadcasted_iota(jnp.int32, (blk, 2 * blk), 0)
    cols = lax.broadcasted_iota(jnp.int32, (blk, 2 * blk), 1)
    band = (cols >= rows) & (cols <= rows + blk)
    in_cur = cols >= blk
    left = lax.broadcasted_iota(jnp.int32, (blk, LANES), 1) < ATTN_HEAD_DIM
    ones = jnp.ones((2 * blk, LANES), BF16)
    nt = (((1,), (1,)), ((), ()))

    def pair_attend(q_pair, k_cat, v_cat, has_prev):
        zero = jnp.zeros_like(q_pair)
        q2 = jnp.concatenate([jnp.where(left, q_pair, zero), jnp.where(left, zero, q_pair)], axis=0)
        s = lax.dot_general(q2, k_cat, nt, preferred_element_type=F32)
        valid = band & (in_cur | has_prev)
        s = jnp.where(jnp.concatenate([valid, valid], axis=0), s, MASK_VALUE)
        m = jnp.max(s, axis=-1, keepdims=True)
        p = jnp.exp(s - m).astype(BF16)
        r = jnp.dot(p, jnp.concatenate([v_cat, ones], axis=1), preferred_element_type=F32)
        acc = jnp.where(left, r[:blk, :LANES], r[blk:, :LANES])
        den = jnp.where(left, r[:blk, LANES:], r[blk:, LANES:])
        m_pair = jnp.where(left, jnp.broadcast_to(m[:blk], (blk, LANES)), jnp.broadcast_to(m[blk:], (blk, LANES)))
        return acc / den, m_pair + jnp.log(den)

    def with_prev(cur_ref, prev_ref, r0, back, prev_rows, first, cs):
        before = cur_ref[pl.ds(pl.multiple_of(jnp.maximum(r0 - back, 0), blk), blk), cs]
        if prev_rows is not None:
            before = jnp.where(first, prev_ref[prev_rows, cs], before)
        return jnp.concatenate([before, cur_ref[pl.ds(r0, blk), cs]], axis=0)

    def tile_body(t, carry):
        for p in range(tiles):
            i = t * tiles + p
            r0 = pl.multiple_of(i * blk, blk)
            for j in range(npair):
                cs = slice(j * LANES, (j + 1) * LANES)
                rows1 = slice(0, blk) if p == 0 else None
                o, lse = pair_attend(q1_ref[pl.ds(r0, blk), cs],
                                     with_prev(k1c_ref, k1p_ref, r0, blk, rows1, t == 0, cs),
                                     with_prev(v1c_ref, v1p_ref, r0, blk, rows1, t == 0, cs),
                                     not_first_span | (i > 0))
                o1_s[j][pl.ds(r0, blk), :] = o
                l1_s[j][pl.ds(r0, blk), :] = lse
                rows2 = slice(p * blk, (p + 1) * blk)
                o, lse = pair_attend(q2_ref[pl.ds(r0, blk), cs],
                                     with_prev(k2c_ref, k2p_ref, r0, tile, rows2, t == 0, cs),
                                     with_prev(v2c_ref, v2p_ref, r0, tile, rows2, t == 0, cs),
                                     not_first_span | (t > 0))
                tok = pl.ds(pl.multiple_of(t * tile, tile) + p, blk, stride=r2)
                o2_s[j][tok, :] = o
                l2_s[j][tok, :] = lse
                rr = pl.multiple_of(i * per3, per3)
                gather = lambda ref: jnp.concatenate(
                    [ref[pl.ds(rr + u * tile, per3), cs] for u in range(tiles)], axis=0)
                k_cat = jnp.concatenate([gather(k3p_ref), gather(k3c_ref)], axis=0)
                v_cat = jnp.concatenate([gather(v3p_ref), gather(v3c_ref)], axis=0)
                o, lse = pair_attend(gather(q3_ref), k_cat, v_cat, not_first_span)
                tok = pl.ds(i, blk, stride=r3)
                o3_s[j][tok, :] = o
                l3_s[j][tok, :] = lse
        return carry

    lax.fori_loop(0, tiles, tile_body, 0)

    for j in range(npair):
        for n in range(span // ATTN_MERGE_ROWS):
            rs = slice(n * ATTN_MERGE_ROWS, (n + 1) * ATTN_MERGE_ROWS)
            l1, l2, l3 = l1_s[j][rs, :], l2_s[j][rs, :], l3_s[j][rs, :]
            m = jnp.maximum(jnp.maximum(l1, l2), l3)
            e1, e2, e3 = jnp.exp(l1 - m), jnp.exp(l2 - m), jnp.exp(l3 - m)
            num = e1 * o1_s[j][rs, :] + e2 * o2_s[j][rs, :] + e3 * o3_s[j][rs, :]
            o_ref[rs, j * LANES:(j + 1) * LANES] = (num / (e1 + e2 + e3)).astype(o_ref.dtype)


def _dilated_attention(qs, ks, vs, B, S):
    span = ATTN_SPAN
    nsp = S // span
    gw = ATTN_OUT_WIDTH
    T = B * S
    cur = lambda b, s: (b * nsp + s, 0)

    def prev(rows):
        per = span // rows
        return pl.BlockSpec((rows, gw), lambda b, s: (jnp.maximum((b * nsp + s) * per - 1, 0), 0))

    halos = (ATTN_BLK, TM_IN, span)
    full = pl.BlockSpec((span, gw), cur)
    kv_specs = []
    for h in halos:
        kv_specs += [full, prev(h)]
    kv_args = lambda arrs: [a for arr in arrs for a in (arr, arr)]
    return pl.pallas_call(
        _attn_kernel,
        grid=(B, nsp),
        in_specs=[full] * 3 + kv_specs + kv_specs,
        out_specs=full,
        out_shape=jax.ShapeDtypeStruct((T, gw), BF16),
        scratch_shapes=[pltpu.VMEM((span, LANES), F32)] * (6 * (gw // LANES)),
        compiler_params=_params("parallel", "parallel"),
        name="dilated_attn",
    )(*qs, *kv_args(ks), *kv_args(vs))


def _merge_kernel(x_ref, og_ref, oa_ref, ma_ref, mb_ref,
                  gate_ref, scale_ref, shift_ref, n2w_ref, wg_ref, wa_ref, wo_ref, x1_ref, h2_ref, *, tiles_per_seq):
    b = pl.ds(pl.program_id(0) // tiles_per_seq, 1)
    gate, scale, shift = gate_ref[b, :], scale_ref[b, :], shift_ref[b, :]
    sub = x_ref.shape[0] // MERGE_SPLIT
    for r in range(MERGE_SPLIT):
        rs = slice(r * sub, (r + 1) * sub)
        y_attn = jnp.dot(oa_ref[rs, :], wa_ref[...], preferred_element_type=F32)
        y_gla = jnp.dot(og_ref[rs, :], wg_ref[...], preferred_element_type=F32)
        mixed = (ma_ref[rs, :].astype(F32) * y_gla + mb_ref[rs, :].astype(F32) * y_attn).astype(BF16)
        x1 = x_ref[rs, :] + gate * jnp.dot(mixed, wo_ref[...], preferred_element_type=F32)
        x1_ref[rs, :] = x1
        h2_ref[rs, :] = _rmsnorm_mod(x1, n2w_ref[...], scale, shift).astype(h2_ref.dtype)


def _merge(x2, og, oa, ma, mb, mod, n2w, wg, wa, wo, B):
    T, D = x2.shape
    tm = TM_MERGE
    row = lambda i: (i, 0)
    const = lambda i: (0, 0)
    return pl.pallas_call(
        functools.partial(_merge_kernel, tiles_per_seq=T // B // tm),
        grid=(T // tm,),
        in_specs=[pl.BlockSpec((tm, D), row), pl.BlockSpec((tm, GLA_V_WIDTH), row),
                  pl.BlockSpec((tm, ATTN_OUT_WIDTH), row)]
                 + [pl.BlockSpec((tm, D), row)] * 2
                 + [_mod_spec(MOD_GATE1, D), _mod_spec(MOD_SCALE2, D), _mod_spec(MOD_SHIFT2, D)]
                 + [pl.BlockSpec((1, D), const),
                    pl.BlockSpec(wg.shape, const), pl.BlockSpec(wa.shape, const), pl.BlockSpec(wo.shape, const)],
        out_specs=[pl.BlockSpec((tm, D), row)] * 2,
        out_shape=[jax.ShapeDtypeStruct((T, D), F32), jax.ShapeDtypeStruct((T, D), BF16)],
        compiler_params=_params("parallel"),
        name="merge",
    )(x2, og, oa, ma, mb, mod, mod, mod, n2w, wg, wa, wo)


def _ffn_kernel(h_ref, hprev_ref, x1_ref, wup_ref, cw_ref, cb_ref, wd_ref, gate_ref, fw_ref, o_ref,
                hcat_s, u_s, hid_s, order_s, *, tiles_per_seq):
    i = pl.program_id(0)
    tm = h_ref.shape[0]
    halo = hprev_ref.shape[0]
    tf = FFN_CHUNK
    half = tm // 2
    hcat_s[0:halo] = jnp.where(i % tiles_per_seq == 0, jnp.zeros_like(hprev_ref[...]), hprev_ref[...])
    hcat_s[halo:] = h_ref[...]
    nchunk = D_FF // tf
    branches = (0, D_FF)

    def up_project(j):
        for n, off in enumerate(branches):
            u = jnp.dot(hcat_s[...], wup_ref[:, off + j * tf:off + (j + 1) * tf], preferred_element_type=F32)
            for c in range(tf // LANES):
                u_s[j % 2, n, c] = u[:, c * LANES:(c + 1) * LANES]

    def conv(j, n, c, parity):
        u = u_s.at[j % 2, n, c]
        cs = slice(branches[n] + j * tf + c * LANES, branches[n] + j * tf + (c + 1) * LANES)
        tap = lambda back: u[pl.ds(halo + parity - back, half, stride=2), :]
        k = 2.0 ** -0.5
        return (cb_ref[:, cs] * k + (cw_ref[0:1, cs] * k) * tap(2) + (cw_ref[1:2, cs] * k) * tap(1)
                + (cw_ref[2:3, cs] * k) * tap(0))

    up_project(0)
    for j in range(nchunk):
        if j + 1 < nchunk:
            up_project(j + 1)
        for c in range(tf // LANES):
            for parity in range(2):
                t = conv(j, 1, c, parity)
                hidden = (t * (1.0 + lax.erf(t))) * conv(j, 0, c, parity)
                hid_s[parity * half:(parity + 1) * half, j * tf + c * LANES:j * tf + (c + 1) * LANES] = (
                    hidden.astype(hid_s.dtype))

    gate = gate_ref[pl.ds(i // tiles_per_seq, 1), :]
    d_model = o_ref.shape[1]
    sq = jnp.zeros((tm, 1), F32)
    for n in range(d_model // FFN_DOWN_SLAB):
        ns = slice(n * FFN_DOWN_SLAB, (n + 1) * FFN_DOWN_SLAB)
        down = jnp.dot(hid_s[...], wd_ref[:, ns], preferred_element_type=F32)
        for c in range(FFN_DOWN_SLAB // LANES):
            g = n * (FFN_DOWN_SLAB // LANES) + c
            cs = slice(g * LANES, (g + 1) * LANES)
            for parity in range(2):
                order_s[g, pl.ds(parity, half, stride=2), :] = (
                    down[parity * half:(parity + 1) * half, c * LANES:(c + 1) * LANES])
            x2 = x1_ref[:, cs] + gate[:, cs] * order_s[g]
            sq = sq + jnp.sum(x2 * x2, axis=-1, keepdims=True)
            o_ref[:, cs] = x2
    o_ref[...] = o_ref[...] * lax.rsqrt(sq * (1.0 / d_model) + EPS) * fw_ref[...]


def _ffn(h2, x1, w_up, conv_w, conv_b, w_down, mod, final_w, B):
    T, D = x1.shape
    tm, tf, halo = TM_FFN, FFN_CHUNK, FFN_HALO
    row = lambda i: (i, 0)
    const = lambda i: (0, 0)
    single = dict(pipeline_mode=pl.Buffered(1))
    return pl.pallas_call(
        functools.partial(_ffn_kernel, tiles_per_seq=T // B // tm),
        grid=(T // tm,),
        in_specs=[pl.BlockSpec((tm, D), row),
                  pl.BlockSpec((halo, D), lambda i: (jnp.maximum(i * (tm // halo) - 1, 0), 0)),
                  pl.BlockSpec((tm, D), row),
                  pl.BlockSpec(w_up.shape, const, **single),
                  pl.BlockSpec(conv_w.shape, const, **single),
                  pl.BlockSpec(conv_b.shape, const, **single),
                  pl.BlockSpec(w_down.shape, const, **single),
                  _mod_spec(MOD_GATE2, D),
                  pl.BlockSpec((1, D), const)],
        out_specs=pl.BlockSpec((tm, D), row),
        out_shape=jax.ShapeDtypeStruct((T, D), F32),
        scratch_shapes=[pltpu.VMEM((halo + tm, D), BF16),
                        pltpu.VMEM((2, 2, tf // LANES, halo + tm, LANES), F32),
                        pltpu.VMEM((tm, D_FF), BF16),
                        pltpu.VMEM((D // LANES, tm, LANES), F32)],
        compiler_params=_params("parallel"),
        name="ffn",
    )(h2, h2, x1, w_up, conv_w, conv_b, w_down, mod, final_w)


def kernel(x, c, positions, ada_w, ada_b, norm1_w, w_in, gla_gate_w2, gla_gate_b, gla_norm_w, w_gla_branch,
           w_attn_branch, w_out, norm2_w, w_up, conv_w, conv_b, w_down, final_norm_w):
    B, S, D = x.shape
    T = B * S
    depth = ada_w.shape[0]
    assert depth == 1, "the final norm is fused into the (single) layer's ffn"
    assert all(window // dilation == ATTN_BLK for window, dilation in ATTN_GROUPS)
    cos, sin = _rope_tables(positions)
    x2 = x.reshape(T, D)
    for layer in range(depth):
        mod = _modulation(c, ada_w[layer], ada_b[layer])

        og, (q1, q2, q3, k1, k2, k3, v1, v2, v3, ma, mb) = _in_proj_gla(
            x2, norm1_w[layer].reshape(1, D), mod, cos, sin, w_in[layer], gla_gate_w2[layer],
            gla_gate_b[layer].reshape(1, -1), gla_norm_w[layer].reshape(1, -1), B)
        oa = _dilated_attention((q1, q2, q3), (k1, k2, k3), (v1, v2, v3), B, S)

        x1, h2 = _merge(x2, og, oa, ma, mb, mod, norm2_w[layer].reshape(1, D),
                        w_gla_branch[layer].astype(BF16), w_attn_branch[layer].astype(BF16),
                        w_out[layer].astype(BF16), B)

        x2 = _ffn(h2, x1, w_up[layer].astype(BF16), conv_w[layer], conv_b[layer].reshape(1, -1),
                  w_down[layer].astype(BF16), mod, final_norm_w.reshape(1, D), B)
    return x2.reshape(B, S, D)
```

```python
import functools

import jax
import jax.numpy as jnp
from jax import lax
from jax.experimental import pallas as pl
from jax.experimental.pallas import tpu as pltpu

F32 = jnp.float32
BF16 = jnp.bfloat16

D_MODEL = 1024
GLA_HEADS = 4
GLA_DK = 128
GLA_DV = 256
GLA_LOWRANK = 16
GLA_TAU = 16.0
GLA_QK_WIDTH = GLA_HEADS * GLA_DK
GLA_V_WIDTH = GLA_HEADS * GLA_DV
ATTN_GROUPS = ((128, 1), (512, 4), (2048, 16))
ATTN_HEADS_PER_GROUP = 4
ATTN_HEAD_DIM = 64
ATTN_WIDTH = ATTN_HEADS_PER_GROUP * len(ATTN_GROUPS) * ATTN_HEAD_DIM
ATTN_OUT_WIDTH = ATTN_HEADS_PER_GROUP * ATTN_HEAD_DIM
ROPE_THETA = 10000.0
D_FF = 2816
CONV_WIDTH = 3
EPS = 1e-6
IN_WIDTHS = (GLA_QK_WIDTH, GLA_QK_WIDTH, GLA_V_WIDTH, GLA_V_WIDTH, GLA_LOWRANK,
             ATTN_WIDTH, ATTN_WIDTH, ATTN_WIDTH, D_MODEL, D_MODEL)

LANES = 128
VMEM_LIMIT_BYTES = 56 * 1024 * 1024

TM_IN = 512
GLA_CHUNK = 64
ATTN_BLK = 128
ATTN_SPAN = 2048
ATTN_MERGE_ROWS = 256
TM_MERGE = 1024
MERGE_SPLIT = 2
TM_FFN = 512
FFN_CHUNK = 256
FFN_HALO = 16
FFN_DOWN_SLAB = 256
MASK_VALUE = -1e30


def _params(*sem):
    return pltpu.CompilerParams(dimension_semantics=sem, vmem_limit_bytes=VMEM_LIMIT_BYTES)


def _split_hi_lo(a):
    hi = a.astype(BF16)
    lo = (a - hi.astype(F32)).astype(BF16)
    return hi, lo


def _mod_kernel(c_ref, w_ref, b_ref, o_ref):
    c = c_ref[...]
    s = c * (1.0 / (1.0 + jnp.exp(-c)))
    s_hi, s_lo = _split_hi_lo(s)
    lhs = jnp.concatenate([s_hi, s_lo], axis=0)
    w_hi, w_lo = _split_hi_lo(w_ref[...])
    acc = jnp.dot(lhs, w_hi, preferred_element_type=F32) + jnp.dot(lhs, w_lo, preferred_element_type=F32)
    o_ref[...] = acc[0:8] + acc[8:16] + b_ref[...]


def _modulation(c, ada_w, ada_b):
    B, D = c.shape
    N = ada_w.shape[1]
    tn = 1024
    c8 = jnp.pad(c, ((0, 8 - B), (0, 0)))
    out = pl.pallas_call(
        _mod_kernel,
        grid=(N // tn,),
        in_specs=[pl.BlockSpec((8, D), lambda j: (0, 0)),
                  pl.BlockSpec((D, tn), lambda j: (0, j)),
                  pl.BlockSpec((1, tn), lambda j: (0, j))],
        out_specs=pl.BlockSpec((8, tn), lambda j: (0, j)),
        out_shape=jax.ShapeDtypeStruct((8, N), F32),
        compiler_params=_params("parallel"),
        name="modulation",
    )(c8, ada_w, ada_b.reshape(1, N))
    return out


def _mod_spec(which, D):
    return pl.BlockSpec((8, D), lambda i: (0, which))


MOD_SHIFT1, MOD_SCALE1, MOD_GATE1, MOD_SHIFT2, MOD_SCALE2, MOD_GATE2 = range(6)


def _rope_kernel(pos_ref, invf_ref, *refs):
    n_cast = (len(refs) - 2) // 2
    cos_ref, sin_ref = refs[n_cast:n_cast + 2]
    for src, dst in zip(refs[:n_cast], refs[n_cast + 2:]):
        dst[...] = src[...].astype(dst.dtype)

    half = ATTN_HEAD_DIM // 2
    groups = LANES // half
    tr = pos_ref.shape[1]
    pos = jnp.concatenate([pos_ref[...].astype(F32), jnp.zeros((8 - groups, tr), F32)], axis=0)
    pos_t = jnp.transpose(pos)
    lane = lax.broadcasted_iota(jnp.int32, (tr, LANES), 1)
    group = lane // half
    pos_dense = jnp.zeros((tr, LANES), F32)
    for q in range(groups):
        pos_dense = jnp.where(group == q, pos_t[:, q:q + 1], pos_dense)
    ang = pos_dense * invf_ref[...]
    first_half = (lane % ATTN_HEAD_DIM) < half
    for table, out_ref, signed in ((jnp.cos(ang), cos_ref, False), (jnp.sin(ang), sin_ref, True)):
        for q in range(groups):
            spread = jnp.where(group == q, table, 0.0)
            shift = groups // 2
            while shift >= 1:
                spread = spread + pltpu.roll(spread, shift * half, axis=1)
                shift //= 2
            out_ref[q] = jnp.where(first_half, -spread, spread) if signed else spread


def _rope_tables(positions, weights):
    T = positions.size
    half = ATTN_HEAD_DIM // 2
    groups = LANES // half
    inv_freq = ROPE_THETA ** (-jnp.arange(half, dtype=F32) / half)
    invf = jnp.tile(inv_freq, groups).reshape(1, LANES)
    per = T // groups
    tr = 512
    steps = per // tr
    slab = lambda w: pl.BlockSpec((w.shape[0] // steps, w.shape[1]), lambda i: (i, 0))
    outs = pl.pallas_call(
        _rope_kernel,
        grid=(steps,),
        in_specs=[pl.BlockSpec((groups, tr), lambda i: (0, i)),
                  pl.BlockSpec((1, LANES), lambda i: (0, 0))] + [slab(w) for w in weights],
        out_specs=[pl.BlockSpec((groups, tr, LANES), lambda i: (0, i, 0))] * 2 + [slab(w) for w in weights],
        out_shape=[jax.ShapeDtypeStruct((groups, per, LANES), F32)] * 2
                  + [jax.ShapeDtypeStruct(w.shape, BF16) for w in weights],
        compiler_params=_params("parallel"),
        name="rope_tables",
    )(positions.reshape(groups, per), invf, *weights)
    return outs[0].reshape(T, LANES), outs[1].reshape(T, LANES), outs[2:]


def _rmsnorm_mod(x, w, scale, shift):
    ms = jnp.mean(x * x, axis=-1, keepdims=True)
    return (x * lax.rsqrt(ms + EPS) * w) * (1.0 + scale) + shift


def _rotate_half_pairs(t, cos, sin_signed):
    lane = lax.broadcasted_iota(jnp.int32, t.shape, 1)
    first_half = (lane % ATTN_HEAD_DIM) < (ATTN_HEAD_DIM // 2)
    from_right = pltpu.roll(t, LANES - ATTN_HEAD_DIM // 2, axis=1)
    from_left = pltpu.roll(t, ATTN_HEAD_DIM // 2, axis=1)
    return t * cos + jnp.where(first_half, from_right, from_left) * sin_signed


def _store_residue_major(ref, perm_ref, slab, col, dilation):
    if dilation == 1:
        ref[:, col:col + LANES] = slab.astype(ref.dtype)
        return
    perm_ref[...] = slab
    n = slab.shape[0] // dilation
    for p in range(dilation):
        ref[p * n:(p + 1) * n, col:col + LANES] = perm_ref[pl.ds(p, n, stride=dilation), :].astype(ref.dtype)


IN_ALIGNED = sum(IN_WIDTHS[:4])
IN_SHIFTED = sum(IN_WIDTHS[5:])


def _in_proj_gla_kernel(x_ref, n1w_ref, scale_ref, shift_ref, cos_ref, sin_ref, w_ref, w2_ref, b2_ref, nw_ref,
                        og_ref, q1_ref, q2_ref, q3_ref, k1_ref, k2_ref, k3_ref, v1_ref, v2_ref, v3_ref,
                        ma_ref, mb_ref,
                        perm_ref, wsh_ref, wlr_ref, gq_s, gk_s, gv_s, gr_s, la_s, state_ref,
                        *, tiles_per_seq, n_tiles):
    i = pl.program_id(0)

    @pl.when(i == 0)
    def _():
        D = w_ref.shape[0]
        lane = lax.broadcasted_iota(jnp.int32, (D, LANES), 1)
        nblk = IN_SHIFTED // LANES
        first = w_ref[:, IN_ALIGNED:IN_ALIGNED + LANES]
        wlr_ref[...] = jnp.where(lane < GLA_LOWRANK, first, jnp.zeros_like(first))
        rolled = pltpu.roll(first.astype(F32), LANES - GLA_LOWRANK, axis=1)
        for c in range(nblk):
            nxt = w_ref[:, IN_ALIGNED + (c + 1) * LANES:IN_ALIGNED + (c + 2) * LANES].astype(F32)
            rolled_nxt = pltpu.roll(nxt, LANES - GLA_LOWRANK, axis=1)
            wsh_ref[:, c * LANES:(c + 1) * LANES] = jnp.where(
                lane < LANES - GLA_LOWRANK, rolled, rolled_nxt).astype(wsh_ref.dtype)
            rolled = rolled_nxt
        for ref in (gq_s, gk_s, gv_s, gr_s, la_s, state_ref):
            ref[...] = jnp.zeros_like(ref)

    first_of_seq = (i + tiles_per_seq - 1) % tiles_per_seq == 0
    chunks = _gla_chunks(gq_s, gk_s, gv_s, la_s, gr_s, nw_ref, og_ref, state_ref, first_of_seq)

    @pl.when(i == n_tiles)
    def _():
        for chunk in chunks:
            chunk()

    @pl.when(i < n_tiles)
    def _():
        b = i // tiles_per_seq
        h = _rmsnorm_mod(x_ref[...], n1w_ref[...], scale_ref[pl.ds(b, 1), :], shift_ref[pl.ds(b, 1), :]).astype(BF16)

        def proj(col, width):
            return jnp.dot(h, wsh_ref[:, col:col + width], preferred_element_type=F32)

        cos = cos_ref[...]
        sin = sin_ref[...]
        q_scale = ATTN_HEAD_DIM ** -0.5
        dilations = [d for _, d in ATTN_GROUPS]

        def rope_piece(ref, col, scale, dilation):
            def run():
                t = proj(col, ATTN_OUT_WIDTH)
                for s in range(0, ATTN_OUT_WIDTH, LANES):
                    rot = _rotate_half_pairs(t[:, s:s + LANES], cos, sin) * scale
                    _store_residue_major(ref, perm_ref, rot, s, dilation)
            return run

        def value_piece(ref, col, dilation):
            def run():
                t = proj(col, ATTN_OUT_WIDTH)
                for s in range(0, ATTN_OUT_WIDTH, LANES):
                    _store_residue_major(ref, perm_ref, t[:, s:s + LANES], s, dilation)
            return run

        def gate_piece(ref, col, s):
            def run():
                z = proj(col + s, 512)
                ref[:, s:s + 512] = (1.0 / (1.0 + jnp.exp(-z))).astype(ref.dtype)
            return run

        pieces = []
        col = 0
        for refs, scale in (((q1_ref, q2_ref, q3_ref), q_scale), ((k1_ref, k2_ref, k3_ref), 1.0)):
            for g, ref in enumerate(refs):
                pieces.append(rope_piece(ref, col, scale, dilations[g]))
                col += ATTN_OUT_WIDTH
        for g, ref in enumerate((v1_ref, v2_ref, v3_ref)):
            pieces.append(value_piece(ref, col, dilations[g]))
            col += ATTN_OUT_WIDTH
        for ref in (ma_ref, mb_ref):
            for s in range(0, D_MODEL, 512):
                pieces.append(gate_piece(ref, col, s))
            col += D_MODEL

        for n in range(max(len(pieces), len(chunks))):
            if n < len(pieces):
                pieces[n]()
            if n < len(chunks):
                chunks[n]()

        col = 0
        for ref, width in ((gq_s, GLA_QK_WIDTH), (gk_s, GLA_QK_WIDTH)):
            ref[...] = jnp.dot(h, w_ref[:, col:col + width], preferred_element_type=F32).astype(ref.dtype)
            col += width
        for ref in (gv_s, gr_s):
            for s in range(0, GLA_V_WIDTH, 512):
                ref[:, s:s + 512] = jnp.dot(h, w_ref[:, col + s:col + s + 512],
                                            preferred_element_type=F32).astype(ref.dtype)
            col += GLA_V_WIDTH
        g_lr = jnp.dot(h, wlr_ref[...], preferred_element_type=F32)
        g_hi, g_lo = _split_hi_lo(g_lr)
        w2_hi, w2_lo = _split_hi_lo(w2_ref[...])
        z = (jnp.dot(g_hi, w2_hi, preferred_element_type=F32) + jnp.dot(g_lo, w2_hi, preferred_element_type=F32)
             + jnp.dot(g_hi, w2_lo, preferred_element_type=F32)) + b2_ref[...]
        log_sig = jnp.minimum(z, 0.0) - jnp.log(1.0 + jnp.exp(-jnp.abs(z)))
        la_s[...] = log_sig * (1.0 / GLA_TAU)


def _gla_chunks(q_ref, k_ref, v_ref, la_ref, gr_ref, nw_ref, o_ref, state_ref, first_of_seq):
    C = GLA_CHUNK
    heads = range(GLA_HEADS)
    rows = lax.broadcasted_iota(jnp.int32, (C, C), 0)
    cols = lax.broadcasted_iota(jnp.int32, (C, C), 1)
    causal = cols <= rows
    tri = causal.astype(BF16)
    q_scale = GLA_DK ** -0.5
    nw = jnp.concatenate([nw_ref[...]] * GLA_HEADS, axis=1)
    nt = (((1,), (1,)), ((), ()))
    ks = lambda a, h: a[:, h * GLA_DK:(h + 1) * GLA_DK]
    vs = lambda a, h: a[:, h * GLA_DV:(h + 1) * GLA_DV]

    def run_chunk(c):
        sl = pl.ds(c * C, C)
        q = q_ref[sl, :].astype(F32) * q_scale
        k = k_ref[sl, :].astype(F32)
        v = v_ref[sl, :]
        la_hi, la_lo = _split_hi_lo(la_ref[sl, :])
        cum = jnp.dot(tri, la_hi, preferred_element_type=F32) + jnp.dot(tri, la_lo, preferred_element_type=F32)
        mid = cum[C // 2 - 1:C // 2, :]
        last = cum[C - 1:C, :]
        qg = (q * jnp.exp(cum - mid)).astype(BF16)
        kg = (k * jnp.exp(mid - cum)).astype(BF16)
        q_in = (q * jnp.exp(cum)).astype(BF16)
        k_out = k * jnp.exp(last - cum)
        decay_row = jnp.broadcast_to(jnp.exp(last), (8, GLA_QK_WIDTH))
        g = gr_ref[sl, :].astype(F32)
        gate = nw * (g * (1.0 / (1.0 + jnp.exp(-g))))

        attn = [lax.dot_general(ks(qg, h), ks(kg, h), nt, preferred_element_type=F32) for h in heads]
        attn = [jnp.where(causal, a, 0.0).astype(BF16) for a in attn]
        if c == 0:
            state = [jnp.where(first_of_seq, 0.0, state_ref[h]) for h in heads]
        else:
            state = [state_ref[h] for h in heads]
        o = [jnp.dot(attn[h], vs(v, h), preferred_element_type=F32)
             + jnp.dot(ks(q_in, h), state[h].astype(BF16), preferred_element_type=F32) for h in heads]
        k_out_t = [jnp.transpose(ks(k_out, h)).astype(BF16) for h in heads]
        decay = [jnp.transpose(ks(decay_row, h))[:, 0:1] for h in heads]
        for h in heads:
            state_ref[h] = decay[h] * state[h] + jnp.dot(k_out_t[h], vs(v, h), preferred_element_type=F32)
        ms = [jnp.mean(o[h] * o[h], axis=-1, keepdims=True) for h in heads]
        for h in heads:
            o_ref[sl, h * GLA_DV:(h + 1) * GLA_DV] = (o[h] * lax.rsqrt(ms[h] + EPS) * vs(gate, h)).astype(o_ref.dtype)

    return [functools.partial(run_chunk, c) for c in range(q_ref.shape[0] // C)]


def _in_proj_gla(x2, n1w, mod, cos, sin, w_in, w2, b2, gla_norm_w, B):
    T, D = x2.shape
    tm = TM_IN
    n_tiles = T // tm
    cur = lambda i: (jnp.minimum(i, n_tiles - 1), 0)
    lag = lambda i: (jnp.maximum(i - 1, 0), 0)
    const = lambda i: (0, 0)
    widths = (ATTN_OUT_WIDTH,) * 9 + (D_MODEL, D_MODEL)
    single = dict(pipeline_mode=pl.Buffered(1))
    pad_to = IN_ALIGNED + IN_SHIFTED + LANES
    w2p = jnp.pad(w2, ((0, LANES - GLA_LOWRANK), (0, 0)))
    outs = pl.pallas_call(
        functools.partial(_in_proj_gla_kernel, tiles_per_seq=T // B // tm, n_tiles=n_tiles),
        grid=(n_tiles + 1,),
        in_specs=[pl.BlockSpec((tm, D), cur),
                  pl.BlockSpec((1, D), const),
                  _mod_spec(MOD_SCALE1, D),
                  _mod_spec(MOD_SHIFT1, D),
                  pl.BlockSpec((tm, LANES), cur),
                  pl.BlockSpec((tm, LANES), cur),
                  pl.BlockSpec((D, pad_to), const, **single),
                  pl.BlockSpec(w2p.shape, const, **single),
                  pl.BlockSpec(b2.shape, const, **single),
                  pl.BlockSpec(gla_norm_w.shape, const, **single)],
        out_specs=[pl.BlockSpec((tm, GLA_V_WIDTH), lag)] + [pl.BlockSpec((tm, w), cur) for w in widths],
        out_shape=[jax.ShapeDtypeStruct((T, GLA_V_WIDTH), BF16)]
                  + [jax.ShapeDtypeStruct((T, w), BF16) for w in widths],
        scratch_shapes=[pltpu.VMEM((tm, LANES), F32),
                        pltpu.VMEM((D, IN_SHIFTED), BF16),
                        pltpu.VMEM((D, LANES), BF16),
                        pltpu.VMEM((tm, GLA_QK_WIDTH), BF16),
                        pltpu.VMEM((tm, GLA_QK_WIDTH), BF16),
                        pltpu.VMEM((tm, GLA_V_WIDTH), BF16),
                        pltpu.VMEM((tm, GLA_V_WIDTH), BF16),
                        pltpu.VMEM((tm, GLA_QK_WIDTH), F32),
                        pltpu.VMEM((GLA_HEADS, GLA_DK, GLA_DV), F32)],
        compiler_params=_params("arbitrary"),
        name="in_proj_gla",
    )(x2, n1w, mod, mod, cos, sin, w_in, w2p, b2, gla_norm_w)
    return outs[0], outs[1:]


def _attn_kernel(q1_ref, q2_ref, q3_ref, k1c_ref, k1p_ref, k2c_ref, k2p_ref, k3c_ref, k3p_ref,
                 v1c_ref, v1p_ref, v2c_ref, v2p_ref, v3c_ref, v3p_ref, wup_ref, wdn_ref,
                 o_ref, wup_bf_ref, wdn_bf_ref, *slabs):
    wup_bf_ref[...] = wup_ref[...].astype(wup_bf_ref.dtype)
    wdn_bf_ref[...] = wdn_ref[...].astype(wdn_bf_ref.dtype)

    blk = ATTN_BLK
    tile = TM_IN
    span = ATTN_SPAN
    tiles = span // tile
    r2, r3 = ATTN_GROUPS[1][1], ATTN_GROUPS[2][1]
    per3 = tile // r3
    not_first_span = pl.program_id(1) > 0
    npair = ATTN_OUT_WIDTH // LANES
    o1_s, o2_s, o3_s, l1_s, l2_s, l3_s = [slabs[n * npair:(n + 1) * npair] for n in range(6)]

    rows = lax.broadcasted_iota(jnp.int32, (blk, 2 * blk), 0)
    cols = lax.broadcasted_iota(jnp.int32, (blk, 2 * blk), 1)
    band = (cols >= rows) & (cols <= rows + blk)
    in_cur = cols >= blk
    left = lax.broadcasted_iota(jnp.int32, (blk, LANES), 1) < ATTN_HEAD_DIM
    ones = jnp.ones((2 * blk, LANES), BF16)
    nt = (((1,), (1,)), ((), ()))

    def pair_attend(q_pair, k_cat, v_cat, has_prev):
        zero = jnp.zeros_like(q_pair)
        q2 = jnp.concatenate([jnp.where(left, q_pair, zero), jnp.where(left, zero, q_pair)], axis=0)
        s = lax.dot_general(q2, k_cat, nt, preferred_element_type=F32)
        valid = band & (in_cur | has_prev)
        s = jnp.where(jnp.concatenate([valid, valid], axis=0), s, MASK_VALUE)
        m = jnp.max(s, axis=-1, keepdims=True)
        p = jnp.exp(s - m).astype(BF16)
        r = jnp.dot(p, jnp.concatenate([v_cat, ones], axis=1), preferred_element_type=F32)
        acc = jnp.where(left, r[:blk, :LANES], r[blk:, :LANES])
        den = jnp.where(left, r[:blk, LANES:], r[blk:, LANES:])
        m_pair = jnp.where(left, jnp.broadcast_to(m[:blk], (blk, LANES)), jnp.broadcast_to(m[blk:], (blk, LANES)))
        return acc / den, m_pair + jnp.log(den)

    def with_prev(cur_ref, prev_ref, r0, back, prev_rows, first, cs):
        before = cur_ref[pl.ds(pl.multiple_of(jnp.maximum(r0 - back, 0), blk), blk), cs]
        if prev_rows is not None:
            before = jnp.where(first, prev_ref[prev_rows, cs], before)
        return jnp.concatenate([before, cur_ref[pl.ds(r0, blk), cs]], axis=0)

    def tile_body(t, carry):
        for p in range(tiles):
            i = t * tiles + p
            r0 = pl.multiple_of(i * blk, blk)
            for j in range(npair):
                cs = slice(j * LANES, (j + 1) * LANES)
                rows1 = slice(0, blk) if p == 0 else None
                o, lse = pair_attend(q1_ref[pl.ds(r0, blk), cs],
                                     with_prev(k1c_ref, k1p_ref, r0, blk, rows1, t == 0, cs),
                                     with_prev(v1c_ref, v1p_ref, r0, blk, rows1, t == 0, cs),
                                     not_first_span | (i > 0))
                o1_s[j][pl.ds(r0, blk), :] = o
                l1_s[j][pl.ds(r0, blk), :] = lse
                rows2 = slice(p * blk, (p + 1) * blk)
                o, lse = pair_attend(q2_ref[pl.ds(r0, blk), cs],
                                     with_prev(k2c_ref, k2p_ref, r0, tile, rows2, t == 0, cs),
                                     with_prev(v2c_ref, v2p_ref, r0, tile, rows2, t == 0, cs),
                                     not_first_span | (t > 0))
                tok = pl.ds(pl.multiple_of(t * tile, tile) + p, blk, stride=r2)
                o2_s[j][tok, :] = o
                l2_s[j][tok, :] = lse
                rr = pl.multiple_of(i * per3, per3)
                gather = lambda ref: jnp.concatenate(
                    [ref[pl.ds(rr + u * tile, per3), cs] for u in range(tiles)], axis=0)
                k_cat = jnp.concatenate([gather(k3p_ref), gather(k3c_ref)], axis=0)
                v_cat = jnp.concatenate([gather(v3p_ref), gather(v3c_ref)], axis=0)
                o, lse = pair_attend(gather(q3_ref), k_cat, v_cat, not_first_span)
                tok = pl.ds(i, blk, stride=r3)
                o3_s[j][tok, :] = o
                l3_s[j][tok, :] = lse
        return carry

    lax.fori_loop(0, tiles, tile_body, 0)

    for j in range(npair):
        for n in range(span // ATTN_MERGE_ROWS):
            rs = slice(n * ATTN_MERGE_ROWS, (n + 1) * ATTN_MERGE_ROWS)
            l1, l2, l3 = l1_s[j][rs, :], l2_s[j][rs, :], l3_s[j][rs, :]
            m = jnp.maximum(jnp.maximum(l1, l2), l3)
            e1, e2, e3 = jnp.exp(l1 - m), jnp.exp(l2 - m), jnp.exp(l3 - m)
            num = e1 * o1_s[j][rs, :] + e2 * o2_s[j][rs, :] + e3 * o3_s[j][rs, :]
            o_ref[rs, j * LANES:(j + 1) * LANES] = (num / (e1 + e2 + e3)).astype(o_ref.dtype)


def _dilated_attention(qs, ks, vs, w_up, w_down, B, S):
    span = ATTN_SPAN
    nsp = S // span
    gw = ATTN_OUT_WIDTH
    T = B * S
    cur = lambda b, s: (b * nsp + s, 0)
    steps = B * nsp
    slab = lambda w: pl.BlockSpec((w.shape[0] // steps, w.shape[1]), cur)

    def prev(rows):
        per = span // rows
        return pl.BlockSpec((rows, gw), lambda b, s: (jnp.maximum((b * nsp + s) * per - 1, 0), 0))

    halos = (ATTN_BLK, TM_IN, span)
    full = pl.BlockSpec((span, gw), cur)
    kv_specs = []
    for h in halos:
        kv_specs += [full, prev(h)]
    kv_args = lambda arrs: [a for arr in arrs for a in (arr, arr)]
    return pl.pallas_call(
        _attn_kernel,
        grid=(B, nsp),
        in_specs=[full] * 3 + kv_specs + kv_specs + [slab(w_up), slab(w_down)],
        out_specs=[full, slab(w_up), slab(w_down)],
        out_shape=[jax.ShapeDtypeStruct((T, gw), BF16), jax.ShapeDtypeStruct(w_up.shape, BF16),
                   jax.ShapeDtypeStruct(w_down.shape, BF16)],
        scratch_shapes=[pltpu.VMEM((span, LANES), F32)] * (6 * (gw // LANES)),
        compiler_params=_params("parallel", "parallel"),
        name="dilated_attn",
    )(*qs, *kv_args(ks), *kv_args(vs), w_up, w_down)


def _merge_kernel(x_ref, og_ref, oa_ref, ma_ref, mb_ref,
                  gate_ref, scale_ref, shift_ref, n2w_ref, wg_ref, wa_ref, wo_ref, x1_ref, h2_ref, *, tiles_per_seq):
    b = pl.ds(pl.program_id(0) // tiles_per_seq, 1)
    gate, scale, shift = gate_ref[b, :], scale_ref[b, :], shift_ref[b, :]
    sub = x_ref.shape[0] // MERGE_SPLIT
    for r in range(MERGE_SPLIT):
        rs = slice(r * sub, (r + 1) * sub)
        y_attn = jnp.dot(oa_ref[rs, :], wa_ref[...], preferred_element_type=F32)
        y_gla = jnp.dot(og_ref[rs, :], wg_ref[...], preferred_element_type=F32)
        mixed = (ma_ref[rs, :].astype(F32) * y_gla + mb_ref[rs, :].astype(F32) * y_attn).astype(BF16)
        x1 = x_ref[rs, :] + gate * jnp.dot(mixed, wo_ref[...], preferred_element_type=F32)
        x1_ref[rs, :] = x1
        h2_ref[rs, :] = _rmsnorm_mod(x1, n2w_ref[...], scale, shift).astype(h2_ref.dtype)


def _merge(x2, og, oa, ma, mb, mod, n2w, wg, wa, wo, B):
    T, D = x2.shape
    tm = TM_MERGE
    row = lambda i: (i, 0)
    const = lambda i: (0, 0)
    return pl.pallas_call(
        functools.partial(_merge_kernel, tiles_per_seq=T // B // tm),
        grid=(T // tm,),
        in_specs=[pl.BlockSpec((tm, D), row), pl.BlockSpec((tm, GLA_V_WIDTH), row),
                  pl.BlockSpec((tm, ATTN_OUT_WIDTH), row)]
                 + [pl.BlockSpec((tm, D), row)] * 2
                 + [_mod_spec(MOD_GATE1, D), _mod_spec(MOD_SCALE2, D), _mod_spec(MOD_SHIFT2, D)]
                 + [pl.BlockSpec((1, D), const),
                    pl.BlockSpec(wg.shape, const), pl.BlockSpec(wa.shape, const), pl.BlockSpec(wo.shape, const)],
        out_specs=[pl.BlockSpec((tm, D), row)] * 2,
        out_shape=[jax.ShapeDtypeStruct((T, D), F32), jax.ShapeDtypeStruct((T, D), BF16)],
        compiler_params=_params("parallel"),
        name="merge",
    )(x2, og, oa, ma, mb, mod, mod, mod, n2w, wg, wa, wo)


def _ffn_kernel(h_ref, hprev_ref, x1_ref, wup_ref, cw_ref, cb_ref, wd_ref, gate_ref, fw_ref, o_ref,
                hcat_s, u_s, hid_s, order_s, *, tiles_per_seq):
    i = pl.program_id(0)
    tm = h_ref.shape[0]
    halo = hprev_ref.shape[0]
    tf = FFN_CHUNK
    half = tm // 2
    hcat_s[0:halo] = jnp.where(i % tiles_per_seq == 0, jnp.zeros_like(hprev_ref[...]), hprev_ref[...])
    hcat_s[halo:] = h_ref[...]
    nchunk = D_FF // tf
    branches = (0, D_FF)

    def up_project(j):
        for n, off in enumerate(branches):
            u = jnp.dot(hcat_s[...], wup_ref[:, off + j * tf:off + (j + 1) * tf], preferred_element_type=F32)
            for c in range(tf // LANES):
                u_s[j % 2, n, c] = u[:, c * LANES:(c + 1) * LANES]

    def conv(j, n, c, parity):
        u = u_s.at[j % 2, n, c]
        cs = slice(branches[n] + j * tf + c * LANES, branches[n] + j * tf + (c + 1) * LANES)
        tap = lambda back: u[pl.ds(halo + parity - back, half, stride=2), :]
        k = 2.0 ** -0.5
        return (cb_ref[:, cs] * k + (cw_ref[0:1, cs] * k) * tap(2) + (cw_ref[1:2, cs] * k) * tap(1)
                + (cw_ref[2:3, cs] * k) * tap(0))

    up_project(0)
    for j in range(nchunk):
        if j + 1 < nchunk:
            up_project(j + 1)
        for c in range(tf // LANES):
            for parity in range(2):
                t = conv(j, 1, c, parity)
                hidden = (t * (1.0 + lax.erf(t))) * conv(j, 0, c, parity)
                hid_s[parity * half:(parity + 1) * half, j * tf + c * LANES:j * tf + (c + 1) * LANES] = (
                    hidden.astype(hid_s.dtype))

    gate = gate_ref[pl.ds(i // tiles_per_seq, 1), :]
    d_model = o_ref.shape[1]
    sq = jnp.zeros((tm, 1), F32)
    for n in range(d_model // FFN_DOWN_SLAB):
        ns = slice(n * FFN_DOWN_SLAB, (n + 1) * FFN_DOWN_SLAB)
        down = jnp.dot(hid_s[...], wd_ref[:, ns], preferred_element_type=F32)
        for c in range(FFN_DOWN_SLAB // LANES):
            g = n * (FFN_DOWN_SLAB // LANES) + c
            cs = slice(g * LANES, (g + 1) * LANES)
            for parity in range(2):
                order_s[g, pl.ds(parity, half, stride=2), :] = (
                    down[parity * half:(parity + 1) * half, c * LANES:(c + 1) * LANES])
            x2 = x1_ref[:, cs] + gate[:, cs] * order_s[g]
            sq = sq + jnp.sum(x2 * x2, axis=-1, keepdims=True)
            o_ref[:, cs] = x2
    o_ref[...] = o_ref[...] * lax.rsqrt(sq * (1.0 / d_model) + EPS) * fw_ref[...]


def _ffn(h2, x1, w_up, conv_w, conv_b, w_down, mod, final_w, B):
    T, D = x1.shape
    tm, tf, halo = TM_FFN, FFN_CHUNK, FFN_HALO
    row = lambda i: (i, 0)
    const = lambda i: (0, 0)
    single = dict(pipeline_mode=pl.Buffered(1))
    return pl.pallas_call(
        functools.partial(_ffn_kernel, tiles_per_seq=T // B // tm),
        grid=(T // tm,),
        in_specs=[pl.BlockSpec((tm, D), row),
                  pl.BlockSpec((halo, D), lambda i: (jnp.maximum(i * (tm // halo) - 1, 0), 0)),
                  pl.BlockSpec((tm, D), row),
                  pl.BlockSpec(w_up.shape, const, **single),
                  pl.BlockSpec(conv_w.shape, const, **single),
                  pl.BlockSpec(conv_b.shape, const, **single),
                  pl.BlockSpec(w_down.shape, const, **single),
                  _mod_spec(MOD_GATE2, D),
                  pl.BlockSpec((1, D), const)],
        out_specs=pl.BlockSpec((tm, D), row),
        out_shape=jax.ShapeDtypeStruct((T, D), F32),
        scratch_shapes=[pltpu.VMEM((halo + tm, D), BF16),
                        pltpu.VMEM((2, 2, tf // LANES, halo + tm, LANES), F32),
                        pltpu.VMEM((tm, D_FF), BF16),
                        pltpu.VMEM((D // LANES, tm, LANES), F32)],
        compiler_params=_params("parallel"),
        name="ffn",
    )(h2, h2, x1, w_up, conv_w, conv_b, w_down, mod, final_w)


def kernel(x, c, positions, ada_w, ada_b, norm1_w, w_in, gla_gate_w2, gla_gate_b, gla_norm_w, w_gla_branch,
           w_attn_branch, w_out, norm2_w, w_up, conv_w, conv_b, w_down, final_norm_w):
    B, S, D = x.shape
    T = B * S
    depth = ada_w.shape[0]
    assert depth == 1, "the final norm is fused into the (single) layer's ffn"
    assert all(window // dilation == ATTN_BLK for window, dilation in ATTN_GROUPS)
    x2 = x.reshape(T, D)
    for layer in range(depth):
        cos, sin, (w_in_bf, w_gla_bf, w_attn_bf, w_out_bf) = _rope_tables(
            positions, (w_in[layer], w_gla_branch[layer], w_attn_branch[layer], w_out[layer]))
        mod = _modulation(c, ada_w[layer], ada_b[layer])

        og, (q1, q2, q3, k1, k2, k3, v1, v2, v3, ma, mb) = _in_proj_gla(
            x2, norm1_w[layer].reshape(1, D), mod, cos, sin, w_in_bf, gla_gate_w2[layer],
            gla_gate_b[layer].reshape(1, -1), gla_norm_w[layer].reshape(1, -1), B)
        oa, w_up_bf, w_down_bf = _dilated_attention((q1, q2, q3), (k1, k2, k3), (v1, v2, v3),
                                                    w_up[layer], w_down[layer], B, S)

        x1, h2 = _merge(x2, og, oa, ma, mb, mod, norm2_w[layer].reshape(1, D), w_gla_bf, w_attn_bf, w_out_bf, B)

        x2 = _ffn(h2, x1, w_up_bf, conv_w[layer], conv_b[layer].reshape(1, -1),
                  w_down_bf, mod, final_norm_w.reshape(1, D), B)
    return x2.reshape(B, S, D)
```

```python
import functools

import jax
import jax.numpy as jnp
from jax import lax
from jax.experimental import pallas as pl
from jax.experimental.pallas import tpu as pltpu

F32 = jnp.float32
BF16 = jnp.bfloat16

D_MODEL = 1024
GLA_HEADS = 4
GLA_DK = 128
GLA_DV = 256
GLA_LOWRANK = 16
GLA_TAU = 16.0
GLA_QK_WIDTH = GLA_HEADS * GLA_DK
GLA_V_WIDTH = GLA_HEADS * GLA_DV
ATTN_GROUPS = ((128, 1), (512, 4), (2048, 16))
ATTN_HEADS_PER_GROUP = 4
ATTN_HEAD_DIM = 64
ATTN_WIDTH = ATTN_HEADS_PER_GROUP * len(ATTN_GROUPS) * ATTN_HEAD_DIM
ATTN_OUT_WIDTH = ATTN_HEADS_PER_GROUP * ATTN_HEAD_DIM
ROPE_THETA = 10000.0
D_FF = 2816
CONV_WIDTH = 3
EPS = 1e-6
IN_WIDTHS = (GLA_QK_WIDTH, GLA_QK_WIDTH, GLA_V_WIDTH, GLA_V_WIDTH, GLA_LOWRANK,
             ATTN_WIDTH, ATTN_WIDTH, ATTN_WIDTH, D_MODEL, D_MODEL)

LANES = 128
VMEM_LIMIT_BYTES = 56 * 1024 * 1024

TM_IN = 512
GLA_CHUNK = 64
ATTN_BLK = 128
ATTN_SPAN = 2048
ATTN_MERGE_ROWS = 256
TM_MERGE = 1024
MERGE_SPLIT = 2
TM_FFN = 512
FFN_CHUNK = 256
FFN_HALO = 16
FFN_DOWN_SLAB = 256
MASK_VALUE = -1e30


def _params(*sem):
    return pltpu.CompilerParams(dimension_semantics=sem, vmem_limit_bytes=VMEM_LIMIT_BYTES)


def _split_hi_lo(a):
    hi = a.astype(BF16)
    lo = (a - hi.astype(F32)).astype(BF16)
    return hi, lo


def _mod_kernel(c_ref, w_ref, b_ref, o_ref):
    c = c_ref[...]
    s = c * (1.0 / (1.0 + jnp.exp(-c)))
    s_hi, s_lo = _split_hi_lo(s)
    lhs = jnp.concatenate([s_hi, s_lo], axis=0)
    w_hi, w_lo = _split_hi_lo(w_ref[...])
    acc = jnp.dot(lhs, w_hi, preferred_element_type=F32) + jnp.dot(lhs, w_lo, preferred_element_type=F32)
    o_ref[...] = acc[0:8] + acc[8:16] + b_ref[...]


def _modulation(c, ada_w, ada_b):
    B, D = c.shape
    N = ada_w.shape[1]
    tn = 1024
    c8 = jnp.pad(c, ((0, 8 - B), (0, 0)))
    out = pl.pallas_call(
        _mod_kernel,
        grid=(N // tn,),
        in_specs=[pl.BlockSpec((8, D), lambda j: (0, 0)),
                  pl.BlockSpec((D, tn), lambda j: (0, j)),
                  pl.BlockSpec((1, tn), lambda j: (0, j))],
        out_specs=pl.BlockSpec((8, tn), lambda j: (0, j)),
        out_shape=jax.ShapeDtypeStruct((8, N), F32),
        compiler_params=_params("parallel"),
        name="modulation",
    )(c8, ada_w, ada_b.reshape(1, N))
    return out


def _mod_spec(which, D):
    return pl.BlockSpec((8, D), lambda i: (0, which))


MOD_SHIFT1, MOD_SCALE1, MOD_GATE1, MOD_SHIFT2, MOD_SCALE2, MOD_GATE2 = range(6)


def _rope_kernel(pos_ref, invf_ref, *refs):
    n_cast = (len(refs) - 2) // 2
    cos_ref, sin_ref = refs[n_cast:n_cast + 2]
    for src, dst in zip(refs[:n_cast], refs[n_cast + 2:]):
        dst[...] = src[...].astype(dst.dtype)

    half = ATTN_HEAD_DIM // 2
    groups = LANES // half
    tr = pos_ref.shape[1]
    pos = jnp.concatenate([pos_ref[...].astype(F32), jnp.zeros((8 - groups, tr), F32)], axis=0)
    pos_t = jnp.transpose(pos)
    lane = lax.broadcasted_iota(jnp.int32, (tr, LANES), 1)
    group = lane // half
    pos_dense = jnp.zeros((tr, LANES), F32)
    for q in range(groups):
        pos_dense = jnp.where(group == q, pos_t[:, q:q + 1], pos_dense)
    ang = pos_dense * invf_ref[...]
    first_half = (lane % ATTN_HEAD_DIM) < half
    for table, out_ref, signed in ((jnp.cos(ang), cos_ref, False), (jnp.sin(ang), sin_ref, True)):
        for q in range(groups):
            spread = jnp.where(group == q, table, 0.0)
            shift = groups // 2
            while shift >= 1:
                spread = spread + pltpu.roll(spread, shift * half, axis=1)
                shift //= 2
            out_ref[q] = jnp.where(first_half, -spread, spread) if signed else spread


def _rope_tables(positions, weights):
    T = positions.size
    half = ATTN_HEAD_DIM // 2
    groups = LANES // half
    inv_freq = ROPE_THETA ** (-jnp.arange(half, dtype=F32) / half)
    invf = jnp.tile(inv_freq, groups).reshape(1, LANES)
    per = T // groups
    tr = 512
    steps = per // tr
    bf16_rows = 16
    slab = lambda w: pl.BlockSpec((-(-w.shape[0] // (steps * bf16_rows)) * bf16_rows, w.shape[1]), lambda i: (i, 0))
    outs = pl.pallas_call(
        _rope_kernel,
        grid=(steps,),
        in_specs=[pl.BlockSpec((groups, tr), lambda i: (0, i)),
                  pl.BlockSpec((1, LANES), lambda i: (0, 0))] + [slab(w) for w in weights],
        out_specs=[pl.BlockSpec((groups, tr, LANES), lambda i: (0, i, 0))] * 2 + [slab(w) for w in weights],
        out_shape=[jax.ShapeDtypeStruct((groups, per, LANES), F32)] * 2
                  + [jax.ShapeDtypeStruct(w.shape, BF16) for w in weights],
        compiler_params=_params("parallel"),
        name="rope_tables",
    )(positions.reshape(groups, per), invf, *weights)
    return outs[0].reshape(T, LANES), outs[1].reshape(T, LANES), outs[2:]


def _rmsnorm_mod(x, w, scale, shift):
    ms = jnp.mean(x * x, axis=-1, keepdims=True)
    return (x * lax.rsqrt(ms + EPS) * w) * (1.0 + scale) + shift


def _rotate_half_pairs(t, cos, sin_signed):
    lane = lax.broadcasted_iota(jnp.int32, t.shape, 1)
    first_half = (lane % ATTN_HEAD_DIM) < (ATTN_HEAD_DIM // 2)
    from_right = pltpu.roll(t, LANES - ATTN_HEAD_DIM // 2, axis=1)
    from_left = pltpu.roll(t, ATTN_HEAD_DIM // 2, axis=1)
    return t * cos + jnp.where(first_half, from_right, from_left) * sin_signed


def _store_residue_major(ref, perm_ref, slab, col, dilation):
    if dilation == 1:
        ref[:, col:col + LANES] = slab.astype(ref.dtype)
        return
    perm_ref[...] = slab
    n = slab.shape[0] // dilation
    for p in range(dilation):
        ref[p * n:(p + 1) * n, col:col + LANES] = perm_ref[pl.ds(p, n, stride=dilation), :].astype(ref.dtype)


IN_ALIGNED = sum(IN_WIDTHS[:4])
IN_SHIFTED_START = IN_ALIGNED + GLA_LOWRANK


def _in_proj_gla_kernel(x_ref, n1w_ref, scale_ref, shift_ref, cos_ref, sin_ref, wt_ref, w2_ref, b2_ref, nw_ref,
                        og_ref, q1_ref, q2_ref, q3_ref, k1_ref, k2_ref, k3_ref, v1_ref, v2_ref, v3_ref,
                        ma_ref, mb_ref,
                        perm_ref, gq_s, gk_s, gv_s, gr_s, la_s, state_ref,
                        *, tiles_per_seq, n_tiles):
    i = pl.program_id(0)
    nt = (((1,), (1,)), ((), ()))

    @pl.when(i == 0)
    def _():
        for ref in (gq_s, gk_s, gv_s, gr_s, la_s, state_ref):
            ref[...] = jnp.zeros_like(ref)

    first_of_seq = (i + tiles_per_seq - 1) % tiles_per_seq == 0
    chunks = _gla_chunks(gq_s, gk_s, gv_s, la_s, gr_s, nw_ref, og_ref, state_ref, first_of_seq)

    @pl.when(i == n_tiles)
    def _():
        for chunk in chunks:
            chunk()

    @pl.when(i < n_tiles)
    def _():
        b = i // tiles_per_seq
        h = _rmsnorm_mod(x_ref[...], n1w_ref[...], scale_ref[pl.ds(b, 1), :], shift_ref[pl.ds(b, 1), :]).astype(BF16)

        def rows_proj(row, width):
            return lax.dot_general(h, wt_ref[row:row + width, :], nt, preferred_element_type=F32)

        def proj(col, width):
            return rows_proj(IN_SHIFTED_START + col, width)

        cos = cos_ref[...]
        sin = sin_ref[...]
        q_scale = ATTN_HEAD_DIM ** -0.5
        dilations = [d for _, d in ATTN_GROUPS]

        def rope_piece(ref, col, scale, dilation):
            def run():
                t = proj(col, ATTN_OUT_WIDTH)
                for s in range(0, ATTN_OUT_WIDTH, LANES):
                    rot = _rotate_half_pairs(t[:, s:s + LANES], cos, sin) * scale
                    _store_residue_major(ref, perm_ref, rot, s, dilation)
            return run

        def value_piece(ref, col, dilation):
            def run():
                t = proj(col, ATTN_OUT_WIDTH)
                for s in range(0, ATTN_OUT_WIDTH, LANES):
                    _store_residue_major(ref, perm_ref, t[:, s:s + LANES], s, dilation)
            return run

        def gate_piece(ref, col, s):
            def run():
                z = proj(col + s, 512)
                ref[:, s:s + 512] = (1.0 / (1.0 + jnp.exp(-z))).astype(ref.dtype)
            return run

        pieces = []
        col = 0
        for refs, scale in (((q1_ref, q2_ref, q3_ref), q_scale), ((k1_ref, k2_ref, k3_ref), 1.0)):
            for g, ref in enumerate(refs):
                pieces.append(rope_piece(ref, col, scale, dilations[g]))
                col += ATTN_OUT_WIDTH
        for g, ref in enumerate((v1_ref, v2_ref, v3_ref)):
            pieces.append(value_piece(ref, col, dilations[g]))
            col += ATTN_OUT_WIDTH
        for ref in (ma_ref, mb_ref):
            for s in range(0, D_MODEL, 512):
                pieces.append(gate_piece(ref, col, s))
            col += D_MODEL

        for n in range(max(len(pieces), len(chunks))):
            if n < len(pieces):
                pieces[n]()
            if n < len(chunks):
                chunks[n]()

        col = 0
        for ref, width in ((gq_s, GLA_QK_WIDTH), (gk_s, GLA_QK_WIDTH)):
            ref[...] = rows_proj(col, width).astype(ref.dtype)
            col += width
        for ref in (gv_s, gr_s):
            for s in range(0, GLA_V_WIDTH, 512):
                ref[:, s:s + 512] = rows_proj(col + s, 512).astype(ref.dtype)
            col += GLA_V_WIDTH
        g_lr = rows_proj(IN_ALIGNED, LANES)
        g_hi, g_lo = _split_hi_lo(g_lr)
        w2_hi, w2_lo = _split_hi_lo(w2_ref[...])
        z = (jnp.dot(g_hi, w2_hi, preferred_element_type=F32) + jnp.dot(g_lo, w2_hi, preferred_element_type=F32)
             + jnp.dot(g_hi, w2_lo, preferred_element_type=F32)) + b2_ref[...]
        log_sig = jnp.minimum(z, 0.0) - jnp.log(1.0 + jnp.exp(-jnp.abs(z)))
        la_s[...] = log_sig * (1.0 / GLA_TAU)


def _gla_chunks(q_ref, k_ref, v_ref, la_ref, gr_ref, nw_ref, o_ref, state_ref, first_of_seq):
    C = GLA_CHUNK
    heads = range(GLA_HEADS)
    rows = lax.broadcasted_iota(jnp.int32, (C, C), 0)
    cols = lax.broadcasted_iota(jnp.int32, (C, C), 1)
    causal = cols <= rows
    tri = causal.astype(BF16)
    q_scale = GLA_DK ** -0.5
    nw = jnp.concatenate([nw_ref[...]] * GLA_HEADS, axis=1)
    nt = (((1,), (1,)), ((), ()))
    ks = lambda a, h: a[:, h * GLA_DK:(h + 1) * GLA_DK]
    vs = lambda a, h: a[:, h * GLA_DV:(h + 1) * GLA_DV]

    def run_chunk(c):
        sl = pl.ds(c * C, C)
        q = q_ref[sl, :].astype(F32) * q_scale
        k = k_ref[sl, :].astype(F32)
        v = v_ref[sl, :]
        la_hi, la_lo = _split_hi_lo(la_ref[sl, :])
        cum = jnp.dot(tri, la_hi, preferred_element_type=F32) + jnp.dot(tri, la_lo, preferred_element_type=F32)
        mid = cum[C // 2 - 1:C // 2, :]
        last = cum[C - 1:C, :]
        qg = (q * jnp.exp(cum - mid)).astype(BF16)
        kg = (k * jnp.exp(mid - cum)).astype(BF16)
        q_in = (q * jnp.exp(cum)).astype(BF16)
        k_out = k * jnp.exp(last - cum)
        decay_row = jnp.broadcast_to(jnp.exp(last), (8, GLA_QK_WIDTH))
        g = gr_ref[sl, :].astype(F32)
        gate = nw * (g * (1.0 / (1.0 + jnp.exp(-g))))

        attn = [lax.dot_general(ks(qg, h), ks(kg, h), nt, preferred_element_type=F32) for h in heads]
        attn = [jnp.where(causal, a, 0.0).astype(BF16) for a in attn]
        if c == 0:
            state = [jnp.where(first_of_seq, 0.0, state_ref[h]) for h in heads]
        else:
            state = [state_ref[h] for h in heads]
        o = [jnp.dot(attn[h], vs(v, h), preferred_element_type=F32)
             + jnp.dot(ks(q_in, h), state[h].astype(BF16), preferred_element_type=F32) for h in heads]
        k_out_t = [jnp.transpose(ks(k_out, h)).astype(BF16) for h in heads]
        decay = [jnp.transpose(ks(decay_row, h))[:, 0:1] for h in heads]
        for h in heads:
            state_ref[h] = decay[h] * state[h] + jnp.dot(k_out_t[h], vs(v, h), preferred_element_type=F32)
        ms = [jnp.mean(o[h] * o[h], axis=-1, keepdims=True) for h in heads]
        for h in heads:
            o_ref[sl, h * GLA_DV:(h + 1) * GLA_DV] = (o[h] * lax.rsqrt(ms[h] + EPS) * vs(gate, h)).astype(o_ref.dtype)

    return [functools.partial(run_chunk, c) for c in range(q_ref.shape[0] // C)]


def _in_proj_gla(x2, n1w, mod, cos, sin, w_in_t, w2, b2, gla_norm_w, B):
    T, D = x2.shape
    tm = TM_IN
    n_tiles = T // tm
    cur = lambda i: (jnp.minimum(i, n_tiles - 1), 0)
    lag = lambda i: (jnp.maximum(i - 1, 0), 0)
    const = lambda i: (0, 0)
    widths = (ATTN_OUT_WIDTH,) * 9 + (D_MODEL, D_MODEL)
    single = dict(pipeline_mode=pl.Buffered(1))
    w2p = jnp.pad(w2, ((0, LANES - GLA_LOWRANK), (0, 0)))
    outs = pl.pallas_call(
        functools.partial(_in_proj_gla_kernel, tiles_per_seq=T // B // tm, n_tiles=n_tiles),
        grid=(n_tiles + 1,),
        in_specs=[pl.BlockSpec((tm, D), cur),
                  pl.BlockSpec((1, D), const),
                  _mod_spec(MOD_SCALE1, D),
                  _mod_spec(MOD_SHIFT1, D),
                  pl.BlockSpec((tm, LANES), cur),
                  pl.BlockSpec((tm, LANES), cur),
                  pl.BlockSpec(w_in_t.shape, const, **single),
                  pl.BlockSpec(w2p.shape, const, **single),
                  pl.BlockSpec(b2.shape, const, **single),
                  pl.BlockSpec(gla_norm_w.shape, const, **single)],
        out_specs=[pl.BlockSpec((tm, GLA_V_WIDTH), lag)] + [pl.BlockSpec((tm, w), cur) for w in widths],
        out_shape=[jax.ShapeDtypeStruct((T, GLA_V_WIDTH), BF16)]
                  + [jax.ShapeDtypeStruct((T, w), BF16) for w in widths],
        scratch_shapes=[pltpu.VMEM((tm, LANES), F32),
                        pltpu.VMEM((tm, GLA_QK_WIDTH), BF16),
                        pltpu.VMEM((tm, GLA_QK_WIDTH), BF16),
                        pltpu.VMEM((tm, GLA_V_WIDTH), BF16),
                        pltpu.VMEM((tm, GLA_V_WIDTH), BF16),
                        pltpu.VMEM((tm, GLA_QK_WIDTH), F32),
                        pltpu.VMEM((GLA_HEADS, GLA_DK, GLA_DV), F32)],
        compiler_params=_params("arbitrary"),
        name="in_proj_gla",
    )(x2, n1w, mod, mod, cos, sin, w_in_t, w2p, b2, gla_norm_w)
    return outs[0], outs[1:]


def _attn_kernel(q1_ref, q2_ref, q3_ref, k1c_ref, k1p_ref, k2c_ref, k2p_ref, k3c_ref, k3p_ref,
                 v1c_ref, v1p_ref, v2c_ref, v2p_ref, v3c_ref, v3p_ref, wup_ref, wdn_ref,
                 o_ref, wup_bf_ref, wdn_bf_ref, *slabs):
    wup_bf_ref[...] = wup_ref[...].astype(wup_bf_ref.dtype)
    wdn_bf_ref[...] = wdn_ref[...].astype(wdn_bf_ref.dtype)

    blk = ATTN_BLK
    tile = TM_IN
    span = ATTN_SPAN
    tiles = span // tile
    r2, r3 = ATTN_GROUPS[1][1], ATTN_GROUPS[2][1]
    per3 = tile // r3
    not_first_span = pl.program_id(1) > 0
    npair = ATTN_OUT_WIDTH // LANES
    o1_s, o2_s, o3_s, l1_s, l2_s, l3_s = [slabs[n * npair:(n + 1) * npair] for n in range(6)]

    rows = lax.broadcasted_iota(jnp.int32, (blk, 2 * blk), 0)
    cols = lax.broadcasted_iota(jnp.int32, (blk, 2 * blk), 1)
    band = (cols >= rows) & (cols <= rows + blk)
    in_cur = cols >= blk
    left = lax.broadcasted_iota(jnp.int32, (blk, LANES), 1) < ATTN_HEAD_DIM
    ones = jnp.ones((2 * blk, LANES), BF16)
    nt = (((1,), (1,)), ((), ()))

    def pair_attend(q_pair, k_cat, v_cat, has_prev):
        zero = jnp.zeros_like(q_pair)
        q2 = jnp.concatenate([jnp.where(left, q_pair, zero), jnp.where(left, zero, q_pair)], axis=0)
        s = lax.dot_general(q2, k_cat, nt, preferred_element_type=F32)
        valid = band & (in_cur | has_prev)
        s = jnp.where(jnp.concatenate([valid, valid], axis=0), s, MASK_VALUE)
        m = jnp.max(s, axis=-1, keepdims=True)
        p = jnp.exp(s - m).astype(BF16)
        r = jnp.dot(p, jnp.concatenate([v_cat, ones], axis=1), preferred_element_type=F32)
        acc = jnp.where(left, r[:blk, :LANES], r[blk:, :LANES])
        den = jnp.where(left, r[:blk, LANES:], r[blk:, LANES:])
        m_pair = jnp.where(left, jnp.broadcast_to(m[:blk], (blk, LANES)), jnp.broadcast_to(m[blk:], (blk, LANES)))
        return acc / den, m_pair + jnp.log(den)

    def with_prev(cur_ref, prev_ref, r0, back, prev_rows, first, cs):
        before = cur_ref[pl.ds(pl.multiple_of(jnp.maximum(r0 - back, 0), blk), blk), cs]
        if prev_rows is not None:
            before = jnp.where(first, prev_ref[prev_rows, cs], before)
        return jnp.concatenate([before, cur_ref[pl.ds(r0, blk), cs]], axis=0)

    def tile_body(t, carry):
        for p in range(tiles):
            i = t * tiles + p
            r0 = pl.multiple_of(i * blk, blk)
            for j in range(npair):
                cs = slice(j * LANES, (j + 1) * LANES)
                rows1 = slice(0, blk) if p == 0 else None
                o, lse = pair_attend(q1_ref[pl.ds(r0, blk), cs],
                                     with_prev(k1c_ref, k1p_ref, r0, blk, rows1, t == 0, cs),
                                     with_prev(v1c_ref, v1p_ref, r0, blk, rows1, t == 0, cs),
                                     not_first_span | (i > 0))
                o1_s[j][pl.ds(r0, blk), :] = o
                l1_s[j][pl.ds(r0, blk), :] = lse
                rows2 = slice(p * blk, (p + 1) * blk)
                o, lse = pair_attend(q2_ref[pl.ds(r0, blk), cs],
                                     with_prev(k2c_ref, k2p_ref, r0, tile, rows2, t == 0, cs),
                                     with_prev(v2c_ref, v2p_ref, r0, tile, rows2, t == 0, cs),
                                     not_first_span | (t > 0))
                tok = pl.ds(pl.multiple_of(t * tile, tile) + p, blk, stride=r2)
                o2_s[j][tok, :] = o
                l2_s[j][tok, :] = lse
                rr = pl.multiple_of(i * per3, per3)
                gather = lambda ref: jnp.concatenate(
                    [ref[pl.ds(rr + u * tile, per3), cs] for u in range(tiles)], axis=0)
                k_cat = jnp.concatenate([gather(k3p_ref), gather(k3c_ref)], axis=0)
                v_cat = jnp.concatenate([gather(v3p_ref), gather(v3c_ref)], axis=0)
                o, lse = pair_attend(gather(q3_ref), k_cat, v_cat, not_first_span)
                tok = pl.ds(i, blk, stride=r3)
                o3_s[j][tok, :] = o
                l3_s[j][tok, :] = lse
        return carry

    lax.fori_loop(0, tiles, tile_body, 0)

    for j in range(npair):
        for n in range(span // ATTN_MERGE_ROWS):
            rs = slice(n * ATTN_MERGE_ROWS, (n + 1) * ATTN_MERGE_ROWS)
            l1, l2, l3 = l1_s[j][rs, :], l2_s[j][rs, :], l3_s[j][rs, :]
            m = jnp.maximum(jnp.maximum(l1, l2), l3)
            e1, e2, e3 = jnp.exp(l1 - m), jnp.exp(l2 - m), jnp.exp(l3 - m)
            num = e1 * o1_s[j][rs, :] + e2 * o2_s[j][rs, :] + e3 * o3_s[j][rs, :]
            o_ref[rs, j * LANES:(j + 1) * LANES] = (num / (e1 + e2 + e3)).astype(o_ref.dtype)


def _dilated_attention(qs, ks, vs, w_up, w_down, B, S):
    span = ATTN_SPAN
    nsp = S // span
    gw = ATTN_OUT_WIDTH
    T = B * S
    cur = lambda b, s: (b * nsp + s, 0)
    steps = B * nsp
    slab = lambda w: pl.BlockSpec((w.shape[0] // steps, w.shape[1]), cur)

    def prev(rows):
        per = span // rows
        return pl.BlockSpec((rows, gw), lambda b, s: (jnp.maximum((b * nsp + s) * per - 1, 0), 0))

    halos = (ATTN_BLK, TM_IN, span)
    full = pl.BlockSpec((span, gw), cur)
    kv_specs = []
    for h in halos:
        kv_specs += [full, prev(h)]
    kv_args = lambda arrs: [a for arr in arrs for a in (arr, arr)]
    return pl.pallas_call(
        _attn_kernel,
        grid=(B, nsp),
        in_specs=[full] * 3 + kv_specs + kv_specs + [slab(w_up), slab(w_down)],
        out_specs=[full, slab(w_up), slab(w_down)],
        out_shape=[jax.ShapeDtypeStruct((T, gw), BF16), jax.ShapeDtypeStruct(w_up.shape, BF16),
                   jax.ShapeDtypeStruct(w_down.shape, BF16)],
        scratch_shapes=[pltpu.VMEM((span, LANES), F32)] * (6 * (gw // LANES)),
        compiler_params=_params("parallel", "parallel"),
        name="dilated_attn",
    )(*qs, *kv_args(ks), *kv_args(vs), w_up, w_down)


def _merge_kernel(x_ref, og_ref, oa_ref, ma_ref, mb_ref,
                  gate_ref, scale_ref, shift_ref, n2w_ref, wg_ref, wa_ref, wo_ref, x1_ref, h2_ref, *, tiles_per_seq):
    b = pl.ds(pl.program_id(0) // tiles_per_seq, 1)
    gate, scale, shift = gate_ref[b, :], scale_ref[b, :], shift_ref[b, :]
    sub = x_ref.shape[0] // MERGE_SPLIT
    for r in range(MERGE_SPLIT):
        rs = slice(r * sub, (r + 1) * sub)
        y_attn = jnp.dot(oa_ref[rs, :], wa_ref[...], preferred_element_type=F32)
        y_gla = jnp.dot(og_ref[rs, :], wg_ref[...], preferred_element_type=F32)
        mixed = (ma_ref[rs, :].astype(F32) * y_gla + mb_ref[rs, :].astype(F32) * y_attn).astype(BF16)
        x1 = x_ref[rs, :] + gate * jnp.dot(mixed, wo_ref[...], preferred_element_type=F32)
        x1_ref[rs, :] = x1
        h2_ref[rs, :] = _rmsnorm_mod(x1, n2w_ref[...], scale, shift).astype(h2_ref.dtype)


def _merge(x2, og, oa, ma, mb, mod, n2w, wg, wa, wo, B):
    T, D = x2.shape
    tm = TM_MERGE
    row = lambda i: (i, 0)
    const = lambda i: (0, 0)
    return pl.pallas_call(
        functools.partial(_merge_kernel, tiles_per_seq=T // B // tm),
        grid=(T // tm,),
        in_specs=[pl.BlockSpec((tm, D), row), pl.BlockSpec((tm, GLA_V_WIDTH), row),
                  pl.BlockSpec((tm, ATTN_OUT_WIDTH), row)]
                 + [pl.BlockSpec((tm, D), row)] * 2
                 + [_mod_spec(MOD_GATE1, D), _mod_spec(MOD_SCALE2, D), _mod_spec(MOD_SHIFT2, D)]
                 + [pl.BlockSpec((1, D), const),
                    pl.BlockSpec(wg.shape, const), pl.BlockSpec(wa.shape, const), pl.BlockSpec(wo.shape, const)],
        out_specs=[pl.BlockSpec((tm, D), row)] * 2,
        out_shape=[jax.ShapeDtypeStruct((T, D), F32), jax.ShapeDtypeStruct((T, D), BF16)],
        compiler_params=_params("parallel"),
        name="merge",
    )(x2, og, oa, ma, mb, mod, mod, mod, n2w, wg, wa, wo)


def _ffn_kernel(h_ref, hprev_ref, x1_ref, wup_ref, cw_ref, cb_ref, wd_ref, gate_ref, fw_ref, o_ref,
                hcat_s, u_s, hid_s, order_s, *, tiles_per_seq):
    i = pl.program_id(0)
    tm = h_ref.shape[0]
    halo = hprev_ref.shape[0]
    tf = FFN_CHUNK
    half = tm // 2
    hcat_s[0:halo] = jnp.where(i % tiles_per_seq == 0, jnp.zeros_like(hprev_ref[...]), hprev_ref[...])
    hcat_s[halo:] = h_ref[...]
    nchunk = D_FF // tf
    branches = (0, D_FF)

    def up_project(j):
        for n, off in enumerate(branches):
            u = jnp.dot(hcat_s[...], wup_ref[:, off + j * tf:off + (j + 1) * tf], preferred_element_type=F32)
            for c in range(tf // LANES):
                u_s[j % 2, n, c] = u[:, c * LANES:(c + 1) * LANES]

    def conv(j, n, c, parity):
        u = u_s.at[j % 2, n, c]
        cs = slice(branches[n] + j * tf + c * LANES, branches[n] + j * tf + (c + 1) * LANES)
        tap = lambda back: u[pl.ds(halo + parity - back, half, stride=2), :]
        k = 2.0 ** -0.5
        return (cb_ref[:, cs] * k + (cw_ref[0:1, cs] * k) * tap(2) + (cw_ref[1:2, cs] * k) * tap(1)
                + (cw_ref[2:3, cs] * k) * tap(0))

    up_project(0)
    for j in range(nchunk):
        if j + 1 < nchunk:
            up_project(j + 1)
        for c in range(tf // LANES):
            for parity in range(2):
                t = conv(j, 1, c, parity)
                hidden = (t * (1.0 + lax.erf(t))) * conv(j, 0, c, parity)
                hid_s[parity * half:(parity + 1) * half, j * tf + c * LANES:j * tf + (c + 1) * LANES] = (
                    hidden.astype(hid_s.dtype))

    gate = gate_ref[pl.ds(i // tiles_per_seq, 1), :]
    d_model = o_ref.shape[1]
    sq = jnp.zeros((tm, 1), F32)
    for n in range(d_model // FFN_DOWN_SLAB):
        ns = slice(n * FFN_DOWN_SLAB, (n + 1) * FFN_DOWN_SLAB)
        down = jnp.dot(hid_s[...], wd_ref[:, ns], preferred_element_type=F32)
        for c in range(FFN_DOWN_SLAB // LANES):
            g = n * (FFN_DOWN_SLAB // LANES) + c
            cs = slice(g * LANES, (g + 1) * LANES)
            for parity in range(2):
                order_s[g, pl.ds(parity, half, stride=2), :] = (
                    down[parity * half:(parity + 1) * half, c * LANES:(c + 1) * LANES])
            x2 = x1_ref[:, cs] + gate[:, cs] * order_s[g]
            sq = sq + jnp.sum(x2 * x2, axis=-1, keepdims=True)
            o_ref[:, cs] = x2
    o_ref[...] = o_ref[...] * lax.rsqrt(sq * (1.0 / d_model) + EPS) * fw_ref[...]


def _ffn(h2, x1, w_up, conv_w, conv_b, w_down, mod, final_w, B):
    T, D = x1.shape
    tm, tf, halo = TM_FFN, FFN_CHUNK, FFN_HALO
    row = lambda i: (i, 0)
    const = lambda i: (0, 0)
    single = dict(pipeline_mode=pl.Buffered(1))
    return pl.pallas_call(
        functools.partial(_ffn_kernel, tiles_per_seq=T // B // tm),
        grid=(T // tm,),
        in_specs=[pl.BlockSpec((tm, D), row),
                  pl.BlockSpec((halo, D), lambda i: (jnp.maximum(i * (tm // halo) - 1, 0), 0)),
                  pl.BlockSpec((tm, D), row),
                  pl.BlockSpec(w_up.shape, const, **single),
                  pl.BlockSpec(conv_w.shape, const, **single),
                  pl.BlockSpec(conv_b.shape, const, **single),
                  pl.BlockSpec(w_down.shape, const, **single),
                  _mod_spec(MOD_GATE2, D),
                  pl.BlockSpec((1, D), const)],
        out_specs=pl.BlockSpec((tm, D), row),
        out_shape=jax.ShapeDtypeStruct((T, D), F32),
        scratch_shapes=[pltpu.VMEM((halo + tm, D), BF16),
                        pltpu.VMEM((2, 2, tf // LANES, halo + tm, LANES), F32),
                        pltpu.VMEM((tm, D_FF), BF16),
                        pltpu.VMEM((D // LANES, tm, LANES), F32)],
        compiler_params=_params("parallel"),
        name="ffn",
    )(h2, h2, x1, w_up, conv_w, conv_b, w_down, mod, final_w)


def kernel(x, c, positions, ada_w, ada_b, norm1_w, w_in, gla_gate_w2, gla_gate_b, gla_norm_w, w_gla_branch,
           w_attn_branch, w_out, norm2_w, w_up, conv_w, conv_b, w_down, final_norm_w):
    B, S, D = x.shape
    T = B * S
    depth = ada_w.shape[0]
    assert depth == 1, "the final norm is fused into the (single) layer's ffn"
    assert all(window // dilation == ATTN_BLK for window, dilation in ATTN_GROUPS)
    x2 = x.reshape(T, D)
    for layer in range(depth):
        cos, sin, (w_in_bf, w_gla_bf, w_attn_bf, w_out_bf) = _rope_tables(
            positions, (w_in[layer].T, w_gla_branch[layer], w_attn_branch[layer], w_out[layer]))
        mod = _modulation(c, ada_w[layer], ada_b[layer])

        og, (q1, q2, q3, k1, k2, k3, v1, v2, v3, ma, mb) = _in_proj_gla(
            x2, norm1_w[layer].reshape(1, D), mod, cos, sin, w_in_bf, gla_gate_w2[layer],
            gla_gate_b[layer].reshape(1, -1), gla_norm_w[layer].reshape(1, -1), B)
        oa, w_up_bf, w_down_bf = _dilated_attention((q1, q2, q3), (k1, k2, k3), (v1, v2, v3),
                                                    w_up[layer], w_down[layer], B, S)

        x1, h2 = _merge(x2, og, oa, ma, mb, mod, norm2_w[layer].reshape(1, D), w_gla_bf, w_attn_bf, w_out_bf, B)

        x2 = _ffn(h2, x1, w_up_bf, conv_w[layer], conv_b[layer].reshape(1, -1),
                  w_down_bf, mod, final_norm_w.reshape(1, D), B)
    return x2.reshape(B, S, D)
```

```python
import functools

import jax
import jax.numpy as jnp
from jax import lax
from jax.experimental import pallas as pl
from jax.experimental.pallas import tpu as pltpu

F32 = jnp.float32
BF16 = jnp.bfloat16

D_MODEL = 1024
GLA_HEADS = 4
GLA_DK = 128
GLA_DV = 256
GLA_LOWRANK = 16
GLA_TAU = 16.0
GLA_QK_WIDTH = GLA_HEADS * GLA_DK
GLA_V_WIDTH = GLA_HEADS * GLA_DV
ATTN_GROUPS = ((128, 1), (512, 4), (2048, 16))
ATTN_HEADS_PER_GROUP = 4
ATTN_HEAD_DIM = 64
ATTN_WIDTH = ATTN_HEADS_PER_GROUP * len(ATTN_GROUPS) * ATTN_HEAD_DIM
ATTN_OUT_WIDTH = ATTN_HEADS_PER_GROUP * ATTN_HEAD_DIM
ROPE_THETA = 10000.0
D_FF = 2816
CONV_WIDTH = 3
EPS = 1e-6
IN_WIDTHS = (GLA_QK_WIDTH, GLA_QK_WIDTH, GLA_V_WIDTH, GLA_V_WIDTH, GLA_LOWRANK,
             ATTN_WIDTH, ATTN_WIDTH, ATTN_WIDTH, D_MODEL, D_MODEL)

LANES = 128
VMEM_LIMIT_BYTES = 56 * 1024 * 1024

TM_IN = 512
GLA_CHUNK = 64
ATTN_BLK = 128
ATTN_SPAN = 2048
ATTN_MERGE_ROWS = 256
TM_MERGE = 1024
MERGE_SPLIT = 2
TM_FFN = 512
FFN_CHUNK = 256
FFN_HALO = 16
FFN_DOWN_SLAB = 256
MASK_VALUE = -1e30
LOG2_E = 1.4426950408889634
LN_2 = 0.6931471805599453


def _params(*sem):
    return pltpu.CompilerParams(dimension_semantics=sem, vmem_limit_bytes=VMEM_LIMIT_BYTES)


def _split_hi_lo(a):
    hi = a.astype(BF16)
    lo = (a - hi.astype(F32)).astype(BF16)
    return hi, lo


def _mod_kernel(c_ref, w_ref, b_ref, o_ref):
    c = c_ref[...]
    s = c * (1.0 / (1.0 + jnp.exp(-c)))
    s_hi, s_lo = _split_hi_lo(s)
    lhs = jnp.concatenate([s_hi, s_lo], axis=0)
    w_hi, w_lo = _split_hi_lo(w_ref[...])
    acc = jnp.dot(lhs, w_hi, preferred_element_type=F32) + jnp.dot(lhs, w_lo, preferred_element_type=F32)
    o_ref[...] = acc[0:8] + acc[8:16] + b_ref[...]


def _modulation(c, ada_w, ada_b):
    B, D = c.shape
    N = ada_w.shape[1]
    tn = 1024
    c8 = jnp.pad(c, ((0, 8 - B), (0, 0)))
    out = pl.pallas_call(
        _mod_kernel,
        grid=(N // tn,),
        in_specs=[pl.BlockSpec((8, D), lambda j: (0, 0)),
                  pl.BlockSpec((D, tn), lambda j: (0, j)),
                  pl.BlockSpec((1, tn), lambda j: (0, j))],
        out_specs=pl.BlockSpec((8, tn), lambda j: (0, j)),
        out_shape=jax.ShapeDtypeStruct((8, N), F32),
        compiler_params=_params("parallel"),
        name="modulation",
    )(c8, ada_w, ada_b.reshape(1, N))
    return out


def _mod_spec(which, D):
    return pl.BlockSpec((8, D), lambda i: (0, which))


MOD_SHIFT1, MOD_SCALE1, MOD_GATE1, MOD_SHIFT2, MOD_SCALE2, MOD_GATE2 = range(6)


def _rope_kernel(pos_ref, invf_ref, *refs):
    n_cast = (len(refs) - 2) // 2
    cos_ref, sin_ref = refs[n_cast:n_cast + 2]
    for src, dst in zip(refs[:n_cast], refs[n_cast + 2:]):
        dst[...] = src[...].astype(dst.dtype)

    half = ATTN_HEAD_DIM // 2
    groups = LANES // half
    tr = pos_ref.shape[1]
    pos = jnp.concatenate([pos_ref[...].astype(F32), jnp.zeros((8 - groups, tr), F32)], axis=0)
    pos_t = jnp.transpose(pos)
    lane = lax.broadcasted_iota(jnp.int32, (tr, LANES), 1)
    group = lane // half
    pos_dense = jnp.zeros((tr, LANES), F32)
    for q in range(groups):
        pos_dense = jnp.where(group == q, pos_t[:, q:q + 1], pos_dense)
    ang = pos_dense * invf_ref[...]
    first_half = (lane % ATTN_HEAD_DIM) < half
    for table, out_ref, signed in ((jnp.cos(ang), cos_ref, False), (jnp.sin(ang), sin_ref, True)):
        for q in range(groups):
            spread = jnp.where(group == q, table, 0.0)
            shift = groups // 2
            while shift >= 1:
                spread = spread + pltpu.roll(spread, shift * half, axis=1)
                shift //= 2
            out_ref[q] = jnp.where(first_half, -spread, spread) if signed else spread


def _rope_tables(positions, weights):
    T = positions.size
    half = ATTN_HEAD_DIM // 2
    groups = LANES // half
    inv_freq = ROPE_THETA ** (-jnp.arange(half, dtype=F32) / half)
    invf = jnp.tile(inv_freq, groups).reshape(1, LANES)
    per = T // groups
    tr = 512
    steps = per // tr
    bf16_rows = 16
    slab = lambda w: pl.BlockSpec((-(-w.shape[0] // (steps * bf16_rows)) * bf16_rows, w.shape[1]), lambda i: (i, 0))
    outs = pl.pallas_call(
        _rope_kernel,
        grid=(steps,),
        in_specs=[pl.BlockSpec((groups, tr), lambda i: (0, i)),
                  pl.BlockSpec((1, LANES), lambda i: (0, 0))] + [slab(w) for w in weights],
        out_specs=[pl.BlockSpec((groups, tr, LANES), lambda i: (0, i, 0))] * 2 + [slab(w) for w in weights],
        out_shape=[jax.ShapeDtypeStruct((groups, per, LANES), F32)] * 2
                  + [jax.ShapeDtypeStruct(w.shape, BF16) for w in weights],
        compiler_params=_params("parallel"),
        name="rope_tables",
    )(positions.reshape(groups, per), invf, *weights)
    return outs[0].reshape(T, LANES), outs[1].reshape(T, LANES), outs[2:]


def _rmsnorm_mod(x, w, scale, shift):
    ms = jnp.mean(x * x, axis=-1, keepdims=True)
    return (x * lax.rsqrt(ms + EPS) * w) * (1.0 + scale) + shift


def _rotate_half_pairs(t, cos, sin_signed):
    lane = lax.broadcasted_iota(jnp.int32, t.shape, 1)
    first_half = (lane % ATTN_HEAD_DIM) < (ATTN_HEAD_DIM // 2)
    from_right = pltpu.roll(t, LANES - ATTN_HEAD_DIM // 2, axis=1)
    from_left = pltpu.roll(t, ATTN_HEAD_DIM // 2, axis=1)
    return t * cos + jnp.where(first_half, from_right, from_left) * sin_signed


def _store_residue_major(ref, perm_ref, slab, col, dilation):
    if dilation == 1:
        ref[:, col:col + LANES] = slab.astype(ref.dtype)
        return
    perm_ref[...] = slab
    n = slab.shape[0] // dilation
    for p in range(dilation):
        ref[p * n:(p + 1) * n, col:col + LANES] = perm_ref[pl.ds(p, n, stride=dilation), :].astype(ref.dtype)


IN_ALIGNED = sum(IN_WIDTHS[:4])
IN_SHIFTED_START = IN_ALIGNED + GLA_LOWRANK


def _in_proj_gla_kernel(x_ref, n1w_ref, scale_ref, shift_ref, cos_ref, sin_ref, wt_ref, w2_ref, b2_ref, nw_ref,
                        og_ref, q1_ref, q2_ref, q3_ref, k1_ref, k2_ref, k3_ref, v1_ref, v2_ref, v3_ref,
                        ma_ref, mb_ref,
                        perm_ref, gq_s, gk_s, gv_s, gr_s, la_s, state_ref,
                        *, tiles_per_seq, n_tiles):
    i = pl.program_id(0)
    nt = (((1,), (1,)), ((), ()))

    @pl.when(i == 0)
    def _():
        for ref in (gq_s, gk_s, gv_s, gr_s, la_s, state_ref):
            ref[...] = jnp.zeros_like(ref)

    first_of_seq = (i + tiles_per_seq - 1) % tiles_per_seq == 0
    chunks = _gla_chunks(gq_s, gk_s, gv_s, la_s, gr_s, nw_ref, og_ref, state_ref, first_of_seq)

    @pl.when(i == n_tiles)
    def _():
        for chunk in chunks:
            chunk()

    @pl.when(i < n_tiles)
    def _():
        b = i // tiles_per_seq
        h = _rmsnorm_mod(x_ref[...], n1w_ref[...], scale_ref[pl.ds(b, 1), :], shift_ref[pl.ds(b, 1), :]).astype(BF16)

        def rows_proj(row, width):
            return lax.dot_general(h, wt_ref[row:row + width, :], nt, preferred_element_type=F32)

        def proj(col, width):
            return rows_proj(IN_SHIFTED_START + col, width)

        cos = cos_ref[...]
        sin = sin_ref[...]
        q_scale = ATTN_HEAD_DIM ** -0.5 * LOG2_E
        dilations = [d for _, d in ATTN_GROUPS]

        def rope_piece(ref, col, scale, dilation):
            def run():
                t = proj(col, ATTN_OUT_WIDTH)
                for s in range(0, ATTN_OUT_WIDTH, LANES):
                    rot = _rotate_half_pairs(t[:, s:s + LANES], cos, sin) * scale
                    _store_residue_major(ref, perm_ref, rot, s, dilation)
            return run

        def value_piece(ref, col, dilation):
            def run():
                t = proj(col, ATTN_OUT_WIDTH)
                for s in range(0, ATTN_OUT_WIDTH, LANES):
                    _store_residue_major(ref, perm_ref, t[:, s:s + LANES], s, dilation)
            return run

        def gate_piece(ref, col, s):
            def run():
                z = proj(col + s, 512)
                ref[:, s:s + 512] = (1.0 / (1.0 + jnp.exp(-z))).astype(ref.dtype)
            return run

        pieces = []
        col = 0
        for refs, scale in (((q1_ref, q2_ref, q3_ref), q_scale), ((k1_ref, k2_ref, k3_ref), 1.0)):
            for g, ref in enumerate(refs):
                pieces.append(rope_piece(ref, col, scale, dilations[g]))
                col += ATTN_OUT_WIDTH
        for g, ref in enumerate((v1_ref, v2_ref, v3_ref)):
            pieces.append(value_piece(ref, col, dilations[g]))
            col += ATTN_OUT_WIDTH
        for ref in (ma_ref, mb_ref):
            for s in range(0, D_MODEL, 512):
                pieces.append(gate_piece(ref, col, s))
            col += D_MODEL

        for n in range(max(len(pieces), len(chunks))):
            if n < len(pieces):
                pieces[n]()
            if n < len(chunks):
                chunks[n]()

        col = 0
        for ref, width in ((gq_s, GLA_QK_WIDTH), (gk_s, GLA_QK_WIDTH)):
            ref[...] = rows_proj(col, width).astype(ref.dtype)
            col += width
        for ref in (gv_s, gr_s):
            for s in range(0, GLA_V_WIDTH, 512):
                ref[:, s:s + 512] = rows_proj(col + s, 512).astype(ref.dtype)
            col += GLA_V_WIDTH
        g_lr = rows_proj(IN_ALIGNED, LANES)
        g_hi, g_lo = _split_hi_lo(g_lr)
        w2_hi, w2_lo = _split_hi_lo(w2_ref[...])
        z = (jnp.dot(g_hi, w2_hi, preferred_element_type=F32) + jnp.dot(g_lo, w2_hi, preferred_element_type=F32)
             + jnp.dot(g_hi, w2_lo, preferred_element_type=F32)) + b2_ref[...]
        log_sig = jnp.minimum(z, 0.0) - jnp.log(1.0 + jnp.exp(-jnp.abs(z)))
        la_s[...] = log_sig * (LOG2_E / GLA_TAU)


def _gla_chunks(q_ref, k_ref, v_ref, la_ref, gr_ref, nw_ref, o_ref, state_ref, first_of_seq):
    C = GLA_CHUNK
    heads = range(GLA_HEADS)
    rows = lax.broadcasted_iota(jnp.int32, (C, C), 0)
    cols = lax.broadcasted_iota(jnp.int32, (C, C), 1)
    causal = cols <= rows
    tri = causal.astype(BF16)
    q_scale = GLA_DK ** -0.5
    nw = jnp.concatenate([nw_ref[...]] * GLA_HEADS, axis=1)
    nt = (((1,), (1,)), ((), ()))
    ks = lambda a, h: a[:, h * GLA_DK:(h + 1) * GLA_DK]
    vs = lambda a, h: a[:, h * GLA_DV:(h + 1) * GLA_DV]

    def run_chunk(c):
        sl = pl.ds(c * C, C)
        q = q_ref[sl, :].astype(F32) * q_scale
        k = k_ref[sl, :].astype(F32)
        v = v_ref[sl, :]
        la_hi, la_lo = _split_hi_lo(la_ref[sl, :])
        cum = jnp.dot(tri, la_hi, preferred_element_type=F32) + jnp.dot(tri, la_lo, preferred_element_type=F32)
        mid = cum[C // 2 - 1:C // 2, :]
        last = cum[C - 1:C, :]
        qg = (q * jnp.exp2(cum - mid)).astype(BF16)
        kg = (k * jnp.exp2(mid - cum)).astype(BF16)
        q_in = (q * jnp.exp2(cum)).astype(BF16)
        k_out = k * jnp.exp2(last - cum)
        decay_row = jnp.broadcast_to(jnp.exp2(last), (8, GLA_QK_WIDTH))
        g = gr_ref[sl, :].astype(F32)
        gate = nw * (g * (1.0 / (1.0 + jnp.exp(-g))))

        attn = [lax.dot_general(ks(qg, h), ks(kg, h), nt, preferred_element_type=F32) for h in heads]
        attn = [jnp.where(causal, a, 0.0).astype(BF16) for a in attn]
        if c == 0:
            state = [jnp.where(first_of_seq, 0.0, state_ref[h]) for h in heads]
        else:
            state = [state_ref[h] for h in heads]
        o = [jnp.dot(attn[h], vs(v, h), preferred_element_type=F32)
             + jnp.dot(ks(q_in, h), state[h].astype(BF16), preferred_element_type=F32) for h in heads]
        k_out_t = [jnp.transpose(ks(k_out, h)).astype(BF16) for h in heads]
        decay = [jnp.transpose(ks(decay_row, h))[:, 0:1] for h in heads]
        for h in heads:
            state_ref[h] = decay[h] * state[h] + jnp.dot(k_out_t[h], vs(v, h), preferred_element_type=F32)
        ms = [jnp.mean(o[h] * o[h], axis=-1, keepdims=True) for h in heads]
        for h in heads:
            o_ref[sl, h * GLA_DV:(h + 1) * GLA_DV] = (o[h] * lax.rsqrt(ms[h] + EPS) * vs(gate, h)).astype(o_ref.dtype)

    return [functools.partial(run_chunk, c) for c in range(q_ref.shape[0] // C)]


def _in_proj_gla(x2, n1w, mod, cos, sin, w_in_t, w2, b2, gla_norm_w, B):
    T, D = x2.shape
    tm = TM_IN
    n_tiles = T // tm
    cur = lambda i: (jnp.minimum(i, n_tiles - 1), 0)
    lag = lambda i: (jnp.maximum(i - 1, 0), 0)
    const = lambda i: (0, 0)
    widths = (ATTN_OUT_WIDTH,) * 9 + (D_MODEL, D_MODEL)
    single = dict(pipeline_mode=pl.Buffered(1))
    w2p = jnp.pad(w2, ((0, LANES - GLA_LOWRANK), (0, 0)))
    outs = pl.pallas_call(
        functools.partial(_in_proj_gla_kernel, tiles_per_seq=T // B // tm, n_tiles=n_tiles),
        grid=(n_tiles + 1,),
        in_specs=[pl.BlockSpec((tm, D), cur),
                  pl.BlockSpec((1, D), const),
                  _mod_spec(MOD_SCALE1, D),
                  _mod_spec(MOD_SHIFT1, D),
                  pl.BlockSpec((tm, LANES), cur),
                  pl.BlockSpec((tm, LANES), cur),
                  pl.BlockSpec(w_in_t.shape, const, **single),
                  pl.BlockSpec(w2p.shape, const, **single),
                  pl.BlockSpec(b2.shape, const, **single),
                  pl.BlockSpec(gla_norm_w.shape, const, **single)],
        out_specs=[pl.BlockSpec((tm, GLA_V_WIDTH), lag)] + [pl.BlockSpec((tm, w), cur) for w in widths],
        out_shape=[jax.ShapeDtypeStruct((T, GLA_V_WIDTH), BF16)]
                  + [jax.ShapeDtypeStruct((T, w), BF16) for w in widths],
        scratch_shapes=[pltpu.VMEM((tm, LANES), F32),
                        pltpu.VMEM((tm, GLA_QK_WIDTH), BF16),
                        pltpu.VMEM((tm, GLA_QK_WIDTH), BF16),
                        pltpu.VMEM((tm, GLA_V_WIDTH), BF16),
                        pltpu.VMEM((tm, GLA_V_WIDTH), BF16),
                        pltpu.VMEM((tm, GLA_QK_WIDTH), F32),
                        pltpu.VMEM((GLA_HEADS, GLA_DK, GLA_DV), F32)],
        compiler_params=_params("arbitrary"),
        name="in_proj_gla",
    )(x2, n1w, mod, mod, cos, sin, w_in_t, w2p, b2, gla_norm_w)
    return outs[0], outs[1:]


def _attn_kernel(q1_ref, q2_ref, q3_ref, k1c_ref, k1p_ref, k2c_ref, k2p_ref, k3c_ref, k3p_ref,
                 v1c_ref, v1p_ref, v2c_ref, v2p_ref, v3c_ref, v3p_ref, wup_ref, wdn_ref,
                 o_ref, wup_bf_ref, wdn_bf_ref, *slabs):
    wup_bf_ref[...] = wup_ref[...].astype(wup_bf_ref.dtype)
    wdn_bf_ref[...] = wdn_ref[...].astype(wdn_bf_ref.dtype)

    blk = ATTN_BLK
    tile = TM_IN
    span = ATTN_SPAN
    tiles = span // tile
    r2, r3 = ATTN_GROUPS[1][1], ATTN_GROUPS[2][1]
    per3 = tile // r3
    not_first_span = pl.program_id(1) > 0
    npair = ATTN_OUT_WIDTH // LANES
    o1_s, o2_s, o3_s, l1_s, l2_s, l3_s = [slabs[n * npair:(n + 1) * npair] for n in range(6)]

    rows = lax.broadcasted_iota(jnp.int32, (blk, 2 * blk), 0)
    cols = lax.broadcasted_iota(jnp.int32, (blk, 2 * blk), 1)
    band = (cols >= rows) & (cols <= rows + blk)
    in_cur = cols >= blk
    left = lax.broadcasted_iota(jnp.int32, (blk, LANES), 1) < ATTN_HEAD_DIM
    ones = jnp.ones((2 * blk, LANES), BF16)
    nt = (((1,), (1,)), ((), ()))

    def pair_attend(q_pair, k_cat, v_cat, has_prev):
        zero = jnp.zeros_like(q_pair)
        q2 = jnp.concatenate([jnp.where(left, q_pair, zero), jnp.where(left, zero, q_pair)], axis=0)
        s = lax.dot_general(q2, k_cat, nt, preferred_element_type=F32)
        valid = band & (in_cur | has_prev)
        s = jnp.where(jnp.concatenate([valid, valid], axis=0), s, MASK_VALUE)
        m = jnp.max(s, axis=-1, keepdims=True)
        p = jnp.exp2(s - m).astype(BF16)
        r = jnp.dot(p, jnp.concatenate([v_cat, ones], axis=1), preferred_element_type=F32)
        acc = jnp.where(left, r[:blk, :LANES], r[blk:, :LANES])
        den = jnp.where(left, r[:blk, LANES:], r[blk:, LANES:])
        m_pair = jnp.where(left, jnp.broadcast_to(m[:blk], (blk, LANES)), jnp.broadcast_to(m[blk:], (blk, LANES)))
        return acc / den, m_pair * LN_2 + jnp.log(den)

    def with_prev(cur_ref, prev_ref, r0, back, prev_rows, first, cs):
        before = cur_ref[pl.ds(pl.multiple_of(jnp.maximum(r0 - back, 0), blk), blk), cs]
        if prev_rows is not None:
            before = jnp.where(first, prev_ref[prev_rows, cs], before)
        return jnp.concatenate([before, cur_ref[pl.ds(r0, blk), cs]], axis=0)

    def tile_body(t, carry):
        for p in range(tiles):
            i = t * tiles + p
            r0 = pl.multiple_of(i * blk, blk)
            for j in range(npair):
                cs = slice(j * LANES, (j + 1) * LANES)
                rows1 = slice(0, blk) if p == 0 else None
                o, lse = pair_attend(q1_ref[pl.ds(r0, blk), cs],
                                     with_prev(k1c_ref, k1p_ref, r0, blk, rows1, t == 0, cs),
                                     with_prev(v1c_ref, v1p_ref, r0, blk, rows1, t == 0, cs),
                                     not_first_span | (i > 0))
                o1_s[j][pl.ds(r0, blk), :] = o
                l1_s[j][pl.ds(r0, blk), :] = lse
                rows2 = slice(p * blk, (p + 1) * blk)
                o, lse = pair_attend(q2_ref[pl.ds(r0, blk), cs],
                                     with_prev(k2c_ref, k2p_ref, r0, tile, rows2, t == 0, cs),
                                     with_prev(v2c_ref, v2p_ref, r0, tile, rows2, t == 0, cs),
                                     not_first_span | (t > 0))
                tok = pl.ds(pl.multiple_of(t * tile, tile) + p, blk, stride=r2)
                o2_s[j][tok, :] = o
                l2_s[j][tok, :] = lse
                rr = pl.multiple_of(i * per3, per3)
                gather = lambda ref: jnp.concatenate(
                    [ref[pl.ds(rr + u * tile, per3), cs] for u in range(tiles)], axis=0)
                k_cat = jnp.concatenate([gather(k3p_ref), gather(k3c_ref)], axis=0)
                v_cat = jnp.concatenate([gather(v3p_ref), gather(v3c_ref)], axis=0)
                o, lse = pair_attend(gather(q3_ref), k_cat, v_cat, not_first_span)
                tok = pl.ds(i, blk, stride=r3)
                o3_s[j][tok, :] = o
                l3_s[j][tok, :] = lse
        return carry

    lax.fori_loop(0, tiles, tile_body, 0)

    for j in range(npair):
        for n in range(span // ATTN_MERGE_ROWS):
            rs = slice(n * ATTN_MERGE_ROWS, (n + 1) * ATTN_MERGE_ROWS)
            l1, l2, l3 = l1_s[j][rs, :], l2_s[j][rs, :], l3_s[j][rs, :]
            m = jnp.maximum(jnp.maximum(l1, l2), l3)
            e1, e2, e3 = jnp.exp(l1 - m), jnp.exp(l2 - m), jnp.exp(l3 - m)
            num = e1 * o1_s[j][rs, :] + e2 * o2_s[j][rs, :] + e3 * o3_s[j][rs, :]
            o_ref[rs, j * LANES:(j + 1) * LANES] = (num / (e1 + e2 + e3)).astype(o_ref.dtype)


def _dilated_attention(qs, ks, vs, w_up, w_down, B, S):
    span = ATTN_SPAN
    nsp = S // span
    gw = ATTN_OUT_WIDTH
    T = B * S
    cur = lambda b, s: (b * nsp + s, 0)
    steps = B * nsp
    slab = lambda w: pl.BlockSpec((w.shape[0] // steps, w.shape[1]), cur)

    def prev(rows):
        per = span // rows
        return pl.BlockSpec((rows, gw), lambda b, s: (jnp.maximum((b * nsp + s) * per - 1, 0), 0))

    halos = (ATTN_BLK, TM_IN, span)
    full = pl.BlockSpec((span, gw), cur)
    kv_specs = []
    for h in halos:
        kv_specs += [full, prev(h)]
    kv_args = lambda arrs: [a for arr in arrs for a in (arr, arr)]
    return pl.pallas_call(
        _attn_kernel,
        grid=(B, nsp),
        in_specs=[full] * 3 + kv_specs + kv_specs + [slab(w_up), slab(w_down)],
        out_specs=[full, slab(w_up), slab(w_down)],
        out_shape=[jax.ShapeDtypeStruct((T, gw), BF16), jax.ShapeDtypeStruct(w_up.shape, BF16),
                   jax.ShapeDtypeStruct(w_down.shape, BF16)],
        scratch_shapes=[pltpu.VMEM((span, LANES), F32)] * (6 * (gw // LANES)),
        compiler_params=_params("parallel", "parallel"),
        name="dilated_attn",
    )(*qs, *kv_args(ks), *kv_args(vs), w_up, w_down)


def _merge_kernel(x_ref, og_ref, oa_ref, ma_ref, mb_ref,
                  gate_ref, scale_ref, shift_ref, n2w_ref, wg_ref, wa_ref, wo_ref, x1_ref, h2_ref, *, tiles_per_seq):
    b = pl.ds(pl.program_id(0) // tiles_per_seq, 1)
    gate, scale, shift = gate_ref[b, :], scale_ref[b, :], shift_ref[b, :]
    sub = x_ref.shape[0] // MERGE_SPLIT
    for r in range(MERGE_SPLIT):
        rs = slice(r * sub, (r + 1) * sub)
        y_attn = jnp.dot(oa_ref[rs, :], wa_ref[...], preferred_element_type=F32)
        y_gla = jnp.dot(og_ref[rs, :], wg_ref[...], preferred_element_type=F32)
        mixed = (ma_ref[rs, :].astype(F32) * y_gla + mb_ref[rs, :].astype(F32) * y_attn).astype(BF16)
        x1 = x_ref[rs, :] + gate * jnp.dot(mixed, wo_ref[...], preferred_element_type=F32)
        x1_ref[rs, :] = x1
        h2_ref[rs, :] = _rmsnorm_mod(x1, n2w_ref[...], scale, shift).astype(h2_ref.dtype)


def _merge(x2, og, oa, ma, mb, mod, n2w, wg, wa, wo, B):
    T, D = x2.shape
    tm = TM_MERGE
    row = lambda i: (i, 0)
    const = lambda i: (0, 0)
    return pl.pallas_call(
        functools.partial(_merge_kernel, tiles_per_seq=T // B // tm),
        grid=(T // tm,),
        in_specs=[pl.BlockSpec((tm, D), row), pl.BlockSpec((tm, GLA_V_WIDTH), row),
                  pl.BlockSpec((tm, ATTN_OUT_WIDTH), row)]
                 + [pl.BlockSpec((tm, D), row)] * 2
                 + [_mod_spec(MOD_GATE1, D), _mod_spec(MOD_SCALE2, D), _mod_spec(MOD_SHIFT2, D)]
                 + [pl.BlockSpec((1, D), const),
                    pl.BlockSpec(wg.shape, const), pl.BlockSpec(wa.shape, const), pl.BlockSpec(wo.shape, const)],
        out_specs=[pl.BlockSpec((tm, D), row)] * 2,
        out_shape=[jax.ShapeDtypeStruct((T, D), F32), jax.ShapeDtypeStruct((T, D), BF16)],
        compiler_params=_params("parallel"),
        name="merge",
    )(x2, og, oa, ma, mb, mod, mod, mod, n2w, wg, wa, wo)


def _ffn_kernel(h_ref, hprev_ref, x1_ref, wup_ref, cw_ref, cb_ref, wd_ref, gate_ref, fw_ref, o_ref,
                hcat_s, u_s, hid_s, order_s, *, tiles_per_seq):
    i = pl.program_id(0)
    tm = h_ref.shape[0]
    halo = hprev_ref.shape[0]
    tf = FFN_CHUNK
    half = tm // 2
    hcat_s[0:halo] = jnp.where(i % tiles_per_seq == 0, jnp.zeros_like(hprev_ref[...]), hprev_ref[...])
    hcat_s[halo:] = h_ref[...]
    nchunk = D_FF // tf
    branches = (0, D_FF)

    def up_project(j):
        for n, off in enumerate(branches):
            u = jnp.dot(hcat_s[...], wup_ref[:, off + j * tf:off + (j + 1) * tf], preferred_element_type=F32)
            for c in range(tf // LANES):
                u_s[j % 2, n, c] = u[:, c * LANES:(c + 1) * LANES]

    def conv(j, n, c, parity):
        u = u_s.at[j % 2, n, c]
        cs = slice(branches[n] + j * tf + c * LANES, branches[n] + j * tf + (c + 1) * LANES)
        tap = lambda back: u[pl.ds(halo + parity - back, half, stride=2), :]
        k = 2.0 ** -0.5
        return (cb_ref[:, cs] * k + (cw_ref[0:1, cs] * k) * tap(2) + (cw_ref[1:2, cs] * k) * tap(1)
                + (cw_ref[2:3, cs] * k) * tap(0))

    up_project(0)
    for j in range(nchunk):
        if j + 1 < nchunk:
            up_project(j + 1)
        for c in range(tf // LANES):
            for parity in range(2):
                t = conv(j, 1, c, parity)
                hidden = (t * (1.0 + lax.erf(t))) * conv(j, 0, c, parity)
                hid_s[parity * half:(parity + 1) * half, j * tf + c * LANES:j * tf + (c + 1) * LANES] = (
                    hidden.astype(hid_s.dtype))

    gate = gate_ref[pl.ds(i // tiles_per_seq, 1), :]
    d_model = o_ref.shape[1]
    sq = jnp.zeros((tm, 1), F32)
    for n in range(d_model // FFN_DOWN_SLAB):
        ns = slice(n * FFN_DOWN_SLAB, (n + 1) * FFN_DOWN_SLAB)
        down = jnp.dot(hid_s[...], wd_ref[:, ns], preferred_element_type=F32)
        for c in range(FFN_DOWN_SLAB // LANES):
            g = n * (FFN_DOWN_SLAB // LANES) + c
            cs = slice(g * LANES, (g + 1) * LANES)
            for parity in range(2):
                order_s[g, pl.ds(parity, half, stride=2), :] = (
                    down[parity * half:(parity + 1) * half, c * LANES:(c + 1) * LANES])
            x2 = x1_ref[:, cs] + gate[:, cs] * order_s[g]
            sq = sq + jnp.sum(x2 * x2, axis=-1, keepdims=True)
            o_ref[:, cs] = x2
    o_ref[...] = o_ref[...] * lax.rsqrt(sq * (1.0 / d_model) + EPS) * fw_ref[...]


def _ffn(h2, x1, w_up, conv_w, conv_b, w_down, mod, final_w, B):
    T, D = x1.shape
    tm, tf, halo = TM_FFN, FFN_CHUNK, FFN_HALO
    row = lambda i: (i, 0)
    const = lambda i: (0, 0)
    single = dict(pipeline_mode=pl.Buffered(1))
    return pl.pallas_call(
        functools.partial(_ffn_kernel, tiles_per_seq=T // B // tm),
        grid=(T // tm,),
        in_specs=[pl.BlockSpec((tm, D), row),
                  pl.BlockSpec((halo, D), lambda i: (jnp.maximum(i * (tm // halo) - 1, 0), 0)),
                  pl.BlockSpec((tm, D), row),
                  pl.BlockSpec(w_up.shape, const, **single),
                  pl.BlockSpec(conv_w.shape, const, **single),
                  pl.BlockSpec(conv_b.shape, const, **single),
                  pl.BlockSpec(w_down.shape, const, **single),
                  _mod_spec(MOD_GATE2, D),
                  pl.BlockSpec((1, D), const)],
        out_specs=pl.BlockSpec((tm, D), row),
        out_shape=jax.ShapeDtypeStruct((T, D), F32),
        scratch_shapes=[pltpu.VMEM((halo + tm, D), BF16),
                        pltpu.VMEM((2, 2, tf // LANES, halo + tm, LANES), F32),
                        pltpu.VMEM((tm, D_FF), BF16),
                        pltpu.VMEM((D // LANES, tm, LANES), F32)],
        compiler_params=_params("parallel"),
        name="ffn",
    )(h2, h2, x1, w_up, conv_w, conv_b, w_down, mod, final_w)


def kernel(x, c, positions, ada_w, ada_b, norm1_w, w_in, gla_gate_w2, gla_gate_b, gla_norm_w, w_gla_branch,
           w_attn_branch, w_out, norm2_w, w_up, conv_w, conv_b, w_down, final_norm_w):
    B, S, D = x.shape
    T = B * S
    depth = ada_w.shape[0]
    assert depth == 1, "the final norm is fused into the (single) layer's ffn"
    assert all(window // dilation == ATTN_BLK for window, dilation in ATTN_GROUPS)
    x2 = x.reshape(T, D)
    for layer in range(depth):
        cos, sin, (w_in_bf, w_gla_bf, w_attn_bf, w_out_bf) = _rope_tables(
            positions, (w_in[layer].T, w_gla_branch[layer], w_attn_branch[layer], w_out[layer]))
        mod = _modulation(c, ada_w[layer], ada_b[layer])

        og, (q1, q2, q3, k1, k2, k3, v1, v2, v3, ma, mb) = _in_proj_gla(
            x2, norm1_w[layer].reshape(1, D), mod, cos, sin, w_in_bf, gla_gate_w2[layer],
            gla_gate_b[layer].reshape(1, -1), gla_norm_w[layer].reshape(1, -1), B)
        oa, w_up_bf, w_down_bf = _dilated_attention((q1, q2, q3), (k1, k2, k3), (v1, v2, v3),
                                                    w_up[layer], w_down[layer], B, S)

        x1, h2 = _merge(x2, og, oa, ma, mb, mod, norm2_w[layer].reshape(1, D), w_gla_bf, w_attn_bf, w_out_bf, B)

        x2 = _ffn(h2, x1, w_up_bf, conv_w[layer], conv_b[layer].reshape(1, -1),
                  w_down_bf, mod, final_norm_w.reshape(1, D), B)
    return x2.reshape(B, S, D)
```

```python
import functools

import jax
import jax.numpy as jnp
from jax import lax
from jax.experimental import pallas as pl
from jax.experimental.pallas import tpu as pltpu

F32 = jnp.float32
BF16 = jnp.bfloat16

D_MODEL = 1024
GLA_HEADS = 4
GLA_DK = 128
GLA_DV = 256
GLA_LOWRANK = 16
GLA_TAU = 16.0
GLA_QK_WIDTH = GLA_HEADS * GLA_DK
GLA_V_WIDTH = GLA_HEADS * GLA_DV
ATTN_GROUPS = ((128, 1), (512, 4), (2048, 16))
ATTN_HEADS_PER_GROUP = 4
ATTN_HEAD_DIM = 64
ATTN_WIDTH = ATTN_HEADS_PER_GROUP * len(ATTN_GROUPS) * ATTN_HEAD_DIM
ATTN_OUT_WIDTH = ATTN_HEADS_PER_GROUP * ATTN_HEAD_DIM
ROPE_THETA = 10000.0
D_FF = 2816
CONV_WIDTH = 3
EPS = 1e-6
IN_WIDTHS = (GLA_QK_WIDTH, GLA_QK_WIDTH, GLA_V_WIDTH, GLA_V_WIDTH, GLA_LOWRANK,
             ATTN_WIDTH, ATTN_WIDTH, ATTN_WIDTH, D_MODEL, D_MODEL)

LANES = 128
VMEM_LIMIT_BYTES = 56 * 1024 * 1024

PREP_ROWS = 512
TM_IN = 512
GLA_CHUNK = 64
ATTN_BLK = 128
ATTN_SPAN = 2048
ATTN_MERGE_ROWS = 256
ATTN_N_CAST = 5
TM_MERGE = 1024
MERGE_SPLIT = 2
TM_FFN = 512
FFN_CHUNK = 256
FFN_HALO = 16
FFN_DOWN_SLAB = 256
MASK_VALUE = -1e30
LOG2_E = 1.4426950408889634
LN_2 = 0.6931471805599453


def _params(*sem):
    return pltpu.CompilerParams(dimension_semantics=sem, vmem_limit_bytes=VMEM_LIMIT_BYTES)


def _split_hi_lo(a):
    hi = a.astype(BF16)
    lo = (a - hi.astype(F32)).astype(BF16)
    return hi, lo


def _mod_spec(which, D):
    return pl.BlockSpec((8, D), lambda i: (0, which))


MOD_SHIFT1, MOD_SCALE1, MOD_GATE1, MOD_SHIFT2, MOD_SCALE2, MOD_GATE2 = range(6)


def _prep_kernel(pos_ref, invf_ref, c_ref, adaw_ref, adab_ref, wt_ref, cos_ref, sin_ref, mod_ref, wt_bf_ref):
    c = c_ref[...]
    s = c * (1.0 / (1.0 + jnp.exp(-c)))
    s_hi, s_lo = _split_hi_lo(s)
    lhs = jnp.concatenate([s_hi, s_lo], axis=0)
    w_hi, w_lo = _split_hi_lo(adaw_ref[...])
    acc = jnp.dot(lhs, w_hi, preferred_element_type=F32) + jnp.dot(lhs, w_lo, preferred_element_type=F32)
    mod_ref[...] = acc[0:8] + acc[8:16] + adab_ref[...]

    wt_bf_ref[...] = wt_ref[...].astype(wt_bf_ref.dtype)

    half = ATTN_HEAD_DIM // 2
    groups = LANES // half
    tr = pos_ref.shape[1]
    pos = jnp.concatenate([pos_ref[...].astype(F32), jnp.zeros((8 - groups, tr), F32)], axis=0)
    pos_t = jnp.transpose(pos)
    lane = lax.broadcasted_iota(jnp.int32, (tr, LANES), 1)
    group = lane // half
    pos_dense = jnp.zeros((tr, LANES), F32)
    for q in range(groups):
        pos_dense = jnp.where(group == q, pos_t[:, q:q + 1], pos_dense)
    ang = pos_dense * invf_ref[...]
    first_half = (lane % ATTN_HEAD_DIM) < half
    for table, out_ref, signed in ((jnp.cos(ang), cos_ref, False), (jnp.sin(ang), sin_ref, True)):
        for q in range(groups):
            spread = jnp.where(group == q, table, 0.0)
            shift = groups // 2
            while shift >= 1:
                spread = spread + pltpu.roll(spread, shift * half, axis=1)
                shift //= 2
            out_ref[q] = jnp.where(first_half, -spread, spread) if signed else spread


def _prep(positions, c, ada_w, ada_b, w_in_t):
    T = positions.size
    B, D = c.shape
    N = ada_w.shape[1]
    half = ATTN_HEAD_DIM // 2
    groups = LANES // half
    inv_freq = ROPE_THETA ** (-jnp.arange(half, dtype=F32) / half)
    invf = jnp.tile(inv_freq, groups).reshape(1, LANES)
    per = T // groups
    tr = PREP_ROWS
    steps = per // tr
    tn = N // steps
    bf16_rows = 16
    wt_slab = pl.BlockSpec((-(-w_in_t.shape[0] // (steps * bf16_rows)) * bf16_rows, w_in_t.shape[1]), lambda i: (i, 0))
    c8 = jnp.pad(c, ((0, 8 - B), (0, 0)))
    cos, sin, mod, wt_bf = pl.pallas_call(
        _prep_kernel,
        grid=(steps,),
        in_specs=[pl.BlockSpec((groups, tr), lambda i: (0, i)),
                  pl.BlockSpec((1, LANES), lambda i: (0, 0)),
                  pl.BlockSpec((8, D), lambda i: (0, 0)),
                  pl.BlockSpec((D, tn), lambda i: (0, i)),
                  pl.BlockSpec((1, tn), lambda i: (0, i)),
                  wt_slab],
        out_specs=[pl.BlockSpec((groups, tr, LANES), lambda i: (0, i, 0))] * 2
                  + [pl.BlockSpec((8, tn), lambda i: (0, i)), wt_slab],
        out_shape=[jax.ShapeDtypeStruct((groups, per, LANES), F32)] * 2
                  + [jax.ShapeDtypeStruct((8, N), F32), jax.ShapeDtypeStruct(w_in_t.shape, BF16)],
        compiler_params=_params("parallel"),
        name="prep",
    )(positions.reshape(groups, per), invf, c8, ada_w, ada_b.reshape(1, N), w_in_t)
    return cos.reshape(T, LANES), sin.reshape(T, LANES), mod, wt_bf


def _rmsnorm_mod(x, w, scale, shift):
    ms = jnp.mean(x * x, axis=-1, keepdims=True)
    return (x * lax.rsqrt(ms + EPS) * w) * (1.0 + scale) + shift


def _rotate_half_pairs(t, cos, sin_signed):
    lane = lax.broadcasted_iota(jnp.int32, t.shape, 1)
    first_half = (lane % ATTN_HEAD_DIM) < (ATTN_HEAD_DIM // 2)
    from_right = pltpu.roll(t, LANES - ATTN_HEAD_DIM // 2, axis=1)
    from_left = pltpu.roll(t, ATTN_HEAD_DIM // 2, axis=1)
    return t * cos + jnp.where(first_half, from_right, from_left) * sin_signed


def _store_residue_major(ref, perm_ref, slab, col, dilation):
    if dilation == 1:
        ref[:, col:col + LANES] = slab.astype(ref.dtype)
        return
    perm_ref[...] = slab
    n = slab.shape[0] // dilation
    for p in range(dilation):
        ref[p * n:(p + 1) * n, col:col + LANES] = perm_ref[pl.ds(p, n, stride=dilation), :].astype(ref.dtype)


IN_ALIGNED = sum(IN_WIDTHS[:4])
IN_SHIFTED_START = IN_ALIGNED + GLA_LOWRANK


def _in_proj_gla_kernel(x_ref, n1w_ref, scale_ref, shift_ref, cos_ref, sin_ref, wt_ref, w2_ref, b2_ref, nw_ref,
                        og_ref, q1_ref, q2_ref, q3_ref, k1_ref, k2_ref, k3_ref, v1_ref, v2_ref, v3_ref,
                        ma_ref, mb_ref,
                        perm_ref, gq_s, gk_s, gv_s, gr_s, la_s, state_ref,
                        *, tiles_per_seq, n_tiles):
    i = pl.program_id(0)
    nt = (((1,), (1,)), ((), ()))

    @pl.when(i == 0)
    def _():
        for ref in (gq_s, gk_s, gv_s, gr_s, la_s, state_ref):
            ref[...] = jnp.zeros_like(ref)

    first_of_seq = (i + tiles_per_seq - 1) % tiles_per_seq == 0
    chunks = _gla_chunks(gq_s, gk_s, gv_s, la_s, gr_s, nw_ref, og_ref, state_ref, first_of_seq)

    @pl.when(i == n_tiles)
    def _():
        for chunk in chunks:
            chunk()

    @pl.when(i < n_tiles)
    def _():
        b = i // tiles_per_seq
        h = _rmsnorm_mod(x_ref[...], n1w_ref[...], scale_ref[pl.ds(b, 1), :], shift_ref[pl.ds(b, 1), :]).astype(BF16)

        def rows_proj(row, width):
            return lax.dot_general(h, wt_ref[row:row + width, :], nt, preferred_element_type=F32)

        def proj(col, width):
            return rows_proj(IN_SHIFTED_START + col, width)

        cos = cos_ref[...]
        sin = sin_ref[...]
        q_scale = ATTN_HEAD_DIM ** -0.5 * LOG2_E
        dilations = [d for _, d in ATTN_GROUPS]

        def rope_piece(ref, col, scale, dilation):
            def run():
                t = proj(col, ATTN_OUT_WIDTH)
                for s in range(0, ATTN_OUT_WIDTH, LANES):
                    rot = _rotate_half_pairs(t[:, s:s + LANES], cos, sin) * scale
                    _store_residue_major(ref, perm_ref, rot, s, dilation)
            return run

        def value_piece(ref, col, dilation):
            def run():
                t = proj(col, ATTN_OUT_WIDTH)
                for s in range(0, ATTN_OUT_WIDTH, LANES):
                    _store_residue_major(ref, perm_ref, t[:, s:s + LANES], s, dilation)
            return run

        def gate_piece(ref, col, s):
            def run():
                z = proj(col + s, 512)
                ref[:, s:s + 512] = (1.0 / (1.0 + jnp.exp(-z))).astype(ref.dtype)
            return run

        pieces = []
        col = 0
        for refs, scale in (((q1_ref, q2_ref, q3_ref), q_scale), ((k1_ref, k2_ref, k3_ref), 1.0)):
            for g, ref in enumerate(refs):
                pieces.append(rope_piece(ref, col, scale, dilations[g]))
                col += ATTN_OUT_WIDTH
        for g, ref in enumerate((v1_ref, v2_ref, v3_ref)):
            pieces.append(value_piece(ref, col, dilations[g]))
            col += ATTN_OUT_WIDTH
        for ref in (ma_ref, mb_ref):
            for s in range(0, D_MODEL, 512):
                pieces.append(gate_piece(ref, col, s))
            col += D_MODEL

        for n in range(max(len(pieces), len(chunks))):
            if n < len(pieces):
                pieces[n]()
            if n < len(chunks):
                chunks[n]()

        col = 0
        for ref, width in ((gq_s, GLA_QK_WIDTH), (gk_s, GLA_QK_WIDTH)):
            ref[...] = rows_proj(col, width).astype(ref.dtype)
            col += width
        for ref in (gv_s, gr_s):
            for s in range(0, GLA_V_WIDTH, 512):
                ref[:, s:s + 512] = rows_proj(col + s, 512).astype(ref.dtype)
            col += GLA_V_WIDTH
        g_lr = rows_proj(IN_ALIGNED, LANES)
        g_hi, g_lo = _split_hi_lo(g_lr)
        w2_hi, w2_lo = _split_hi_lo(w2_ref[...])
        z = (jnp.dot(g_hi, w2_hi, preferred_element_type=F32) + jnp.dot(g_lo, w2_hi, preferred_element_type=F32)
             + jnp.dot(g_hi, w2_lo, preferred_element_type=F32)) + b2_ref[...]
        log_sig = jnp.minimum(z, 0.0) - jnp.log(1.0 + jnp.exp(-jnp.abs(z)))
        la_s[...] = log_sig * (LOG2_E / GLA_TAU)


def _gla_chunks(q_ref, k_ref, v_ref, la_ref, gr_ref, nw_ref, o_ref, state_ref, first_of_seq):
    C = GLA_CHUNK
    heads = range(GLA_HEADS)
    rows = lax.broadcasted_iota(jnp.int32, (C, C), 0)
    cols = lax.broadcasted_iota(jnp.int32, (C, C), 1)
    causal = cols <= rows
    tri = causal.astype(BF16)
    q_scale = GLA_DK ** -0.5
    nw = jnp.concatenate([nw_ref[...]] * GLA_HEADS, axis=1)
    nt = (((1,), (1,)), ((), ()))
    ks = lambda a, h: a[:, h * GLA_DK:(h + 1) * GLA_DK]
    vs = lambda a, h: a[:, h * GLA_DV:(h + 1) * GLA_DV]

    def run_chunk(c):
        sl = pl.ds(c * C, C)
        q = q_ref[sl, :].astype(F32) * q_scale
        k = k_ref[sl, :].astype(F32)
        v = v_ref[sl, :]
        la_hi, la_lo = _split_hi_lo(la_ref[sl, :])
        cum = jnp.dot(tri, la_hi, preferred_element_type=F32) + jnp.dot(tri, la_lo, preferred_element_type=F32)
        mid = cum[C // 2 - 1:C // 2, :]
        last = cum[C - 1:C, :]
        qg = (q * jnp.exp2(cum - mid)).astype(BF16)
        kg = (k * jnp.exp2(mid - cum)).astype(BF16)
        q_in = (q * jnp.exp2(cum)).astype(BF16)
        k_out = k * jnp.exp2(last - cum)
        decay_row = jnp.broadcast_to(jnp.exp2(last), (8, GLA_QK_WIDTH))
        g = gr_ref[sl, :].astype(F32)
        gate = nw * (g * (1.0 / (1.0 + jnp.exp(-g))))

        attn = [lax.dot_general(ks(qg, h), ks(kg, h), nt, preferred_element_type=F32) for h in heads]
        attn = [jnp.where(causal, a, 0.0).astype(BF16) for a in attn]
        if c == 0:
            state = [jnp.where(first_of_seq, 0.0, state_ref[h]) for h in heads]
        else:
            state = [state_ref[h] for h in heads]
        o = [jnp.dot(attn[h], vs(v, h), preferred_element_type=F32)
             + jnp.dot(ks(q_in, h), state[h].astype(BF16), preferred_element_type=F32) for h in heads]
        k_out_t = [jnp.transpose(ks(k_out, h)).astype(BF16) for h in heads]
        decay = [jnp.transpose(ks(decay_row, h))[:, 0:1] for h in heads]
        for h in heads:
            state_ref[h] = decay[h] * state[h] + jnp.dot(k_out_t[h], vs(v, h), preferred_element_type=F32)
        ms = [jnp.mean(o[h] * o[h], axis=-1, keepdims=True) for h in heads]
        for h in heads:
            o_ref[sl, h * GLA_DV:(h + 1) * GLA_DV] = (o[h] * lax.rsqrt(ms[h] + EPS) * vs(gate, h)).astype(o_ref.dtype)

    return [functools.partial(run_chunk, c) for c in range(q_ref.shape[0] // C)]


def _in_proj_gla(x2, n1w, mod, cos, sin, w_in_t, w2, b2, gla_norm_w, B):
    T, D = x2.shape
    tm = TM_IN
    n_tiles = T // tm
    cur = lambda i: (jnp.minimum(i, n_tiles - 1), 0)
    lag = lambda i: (jnp.maximum(i - 1, 0), 0)
    const = lambda i: (0, 0)
    widths = (ATTN_OUT_WIDTH,) * 9 + (D_MODEL, D_MODEL)
    single = dict(pipeline_mode=pl.Buffered(1))
    w2p = jnp.pad(w2, ((0, LANES - GLA_LOWRANK), (0, 0)))
    outs = pl.pallas_call(
        functools.partial(_in_proj_gla_kernel, tiles_per_seq=T // B // tm, n_tiles=n_tiles),
        grid=(n_tiles + 1,),
        in_specs=[pl.BlockSpec((tm, D), cur),
                  pl.BlockSpec((1, D), const),
                  _mod_spec(MOD_SCALE1, D),
                  _mod_spec(MOD_SHIFT1, D),
                  pl.BlockSpec((tm, LANES), cur),
                  pl.BlockSpec((tm, LANES), cur),
                  pl.BlockSpec(w_in_t.shape, const, **single),
                  pl.BlockSpec(w2p.shape, const, **single),
                  pl.BlockSpec(b2.shape, const, **single),
                  pl.BlockSpec(gla_norm_w.shape, const, **single)],
        out_specs=[pl.BlockSpec((tm, GLA_V_WIDTH), lag)] + [pl.BlockSpec((tm, w), cur) for w in widths],
        out_shape=[jax.ShapeDtypeStruct((T, GLA_V_WIDTH), BF16)]
                  + [jax.ShapeDtypeStruct((T, w), BF16) for w in widths],
        scratch_shapes=[pltpu.VMEM((tm, LANES), F32),
                        pltpu.VMEM((tm, GLA_QK_WIDTH), BF16),
                        pltpu.VMEM((tm, GLA_QK_WIDTH), BF16),
                        pltpu.VMEM((tm, GLA_V_WIDTH), BF16),
                        pltpu.VMEM((tm, GLA_V_WIDTH), BF16),
                        pltpu.VMEM((tm, GLA_QK_WIDTH), F32),
                        pltpu.VMEM((GLA_HEADS, GLA_DK, GLA_DV), F32)],
        compiler_params=_params("arbitrary"),
        name="in_proj_gla",
    )(x2, n1w, mod, mod, cos, sin, w_in_t, w2p, b2, gla_norm_w)
    return outs[0], outs[1:]


def _attn_kernel(q1_ref, q2_ref, q3_ref, k1c_ref, k1p_ref, k2c_ref, k2p_ref, k3c_ref, k3p_ref,
                 v1c_ref, v1p_ref, v2c_ref, v2p_ref, v3c_ref, v3p_ref, *refs):
    n_cast = ATTN_N_CAST
    o_ref = refs[n_cast]
    slabs = refs[2 * n_cast + 1:]
    for src, dst in zip(refs[:n_cast], refs[n_cast + 1:2 * n_cast + 1]):
        dst[...] = src[...].astype(dst.dtype)

    blk = ATTN_BLK
    tile = TM_IN
    span = ATTN_SPAN
    tiles = span // tile
    r2, r3 = ATTN_GROUPS[1][1], ATTN_GROUPS[2][1]
    per3 = tile // r3
    not_first_span = pl.program_id(1) > 0
    npair = ATTN_OUT_WIDTH // LANES
    o1_s, o2_s, o3_s, l1_s, l2_s, l3_s = [slabs[n * npair:(n + 1) * npair] for n in range(6)]

    rows = lax.broadcasted_iota(jnp.int32, (blk, 2 * blk), 0)
    cols = lax.broadcasted_iota(jnp.int32, (blk, 2 * blk), 1)
    band = (cols >= rows) & (cols <= rows + blk)
    in_cur = cols >= blk
    left = lax.broadcasted_iota(jnp.int32, (blk, LANES), 1) < ATTN_HEAD_DIM
    ones = jnp.ones((2 * blk, LANES), BF16)
    nt = (((1,), (1,)), ((), ()))

    def pair_attend(q_pair, k_cat, v_cat, has_prev):
        zero = jnp.zeros_like(q_pair)
        q2 = jnp.concatenate([jnp.where(left, q_pair, zero), jnp.where(left, zero, q_pair)], axis=0)
        s = lax.dot_general(q2, k_cat, nt, preferred_element_type=F32)
        valid = band & (in_cur | has_prev)
        s = jnp.where(jnp.concatenate([valid, valid], axis=0), s, MASK_VALUE)
        m = jnp.max(s, axis=-1, keepdims=True)
        p = jnp.exp2(s - m).astype(BF16)
        r = jnp.dot(p, jnp.concatenate([v_cat, ones], axis=1), preferred_element_type=F32)
        acc = jnp.where(left, r[:blk, :LANES], r[blk:, :LANES])
        den = jnp.where(left, r[:blk, LANES:], r[blk:, LANES:])
        m_pair = jnp.where(left, jnp.broadcast_to(m[:blk], (blk, LANES)), jnp.broadcast_to(m[blk:], (blk, LANES)))
        return acc / den, m_pair * LN_2 + jnp.log(den)

    def with_prev(cur_ref, prev_ref, r0, back, prev_rows, first, cs):
        before = cur_ref[pl.ds(pl.multiple_of(jnp.maximum(r0 - back, 0), blk), blk), cs]
        if prev_rows is not None:
            before = jnp.where(first, prev_ref[prev_rows, cs], before)
        return jnp.concatenate([before, cur_ref[pl.ds(r0, blk), cs]], axis=0)

    def tile_body(t, carry):
        for p in range(tiles):
            i = t * tiles + p
            r0 = pl.multiple_of(i * blk, blk)
            for j in range(npair):
                cs = slice(j * LANES, (j + 1) * LANES)
                rows1 = slice(0, blk) if p == 0 else None
                o, lse = pair_attend(q1_ref[pl.ds(r0, blk), cs],
                                     with_prev(k1c_ref, k1p_ref, r0, blk, rows1, t == 0, cs),
                                     with_prev(v1c_ref, v1p_ref, r0, blk, rows1, t == 0, cs),
                                     not_first_span | (i > 0))
                o1_s[j][pl.ds(r0, blk), :] = o
                l1_s[j][pl.ds(r0, blk), :] = lse
                rows2 = slice(p * blk, (p + 1) * blk)
                o, lse = pair_attend(q2_ref[pl.ds(r0, blk), cs],
                                     with_prev(k2c_ref, k2p_ref, r0, tile, rows2, t == 0, cs),
                                     with_prev(v2c_ref, v2p_ref, r0, tile, rows2, t == 0, cs),
                                     not_first_span | (t > 0))
                tok = pl.ds(pl.multiple_of(t * tile, tile) + p, blk, stride=r2)
                o2_s[j][tok, :] = o
                l2_s[j][tok, :] = lse
                rr = pl.multiple_of(i * per3, per3)
                gather = lambda ref: jnp.concatenate(
                    [ref[pl.ds(rr + u * tile, per3), cs] for u in range(tiles)], axis=0)
                k_cat = jnp.concatenate([gather(k3p_ref), gather(k3c_ref)], axis=0)
                v_cat = jnp.concatenate([gather(v3p_ref), gather(v3c_ref)], axis=0)
                o, lse = pair_attend(gather(q3_ref), k_cat, v_cat, not_first_span)
                tok = pl.ds(i, blk, stride=r3)
                o3_s[j][tok, :] = o
                l3_s[j][tok, :] = lse
        return carry

    lax.fori_loop(0, tiles, tile_body, 0)

    for j in range(npair):
        for n in range(span // ATTN_MERGE_ROWS):
            rs = slice(n * ATTN_MERGE_ROWS, (n + 1) * ATTN_MERGE_ROWS)
            l1, l2, l3 = l1_s[j][rs, :], l2_s[j][rs, :], l3_s[j][rs, :]
            m = jnp.maximum(jnp.maximum(l1, l2), l3)
            e1, e2, e3 = jnp.exp(l1 - m), jnp.exp(l2 - m), jnp.exp(l3 - m)
            num = e1 * o1_s[j][rs, :] + e2 * o2_s[j][rs, :] + e3 * o3_s[j][rs, :]
            o_ref[rs, j * LANES:(j + 1) * LANES] = (num / (e1 + e2 + e3)).astype(o_ref.dtype)


def _dilated_attention(qs, ks, vs, weights, B, S):
    assert len(weights) == ATTN_N_CAST
    span = ATTN_SPAN
    nsp = S // span
    gw = ATTN_OUT_WIDTH
    T = B * S
    cur = lambda b, s: (b * nsp + s, 0)
    steps = B * nsp
    slab = lambda w: pl.BlockSpec((w.shape[0] // steps, w.shape[1]), cur)

    def prev(rows):
        per = span // rows
        return pl.BlockSpec((rows, gw), lambda b, s: (jnp.maximum((b * nsp + s) * per - 1, 0), 0))

    halos = (ATTN_BLK, TM_IN, span)
    full = pl.BlockSpec((span, gw), cur)
    kv_specs = []
    for h in halos:
        kv_specs += [full, prev(h)]
    kv_args = lambda arrs: [a for arr in arrs for a in (arr, arr)]
    outs = pl.pallas_call(
        _attn_kernel,
        grid=(B, nsp),
        in_specs=[full] * 3 + kv_specs + kv_specs + [slab(w) for w in weights],
        out_specs=[full] + [slab(w) for w in weights],
        out_shape=[jax.ShapeDtypeStruct((T, gw), BF16)] + [jax.ShapeDtypeStruct(w.shape, BF16) for w in weights],
        scratch_shapes=[pltpu.VMEM((span, LANES), F32)] * (6 * (gw // LANES)),
        compiler_params=_params("parallel", "parallel"),
        name="dilated_attn",
    )(*qs, *kv_args(ks), *kv_args(vs), *weights)
    return outs[0], outs[1:]


def _merge_kernel(x_ref, og_ref, oa_ref, ma_ref, mb_ref,
                  gate_ref, scale_ref, shift_ref, n2w_ref, wg_ref, wa_ref, wo_ref, x1_ref, h2_ref, *, tiles_per_seq):
    b = pl.ds(pl.program_id(0) // tiles_per_seq, 1)
    gate, scale, shift = gate_ref[b, :], scale_ref[b, :], shift_ref[b, :]
    sub = x_ref.shape[0] // MERGE_SPLIT
    for r in range(MERGE_SPLIT):
        rs = slice(r * sub, (r + 1) * sub)
        y_attn = jnp.dot(oa_ref[rs, :], wa_ref[...], preferred_element_type=F32)
        y_gla = jnp.dot(og_ref[rs, :], wg_ref[...], preferred_element_type=F32)
        mixed = (ma_ref[rs, :].astype(F32) * y_gla + mb_ref[rs, :].astype(F32) * y_attn).astype(BF16)
        x1 = x_ref[rs, :] + gate * jnp.dot(mixed, wo_ref[...], preferred_element_type=F32)
        x1_ref[rs, :] = x1
        h2_ref[rs, :] = _rmsnorm_mod(x1, n2w_ref[...], scale, shift).astype(h2_ref.dtype)


def _merge(x2, og, oa, ma, mb, mod, n2w, wg, wa, wo, B):
    T, D = x2.shape
    tm = TM_MERGE
    row = lambda i: (i, 0)
    const = lambda i: (0, 0)
    return pl.pallas_call(
        functools.partial(_merge_kernel, tiles_per_seq=T // B // tm),
        grid=(T // tm,),
        in_specs=[pl.BlockSpec((tm, D), row), pl.BlockSpec((tm, GLA_V_WIDTH), row),
                  pl.BlockSpec((tm, ATTN_OUT_WIDTH), row)]
                 + [pl.BlockSpec((tm, D), row)] * 2
                 + [_mod_spec(MOD_GATE1, D), _mod_spec(MOD_SCALE2, D), _mod_spec(MOD_SHIFT2, D)]
                 + [pl.BlockSpec((1, D), const),
                    pl.BlockSpec(wg.shape, const), pl.BlockSpec(wa.shape, const), pl.BlockSpec(wo.shape, const)],
        out_specs=[pl.BlockSpec((tm, D), row)] * 2,
        out_shape=[jax.ShapeDtypeStruct((T, D), F32), jax.ShapeDtypeStruct((T, D), BF16)],
        compiler_params=_params("parallel"),
        name="merge",
    )(x2, og, oa, ma, mb, mod, mod, mod, n2w, wg, wa, wo)


def _ffn_kernel(h_ref, hprev_ref, x1_ref, wup_ref, cw_ref, cb_ref, wd_ref, gate_ref, fw_ref, o_ref,
                hcat_s, u_s, hid_s, order_s, *, tiles_per_seq):
    i = pl.program_id(0)
    tm = h_ref.shape[0]
    halo = hprev_ref.shape[0]
    tf = FFN_CHUNK
    half = tm // 2
    hcat_s[0:halo] = jnp.where(i % tiles_per_seq == 0, jnp.zeros_like(hprev_ref[...]), hprev_ref[...])
    hcat_s[halo:] = h_ref[...]
    nchunk = D_FF // tf
    branches = (0, D_FF)

    def up_project(j):
        for n, off in enumerate(branches):
            u = jnp.dot(hcat_s[...], wup_ref[:, off + j * tf:off + (j + 1) * tf], preferred_element_type=F32)
            for c in range(tf // LANES):
                u_s[j % 2, n, c] = u[:, c * LANES:(c + 1) * LANES]

    def conv(j, n, c, parity):
        u = u_s.at[j % 2, n, c]
        cs = slice(branches[n] + j * tf + c * LANES, branches[n] + j * tf + (c + 1) * LANES)
        tap = lambda back: u[pl.ds(halo + parity - back, half, stride=2), :]
        k = 2.0 ** -0.5
        return (cb_ref[:, cs] * k + (cw_ref[0:1, cs] * k) * tap(2) + (cw_ref[1:2, cs] * k) * tap(1)
                + (cw_ref[2:3, cs] * k) * tap(0))

    up_project(0)
    for j in range(nchunk):
        if j + 1 < nchunk:
            up_project(j + 1)
        for c in range(tf // LANES):
            for parity in range(2):
                t = conv(j, 1, c, parity)
                hidden = (t * (1.0 + lax.erf(t))) * conv(j, 0, c, parity)
                hid_s[parity * half:(parity + 1) * half, j * tf + c * LANES:j * tf + (c + 1) * LANES] = (
                    hidden.astype(hid_s.dtype))

    gate = gate_ref[pl.ds(i // tiles_per_seq, 1), :]
    d_model = o_ref.shape[1]
    sq = jnp.zeros((tm, 1), F32)
    for n in range(d_model // FFN_DOWN_SLAB):
        ns = slice(n * FFN_DOWN_SLAB, (n + 1) * FFN_DOWN_SLAB)
        down = jnp.dot(hid_s[...], wd_ref[:, ns], preferred_element_type=F32)
        for c in range(FFN_DOWN_SLAB // LANES):
            g = n * (FFN_DOWN_SLAB // LANES) + c
            cs = slice(g * LANES, (g + 1) * LANES)
            for parity in range(2):
                order_s[g, pl.ds(parity, half, stride=2), :] = (
                    down[parity * half:(parity + 1) * half, c * LANES:(c + 1) * LANES])
            x2 = x1_ref[:, cs] + gate[:, cs] * order_s[g]
            sq = sq + jnp.sum(x2 * x2, axis=-1, keepdims=True)
            o_ref[:, cs] = x2
    o_ref[...] = o_ref[...] * lax.rsqrt(sq * (1.0 / d_model) + EPS) * fw_ref[...]


def _ffn(h2, x1, w_up, conv_w, conv_b, w_down, mod, final_w, B):
    T, D = x1.shape
    tm, tf, halo = TM_FFN, FFN_CHUNK, FFN_HALO
    row = lambda i: (i, 0)
    const = lambda i: (0, 0)
    single = dict(pipeline_mode=pl.Buffered(1))
    return pl.pallas_call(
        functools.partial(_ffn_kernel, tiles_per_seq=T // B // tm),
        grid=(T // tm,),
        in_specs=[pl.BlockSpec((tm, D), row),
                  pl.BlockSpec((halo, D), lambda i: (jnp.maximum(i * (tm // halo) - 1, 0), 0)),
                  pl.BlockSpec((tm, D), row),
                  pl.BlockSpec(w_up.shape, const, **single),
                  pl.BlockSpec(conv_w.shape, const, **single),
                  pl.BlockSpec(conv_b.shape, const, **single),
                  pl.BlockSpec(w_down.shape, const, **single),
                  _mod_spec(MOD_GATE2, D),
                  pl.BlockSpec((1, D), const)],
        out_specs=pl.BlockSpec((tm, D), row),
        out_shape=jax.ShapeDtypeStruct((T, D), F32),
        scratch_shapes=[pltpu.VMEM((halo + tm, D), BF16),
                        pltpu.VMEM((2, 2, tf // LANES, halo + tm, LANES), F32),
                        pltpu.VMEM((tm, D_FF), BF16),
                        pltpu.VMEM((D // LANES, tm, LANES), F32)],
        compiler_params=_params("parallel"),
        name="ffn",
    )(h2, h2, x1, w_up, conv_w, conv_b, w_down, mod, final_w)


def kernel(x, c, positions, ada_w, ada_b, norm1_w, w_in, gla_gate_w2, gla_gate_b, gla_norm_w, w_gla_branch,
           w_attn_branch, w_out, norm2_w, w_up, conv_w, conv_b, w_down, final_norm_w):
    B, S, D = x.shape
    T = B * S
    depth = ada_w.shape[0]
    assert depth == 1, "the final norm is fused into the (single) layer's ffn"
    assert all(window // dilation == ATTN_BLK for window, dilation in ATTN_GROUPS)
    x2 = x.reshape(T, D)
    for layer in range(depth):
        cos, sin, mod, w_in_bf = _prep(positions, c, ada_w[layer], ada_b[layer], w_in[layer].T)

        og, (q1, q2, q3, k1, k2, k3, v1, v2, v3, ma, mb) = _in_proj_gla(
            x2, norm1_w[layer].reshape(1, D), mod, cos, sin, w_in_bf, gla_gate_w2[layer],
            gla_gate_b[layer].reshape(1, -1), gla_norm_w[layer].reshape(1, -1), B)
        oa, (w_up_bf, w_down_bf, w_gla_bf, w_attn_bf, w_out_bf) = _dilated_attention(
            (q1, q2, q3), (k1, k2, k3), (v1, v2, v3),
            (w_up[layer], w_down[layer], w_gla_branch[layer], w_attn_branch[layer], w_out[layer]), B, S)

        x1, h2 = _merge(x2, og, oa, ma, mb, mod, norm2_w[layer].reshape(1, D), w_gla_bf, w_attn_bf, w_out_bf, B)

        x2 = _ffn(h2, x1, w_up_bf, conv_w[layer], conv_b[layer].reshape(1, -1),
                  w_down_bf, mod, final_norm_w.reshape(1, D), B)
    return x2.reshape(B, S, D)
```

```python
import functools

import jax
import jax.numpy as jnp
from jax import lax
from jax.experimental import pallas as pl
from jax.experimental.pallas import tpu as pltpu

F32 = jnp.float32
BF16 = jnp.bfloat16

D_MODEL = 1024
GLA_HEADS = 4
GLA_DK = 128
GLA_DV = 256
GLA_LOWRANK = 16
GLA_TAU = 16.0
GLA_QK_WIDTH = GLA_HEADS * GLA_DK
GLA_V_WIDTH = GLA_HEADS * GLA_DV
ATTN_GROUPS = ((128, 1), (512, 4), (2048, 16))
ATTN_HEADS_PER_GROUP = 4
ATTN_HEAD_DIM = 64
ATTN_WIDTH = ATTN_HEADS_PER_GROUP * len(ATTN_GROUPS) * ATTN_HEAD_DIM
ATTN_OUT_WIDTH = ATTN_HEADS_PER_GROUP * ATTN_HEAD_DIM
ROPE_THETA = 10000.0
D_FF = 2816
CONV_WIDTH = 3
EPS = 1e-6
IN_WIDTHS = (GLA_QK_WIDTH, GLA_QK_WIDTH, GLA_V_WIDTH, GLA_V_WIDTH, GLA_LOWRANK,
             ATTN_WIDTH, ATTN_WIDTH, ATTN_WIDTH, D_MODEL, D_MODEL)

LANES = 128
VMEM_LIMIT_BYTES = 56 * 1024 * 1024

PREP_ROWS = 512
TM_IN = 512
GLA_CHUNK = 64
ATTN_BLK = 128
ATTN_SPAN = 2048
ATTN_MERGE_ROWS = 256
ATTN_N_CAST = 5
TM_MERGE = 1024
MERGE_SPLIT = 2
TM_FFN = 512
FFN_CHUNK = 256
FFN_HALO = 16
FFN_DOWN_SLAB = 256
MASK_VALUE = -1e30
LOG2_E = 1.4426950408889634
LN_2 = 0.6931471805599453


def _params(*sem):
    return pltpu.CompilerParams(dimension_semantics=sem, vmem_limit_bytes=VMEM_LIMIT_BYTES)


def _split_hi_lo(a):
    hi = a.astype(BF16)
    lo = (a - hi.astype(F32)).astype(BF16)
    return hi, lo


def _mod_spec(which, D):
    return pl.BlockSpec((8, D), lambda i: (0, which))


MOD_SHIFT1, MOD_SCALE1, MOD_GATE1, MOD_SHIFT2, MOD_SCALE2, MOD_GATE2 = range(6)


def _prep_kernel(pos_ref, invf_ref, c_ref, adaw_ref, adab_ref, wt_ref, cos_ref, sin_ref, mod_ref, wt_bf_ref):
    c = c_ref[...]
    s = c * (1.0 / (1.0 + jnp.exp(-c)))
    s_hi, s_lo = _split_hi_lo(s)
    lhs = jnp.concatenate([s_hi, s_lo], axis=0)
    w_hi, w_lo = _split_hi_lo(adaw_ref[...])
    acc = jnp.dot(lhs, w_hi, preferred_element_type=F32) + jnp.dot(lhs, w_lo, preferred_element_type=F32)
    mod_ref[...] = acc[0:8] + acc[8:16] + adab_ref[...]

    wt_bf_ref[...] = wt_ref[...].astype(wt_bf_ref.dtype)

    half = ATTN_HEAD_DIM // 2
    groups = LANES // half
    tr = pos_ref.shape[1]
    pos = jnp.concatenate([pos_ref[...].astype(F32), jnp.zeros((8 - groups, tr), F32)], axis=0)
    pos_t = jnp.transpose(pos)
    lane = lax.broadcasted_iota(jnp.int32, (tr, LANES), 1)
    group = lane // half
    pos_dense = jnp.zeros((tr, LANES), F32)
    for q in range(groups):
        pos_dense = jnp.where(group == q, pos_t[:, q:q + 1], pos_dense)
    ang = pos_dense * invf_ref[...]
    first_half = (lane % ATTN_HEAD_DIM) < half
    for table, out_ref, signed in ((jnp.cos(ang), cos_ref, False), (jnp.sin(ang), sin_ref, True)):
        for q in range(groups):
            spread = jnp.where(group == q, table, 0.0)
            shift = groups // 2
            while shift >= 1:
                spread = spread + pltpu.roll(spread, shift * half, axis=1)
                shift //= 2
            out_ref[q] = jnp.where(first_half, -spread, spread) if signed else spread


def _prep(positions, c, ada_w, ada_b, w_in_t):
    T = positions.size
    B, D = c.shape
    N = ada_w.shape[1]
    half = ATTN_HEAD_DIM // 2
    groups = LANES // half
    inv_freq = ROPE_THETA ** (-jnp.arange(half, dtype=F32) / half)
    invf = jnp.tile(inv_freq, groups).reshape(1, LANES)
    per = T // groups
    tr = PREP_ROWS
    steps = per // tr
    tn = N // steps
    bf16_rows = 16
    wt_slab = pl.BlockSpec((-(-w_in_t.shape[0] // (steps * bf16_rows)) * bf16_rows, w_in_t.shape[1]), lambda i: (i, 0))
    c8 = jnp.pad(c, ((0, 8 - B), (0, 0)))
    cos, sin, mod, wt_bf = pl.pallas_call(
        _prep_kernel,
        grid=(steps,),
        in_specs=[pl.BlockSpec((groups, tr), lambda i: (0, i)),
                  pl.BlockSpec((1, LANES), lambda i: (0, 0)),
                  pl.BlockSpec((8, D), lambda i: (0, 0)),
                  pl.BlockSpec((D, tn), lambda i: (0, i)),
                  pl.BlockSpec((1, tn), lambda i: (0, i)),
                  wt_slab],
        out_specs=[pl.BlockSpec((groups, tr, LANES), lambda i: (0, i, 0))] * 2
                  + [pl.BlockSpec((8, tn), lambda i: (0, i)), wt_slab],
        out_shape=[jax.ShapeDtypeStruct((groups, per, LANES), F32)] * 2
                  + [jax.ShapeDtypeStruct((8, N), F32), jax.ShapeDtypeStruct(w_in_t.shape, BF16)],
        compiler_params=_params("parallel"),
        name="prep",
    )(positions.reshape(groups, per), invf, c8, ada_w, ada_b.reshape(1, N), w_in_t)
    return cos.reshape(T, LANES), sin.reshape(T, LANES), mod, wt_bf


def _rmsnorm_mod(x, w, scale, shift):
    ms = jnp.mean(x * x, axis=-1, keepdims=True)
    return (x * lax.rsqrt(ms + EPS) * w) * (1.0 + scale) + shift


def _rotate_half_pairs(t, cos, sin_signed):
    lane = lax.broadcasted_iota(jnp.int32, t.shape, 1)
    first_half = (lane % ATTN_HEAD_DIM) < (ATTN_HEAD_DIM // 2)
    from_right = pltpu.roll(t, LANES - ATTN_HEAD_DIM // 2, axis=1)
    from_left = pltpu.roll(t, ATTN_HEAD_DIM // 2, axis=1)
    return t * cos + jnp.where(first_half, from_right, from_left) * sin_signed


def _store_residue_major(ref, perm_ref, slab, col, dilation):
    if dilation == 1:
        ref[:, col:col + LANES] = slab.astype(ref.dtype)
        return
    perm_ref[...] = slab
    n = slab.shape[0] // dilation
    for p in range(dilation):
        ref[p * n:(p + 1) * n, col:col + LANES] = perm_ref[pl.ds(p, n, stride=dilation), :].astype(ref.dtype)


IN_ALIGNED = sum(IN_WIDTHS[:4])
IN_SHIFTED_START = IN_ALIGNED + GLA_LOWRANK


def _in_proj_gla_kernel(x_ref, n1w_ref, scale_ref, shift_ref, cos_ref, sin_ref, wt_ref, w2_ref, b2_ref, nw_ref,
                        og_ref, q1_ref, q2_ref, q3_ref, k1_ref, k2_ref, k3_ref, v1_ref, v2_ref, v3_ref,
                        ma_ref, mb_ref,
                        perm_ref, gq_s, gk_s, gv_s, gr_s, la_s, state_ref,
                        *, tiles_per_seq, n_tiles):
    i = pl.program_id(0)
    nt = (((1,), (1,)), ((), ()))

    @pl.when(i == 0)
    def _():
        for ref in (gq_s, gk_s, gv_s, gr_s, la_s, state_ref):
            ref[...] = jnp.zeros_like(ref)

    first_of_seq = (i + tiles_per_seq - 1) % tiles_per_seq == 0
    chunks = _gla_chunks(gq_s, gk_s, gv_s, la_s, gr_s, nw_ref, og_ref, state_ref, first_of_seq)

    @pl.when(i == n_tiles)
    def _():
        for chunk in chunks:
            chunk()

    @pl.when(i < n_tiles)
    def _():
        b = i // tiles_per_seq
        h = _rmsnorm_mod(x_ref[...], n1w_ref[...], scale_ref[pl.ds(b, 1), :], shift_ref[pl.ds(b, 1), :]).astype(BF16)

        def rows_proj(row, width):
            return lax.dot_general(h, wt_ref[row:row + width, :], nt, preferred_element_type=F32)

        def proj(col, width):
            return rows_proj(IN_SHIFTED_START + col, width)

        cos = cos_ref[...]
        sin = sin_ref[...]
        q_scale = ATTN_HEAD_DIM ** -0.5 * LOG2_E
        dilations = [d for _, d in ATTN_GROUPS]

        def rope_piece(ref, col, scale, dilation):
            def run():
                t = proj(col, ATTN_OUT_WIDTH)
                for s in range(0, ATTN_OUT_WIDTH, LANES):
                    rot = _rotate_half_pairs(t[:, s:s + LANES], cos, sin) * scale
                    _store_residue_major(ref, perm_ref, rot, s, dilation)
            return run

        def value_piece(ref, col, dilation):
            def run():
                t = proj(col, ATTN_OUT_WIDTH)
                for s in range(0, ATTN_OUT_WIDTH, LANES):
                    _store_residue_major(ref, perm_ref, t[:, s:s + LANES], s, dilation)
            return run

        def gate_piece(ref, col, s):
            def run():
                z = proj(col + s, 512)
                ref[:, s:s + 512] = (1.0 / (1.0 + jnp.exp(-z))).astype(ref.dtype)
            return run

        pieces = []
        col = 0
        for refs, scale in (((q1_ref, q2_ref, q3_ref), q_scale), ((k1_ref, k2_ref, k3_ref), 1.0)):
            for g, ref in enumerate(refs):
                pieces.append(rope_piece(ref, col, scale, dilations[g]))
                col += ATTN_OUT_WIDTH
        for g, ref in enumerate((v1_ref, v2_ref, v3_ref)):
            pieces.append(value_piece(ref, col, dilations[g]))
            col += ATTN_OUT_WIDTH
        for ref in (ma_ref, mb_ref):
            for s in range(0, D_MODEL, 512):
                pieces.append(gate_piece(ref, col, s))
            col += D_MODEL

        for n in range(max(len(pieces), len(chunks))):
            if n < len(pieces):
                pieces[n]()
            if n < len(chunks):
                chunks[n]()

        col = 0
        for ref, width in ((gq_s, GLA_QK_WIDTH), (gk_s, GLA_QK_WIDTH)):
            ref[...] = rows_proj(col, width).astype(ref.dtype)
            col += width
        for ref in (gv_s, gr_s):
            for s in range(0, GLA_V_WIDTH, 512):
                ref[:, s:s + 512] = rows_proj(col + s, 512).astype(ref.dtype)
            col += GLA_V_WIDTH
        g_lr = rows_proj(IN_ALIGNED, LANES)
        g_hi, g_lo = _split_hi_lo(g_lr)
        w2_hi, w2_lo = _split_hi_lo(w2_ref[...])
        z = (jnp.dot(g_hi, w2_hi, preferred_element_type=F32) + jnp.dot(g_lo, w2_hi, preferred_element_type=F32)
             + jnp.dot(g_hi, w2_lo, preferred_element_type=F32)) + b2_ref[...]
        log_sig = jnp.minimum(z, 0.0) - jnp.log(1.0 + jnp.exp(-jnp.abs(z)))
        la_s[...] = log_sig * (LOG2_E / GLA_TAU)


def _gla_chunks(q_ref, k_ref, v_ref, la_ref, gr_ref, nw_ref, o_ref, state_ref, first_of_seq):
    C = GLA_CHUNK
    heads = range(GLA_HEADS)
    rows = lax.broadcasted_iota(jnp.int32, (C, C), 0)
    cols = lax.broadcasted_iota(jnp.int32, (C, C), 1)
    causal = cols <= rows
    tri = causal.astype(BF16)
    q_scale = GLA_DK ** -0.5
    nw = jnp.concatenate([nw_ref[...]] * GLA_HEADS, axis=1)
    nt = (((1,), (1,)), ((), ()))
    ks = lambda a, h: a[:, h * GLA_DK:(h + 1) * GLA_DK]
    vs = lambda a, h: a[:, h * GLA_DV:(h + 1) * GLA_DV]

    def run_chunk(c):
        sl = pl.ds(c * C, C)
        q = q_ref[sl, :].astype(F32) * q_scale
        k = k_ref[sl, :].astype(F32)
        v = v_ref[sl, :]
        la_hi, la_lo = _split_hi_lo(la_ref[sl, :])
        cum = jnp.dot(tri, la_hi, preferred_element_type=F32) + jnp.dot(tri, la_lo, preferred_element_type=F32)
        mid = cum[C // 2 - 1:C // 2, :]
        last = cum[C - 1:C, :]
        qg = (q * jnp.exp2(cum - mid)).astype(BF16)
        kg = (k * jnp.exp2(mid - cum)).astype(BF16)
        q_in = (q * jnp.exp2(cum)).astype(BF16)
        k_out = k * jnp.exp2(last - cum)
        decay_row = jnp.broadcast_to(jnp.exp2(last), (8, GLA_QK_WIDTH))
        g = gr_ref[sl, :].astype(F32)
        gate = nw * (g * (1.0 / (1.0 + jnp.exp(-g))))

        attn = [lax.dot_general(ks(qg, h), ks(kg, h), nt, preferred_element_type=F32) for h in heads]
        attn = [jnp.where(causal, a, 0.0).astype(BF16) for a in attn]
        if c == 0:
            state = [jnp.where(first_of_seq, 0.0, state_ref[h]) for h in heads]
        else:
            state = [state_ref[h] for h in heads]
        o = [jnp.dot(attn[h], vs(v, h), preferred_element_type=F32)
             + jnp.dot(ks(q_in, h), state[h].astype(BF16), preferred_element_type=F32) for h in heads]
        k_out_t = [jnp.transpose(ks(k_out, h)).astype(BF16) for h in heads]
        decay = [jnp.transpose(ks(decay_row, h))[:, 0:1] for h in heads]
        for h in heads:
            state_ref[h] = decay[h] * state[h] + jnp.dot(k_out_t[h], vs(v, h), preferred_element_type=F32)
        ms = [jnp.mean(o[h] * o[h], axis=-1, keepdims=True) for h in heads]
        for h in heads:
            o_ref[sl, h * GLA_DV:(h + 1) * GLA_DV] = (o[h] * lax.rsqrt(ms[h] + EPS) * vs(gate, h)).astype(o_ref.dtype)

    return [functools.partial(run_chunk, c) for c in range(q_ref.shape[0] // C)]


def _in_proj_gla(x2, n1w, mod, cos, sin, w_in_t, w2, b2, gla_norm_w, B):
    T, D = x2.shape
    tm = TM_IN
    n_tiles = T // tm
    cur = lambda i: (jnp.minimum(i, n_tiles - 1), 0)
    lag = lambda i: (jnp.maximum(i - 1, 0), 0)
    const = lambda i: (0, 0)
    widths = (ATTN_OUT_WIDTH,) * 9 + (D_MODEL, D_MODEL)
    single = dict(pipeline_mode=pl.Buffered(1))
    w2p = jnp.pad(w2, ((0, LANES - GLA_LOWRANK), (0, 0)))
    outs = pl.pallas_call(
        functools.partial(_in_proj_gla_kernel, tiles_per_seq=T // B // tm, n_tiles=n_tiles),
        grid=(n_tiles + 1,),
        in_specs=[pl.BlockSpec((tm, D), cur),
                  pl.BlockSpec((1, D), const),
                  _mod_spec(MOD_SCALE1, D),
                  _mod_spec(MOD_SHIFT1, D),
                  pl.BlockSpec((tm, LANES), cur),
                  pl.BlockSpec((tm, LANES), cur),
                  pl.BlockSpec(w_in_t.shape, const, **single),
                  pl.BlockSpec(w2p.shape, const, **single),
                  pl.BlockSpec(b2.shape, const, **single),
                  pl.BlockSpec(gla_norm_w.shape, const, **single)],
        out_specs=[pl.BlockSpec((tm, GLA_V_WIDTH), lag)] + [pl.BlockSpec((tm, w), cur) for w in widths],
        out_shape=[jax.ShapeDtypeStruct((T, GLA_V_WIDTH), BF16)]
                  + [jax.ShapeDtypeStruct((T, w), BF16) for w in widths],
        scratch_shapes=[pltpu.VMEM((tm, LANES), F32),
                        pltpu.VMEM((tm, GLA_QK_WIDTH), BF16),
                        pltpu.VMEM((tm, GLA_QK_WIDTH), BF16),
                        pltpu.VMEM((tm, GLA_V_WIDTH), BF16),
                        pltpu.VMEM((tm, GLA_V_WIDTH), BF16),
                        pltpu.VMEM((tm, GLA_QK_WIDTH), F32),
                        pltpu.VMEM((GLA_HEADS, GLA_DK, GLA_DV), F32)],
        compiler_params=_params("arbitrary"),
        name="in_proj_gla",
    )(x2, n1w, mod, mod, cos, sin, w_in_t, w2p, b2, gla_norm_w)
    return outs[0], outs[1:]


def _attn_kernel(q1_ref, q2_ref, q3_ref, k1c_ref, k1p_ref, k2c_ref, k2p_ref, k3c_ref, k3p_ref,
                 v1c_ref, v1p_ref, v2c_ref, v2p_ref, v3c_ref, v3p_ref, *refs):
    n_cast = ATTN_N_CAST
    o_ref = refs[n_cast]
    slabs = refs[2 * n_cast + 1:]
    for src, dst in zip(refs[:n_cast], refs[n_cast + 1:2 * n_cast + 1]):
        dst[...] = src[...].astype(dst.dtype)

    blk = ATTN_BLK
    tile = TM_IN
    span = ATTN_SPAN
    tiles = span // tile
    r2, r3 = ATTN_GROUPS[1][1], ATTN_GROUPS[2][1]
    per3 = tile // r3
    not_first_span = pl.program_id(1) > 0
    npair = ATTN_OUT_WIDTH // LANES
    o1_s, o2_s, o3_s, l1_s, l2_s, l3_s = [slabs[n * npair:(n + 1) * npair] for n in range(6)]

    rows = lax.broadcasted_iota(jnp.int32, (blk, 2 * blk), 0)
    cols = lax.broadcasted_iota(jnp.int32, (blk, 2 * blk), 1)
    band = (cols >= rows) & (cols <= rows + blk)
    in_cur = cols >= blk
    left = lax.broadcasted_iota(jnp.int32, (blk, LANES), 1) < ATTN_HEAD_DIM
    ones = jnp.ones((2 * blk, LANES), BF16)
    nt = (((1,), (1,)), ((), ()))

    def pair_attend(q_pair, k_cat, v_cat, has_prev):
        zero = jnp.zeros_like(q_pair)
        q2 = jnp.concatenate([jnp.where(left, q_pair, zero), jnp.where(left, zero, q_pair)], axis=0)
        s = lax.dot_general(q2, k_cat, nt, preferred_element_type=F32)
        valid = band & (in_cur | has_prev)
        s = jnp.where(jnp.concatenate([valid, valid], axis=0), s, MASK_VALUE)
        m = jnp.max(s, axis=-1, keepdims=True)
        p = jnp.exp2(s - m).astype(BF16)
        r = jnp.dot(p, jnp.concatenate([v_cat, ones], axis=1), preferred_element_type=F32)
        acc = jnp.where(left, r[:blk, :LANES], r[blk:, :LANES])
        den = jnp.where(left, r[:blk, LANES:], r[blk:, LANES:])
        m_pair = jnp.where(left, jnp.broadcast_to(m[:blk], (blk, LANES)), jnp.broadcast_to(m[blk:], (blk, LANES)))
        return acc / den, m_pair * LN_2 + jnp.log(den)

    def with_prev(cur_ref, prev_ref, r0, back, prev_rows, first, cs):
        before = cur_ref[pl.ds(pl.multiple_of(jnp.maximum(r0 - back, 0), blk), blk), cs]
        if prev_rows is not None:
            before = jnp.where(first, prev_ref[prev_rows, cs], before)
        return jnp.concatenate([before, cur_ref[pl.ds(r0, blk), cs]], axis=0)

    def tile_body(t, carry):
        for p in range(tiles):
            i = t * tiles + p
            r0 = pl.multiple_of(i * blk, blk)
            for j in range(npair):
                cs = slice(j * LANES, (j + 1) * LANES)
                rows1 = slice(0, blk) if p == 0 else None
                o, lse = pair_attend(q1_ref[pl.ds(r0, blk), cs],
                                     with_prev(k1c_ref, k1p_ref, r0, blk, rows1, t == 0, cs),
                                     with_prev(v1c_ref, v1p_ref, r0, blk, rows1, t == 0, cs),
                                     not_first_span | (i > 0))
                o1_s[j][pl.ds(r0, blk), :] = o
                l1_s[j][pl.ds(r0, blk), :] = lse
                rows2 = slice(p * blk, (p + 1) * blk)
                o, lse = pair_attend(q2_ref[pl.ds(r0, blk), cs],
                                     with_prev(k2c_ref, k2p_ref, r0, tile, rows2, t == 0, cs),
                                     with_prev(v2c_ref, v2p_ref, r0, tile, rows2, t == 0, cs),
                                     not_first_span | (t > 0))
                tok = pl.ds(pl.multiple_of(t * tile, tile) + p, blk, stride=r2)
                o2_s[j][tok, :] = o
                l2_s[j][tok, :] = lse
                rr = pl.multiple_of(i * per3, per3)
                gather = lambda ref: jnp.concatenate(
                    [ref[pl.ds(rr + u * tile, per3), cs] for u in range(tiles)], axis=0)
                k_cat = jnp.concatenate([gather(k3p_ref), gather(k3c_ref)], axis=0)
                v_cat = jnp.concatenate([gather(v3p_ref), gather(v3c_ref)], axis=0)
                o, lse = pair_attend(gather(q3_ref), k_cat, v_cat, not_first_span)
                tok = pl.ds(i, blk, stride=r3)
                o3_s[j][tok, :] = o
                l3_s[j][tok, :] = lse
        return carry

    lax.fori_loop(0, tiles, tile_body, 0)

    for j in range(npair):
        for n in range(span // ATTN_MERGE_ROWS):
            rs = slice(n * ATTN_MERGE_ROWS, (n + 1) * ATTN_MERGE_ROWS)
            l1, l2, l3 = l1_s[j][rs, :], l2_s[j][rs, :], l3_s[j][rs, :]
            m = jnp.maximum(jnp.maximum(l1, l2), l3)
            e1, e2, e3 = jnp.exp(l1 - m), jnp.exp(l2 - m), jnp.exp(l3 - m)
            num = e1 * o1_s[j][rs, :] + e2 * o2_s[j][rs, :] + e3 * o3_s[j][rs, :]
            o_ref[rs, j * LANES:(j + 1) * LANES] = (num / (e1 + e2 + e3)).astype(o_ref.dtype)


def _dilated_attention(qs, ks, vs, weights, B, S):
    assert len(weights) == ATTN_N_CAST
    span = ATTN_SPAN
    nsp = S // span
    gw = ATTN_OUT_WIDTH
    T = B * S
    cur = lambda b, s: (b * nsp + s, 0)
    steps = B * nsp
    slab = lambda w: pl.BlockSpec((w.shape[0] // steps, w.shape[1]), cur)

    def prev(rows):
        per = span // rows
        return pl.BlockSpec((rows, gw), lambda b, s: (jnp.maximum((b * nsp + s) * per - 1, 0), 0))

    halos = (ATTN_BLK, TM_IN, span)
    full = pl.BlockSpec((span, gw), cur)
    kv_specs = []
    for h in halos:
        kv_specs += [full, prev(h)]
    kv_args = lambda arrs: [a for arr in arrs for a in (arr, arr)]
    outs = pl.pallas_call(
        _attn_kernel,
        grid=(B, nsp),
        in_specs=[full] * 3 + kv_specs + kv_specs + [slab(w) for w in weights],
        out_specs=[full] + [slab(w) for w in weights],
        out_shape=[jax.ShapeDtypeStruct((T, gw), BF16)] + [jax.ShapeDtypeStruct(w.shape, BF16) for w in weights],
        scratch_shapes=[pltpu.VMEM((span, LANES), F32)] * (6 * (gw // LANES)),
        compiler_params=_params("parallel", "parallel"),
        name="dilated_attn",
    )(*qs, *kv_args(ks), *kv_args(vs), *weights)
    return outs[0], outs[1:]


def _merge_kernel(x_ref, og_ref, oa_ref, ma_ref, mb_ref,
                  gate_ref, scale_ref, shift_ref, n2w_ref, wg_ref, wa_ref, wo_ref, x1_ref, h2_ref, *, tiles_per_seq):
    b = pl.ds(pl.program_id(0) // tiles_per_seq, 1)
    gate, scale, shift = gate_ref[b, :], scale_ref[b, :], shift_ref[b, :]
    sub = x_ref.shape[0] // MERGE_SPLIT
    for r in range(MERGE_SPLIT):
        rs = slice(r * sub, (r + 1) * sub)
        y_attn = jnp.dot(oa_ref[rs, :], wa_ref[...], preferred_element_type=F32)
        y_gla = jnp.dot(og_ref[rs, :], wg_ref[...], preferred_element_type=F32)
        mixed = (ma_ref[rs, :].astype(F32) * y_gla + mb_ref[rs, :].astype(F32) * y_attn).astype(BF16)
        x1 = x_ref[rs, :] + gate * jnp.dot(mixed, wo_ref[...], preferred_element_type=F32)
        x1_ref[rs, :] = x1
        h2_ref[rs, :] = _rmsnorm_mod(x1, n2w_ref[...], scale, shift).astype(h2_ref.dtype)


def _merge(x2, og, oa, ma, mb, mod, n2w, wg, wa, wo, B):
    T, D = x2.shape
    tm = TM_MERGE
    row = lambda i: (i, 0)
    const = lambda i: (0, 0)
    return pl.pallas_call(
        functools.partial(_merge_kernel, tiles_per_seq=T // B // tm),
        grid=(T // tm,),
        in_specs=[pl.BlockSpec((tm, D), row), pl.BlockSpec((tm, GLA_V_WIDTH), row),
                  pl.BlockSpec((tm, ATTN_OUT_WIDTH), row)]
                 + [pl.BlockSpec((tm, D), row)] * 2
                 + [_mod_spec(MOD_GATE1, D), _mod_spec(MOD_SCALE2, D), _mod_spec(MOD_SHIFT2, D)]
                 + [pl.BlockSpec((1, D), const),
                    pl.BlockSpec(wg.shape, const), pl.BlockSpec(wa.shape, const), pl.BlockSpec(wo.shape, const)],
        out_specs=[pl.BlockSpec((tm, D), row)] * 2,
        out_shape=[jax.ShapeDtypeStruct((T, D), F32), jax.ShapeDtypeStruct((T, D), BF16)],
        compiler_params=_params("parallel"),
        name="merge",
    )(x2, og, oa, ma, mb, mod, mod, mod, n2w, wg, wa, wo)


def _ffn_kernel(h_ref, hprev_ref, x1_ref, wup_ref, cw_ref, cb_ref, wd_ref, gate_ref, fw_ref, o_ref,
                hcat_s, u_s, hid_a, hid_b, order_s, *, tiles_per_seq, n_tiles):
    i = pl.program_id(0)
    tm = h_ref.shape[0]
    halo = hprev_ref.shape[0]
    tf = FFN_CHUNK
    half = tm // 2
    d_model = o_ref.shape[1]
    nchunk = D_FF // tf
    branches = (0, D_FF)

    def finish_previous(hid_prev):
        gate = gate_ref[pl.ds(jnp.maximum(i - 1, 0) // tiles_per_seq, 1), :]
        sq = jnp.zeros((tm, 1), F32)
        for n in range(d_model // FFN_DOWN_SLAB):
            ns = slice(n * FFN_DOWN_SLAB, (n + 1) * FFN_DOWN_SLAB)
            down = jnp.dot(hid_prev[...], wd_ref[:, ns], preferred_element_type=F32)
            for c in range(FFN_DOWN_SLAB // LANES):
                g = n * (FFN_DOWN_SLAB // LANES) + c
                cs = slice(g * LANES, (g + 1) * LANES)
                for parity in range(2):
                    order_s[g, pl.ds(parity, half, stride=2), :] = (
                        down[parity * half:(parity + 1) * half, c * LANES:(c + 1) * LANES])
                x2 = x1_ref[:, cs] + gate[:, cs] * order_s[g]
                sq = sq + jnp.sum(x2 * x2, axis=-1, keepdims=True)
                o_ref[:, cs] = x2
        o_ref[...] = o_ref[...] * lax.rsqrt(sq * (1.0 / d_model) + EPS) * fw_ref[...]

    def up_project(j):
        for n, off in enumerate(branches):
            u = jnp.dot(hcat_s[...], wup_ref[:, off + j * tf:off + (j + 1) * tf], preferred_element_type=F32)
            for c in range(tf // LANES):
                u_s[j % 2, n, c] = u[:, c * LANES:(c + 1) * LANES]

    def conv(j, n, c, parity):
        u = u_s.at[j % 2, n, c]
        cs = slice(branches[n] + j * tf + c * LANES, branches[n] + j * tf + (c + 1) * LANES)
        tap = lambda back: u[pl.ds(halo + parity - back, half, stride=2), :]
        k = 2.0 ** -0.5
        return (cb_ref[:, cs] * k + (cw_ref[0:1, cs] * k) * tap(2) + (cw_ref[1:2, cs] * k) * tap(1)
                + (cw_ref[2:3, cs] * k) * tap(0))

    def hidden_of_tile(hid_cur):
        hcat_s[0:halo] = jnp.where(i % tiles_per_seq == 0, jnp.zeros_like(hprev_ref[...]), hprev_ref[...])
        hcat_s[halo:] = h_ref[...]
        up_project(0)
        for j in range(nchunk):
            if j + 1 < nchunk:
                up_project(j + 1)
            for c in range(tf // LANES):
                for parity in range(2):
                    t = conv(j, 1, c, parity)
                    hidden = (t * (1.0 + lax.erf(t))) * conv(j, 0, c, parity)
                    hid_cur[parity * half:(parity + 1) * half, j * tf + c * LANES:j * tf + (c + 1) * LANES] = (
                        hidden.astype(hid_cur.dtype))

    @pl.when(i == 0)
    def _():
        hid_b[...] = jnp.zeros_like(hid_b)

    for parity_of_step, (hid_cur, hid_prev) in enumerate(((hid_a, hid_b), (hid_b, hid_a))):
        @pl.when((i % 2 == parity_of_step) & (i < n_tiles))
        def _():
            finish_previous(hid_prev)
            hidden_of_tile(hid_cur)

        @pl.when((i % 2 == parity_of_step) & (i == n_tiles))
        def _():
            finish_previous(hid_prev)


def _ffn(h2, x1, w_up, conv_w, conv_b, w_down, mod, final_w, B):
    T, D = x1.shape
    tm, tf, halo = TM_FFN, FFN_CHUNK, FFN_HALO
    n_tiles = T // tm
    cur = lambda i: (jnp.minimum(i, n_tiles - 1), 0)
    lag = lambda i: (jnp.maximum(i - 1, 0), 0)
    const = lambda i: (0, 0)
    single = dict(pipeline_mode=pl.Buffered(1))
    return pl.pallas_call(
        functools.partial(_ffn_kernel, tiles_per_seq=T // B // tm, n_tiles=n_tiles),
        grid=(n_tiles + 1,),
        in_specs=[pl.BlockSpec((tm, D), cur),
                  pl.BlockSpec((halo, D), lambda i: (jnp.clip(i * (tm // halo) - 1, 0, T // halo - 1), 0)),
                  pl.BlockSpec((tm, D), lag),
                  pl.BlockSpec(w_up.shape, const, **single),
                  pl.BlockSpec(conv_w.shape, const, **single),
                  pl.BlockSpec(conv_b.shape, const, **single),
                  pl.BlockSpec(w_down.shape, const, **single),
                  _mod_spec(MOD_GATE2, D),
                  pl.BlockSpec((1, D), const)],
        out_specs=pl.BlockSpec((tm, D), lag),
        out_shape=jax.ShapeDtypeStruct((T, D), F32),
        scratch_shapes=[pltpu.VMEM((halo + tm, D), BF16),
                        pltpu.VMEM((2, 2, tf // LANES, halo + tm, LANES), F32),
                        pltpu.VMEM((tm, D_FF), BF16),
                        pltpu.VMEM((tm, D_FF), BF16),
                        pltpu.VMEM((D // LANES, tm, LANES), F32)],
        compiler_params=_params("arbitrary"),
        name="ffn",
    )(h2, h2, x1, w_up, conv_w, conv_b, w_down, mod, final_w)


def kernel(x, c, positions, ada_w, ada_b, norm1_w, w_in, gla_gate_w2, gla_gate_b, gla_norm_w, w_gla_branch,
           w_attn_branch, w_out, norm2_w, w_up, conv_w, conv_b, w_down, final_norm_w):
    B, S, D = x.shape
    T = B * S
    depth = ada_w.shape[0]
    assert depth == 1, "the final norm is fused into the (single) layer's ffn"
    assert all(window // dilation == ATTN_BLK for window, dilation in ATTN_GROUPS)
    x2 = x.reshape(T, D)
    for layer in range(depth):
        cos, sin, mod, w_in_bf = _prep(positions, c, ada_w[layer], ada_b[layer], w_in[layer].T)

        og, (q1, q2, q3, k1, k2, k3, v1, v2, v3, ma, mb) = _in_proj_gla(
            x2, norm1_w[layer].reshape(1, D), mod, cos, sin, w_in_bf, gla_gate_w2[layer],
            gla_gate_b[layer].reshape(1, -1), gla_norm_w[layer].reshape(1, -1), B)
        oa, (w_up_bf, w_down_bf, w_gla_bf, w_attn_bf, w_out_bf) = _dilated_attention(
            (q1, q2, q3), (k1, k2, k3), (v1, v2, v3),
            (w_up[layer], w_down[layer], w_gla_branch[layer], w_attn_branch[layer], w_out[layer]), B, S)

        x1, h2 = _merge(x2, og, oa, ma, mb, mod, norm2_w[layer].reshape(1, D), w_gla_bf, w_attn_bf, w_out_bf, B)

        x2 = _ffn(h2, x1, w_up_bf, conv_w[layer], conv_b[layer].reshape(1, -1),
                  w_down_bf, mod, final_norm_w.reshape(1, D), B)
    return x2.reshape(B, S, D)
```

```python
import functools

import jax
import jax.numpy as jnp
from jax import lax
from jax.experimental import pallas as pl
from jax.experimental.pallas import tpu as pltpu

F32 = jnp.float32
BF16 = jnp.bfloat16

D_MODEL = 1024
GLA_HEADS = 4
GLA_DK = 128
GLA_DV = 256
GLA_LOWRANK = 16
GLA_TAU = 16.0
GLA_QK_WIDTH = GLA_HEADS * GLA_DK
GLA_V_WIDTH = GLA_HEADS * GLA_DV
ATTN_GROUPS = ((128, 1), (512, 4), (2048, 16))
ATTN_HEADS_PER_GROUP = 4
ATTN_HEAD_DIM = 64
ATTN_WIDTH = ATTN_HEADS_PER_GROUP * len(ATTN_GROUPS) * ATTN_HEAD_DIM
ATTN_OUT_WIDTH = ATTN_HEADS_PER_GROUP * ATTN_HEAD_DIM
ROPE_THETA = 10000.0
D_FF = 2816
CONV_WIDTH = 3
EPS = 1e-6
IN_WIDTHS = (GLA_QK_WIDTH, GLA_QK_WIDTH, GLA_V_WIDTH, GLA_V_WIDTH, GLA_LOWRANK,
             ATTN_WIDTH, ATTN_WIDTH, ATTN_WIDTH, D_MODEL, D_MODEL)

LANES = 128
VMEM_LIMIT_BYTES = 56 * 1024 * 1024

PREP_ROWS = 512
TM_IN = 512
GLA_CHUNK = 64
ATTN_BLK = 128
ATTN_SPAN = 2048
ATTN_MERGE_ROWS = 256
ATTN_N_CAST = 5
TM_MERGE = 1024
MERGE_SPLIT = 2
TM_FFN = 512
FFN_CHUNK = 256
FFN_HALO = 16
FFN_DOWN_SLAB = 256
MASK_VALUE = -1e30
LOG2_E = 1.4426950408889634
LN_2 = 0.6931471805599453


def _params(*sem):
    return pltpu.CompilerParams(dimension_semantics=sem, vmem_limit_bytes=VMEM_LIMIT_BYTES)


def _split_hi_lo(a):
    hi = a.astype(BF16)
    lo = (a - hi.astype(F32)).astype(BF16)
    return hi, lo


def _mod_spec(which, D):
    return pl.BlockSpec((8, D), lambda i: (0, which))


MOD_SHIFT1, MOD_SCALE1, MOD_GATE1, MOD_SHIFT2, MOD_SCALE2, MOD_GATE2 = range(6)


def _prep_kernel(pos_ref, invf_ref, c_ref, adaw_ref, adab_ref, wt_ref, cos_ref, sin_ref, mod_ref, wt_bf_ref):
    c = c_ref[...]
    s = c * (1.0 / (1.0 + jnp.exp(-c)))
    s_hi, s_lo = _split_hi_lo(s)
    lhs = jnp.concatenate([s_hi, s_lo], axis=0)
    w_hi, w_lo = _split_hi_lo(adaw_ref[...])
    acc = jnp.dot(lhs, w_hi, preferred_element_type=F32) + jnp.dot(lhs, w_lo, preferred_element_type=F32)
    mod_ref[...] = acc[0:8] + acc[8:16] + adab_ref[...]

    wt_bf_ref[...] = wt_ref[...].astype(wt_bf_ref.dtype)

    half = ATTN_HEAD_DIM // 2
    groups = LANES // half
    tr = pos_ref.shape[1]
    pos = jnp.concatenate([pos_ref[...].astype(F32), jnp.zeros((8 - groups, tr), F32)], axis=0)
    pos_t = jnp.transpose(pos)
    lane = lax.broadcasted_iota(jnp.int32, (tr, LANES), 1)
    group = lane // half
    pos_dense = jnp.zeros((tr, LANES), F32)
    for q in range(groups):
        pos_dense = jnp.where(group == q, pos_t[:, q:q + 1], pos_dense)
    ang = pos_dense * invf_ref[...]
    first_half = (lane % ATTN_HEAD_DIM) < half
    for table, out_ref, signed in ((jnp.cos(ang), cos_ref, False), (jnp.sin(ang), sin_ref, True)):
        for q in range(groups):
            spread = jnp.where(group == q, table, 0.0)
            shift = groups // 2
            while shift >= 1:
                spread = spread + pltpu.roll(spread, shift * half, axis=1)
                shift //= 2
            out_ref[q] = jnp.where(first_half, -spread, spread) if signed else spread


def _prep(positions, c, ada_w, ada_b, w_in_t):
    T = positions.size
    B, D = c.shape
    N = ada_w.shape[1]
    half = ATTN_HEAD_DIM // 2
    groups = LANES // half
    inv_freq = ROPE_THETA ** (-jnp.arange(half, dtype=F32) / half)
    invf = jnp.tile(inv_freq, groups).reshape(1, LANES)
    per = T // groups
    tr = PREP_ROWS
    steps = per // tr
    tn = N // steps
    bf16_rows = 16
    wt_slab = pl.BlockSpec((-(-w_in_t.shape[0] // (steps * bf16_rows)) * bf16_rows, w_in_t.shape[1]), lambda i: (i, 0))
    c8 = jnp.pad(c, ((0, 8 - B), (0, 0)))
    cos, sin, mod, wt_bf = pl.pallas_call(
        _prep_kernel,
        grid=(steps,),
        in_specs=[pl.BlockSpec((groups, tr), lambda i: (0, i)),
                  pl.BlockSpec((1, LANES), lambda i: (0, 0)),
                  pl.BlockSpec((8, D), lambda i: (0, 0)),
                  pl.BlockSpec((D, tn), lambda i: (0, i)),
                  pl.BlockSpec((1, tn), lambda i: (0, i)),
                  wt_slab],
        out_specs=[pl.BlockSpec((groups, tr, LANES), lambda i: (0, i, 0))] * 2
                  + [pl.BlockSpec((8, tn), lambda i: (0, i)), wt_slab],
        out_shape=[jax.ShapeDtypeStruct((groups, per, LANES), F32)] * 2
                  + [jax.ShapeDtypeStruct((8, N), F32), jax.ShapeDtypeStruct(w_in_t.shape, BF16)],
        compiler_params=_params("parallel"),
        name="prep",
    )(positions.reshape(groups, per), invf, c8, ada_w, ada_b.reshape(1, N), w_in_t)
    return cos.reshape(T, LANES), sin.reshape(T, LANES), mod, wt_bf


def _rmsnorm_mod(x, w, scale, shift):
    ms = jnp.mean(x * x, axis=-1, keepdims=True)
    return (x * lax.rsqrt(ms + EPS) * w) * (1.0 + scale) + shift


def _rotate_half_pairs(t, cos, sin_signed):
    lane = lax.broadcasted_iota(jnp.int32, t.shape, 1)
    first_half = (lane % ATTN_HEAD_DIM) < (ATTN_HEAD_DIM // 2)
    from_right = pltpu.roll(t, LANES - ATTN_HEAD_DIM // 2, axis=1)
    from_left = pltpu.roll(t, ATTN_HEAD_DIM // 2, axis=1)
    return t * cos + jnp.where(first_half, from_right, from_left) * sin_signed


def _store_residue_major(ref, perm_ref, slab, col, dilation):
    if dilation == 1:
        ref[:, col:col + LANES] = slab.astype(ref.dtype)
        return
    perm_ref[...] = slab
    n = slab.shape[0] // dilation
    for p in range(dilation):
        ref[p * n:(p + 1) * n, col:col + LANES] = perm_ref[pl.ds(p, n, stride=dilation), :].astype(ref.dtype)


IN_ALIGNED = sum(IN_WIDTHS[:4])
IN_SHIFTED_START = IN_ALIGNED + GLA_LOWRANK


def _in_proj_gla_kernel(x_ref, n1w_ref, scale_ref, shift_ref, cos_ref, sin_ref, wt_ref, w2_ref, b2_ref, nw_ref,
                        og_ref, q1_ref, q2_ref, q3_ref, k1_ref, k2_ref, k3_ref, v1_ref, v2_ref, v3_ref,
                        ma_ref, mb_ref,
                        perm_ref, gq_s, gk_s, gv_s, gr_s, la_s, state_ref,
                        *, tiles_per_seq, n_tiles):
    i = pl.program_id(0)
    nt = (((1,), (1,)), ((), ()))

    @pl.when(i == 0)
    def _():
        for ref in (gq_s, gk_s, gv_s, gr_s, la_s, state_ref):
            ref[...] = jnp.zeros_like(ref)

    first_of_seq = (i + tiles_per_seq - 1) % tiles_per_seq == 0
    chunks = _gla_chunks(gq_s, gk_s, gv_s, la_s, gr_s, nw_ref, og_ref, state_ref, first_of_seq)

    @pl.when(i == n_tiles)
    def _():
        for chunk in chunks:
            chunk()

    @pl.when(i < n_tiles)
    def _():
        b = i // tiles_per_seq
        h = _rmsnorm_mod(x_ref[...], n1w_ref[...], scale_ref[pl.ds(b, 1), :], shift_ref[pl.ds(b, 1), :]).astype(BF16)

        def rows_proj(row, width):
            return lax.dot_general(h, wt_ref[row:row + width, :], nt, preferred_element_type=F32)

        def proj(col, width):
            return rows_proj(IN_SHIFTED_START + col, width)

        cos = cos_ref[...]
        sin = sin_ref[...]
        q_scale = ATTN_HEAD_DIM ** -0.5 * LOG2_E
        dilations = [d for _, d in ATTN_GROUPS]

        def rope_piece(ref, col, scale, dilation):
            def run():
                t = proj(col, ATTN_OUT_WIDTH)
                for s in range(0, ATTN_OUT_WIDTH, LANES):
                    rot = _rotate_half_pairs(t[:, s:s + LANES], cos, sin) * scale
                    _store_residue_major(ref, perm_ref, rot, s, dilation)
            return run

        def value_piece(ref, col, dilation):
            def run():
                t = proj(col, ATTN_OUT_WIDTH)
                for s in range(0, ATTN_OUT_WIDTH, LANES):
                    _store_residue_major(ref, perm_ref, t[:, s:s + LANES], s, dilation)
            return run

        def gate_piece(ref, col, s):
            def run():
                z = proj(col + s, 512)
                ref[:, s:s + 512] = (1.0 / (1.0 + jnp.exp(-z))).astype(ref.dtype)
            return run

        pieces = []
        col = 0
        for refs, scale in (((q1_ref, q2_ref, q3_ref), q_scale), ((k1_ref, k2_ref, k3_ref), 1.0)):
            for g, ref in enumerate(refs):
                pieces.append(rope_piece(ref, col, scale, dilations[g]))
                col += ATTN_OUT_WIDTH
        for g, ref in enumerate((v1_ref, v2_ref, v3_ref)):
            pieces.append(value_piece(ref, col, dilations[g]))
            col += ATTN_OUT_WIDTH
        for ref in (ma_ref, mb_ref):
            for s in range(0, D_MODEL, 512):
                pieces.append(gate_piece(ref, col, s))
            col += D_MODEL

        for n in range(max(len(pieces), len(chunks))):
            if n < len(pieces):
                pieces[n]()
            if n < len(chunks):
                chunks[n]()

        col = 0
        for ref, width in ((gq_s, GLA_QK_WIDTH), (gk_s, GLA_QK_WIDTH)):
            ref[...] = rows_proj(col, width).astype(ref.dtype)
            col += width
        for ref in (gv_s, gr_s):
            for s in range(0, GLA_V_WIDTH, 512):
                ref[:, s:s + 512] = rows_proj(col + s, 512).astype(ref.dtype)
            col += GLA_V_WIDTH
        g_lr = rows_proj(IN_ALIGNED, LANES)
        g_hi, g_lo = _split_hi_lo(g_lr)
        w2_hi, w2_lo = _split_hi_lo(w2_ref[...])
        z = (jnp.dot(g_hi, w2_hi, preferred_element_type=F32) + jnp.dot(g_lo, w2_hi, preferred_element_type=F32)
             + jnp.dot(g_hi, w2_lo, preferred_element_type=F32)) + b2_ref[...]
        log_sig = jnp.minimum(z, 0.0) - jnp.log(1.0 + jnp.exp(-jnp.abs(z)))
        la_s[...] = log_sig * (LOG2_E / GLA_TAU)


def _gla_chunks(q_ref, k_ref, v_ref, la_ref, gr_ref, nw_ref, o_ref, state_ref, first_of_seq):
    C = GLA_CHUNK
    H = C // 2
    heads = range(GLA_HEADS)
    rows = lax.broadcasted_iota(jnp.int32, (C, C), 0)
    cols = lax.broadcasted_iota(jnp.int32, (C, C), 1)
    causal = cols <= rows
    tri = causal.astype(BF16)
    same_half = (rows >= H) == (cols >= H)
    upper_rows = lax.broadcasted_iota(jnp.int32, (C, GLA_QK_WIDTH), 0) < H
    q_scale = GLA_DK ** -0.5
    nw = jnp.concatenate([nw_ref[...]] * GLA_HEADS, axis=1)
    nt = (((1,), (1,)), ((), ()))
    ks = lambda a, h: a[:, h * GLA_DK:(h + 1) * GLA_DK]
    vs = lambda a, h: a[:, h * GLA_DV:(h + 1) * GLA_DV]

    def run_chunk(c):
        sl = pl.ds(c * C, C)
        q = q_ref[sl, :].astype(F32) * q_scale
        k = k_ref[sl, :].astype(F32)
        v = v_ref[sl, :]
        la_hi, la_lo = _split_hi_lo(la_ref[sl, :])
        cum = jnp.dot(tri, la_hi, preferred_element_type=F32) + jnp.dot(tri, la_lo, preferred_element_type=F32)
        mid = jnp.where(upper_rows, cum[H // 2 - 1:H // 2, :], cum[H + H // 2 - 1:H + H // 2, :])
        edge = cum[H - 1:H, :]
        last = cum[C - 1:C, :]
        qg = (q * jnp.exp2(cum - mid)).astype(BF16)
        kg = (k * jnp.exp2(mid - cum)).astype(BF16)
        zeros_half = jnp.zeros((H, GLA_QK_WIDTH), BF16)
        q_lo = jnp.concatenate([zeros_half, (q[H:] * jnp.exp2(cum[H:] - edge)).astype(BF16)], axis=0)
        k_up = jnp.concatenate([(k[:H] * jnp.exp2(edge - cum[:H])).astype(BF16), zeros_half], axis=0)
        q_in = (q * jnp.exp2(cum)).astype(BF16)
        k_out = k * jnp.exp2(last - cum)
        decay_row = jnp.broadcast_to(jnp.exp2(last), (8, GLA_QK_WIDTH))
        g = gr_ref[sl, :].astype(F32)
        gate = nw * (g * (1.0 / (1.0 + jnp.exp(-g))))

        within = [lax.dot_general(ks(qg, h), ks(kg, h), nt, preferred_element_type=F32) for h in heads]
        across = [lax.dot_general(ks(q_lo, h), ks(k_up, h), nt, preferred_element_type=F32) for h in heads]
        attn = [jnp.where(same_half, jnp.where(causal, within[h], 0.0), across[h]).astype(BF16) for h in heads]
        if c == 0:
            state = [jnp.where(first_of_seq, 0.0, state_ref[h]) for h in heads]
        else:
            state = [state_ref[h] for h in heads]
        o = [jnp.dot(attn[h], vs(v, h), preferred_element_type=F32)
             + jnp.dot(ks(q_in, h), state[h].astype(BF16), preferred_element_type=F32) for h in heads]
        k_out_t = [jnp.transpose(ks(k_out, h)).astype(BF16) for h in heads]
        decay = [jnp.transpose(ks(decay_row, h))[:, 0:1] for h in heads]
        for h in heads:
            state_ref[h] = decay[h] * state[h] + jnp.dot(k_out_t[h], vs(v, h), preferred_element_type=F32)
        ms = [jnp.mean(o[h] * o[h], axis=-1, keepdims=True) for h in heads]
        for h in heads:
            o_ref[sl, h * GLA_DV:(h + 1) * GLA_DV] = (o[h] * lax.rsqrt(ms[h] + EPS) * vs(gate, h)).astype(o_ref.dtype)

    return [functools.partial(run_chunk, c) for c in range(q_ref.shape[0] // C)]


def _in_proj_gla(x2, n1w, mod, cos, sin, w_in_t, w2, b2, gla_norm_w, B):
    T, D = x2.shape
    tm = TM_IN
    n_tiles = T // tm
    cur = lambda i: (jnp.minimum(i, n_tiles - 1), 0)
    lag = lambda i: (jnp.maximum(i - 1, 0), 0)
    const = lambda i: (0, 0)
    widths = (ATTN_OUT_WIDTH,) * 9 + (D_MODEL, D_MODEL)
    single = dict(pipeline_mode=pl.Buffered(1))
    w2p = jnp.pad(w2, ((0, LANES - GLA_LOWRANK), (0, 0)))
    outs = pl.pallas_call(
        functools.partial(_in_proj_gla_kernel, tiles_per_seq=T // B // tm, n_tiles=n_tiles),
        grid=(n_tiles + 1,),
        in_specs=[pl.BlockSpec((tm, D), cur),
                  pl.BlockSpec((1, D), const),
                  _mod_spec(MOD_SCALE1, D),
                  _mod_spec(MOD_SHIFT1, D),
                  pl.BlockSpec((tm, LANES), cur),
                  pl.BlockSpec((tm, LANES), cur),
                  pl.BlockSpec(w_in_t.shape, const, **single),
                  pl.BlockSpec(w2p.shape, const, **single),
                  pl.BlockSpec(b2.shape, const, **single),
                  pl.BlockSpec(gla_norm_w.shape, const, **single)],
        out_specs=[pl.BlockSpec((tm, GLA_V_WIDTH), lag)] + [pl.BlockSpec((tm, w), cur) for w in widths],
        out_shape=[jax.ShapeDtypeStruct((T, GLA_V_WIDTH), BF16)]
                  + [jax.ShapeDtypeStruct((T, w), BF16) for w in widths],
        scratch_shapes=[pltpu.VMEM((tm, LANES), F32),
                        pltpu.VMEM((tm, GLA_QK_WIDTH), BF16),
                        pltpu.VMEM((tm, GLA_QK_WIDTH), BF16),
                        pltpu.VMEM((tm, GLA_V_WIDTH), BF16),
                        pltpu.VMEM((tm, GLA_V_WIDTH), BF16),
                        pltpu.VMEM((tm, GLA_QK_WIDTH), F32),
                        pltpu.VMEM((GLA_HEADS, GLA_DK, GLA_DV), F32)],
        compiler_params=_params("arbitrary"),
        name="in_proj_gla",
    )(x2, n1w, mod, mod, cos, sin, w_in_t, w2p, b2, gla_norm_w)
    return outs[0], outs[1:]


def _attn_kernel(q1_ref, q2_ref, q3_ref, k1c_ref, k1p_ref, k2c_ref, k2p_ref, k3c_ref, k3p_ref,
                 v1c_ref, v1p_ref, v2c_ref, v2p_ref, v3c_ref, v3p_ref, *refs):
    n_cast = ATTN_N_CAST
    o_ref = refs[n_cast]
    slabs = refs[2 * n_cast + 1:]
    for src, dst in zip(refs[:n_cast], refs[n_cast + 1:2 * n_cast + 1]):
        dst[...] = src[...].astype(dst.dtype)

    blk = ATTN_BLK
    tile = TM_IN
    span = ATTN_SPAN
    tiles = span // tile
    r2, r3 = ATTN_GROUPS[1][1], ATTN_GROUPS[2][1]
    per3 = tile // r3
    not_first_span = pl.program_id(1) > 0
    npair = ATTN_OUT_WIDTH // LANES
    o1_s, o2_s, o3_s, l1_s, l2_s, l3_s = [slabs[n * npair:(n + 1) * npair] for n in range(6)]

    rows = lax.broadcasted_iota(jnp.int32, (blk, 2 * blk), 0)
    cols = lax.broadcasted_iota(jnp.int32, (blk, 2 * blk), 1)
    band = (cols >= rows) & (cols <= rows + blk)
    in_cur = cols >= blk
    left = lax.broadcasted_iota(jnp.int32, (blk, LANES), 1) < ATTN_HEAD_DIM
    ones = jnp.ones((2 * blk, LANES), BF16)
    nt = (((1,), (1,)), ((), ()))

    def pair_attend(q_pair, k_cat, v_cat, has_prev):
        zero = jnp.zeros_like(q_pair)
        q2 = jnp.concatenate([jnp.where(left, q_pair, zero), jnp.where(left, zero, q_pair)], axis=0)
        s = lax.dot_general(q2, k_cat, nt, preferred_element_type=F32)
        valid = band & (in_cur | has_prev)
        s = jnp.where(jnp.concatenate([valid, valid], axis=0), s, MASK_VALUE)
        m = jnp.max(s, axis=-1, keepdims=True)
        p = jnp.exp2(s - m).astype(BF16)
        r = jnp.dot(p, jnp.concatenate([v_cat, ones], axis=1), preferred_element_type=F32)
        acc = jnp.where(left, r[:blk, :LANES], r[blk:, :LANES])
        den = jnp.where(left, r[:blk, LANES:], r[blk:, LANES:])
        m_pair = jnp.where(left, jnp.broadcast_to(m[:blk], (blk, LANES)), jnp.broadcast_to(m[blk:], (blk, LANES)))
        return acc / den, m_pair * LN_2 + jnp.log(den)

    def with_prev(cur_ref, prev_ref, r0, back, prev_rows, first, cs):
        before = cur_ref[pl.ds(pl.multiple_of(jnp.maximum(r0 - back, 0), blk), blk), cs]
        if prev_rows is not None:
            before = jnp.where(first, prev_ref[prev_rows, cs], before)
        return jnp.concatenate([before, cur_ref[pl.ds(r0, blk), cs]], axis=0)

    def tile_body(t, carry):
        for p in range(tiles):
            i = t * tiles + p
            r0 = pl.multiple_of(i * blk, blk)
            for j in range(npair):
                cs = slice(j * LANES, (j + 1) * LANES)
                rows1 = slice(0, blk) if p == 0 else None
                o, lse = pair_attend(q1_ref[pl.ds(r0, blk), cs],
                                     with_prev(k1c_ref, k1p_ref, r0, blk, rows1, t == 0, cs),
                                     with_prev(v1c_ref, v1p_ref, r0, blk, rows1, t == 0, cs),
                                     not_first_span | (i > 0))
                o1_s[j][pl.ds(r0, blk), :] = o
                l1_s[j][pl.ds(r0, blk), :] = lse
                rows2 = slice(p * blk, (p + 1) * blk)
                o, lse = pair_attend(q2_ref[pl.ds(r0, blk), cs],
                                     with_prev(k2c_ref, k2p_ref, r0, tile, rows2, t == 0, cs),
                                     with_prev(v2c_ref, v2p_ref, r0, tile, rows2, t == 0, cs),
                                     not_first_span | (t > 0))
                tok = pl.ds(pl.multiple_of(t * tile, tile) + p, blk, stride=r2)
                o2_s[j][tok, :] = o
                l2_s[j][tok, :] = lse
                rr = pl.multiple_of(i * per3, per3)
                gather = lambda ref: jnp.concatenate(
                    [ref[pl.ds(rr + u * tile, per3), cs] for u in range(tiles)], axis=0)
                k_cat = jnp.concatenate([gather(k3p_ref), gather(k3c_ref)], axis=0)
                v_cat = jnp.concatenate([gather(v3p_ref), gather(v3c_ref)], axis=0)
                o, lse = pair_attend(gather(q3_ref), k_cat, v_cat, not_first_span)
                tok = pl.ds(i, blk, stride=r3)
                o3_s[j][tok, :] = o
                l3_s[j][tok, :] = lse
        return carry

    lax.fori_loop(0, tiles, tile_body, 0)

    for j in range(npair):
        for n in range(span // ATTN_MERGE_ROWS):
            rs = slice(n * ATTN_MERGE_ROWS, (n + 1) * ATTN_MERGE_ROWS)
            l1, l2, l3 = l1_s[j][rs, :], l2_s[j][rs, :], l3_s[j][rs, :]
            m = jnp.maximum(jnp.maximum(l1, l2), l3)
            e1, e2, e3 = jnp.exp(l1 - m), jnp.exp(l2 - m), jnp.exp(l3 - m)
            num = e1 * o1_s[j][rs, :] + e2 * o2_s[j][rs, :] + e3 * o3_s[j][rs, :]
            o_ref[rs, j * LANES:(j + 1) * LANES] = (num / (e1 + e2 + e3)).astype(o_ref.dtype)


def _dilated_attention(qs, ks, vs, weights, B, S):
    assert len(weights) == ATTN_N_CAST
    span = ATTN_SPAN
    nsp = S // span
    gw = ATTN_OUT_WIDTH
    T = B * S
    cur = lambda b, s: (b * nsp + s, 0)
    steps = B * nsp
    slab = lambda w: pl.BlockSpec((w.shape[0] // steps, w.shape[1]), cur)

    def prev(rows):
        per = span // rows
        return pl.BlockSpec((rows, gw), lambda b, s: (jnp.maximum((b * nsp + s) * per - 1, 0), 0))

    halos = (ATTN_BLK, TM_IN, span)
    full = pl.BlockSpec((span, gw), cur)
    kv_specs = []
    for h in halos:
        kv_specs += [full, prev(h)]
    kv_args = lambda arrs: [a for arr in arrs for a in (arr, arr)]
    outs = pl.pallas_call(
        _attn_kernel,
        grid=(B, nsp),
        in_specs=[full] * 3 + kv_specs + kv_specs + [slab(w) for w in weights],
        out_specs=[full] + [slab(w) for w in weights],
        out_shape=[jax.ShapeDtypeStruct((T, gw), BF16)] + [jax.ShapeDtypeStruct(w.shape, BF16) for w in weights],
        scratch_shapes=[pltpu.VMEM((span, LANES), F32)] * (6 * (gw // LANES)),
        compiler_params=_params("parallel", "parallel"),
        name="dilated_attn",
    )(*qs, *kv_args(ks), *kv_args(vs), *weights)
    return outs[0], outs[1:]


def _merge_kernel(x_ref, og_ref, oa_ref, ma_ref, mb_ref,
                  gate_ref, scale_ref, shift_ref, n2w_ref, wg_ref, wa_ref, wo_ref, x1_ref, h2_ref, *, tiles_per_seq):
    b = pl.ds(pl.program_id(0) // tiles_per_seq, 1)
    gate, scale, shift = gate_ref[b, :], scale_ref[b, :], shift_ref[b, :]
    sub = x_ref.shape[0] // MERGE_SPLIT
    for r in range(MERGE_SPLIT):
        rs = slice(r * sub, (r + 1) * sub)
        y_attn = jnp.dot(oa_ref[rs, :], wa_ref[...], preferred_element_type=F32)
        y_gla = jnp.dot(og_ref[rs, :], wg_ref[...], preferred_element_type=F32)
        mixed = (ma_ref[rs, :].astype(F32) * y_gla + mb_ref[rs, :].astype(F32) * y_attn).astype(BF16)
        x1 = x_ref[rs, :] + gate * jnp.dot(mixed, wo_ref[...], preferred_element_type=F32)
        x1_ref[rs, :] = x1
        h2_ref[rs, :] = _rmsnorm_mod(x1, n2w_ref[...], scale, shift).astype(h2_ref.dtype)


def _merge(x2, og, oa, ma, mb, mod, n2w, wg, wa, wo, B):
    T, D = x2.shape
    tm = TM_MERGE
    row = lambda i: (i, 0)
    const = lambda i: (0, 0)
    return pl.pallas_call(
        functools.partial(_merge_kernel, tiles_per_seq=T // B // tm),
        grid=(T // tm,),
        in_specs=[pl.BlockSpec((tm, D), row), pl.BlockSpec((tm, GLA_V_WIDTH), row),
                  pl.BlockSpec((tm, ATTN_OUT_WIDTH), row)]
                 + [pl.BlockSpec((tm, D), row)] * 2
                 + [_mod_spec(MOD_GATE1, D), _mod_spec(MOD_SCALE2, D), _mod_spec(MOD_SHIFT2, D)]
                 + [pl.BlockSpec((1, D), const),
                    pl.BlockSpec(wg.shape, const), pl.BlockSpec(wa.shape, const), pl.BlockSpec(wo.shape, const)],
        out_specs=[pl.BlockSpec((tm, D), row)] * 2,
        out_shape=[jax.ShapeDtypeStruct((T, D), F32), jax.ShapeDtypeStruct((T, D), BF16)],
        compiler_params=_params("parallel"),
        name="merge",
    )(x2, og, oa, ma, mb, mod, mod, mod, n2w, wg, wa, wo)


def _ffn_kernel(h_ref, hprev_ref, x1_ref, wup_ref, cw_ref, cb_ref, wd_ref, gate_ref, fw_ref, o_ref,
                hcat_s, u_s, hid_s, order_s, *, tiles_per_seq):
    i = pl.program_id(0)
    tm = h_ref.shape[0]
    halo = hprev_ref.shape[0]
    tf = FFN_CHUNK
    half = tm // 2
    hcat_s[0:halo] = jnp.where(i % tiles_per_seq == 0, jnp.zeros_like(hprev_ref[...]), hprev_ref[...])
    hcat_s[halo:] = h_ref[...]
    nchunk = D_FF // tf
    branches = (0, D_FF)

    def up_project(j):
        for n, off in enumerate(branches):
            u = jnp.dot(hcat_s[...], wup_ref[:, off + j * tf:off + (j + 1) * tf], preferred_element_type=F32)
            for c in range(tf // LANES):
                u_s[j % 2, n, c] = u[:, c * LANES:(c + 1) * LANES]

    def conv(j, n, c, parity):
        u = u_s.at[j % 2, n, c]
        cs = slice(branches[n] + j * tf + c * LANES, branches[n] + j * tf + (c + 1) * LANES)
        tap = lambda back: u[pl.ds(halo + parity - back, half, stride=2), :]
        k = 2.0 ** -0.5
        return (cb_ref[:, cs] * k + (cw_ref[0:1, cs] * k) * tap(2) + (cw_ref[1:2, cs] * k) * tap(1)
                + (cw_ref[2:3, cs] * k) * tap(0))

    up_project(0)
    for j in range(nchunk):
        if j + 1 < nchunk:
            up_project(j + 1)
        for c in range(tf // LANES):
            for parity in range(2):
                t = conv(j, 1, c, parity)
                hidden = (t * (1.0 + lax.erf(t))) * conv(j, 0, c, parity)
                hid_s[parity * half:(parity + 1) * half, j * tf + c * LANES:j * tf + (c + 1) * LANES] = (
                    hidden.astype(hid_s.dtype))

    gate = gate_ref[pl.ds(i // tiles_per_seq, 1), :]
    d_model = o_ref.shape[1]
    sq = jnp.zeros((tm, 1), F32)
    for n in range(d_model // FFN_DOWN_SLAB):
        ns = slice(n * FFN_DOWN_SLAB, (n + 1) * FFN_DOWN_SLAB)
        down = jnp.dot(hid_s[...], wd_ref[:, ns], preferred_element_type=F32)
        for c in range(FFN_DOWN_SLAB // LANES):
            g = n * (FFN_DOWN_SLAB // LANES) + c
            cs = slice(g * LANES, (g + 1) * LANES)
            for parity in range(2):
                order_s[g, pl.ds(parity, half, stride=2), :] = (
                    down[parity * half:(parity + 1) * half, c * LANES:(c + 1) * LANES])
            x2 = x1_ref[:, cs] + gate[:, cs] * order_s[g]
            sq = sq + jnp.sum(x2 * x2, axis=-1, keepdims=True)
            o_ref[:, cs] = x2
    o_ref[...] = o_ref[...] * lax.rsqrt(sq * (1.0 / d_model) + EPS) * fw_ref[...]


def _ffn(h2, x1, w_up, conv_w, conv_b, w_down, mod, final_w, B):
    T, D = x1.shape
    tm, tf, halo = TM_FFN, FFN_CHUNK, FFN_HALO
    row = lambda i: (i, 0)
    const = lambda i: (0, 0)
    single = dict(pipeline_mode=pl.Buffered(1))
    return pl.pallas_call(
        functools.partial(_ffn_kernel, tiles_per_seq=T // B // tm),
        grid=(T // tm,),
        in_specs=[pl.BlockSpec((tm, D), row),
                  pl.BlockSpec((halo, D), lambda i: (jnp.maximum(i * (tm // halo) - 1, 0), 0)),
                  pl.BlockSpec((tm, D), row),
                  pl.BlockSpec(w_up.shape, const, **single),
                  pl.BlockSpec(conv_w.shape, const, **single),
                  pl.BlockSpec(conv_b.shape, const, **single),
                  pl.BlockSpec(w_down.shape, const, **single),
                  _mod_spec(MOD_GATE2, D),
                  pl.BlockSpec((1, D), const)],
        out_specs=pl.BlockSpec((tm, D), row),
        out_shape=jax.ShapeDtypeStruct((T, D), F32),
        scratch_shapes=[pltpu.VMEM((halo + tm, D), BF16),
                        pltpu.VMEM((2, 2, tf // LANES, halo + tm, LANES), F32),
                        pltpu.VMEM((tm, D_FF), BF16),
                        pltpu.VMEM((D // LANES, tm, LANES), F32)],
        compiler_params=_params("parallel"),
        name="ffn",
    )(h2, h2, x1, w_up, conv_w, conv_b, w_down, mod, final_w)


def kernel(x, c, positions, ada_w, ada_b, norm1_w, w_in, gla_gate_w2, gla_gate_b, gla_norm_w, w_gla_branch,
           w_attn_branch, w_out, norm2_w, w_up, conv_w, conv_b, w_down, final_norm_w):
    B, S, D = x.shape
    T = B * S
    depth = ada_w.shape[0]
    assert depth == 1, "the final norm is fused into the (single) layer's ffn"
    assert all(window // dilation == ATTN_BLK for window, dilation in ATTN_GROUPS)
    x2 = x.reshape(T, D)
    for layer in range(depth):
        cos, sin, mod, w_in_bf = _prep(positions, c, ada_w[layer], ada_b[layer], w_in[layer].T)

        og, (q1, q2, q3, k1, k2, k3, v1, v2, v3, ma, mb) = _in_proj_gla(
            x2, norm1_w[layer].reshape(1, D), mod, cos, sin, w_in_bf, gla_gate_w2[layer],
            gla_gate_b[layer].reshape(1, -1), gla_norm_w[layer].reshape(1, -1), B)
        oa, (w_up_bf, w_down_bf, w_gla_bf, w_attn_bf, w_out_bf) = _dilated_attention(
            (q1, q2, q3), (k1, k2, k3), (v1, v2, v3),
            (w_up[layer], w_down[layer], w_gla_branch[layer], w_attn_branch[layer], w_out[layer]), B, S)

        x1, h2 = _merge(x2, og, oa, ma, mb, mod, norm2_w[layer].reshape(1, D), w_gla_bf, w_attn_bf, w_out_bf, B)

        x2 = _ffn(h2, x1, w_up_bf, conv_w[layer], conv_b[layer].reshape(1, -1),
                  w_down_bf, mod, final_norm_w.reshape(1, D), B)
    return x2.reshape(B, S, D)
```

```python
import functools

import jax
import jax.numpy as jnp
from jax import lax
from jax.experimental import pallas as pl
from jax.experimental.pallas import tpu as pltpu

F32 = jnp.float32
BF16 = jnp.bfloat16

D_MODEL = 1024
GLA_HEADS = 4
GLA_DK = 128
GLA_DV = 256
GLA_LOWRANK = 16
GLA_TAU = 16.0
GLA_QK_WIDTH = GLA_HEADS * GLA_DK
GLA_V_WIDTH = GLA_HEADS * GLA_DV
ATTN_GROUPS = ((128, 1), (512, 4), (2048, 16))
ATTN_HEADS_PER_GROUP = 4
ATTN_HEAD_DIM = 64
ATTN_WIDTH = ATTN_HEADS_PER_GROUP * len(ATTN_GROUPS) * ATTN_HEAD_DIM
ATTN_OUT_WIDTH = ATTN_HEADS_PER_GROUP * ATTN_HEAD_DIM
ROPE_THETA = 10000.0
D_FF = 2816
CONV_WIDTH = 3
EPS = 1e-6
IN_WIDTHS = (GLA_QK_WIDTH, GLA_QK_WIDTH, GLA_V_WIDTH, GLA_V_WIDTH, GLA_LOWRANK,
             ATTN_WIDTH, ATTN_WIDTH, ATTN_WIDTH, D_MODEL, D_MODEL)

LANES = 128
VMEM_LIMIT_BYTES = 56 * 1024 * 1024

PREP_ROWS = 512
TM_IN = 512
GLA_CHUNK = 64
ATTN_BLK = 128
ATTN_SPAN = 2048
ATTN_MERGE_ROWS = 256
ATTN_N_CAST = 5
TM_MERGE = 1024
MERGE_SPLIT = 2
TM_FFN = 512
FFN_CHUNK = 256
FFN_HALO = 16
FFN_DOWN_SLAB = 256
MASK_VALUE = float("-inf")
LOG2_E = 1.4426950408889634
LN_2 = 0.6931471805599453


def _params(*sem):
    return pltpu.CompilerParams(dimension_semantics=sem, vmem_limit_bytes=VMEM_LIMIT_BYTES)


def _split_hi_lo(a):
    hi = a.astype(BF16)
    lo = (a - hi.astype(F32)).astype(BF16)
    return hi, lo


def _mod_spec(which, D):
    return pl.BlockSpec((8, D), lambda i: (0, which))


MOD_SHIFT1, MOD_SCALE1, MOD_GATE1, MOD_SHIFT2, MOD_SCALE2, MOD_GATE2 = range(6)


def _prep_kernel(pos_ref, invf_ref, c_ref, adaw_ref, adab_ref, wt_ref, cos_ref, sin_ref, mod_ref, wt_bf_ref):
    c = c_ref[...]
    s = c * (1.0 / (1.0 + jnp.exp(-c)))
    s_hi, s_lo = _split_hi_lo(s)
    lhs = jnp.concatenate([s_hi, s_lo], axis=0)
    w_hi, w_lo = _split_hi_lo(adaw_ref[...])
    acc = jnp.dot(lhs, w_hi, preferred_element_type=F32) + jnp.dot(lhs, w_lo, preferred_element_type=F32)
    mod_ref[...] = acc[0:8] + acc[8:16] + adab_ref[...]

    wt_bf_ref[...] = wt_ref[...].astype(wt_bf_ref.dtype)

    half = ATTN_HEAD_DIM // 2
    groups = LANES // half
    tr = pos_ref.shape[1]
    pos = jnp.concatenate([pos_ref[...].astype(F32), jnp.zeros((8 - groups, tr), F32)], axis=0)
    pos_t = jnp.transpose(pos)
    lane = lax.broadcasted_iota(jnp.int32, (tr, LANES), 1)
    group = lane // half
    pos_dense = jnp.zeros((tr, LANES), F32)
    for q in range(groups):
        pos_dense = jnp.where(group == q, pos_t[:, q:q + 1], pos_dense)
    ang = pos_dense * invf_ref[...]
    first_half = (lane % ATTN_HEAD_DIM) < half
    for table, out_ref, signed in ((jnp.cos(ang), cos_ref, False), (jnp.sin(ang), sin_ref, True)):
        for q in range(groups):
            spread = jnp.where(group == q, table, 0.0)
            shift = groups // 2
            while shift >= 1:
                spread = spread + pltpu.roll(spread, shift * half, axis=1)
                shift //= 2
            out_ref[q] = jnp.where(first_half, -spread, spread) if signed else spread


def _prep(positions, c, ada_w, ada_b, w_in_t):
    T = positions.size
    B, D = c.shape
    N = ada_w.shape[1]
    half = ATTN_HEAD_DIM // 2
    groups = LANES // half
    inv_freq = ROPE_THETA ** (-jnp.arange(half, dtype=F32) / half)
    invf = jnp.tile(inv_freq, groups).reshape(1, LANES)
    per = T // groups
    tr = PREP_ROWS
    steps = per // tr
    tn = N // steps
    bf16_rows = 16
    wt_slab = pl.BlockSpec((-(-w_in_t.shape[0] // (steps * bf16_rows)) * bf16_rows, w_in_t.shape[1]), lambda i: (i, 0))
    c8 = jnp.pad(c, ((0, 8 - B), (0, 0)))
    cos, sin, mod, wt_bf = pl.pallas_call(
        _prep_kernel,
        grid=(steps,),
        in_specs=[pl.BlockSpec((groups, tr), lambda i: (0, i)),
                  pl.BlockSpec((1, LANES), lambda i: (0, 0)),
                  pl.BlockSpec((8, D), lambda i: (0, 0)),
                  pl.BlockSpec((D, tn), lambda i: (0, i)),
                  pl.BlockSpec((1, tn), lambda i: (0, i)),
                  wt_slab],
        out_specs=[pl.BlockSpec((groups, tr, LANES), lambda i: (0, i, 0))] * 2
                  + [pl.BlockSpec((8, tn), lambda i: (0, i)), wt_slab],
        out_shape=[jax.ShapeDtypeStruct((groups, per, LANES), F32)] * 2
                  + [jax.ShapeDtypeStruct((8, N), F32), jax.ShapeDtypeStruct(w_in_t.shape, BF16)],
        compiler_params=_params("parallel"),
        name="prep",
    )(positions.reshape(groups, per), invf, c8, ada_w, ada_b.reshape(1, N), w_in_t)
    return cos.reshape(T, LANES), sin.reshape(T, LANES), mod, wt_bf


def _rmsnorm_mod(x, w, scale, shift):
    ms = jnp.mean(x * x, axis=-1, keepdims=True)
    return (x * lax.rsqrt(ms + EPS) * w) * (1.0 + scale) + shift


def _rotate_half_pairs(t, cos, sin_signed):
    lane = lax.broadcasted_iota(jnp.int32, t.shape, 1)
    first_half = (lane % ATTN_HEAD_DIM) < (ATTN_HEAD_DIM // 2)
    from_right = pltpu.roll(t, LANES - ATTN_HEAD_DIM // 2, axis=1)
    from_left = pltpu.roll(t, ATTN_HEAD_DIM // 2, axis=1)
    return t * cos + jnp.where(first_half, from_right, from_left) * sin_signed


def _store_residue_major(ref, perm_ref, slab, col, dilation):
    if dilation == 1:
        ref[:, col:col + LANES] = slab.astype(ref.dtype)
        return
    perm_ref[...] = slab
    n = slab.shape[0] // dilation
    for p in range(dilation):
        ref[p * n:(p + 1) * n, col:col + LANES] = perm_ref[pl.ds(p, n, stride=dilation), :].astype(ref.dtype)


IN_ALIGNED = sum(IN_WIDTHS[:4])
IN_SHIFTED_START = IN_ALIGNED + GLA_LOWRANK


def _in_proj_gla_kernel(x_ref, n1w_ref, scale_ref, shift_ref, cos_ref, sin_ref, wt_ref, w2_ref, b2_ref, nw_ref,
                        og_ref, q1_ref, q2_ref, q3_ref, k1_ref, k2_ref, k3_ref, v1_ref, v2_ref, v3_ref,
                        ma_ref, mb_ref,
                        perm_ref, gq_s, gk_s, gv_s, gr_s, la_s, state_ref,
                        *, tiles_per_seq, n_tiles):
    i = pl.program_id(0)
    nt = (((1,), (1,)), ((), ()))

    @pl.when(i == 0)
    def _():
        for ref in (gq_s, gk_s, gv_s, gr_s, la_s, state_ref):
            ref[...] = jnp.zeros_like(ref)

    first_of_seq = (i + tiles_per_seq - 1) % tiles_per_seq == 0
    chunks = _gla_chunks(gq_s, gk_s, gv_s, la_s, gr_s, nw_ref, og_ref, state_ref, first_of_seq)

    @pl.when(i == n_tiles)
    def _():
        for chunk in chunks:
            chunk()

    @pl.when(i < n_tiles)
    def _():
        b = i // tiles_per_seq
        h = _rmsnorm_mod(x_ref[...], n1w_ref[...], scale_ref[pl.ds(b, 1), :], shift_ref[pl.ds(b, 1), :]).astype(BF16)

        def rows_proj(row, width):
            return lax.dot_general(h, wt_ref[row:row + width, :], nt, preferred_element_type=F32)

        def proj(col, width):
            return rows_proj(IN_SHIFTED_START + col, width)

        cos = cos_ref[...]
        sin = sin_ref[...]
        q_scale = ATTN_HEAD_DIM ** -0.5 * LOG2_E
        dilations = [d for _, d in ATTN_GROUPS]

        def rope_piece(ref, col, scale, dilation):
            def run():
                t = proj(col, ATTN_OUT_WIDTH)
                for s in range(0, ATTN_OUT_WIDTH, LANES):
                    rot = _rotate_half_pairs(t[:, s:s + LANES], cos, sin) * scale
                    _store_residue_major(ref, perm_ref, rot, s, dilation)
            return run

        def value_piece(ref, col, dilation):
            def run():
                t = proj(col, ATTN_OUT_WIDTH)
                for s in range(0, ATTN_OUT_WIDTH, LANES):
                    _store_residue_major(ref, perm_ref, t[:, s:s + LANES], s, dilation)
            return run

        def gate_piece(ref, col, s):
            def run():
                z = proj(col + s, 512)
                ref[:, s:s + 512] = (1.0 / (1.0 + jnp.exp(-z))).astype(ref.dtype)
            return run

        pieces = []
        col = 0
        for refs, scale in (((q1_ref, q2_ref, q3_ref), q_scale), ((k1_ref, k2_ref, k3_ref), 1.0)):
            for g, ref in enumerate(refs):
                pieces.append(rope_piece(ref, col, scale, dilations[g]))
                col += ATTN_OUT_WIDTH
        for g, ref in enumerate((v1_ref, v2_ref, v3_ref)):
            pieces.append(value_piece(ref, col, dilations[g]))
            col += ATTN_OUT_WIDTH
        for ref in (ma_ref, mb_ref):
            for s in range(0, D_MODEL, 512):
                pieces.append(gate_piece(ref, col, s))
            col += D_MODEL

        for n in range(max(len(pieces), len(chunks))):
            if n < len(pieces):
                pieces[n]()
            if n < len(chunks):
                chunks[n]()

        col = 0
        for ref, width in ((gq_s, GLA_QK_WIDTH), (gk_s, GLA_QK_WIDTH)):
            ref[...] = rows_proj(col, width).astype(ref.dtype)
            col += width
        for ref in (gv_s, gr_s):
            for s in range(0, GLA_V_WIDTH, 512):
                ref[:, s:s + 512] = rows_proj(col + s, 512).astype(ref.dtype)
            col += GLA_V_WIDTH
        g_lr = rows_proj(IN_ALIGNED, LANES)
        g_hi, g_lo = _split_hi_lo(g_lr)
        w2_hi, w2_lo = _split_hi_lo(w2_ref[...])
        z = (jnp.dot(g_hi, w2_hi, preferred_element_type=F32) + jnp.dot(g_lo, w2_hi, preferred_element_type=F32)
             + jnp.dot(g_hi, w2_lo, preferred_element_type=F32)) + b2_ref[...]
        log_sig = jnp.minimum(z, 0.0) - jnp.log(1.0 + jnp.exp(-jnp.abs(z)))
        la_s[...] = log_sig * (LOG2_E / GLA_TAU)


def _gla_chunks(q_ref, k_ref, v_ref, la_ref, gr_ref, nw_ref, o_ref, state_ref, first_of_seq):
    C = GLA_CHUNK
    H = C // 2
    heads = range(GLA_HEADS)
    rows = lax.broadcasted_iota(jnp.int32, (C, C), 0)
    cols = lax.broadcasted_iota(jnp.int32, (C, C), 1)
    causal = cols <= rows
    tri = causal.astype(BF16)
    same_half = (rows >= H) == (cols >= H)
    upper_rows = lax.broadcasted_iota(jnp.int32, (C, GLA_QK_WIDTH), 0) < H
    q_scale = GLA_DK ** -0.5
    nw = jnp.concatenate([nw_ref[...]] * GLA_HEADS, axis=1)
    nt = (((1,), (1,)), ((), ()))
    ks = lambda a, h: a[:, h * GLA_DK:(h + 1) * GLA_DK]
    vs = lambda a, h: a[:, h * GLA_DV:(h + 1) * GLA_DV]

    def run_chunk(c):
        sl = pl.ds(c * C, C)
        q = q_ref[sl, :].astype(F32) * q_scale
        k = k_ref[sl, :].astype(F32)
        v = v_ref[sl, :]
        la_hi, la_lo = _split_hi_lo(la_ref[sl, :])
        cum = jnp.dot(tri, la_hi, preferred_element_type=F32) + jnp.dot(tri, la_lo, preferred_element_type=F32)
        mid = jnp.where(upper_rows, cum[H // 2 - 1:H // 2, :], cum[H + H // 2 - 1:H + H // 2, :])
        edge = cum[H - 1:H, :]
        last = cum[C - 1:C, :]
        qg = (q * jnp.exp2(cum - mid)).astype(BF16)
        kg = (k * jnp.exp2(mid - cum)).astype(BF16)
        zeros_half = jnp.zeros((H, GLA_QK_WIDTH), BF16)
        q_lo = jnp.concatenate([zeros_half, (q[H:] * jnp.exp2(cum[H:] - edge)).astype(BF16)], axis=0)
        k_up = jnp.concatenate([(k[:H] * jnp.exp2(edge - cum[:H])).astype(BF16), zeros_half], axis=0)
        q_in = (q * jnp.exp2(cum)).astype(BF16)
        k_out = k * jnp.exp2(last - cum)
        decay_row = jnp.broadcast_to(jnp.exp2(last), (8, GLA_QK_WIDTH))
        g = gr_ref[sl, :].astype(F32)
        gate = nw * (g * (1.0 / (1.0 + jnp.exp(-g))))

        within = [lax.dot_general(ks(qg, h), ks(kg, h), nt, preferred_element_type=F32) for h in heads]
        across = [lax.dot_general(ks(q_lo, h), ks(k_up, h), nt, preferred_element_type=F32) for h in heads]
        attn = [jnp.where(same_half, jnp.where(causal, within[h], 0.0), across[h]).astype(BF16) for h in heads]
        if c == 0:
            state = [jnp.where(first_of_seq, 0.0, state_ref[h]) for h in heads]
        else:
            state = [state_ref[h] for h in heads]
        o = [jnp.dot(attn[h], vs(v, h), preferred_element_type=F32)
             + jnp.dot(ks(q_in, h), state[h].astype(BF16), preferred_element_type=F32) for h in heads]
        k_out_t = [jnp.transpose(ks(k_out, h)).astype(BF16) for h in heads]
        decay = [jnp.transpose(ks(decay_row, h))[:, 0:1] for h in heads]
        for h in heads:
            state_ref[h] = decay[h] * state[h] + jnp.dot(k_out_t[h], vs(v, h), preferred_element_type=F32)
        ms = [jnp.mean(o[h] * o[h], axis=-1, keepdims=True) for h in heads]
        for h in heads:
            o_ref[sl, h * GLA_DV:(h + 1) * GLA_DV] = (o[h] * lax.rsqrt(ms[h] + EPS) * vs(gate, h)).astype(o_ref.dtype)

    return [functools.partial(run_chunk, c) for c in range(q_ref.shape[0] // C)]


def _in_proj_gla(x2, n1w, mod, cos, sin, w_in_t, w2, b2, gla_norm_w, B):
    T, D = x2.shape
    tm = TM_IN
    n_tiles = T // tm
    cur = lambda i: (jnp.minimum(i, n_tiles - 1), 0)
    lag = lambda i: (jnp.maximum(i - 1, 0), 0)
    const = lambda i: (0, 0)
    widths = (ATTN_OUT_WIDTH,) * 9 + (D_MODEL, D_MODEL)
    single = dict(pipeline_mode=pl.Buffered(1))
    w2p = jnp.pad(w2, ((0, LANES - GLA_LOWRANK), (0, 0)))
    outs = pl.pallas_call(
        functools.partial(_in_proj_gla_kernel, tiles_per_seq=T // B // tm, n_tiles=n_tiles),
        grid=(n_tiles + 1,),
        in_specs=[pl.BlockSpec((tm, D), cur),
                  pl.BlockSpec((1, D), const),
                  _mod_spec(MOD_SCALE1, D),
                  _mod_spec(MOD_SHIFT1, D),
                  pl.BlockSpec((tm, LANES), cur),
                  pl.BlockSpec((tm, LANES), cur),
                  pl.BlockSpec(w_in_t.shape, const, **single),
                  pl.BlockSpec(w2p.shape, const, **single),
                  pl.BlockSpec(b2.shape, const, **single),
                  pl.BlockSpec(gla_norm_w.shape, const, **single)],
        out_specs=[pl.BlockSpec((tm, GLA_V_WIDTH), lag)] + [pl.BlockSpec((tm, w), cur) for w in widths],
        out_shape=[jax.ShapeDtypeStruct((T, GLA_V_WIDTH), BF16)]
                  + [jax.ShapeDtypeStruct((T, w), BF16) for w in widths],
        scratch_shapes=[pltpu.VMEM((tm, LANES), F32),
                        pltpu.VMEM((tm, GLA_QK_WIDTH), BF16),
                        pltpu.VMEM((tm, GLA_QK_WIDTH), BF16),
                        pltpu.VMEM((tm, GLA_V_WIDTH), BF16),
                        pltpu.VMEM((tm, GLA_V_WIDTH), BF16),
                        pltpu.VMEM((tm, GLA_QK_WIDTH), F32),
                        pltpu.VMEM((GLA_HEADS, GLA_DK, GLA_DV), F32)],
        compiler_params=_params("arbitrary"),
        name="in_proj_gla",
    )(x2, n1w, mod, mod, cos, sin, w_in_t, w2p, b2, gla_norm_w)
    return outs[0], outs[1:]


def _attn_kernel(q1_ref, q2_ref, q3_ref, k1c_ref, k1p_ref, k2c_ref, k2p_ref, k3c_ref, k3p_ref,
                 v1c_ref, v1p_ref, v2c_ref, v2p_ref, v3c_ref, v3p_ref, *refs):
    n_cast = ATTN_N_CAST
    o_ref = refs[n_cast]
    slabs = refs[2 * n_cast + 1:]
    for src, dst in zip(refs[:n_cast], refs[n_cast + 1:2 * n_cast + 1]):
        dst[...] = src[...].astype(dst.dtype)

    blk = ATTN_BLK
    tile = TM_IN
    span = ATTN_SPAN
    tiles = span // tile
    r2, r3 = ATTN_GROUPS[1][1], ATTN_GROUPS[2][1]
    per3 = tile // r3
    not_first_span = pl.program_id(1) > 0
    npair = ATTN_OUT_WIDTH // LANES
    o1_s, o2_s, o3_s, l1_s, l2_s, l3_s = [slabs[n * npair:(n + 1) * npair] for n in range(6)]

    rows = lax.broadcasted_iota(jnp.int32, (blk, 2 * blk), 0)
    cols = lax.broadcasted_iota(jnp.int32, (blk, 2 * blk), 1)
    band = (cols >= rows) & (cols <= rows + blk)
    in_cur = cols >= blk
    left = lax.broadcasted_iota(jnp.int32, (blk, LANES), 1) < ATTN_HEAD_DIM
    ones = jnp.ones((2 * blk, LANES), BF16)
    nt = (((1,), (1,)), ((), ()))

    def pair_attend(q_pair, k_cat, v_cat, has_prev):
        zero = jnp.zeros_like(q_pair)
        q2 = jnp.concatenate([jnp.where(left, q_pair, zero), jnp.where(left, zero, q_pair)], axis=0)
        s = lax.dot_general(q2, k_cat, nt, preferred_element_type=F32)
        valid = band & (in_cur | has_prev)
        s = jnp.where(jnp.concatenate([valid, valid], axis=0), s, MASK_VALUE)
        m = jnp.max(s, axis=-1, keepdims=True)
        p = jnp.exp2(s - m).astype(BF16)
        r = jnp.dot(p, jnp.concatenate([v_cat, ones], axis=1), preferred_element_type=F32)
        acc = jnp.where(left, r[:blk, :LANES], r[blk:, :LANES])
        den = jnp.where(left, r[:blk, LANES:], r[blk:, LANES:])
        m_pair = jnp.where(left, jnp.broadcast_to(m[:blk], (blk, LANES)), jnp.broadcast_to(m[blk:], (blk, LANES)))
        return acc / den, m_pair * LN_2 + jnp.log(den)

    def with_prev(cur_ref, prev_ref, r0, back, prev_rows, first, cs):
        before = cur_ref[pl.ds(pl.multiple_of(jnp.maximum(r0 - back, 0), blk), blk), cs]
        if prev_rows is not None:
            before = jnp.where(first, prev_ref[prev_rows, cs], before)
        return jnp.concatenate([before, cur_ref[pl.ds(r0, blk), cs]], axis=0)

    def tile_body(t, carry):
        for p in range(tiles):
            i = t * tiles + p
            r0 = pl.multiple_of(i * blk, blk)
            for j in range(npair):
                cs = slice(j * LANES, (j + 1) * LANES)
                rows1 = slice(0, blk) if p == 0 else None
                o, lse = pair_attend(q1_ref[pl.ds(r0, blk), cs],
                                     with_prev(k1c_ref, k1p_ref, r0, blk, rows1, t == 0, cs),
                                     with_prev(v1c_ref, v1p_ref, r0, blk, rows1, t == 0, cs),
                                     not_first_span | (i > 0))
                o1_s[j][pl.ds(r0, blk), :] = o
                l1_s[j][pl.ds(r0, blk), :] = lse
                rows2 = slice(p * blk, (p + 1) * blk)
                o, lse = pair_attend(q2_ref[pl.ds(r0, blk), cs],
                                     with_prev(k2c_ref, k2p_ref, r0, tile, rows2, t == 0, cs),
                                     with_prev(v2c_ref, v2p_ref, r0, tile, rows2, t == 0, cs),
                                     not_first_span | (t > 0))
                tok = pl.ds(pl.multiple_of(t * tile, tile) + p, blk, stride=r2)
                o2_s[j][tok, :] = o
                l2_s[j][tok, :] = lse
                rr = pl.multiple_of(i * per3, per3)
                gather = lambda ref: jnp.concatenate(
                    [ref[pl.ds(rr + u * tile, per3), cs] for u in range(tiles)], axis=0)
                k_cat = jnp.concatenate([gather(k3p_ref), gather(k3c_ref)], axis=0)
                v_cat = jnp.concatenate([gather(v3p_ref), gather(v3c_ref)], axis=0)
                o, lse = pair_attend(gather(q3_ref), k_cat, v_cat, not_first_span)
                tok = pl.ds(i, blk, stride=r3)
                o3_s[j][tok, :] = o
                l3_s[j][tok, :] = lse
        return carry

    lax.fori_loop(0, tiles, tile_body, 0)

    for j in range(npair):
        for n in range(span // ATTN_MERGE_ROWS):
            rs = slice(n * ATTN_MERGE_ROWS, (n + 1) * ATTN_MERGE_ROWS)
            l1, l2, l3 = l1_s[j][rs, :], l2_s[j][rs, :], l3_s[j][rs, :]
            m = jnp.maximum(jnp.maximum(l1, l2), l3)
            e1, e2, e3 = jnp.exp(l1 - m), jnp.exp(l2 - m), jnp.exp(l3 - m)
            num = e1 * o1_s[j][rs, :] + e2 * o2_s[j][rs, :] + e3 * o3_s[j][rs, :]
            o_ref[rs, j * LANES:(j + 1) * LANES] = (num / (e1 + e2 + e3)).astype(o_ref.dtype)


def _dilated_attention(qs, ks, vs, weights, B, S):
    assert len(weights) == ATTN_N_CAST
    span = ATTN_SPAN
    nsp = S // span
    gw = ATTN_OUT_WIDTH
    T = B * S
    cur = lambda b, s: (b * nsp + s, 0)
    steps = B * nsp
    slab = lambda w: pl.BlockSpec((w.shape[0] // steps, w.shape[1]), cur)

    def prev(rows):
        per = span // rows
        return pl.BlockSpec((rows, gw), lambda b, s: (jnp.maximum((b * nsp + s) * per - 1, 0), 0))

    halos = (ATTN_BLK, TM_IN, span)
    full = pl.BlockSpec((span, gw), cur)
    kv_specs = []
    for h in halos:
        kv_specs += [full, prev(h)]
    kv_args = lambda arrs: [a for arr in arrs for a in (arr, arr)]
    outs = pl.pallas_call(
        _attn_kernel,
        grid=(B, nsp),
        in_specs=[full] * 3 + kv_specs + kv_specs + [slab(w) for w in weights],
        out_specs=[full] + [slab(w) for w in weights],
        out_shape=[jax.ShapeDtypeStruct((T, gw), BF16)] + [jax.ShapeDtypeStruct(w.shape, BF16) for w in weights],
        scratch_shapes=[pltpu.VMEM((span, LANES), F32)] * (6 * (gw // LANES)),
        compiler_params=_params("parallel", "parallel"),
        name="dilated_attn",
    )(*qs, *kv_args(ks), *kv_args(vs), *weights)
    return outs[0], outs[1:]


def _merge_kernel(x_ref, og_ref, oa_ref, ma_ref, mb_ref,
                  gate_ref, scale_ref, shift_ref, n2w_ref, wg_ref, wa_ref, wo_ref, x1_ref, h2_ref, *, tiles_per_seq):
    b = pl.ds(pl.program_id(0) // tiles_per_seq, 1)
    gate, scale, shift = gate_ref[b, :], scale_ref[b, :], shift_ref[b, :]
    sub = x_ref.shape[0] // MERGE_SPLIT
    for r in range(MERGE_SPLIT):
        rs = slice(r * sub, (r + 1) * sub)
        y_attn = jnp.dot(oa_ref[rs, :], wa_ref[...], preferred_element_type=F32)
        y_gla = jnp.dot(og_ref[rs, :], wg_ref[...], preferred_element_type=F32)
        mixed = (ma_ref[rs, :].astype(F32) * y_gla + mb_ref[rs, :].astype(F32) * y_attn).astype(BF16)
        x1 = x_ref[rs, :] + gate * jnp.dot(mixed, wo_ref[...], preferred_element_type=F32)
        x1_ref[rs, :] = x1
        h2_ref[rs, :] = _rmsnorm_mod(x1, n2w_ref[...], scale, shift).astype(h2_ref.dtype)


def _merge(x2, og, oa, ma, mb, mod, n2w, wg, wa, wo, B):
    T, D = x2.shape
    tm = TM_MERGE
    row = lambda i: (i, 0)
    const = lambda i: (0, 0)
    return pl.pallas_call(
        functools.partial(_merge_kernel, tiles_per_seq=T // B // tm),
        grid=(T // tm,),
        in_specs=[pl.BlockSpec((tm, D), row), pl.BlockSpec((tm, GLA_V_WIDTH), row),
                  pl.BlockSpec((tm, ATTN_OUT_WIDTH), row)]
                 + [pl.BlockSpec((tm, D), row)] * 2
                 + [_mod_spec(MOD_GATE1, D), _mod_spec(MOD_SCALE2, D), _mod_spec(MOD_SHIFT2, D)]
                 + [pl.BlockSpec((1, D), const),
                    pl.BlockSpec(wg.shape, const), pl.BlockSpec(wa.shape, const), pl.BlockSpec(wo.shape, const)],
        out_specs=[pl.BlockSpec((tm, D), row)] * 2,
        out_shape=[jax.ShapeDtypeStruct((T, D), F32), jax.ShapeDtypeStruct((T, D), BF16)],
        compiler_params=_params("parallel"),
        name="merge",
    )(x2, og, oa, ma, mb, mod, mod, mod, n2w, wg, wa, wo)


def _ffn_kernel(h_ref, hprev_ref, x1_ref, wup_ref, cw_ref, cb_ref, wd_ref, gate_ref, fw_ref, o_ref,
                hcat_s, u_s, hid_s, order_s, *, tiles_per_seq):
    i = pl.program_id(0)
    tm = h_ref.shape[0]
    halo = hprev_ref.shape[0]
    tf = FFN_CHUNK
    half = tm // 2
    hcat_s[0:halo] = jnp.where(i % tiles_per_seq == 0, jnp.zeros_like(hprev_ref[...]), hprev_ref[...])
    hcat_s[halo:] = h_ref[...]
    nchunk = D_FF // tf
    branches = (0, D_FF)

    def up_project(j):
        for n, off in enumerate(branches):
            u = jnp.dot(hcat_s[...], wup_ref[:, off + j * tf:off + (j + 1) * tf], preferred_element_type=F32)
            for c in range(tf // LANES):
                u_s[j % 2, n, c] = u[:, c * LANES:(c + 1) * LANES]

    def conv(j, n, c, parity):
        u = u_s.at[j % 2, n, c]
        cs = slice(branches[n] + j * tf + c * LANES, branches[n] + j * tf + (c + 1) * LANES)
        tap = lambda back: u[pl.ds(halo + parity - back, half, stride=2), :]
        k = 2.0 ** -0.5
        return (cb_ref[:, cs] * k + (cw_ref[0:1, cs] * k) * tap(2) + (cw_ref[1:2, cs] * k) * tap(1)
                + (cw_ref[2:3, cs] * k) * tap(0))

    up_project(0)
    for j in range(nchunk):
        if j + 1 < nchunk:
            up_project(j + 1)
        for c in range(tf // LANES):
            for parity in range(2):
                t = conv(j, 1, c, parity)
                hidden = (t * (1.0 + lax.erf(t))) * conv(j, 0, c, parity)
                hid_s[parity * half:(parity + 1) * half, j * tf + c * LANES:j * tf + (c + 1) * LANES] = (
                    hidden.astype(hid_s.dtype))

    gate = gate_ref[pl.ds(i // tiles_per_seq, 1), :]
    d_model = o_ref.shape[1]
    sq = jnp.zeros((tm, 1), F32)
    for n in range(d_model // FFN_DOWN_SLAB):
        ns = slice(n * FFN_DOWN_SLAB, (n + 1) * FFN_DOWN_SLAB)
        down = jnp.dot(hid_s[...], wd_ref[:, ns], preferred_element_type=F32)
        for c in range(FFN_DOWN_SLAB // LANES):
            g = n * (FFN_DOWN_SLAB // LANES) + c
            cs = slice(g * LANES, (g + 1) * LANES)
            for parity in range(2):
                order_s[g, pl.ds(parity, half, stride=2), :] = (
                    down[parity * half:(parity + 1) * half, c * LANES:(c + 1) * LANES])
            x2 = x1_ref[:, cs] + gate[:, cs] * order_s[g]
            sq = sq + jnp.sum(x2 * x2, axis=-1, keepdims=True)
            o_ref[:, cs] = x2
    o_ref[...] = o_ref[...] * lax.rsqrt(sq * (1.0 / d_model) + EPS) * fw_ref[...]


def _ffn(h2, x1, w_up, conv_w, conv_b, w_down, mod, final_w, B):
    T, D = x1.shape
    tm, tf, halo = TM_FFN, FFN_CHUNK, FFN_HALO
    row = lambda i: (i, 0)
    const = lambda i: (0, 0)
    single = dict(pipeline_mode=pl.Buffered(1))
    return pl.pallas_call(
        functools.partial(_ffn_kernel, tiles_per_seq=T // B // tm),
        grid=(T // tm,),
        in_specs=[pl.BlockSpec((tm, D), row),
                  pl.BlockSpec((halo, D), lambda i: (jnp.maximum(i * (tm // halo) - 1, 0), 0)),
                  pl.BlockSpec((tm, D), row),
                  pl.BlockSpec(w_up.shape, const, **single),
                  pl.BlockSpec(conv_w.shape, const, **single),
                  pl.BlockSpec(conv_b.shape, const, **single),
                  pl.BlockSpec(w_down.shape, const, **single),
                  _mod_spec(MOD_GATE2, D),
                  pl.BlockSpec((1, D), const)],
        out_specs=pl.BlockSpec((tm, D), row),
        out_shape=jax.ShapeDtypeStruct((T, D), F32),
        scratch_shapes=[pltpu.VMEM((halo + tm, D), BF16),
                        pltpu.VMEM((2, 2, tf // LANES, halo + tm, LANES), F32),
                        pltpu.VMEM((tm, D_FF), BF16),
                        pltpu.VMEM((D // LANES, tm, LANES), F32)],
        compiler_params=_params("parallel"),
        name="ffn",
    )(h2, h2, x1, w_up, conv_w, conv_b, w_down, mod, final_w)


def kernel(x, c, positions, ada_w, ada_b, norm1_w, w_in, gla_gate_w2, gla_gate_b, gla_norm_w, w_gla_branch,
           w_attn_branch, w_out, norm2_w, w_up, conv_w, conv_b, w_down, final_norm_w):
    B, S, D = x.shape
    T = B * S
    depth = ada_w.shape[0]
    assert depth == 1, "the final norm is fused into the (single) layer's ffn"
    assert all(window // dilation == ATTN_BLK for window, dilation in ATTN_GROUPS)
    x2 = x.reshape(T, D)
    for layer in range(depth):
        cos, sin, mod, w_in_bf = _prep(positions, c, ada_w[layer], ada_b[layer], w_in[layer].T)

        og, (q1, q2, q3, k1, k2, k3, v1, v2, v3, ma, mb) = _in_proj_gla(
            x2, norm1_w[layer].reshape(1, D), mod, cos, sin, w_in_bf, gla_gate_w2[layer],
            gla_gate_b[layer].reshape(1, -1), gla_norm_w[layer].reshape(1, -1), B)
        oa, (w_up_bf, w_down_bf, w_gla_bf, w_attn_bf, w_out_bf) = _dilated_attention(
            (q1, q2, q3), (k1, k2, k3), (v1, v2, v3),
            (w_up[layer], w_down[layer], w_gla_branch[layer], w_attn_branch[layer], w_out[layer]), B, S)

        x1, h2 = _merge(x2, og, oa, ma, mb, mod, norm2_w[layer].reshape(1, D), w_gla_bf, w_attn_bf, w_out_bf, B)

        x2 = _ffn(h2, x1, w_up_bf, conv_w[layer], conv_b[layer].reshape(1, -1),
                  w_down_bf, mod, final_norm_w.reshape(1, D), B)
    return x2.reshape(B, S, D)
```

```python
import functools

import jax
import jax.numpy as jnp
from jax import lax
from jax.experimental import pallas as pl
from jax.experimental.pallas import tpu as pltpu

F32 = jnp.float32
BF16 = jnp.bfloat16

D_MODEL = 1024
GLA_HEADS = 4
GLA_DK = 128
GLA_DV = 256
GLA_LOWRANK = 16
GLA_TAU = 16.0
GLA_QK_WIDTH = GLA_HEADS * GLA_DK
GLA_V_WIDTH = GLA_HEADS * GLA_DV
ATTN_GROUPS = ((128, 1), (512, 4), (2048, 16))
ATTN_HEADS_PER_GROUP = 4
ATTN_HEAD_DIM = 64
ATTN_WIDTH = ATTN_HEADS_PER_GROUP * len(ATTN_GROUPS) * ATTN_HEAD_DIM
ATTN_OUT_WIDTH = ATTN_HEADS_PER_GROUP * ATTN_HEAD_DIM
ROPE_THETA = 10000.0
D_FF = 2816
CONV_WIDTH = 3
EPS = 1e-6
IN_WIDTHS = (GLA_QK_WIDTH, GLA_QK_WIDTH, GLA_V_WIDTH, GLA_V_WIDTH, GLA_LOWRANK,
             ATTN_WIDTH, ATTN_WIDTH, ATTN_WIDTH, D_MODEL, D_MODEL)

LANES = 128
VMEM_LIMIT_BYTES = 56 * 1024 * 1024

PREP_ROWS = 512
TM_IN = 512
GLA_CHUNK = 64
ATTN_BLK = 128
ATTN_SPAN = 2048
ATTN_MERGE_ROWS = 256
ATTN_N_CAST = 5
TM_MERGE = 1024
MERGE_SPLIT = 2
TM_FFN = 512
FFN_CHUNK = 256
FFN_HALO = 8
FFN_DOWN_SLAB = 256
MASK_VALUE = float("-inf")
LOG2_E = 1.4426950408889634
LN_2 = 0.6931471805599453


def _params(*sem):
    return pltpu.CompilerParams(dimension_semantics=sem, vmem_limit_bytes=VMEM_LIMIT_BYTES)


def _split_hi_lo(a):
    hi = a.astype(BF16)
    lo = (a - hi.astype(F32)).astype(BF16)
    return hi, lo


def _mod_spec(which, D):
    return pl.BlockSpec((8, D), lambda i: (0, which))


MOD_SHIFT1, MOD_SCALE1, MOD_GATE1, MOD_SHIFT2, MOD_SCALE2, MOD_GATE2 = range(6)


def _prep_kernel(pos_ref, invf_ref, c_ref, adaw_ref, adab_ref, wt_ref, cos_ref, sin_ref, mod_ref, wt_bf_ref):
    c = c_ref[...]
    s = c * (1.0 / (1.0 + jnp.exp(-c)))
    s_hi, s_lo = _split_hi_lo(s)
    lhs = jnp.concatenate([s_hi, s_lo], axis=0)
    w_hi, w_lo = _split_hi_lo(adaw_ref[...])
    acc = jnp.dot(lhs, w_hi, preferred_element_type=F32) + jnp.dot(lhs, w_lo, preferred_element_type=F32)
    mod_ref[...] = acc[0:8] + acc[8:16] + adab_ref[...]

    wt_bf_ref[...] = wt_ref[...].astype(wt_bf_ref.dtype)

    half = ATTN_HEAD_DIM // 2
    groups = LANES // half
    tr = pos_ref.shape[1]
    pos = jnp.concatenate([pos_ref[...].astype(F32), jnp.zeros((8 - groups, tr), F32)], axis=0)
    pos_t = jnp.transpose(pos)
    lane = lax.broadcasted_iota(jnp.int32, (tr, LANES), 1)
    group = lane // half
    pos_dense = jnp.zeros((tr, LANES), F32)
    for q in range(groups):
        pos_dense = jnp.where(group == q, pos_t[:, q:q + 1], pos_dense)
    ang = pos_dense * invf_ref[...]
    first_half = (lane % ATTN_HEAD_DIM) < half
    for table, out_ref, signed in ((jnp.cos(ang), cos_ref, False), (jnp.sin(ang), sin_ref, True)):
        for q in range(groups):
            spread = jnp.where(group == q, table, 0.0)
            shift = groups // 2
            while shift >= 1:
                spread = spread + pltpu.roll(spread, shift * half, axis=1)
                shift //= 2
            out_ref[q] = jnp.where(first_half, -spread, spread) if signed else spread


def _prep(positions, c, ada_w, ada_b, w_in_t):
    T = positions.size
    B, D = c.shape
    N = ada_w.shape[1]
    half = ATTN_HEAD_DIM // 2
    groups = LANES // half
    inv_freq = ROPE_THETA ** (-jnp.arange(half, dtype=F32) / half)
    invf = jnp.tile(inv_freq, groups).reshape(1, LANES)
    per = T // groups
    tr = PREP_ROWS
    steps = per // tr
    tn = N // steps
    bf16_rows = 16
    wt_slab = pl.BlockSpec((-(-w_in_t.shape[0] // (steps * bf16_rows)) * bf16_rows, w_in_t.shape[1]), lambda i: (i, 0))
    c8 = jnp.pad(c, ((0, 8 - B), (0, 0)))
    cos, sin, mod, wt_bf = pl.pallas_call(
        _prep_kernel,
        grid=(steps,),
        in_specs=[pl.BlockSpec((groups, tr), lambda i: (0, i)),
                  pl.BlockSpec((1, LANES), lambda i: (0, 0)),
                  pl.BlockSpec((8, D), lambda i: (0, 0)),
                  pl.BlockSpec((D, tn), lambda i: (0, i)),
                  pl.BlockSpec((1, tn), lambda i: (0, i)),
                  wt_slab],
        out_specs=[pl.BlockSpec((groups, tr, LANES), lambda i: (0, i, 0))] * 2
                  + [pl.BlockSpec((8, tn), lambda i: (0, i)), wt_slab],
        out_shape=[jax.ShapeDtypeStruct((groups, per, LANES), F32)] * 2
                  + [jax.ShapeDtypeStruct((8, N), F32), jax.ShapeDtypeStruct(w_in_t.shape, BF16)],
        compiler_params=_params("parallel"),
        name="prep",
    )(positions.reshape(groups, per), invf, c8, ada_w, ada_b.reshape(1, N), w_in_t)
    return cos.reshape(T, LANES), sin.reshape(T, LANES), mod, wt_bf


def _rmsnorm_mod(x, w, scale, shift):
    ms = jnp.mean(x * x, axis=-1, keepdims=True)
    return (x * lax.rsqrt(ms + EPS) * w) * (1.0 + scale) + shift


def _rotate_half_pairs(t, cos, sin_signed):
    lane = lax.broadcasted_iota(jnp.int32, t.shape, 1)
    first_half = (lane % ATTN_HEAD_DIM) < (ATTN_HEAD_DIM // 2)
    from_right = pltpu.roll(t, LANES - ATTN_HEAD_DIM // 2, axis=1)
    from_left = pltpu.roll(t, ATTN_HEAD_DIM // 2, axis=1)
    return t * cos + jnp.where(first_half, from_right, from_left) * sin_signed


def _store_residue_major(ref, perm_ref, slab, col, dilation):
    if dilation == 1:
        ref[:, col:col + LANES] = slab.astype(ref.dtype)
        return
    perm_ref[...] = slab
    n = slab.shape[0] // dilation
    for p in range(dilation):
        ref[p * n:(p + 1) * n, col:col + LANES] = perm_ref[pl.ds(p, n, stride=dilation), :].astype(ref.dtype)


IN_ALIGNED = sum(IN_WIDTHS[:4])
IN_SHIFTED_START = IN_ALIGNED + GLA_LOWRANK


def _in_proj_gla_kernel(x_ref, n1w_ref, scale_ref, shift_ref, cos_ref, sin_ref, wt_ref, w2_ref, b2_ref, nw_ref,
                        og_ref, q1_ref, q2_ref, q3_ref, k1_ref, k2_ref, k3_ref, v1_ref, v2_ref, v3_ref,
                        ma_ref, mb_ref,
                        perm_ref, gq_s, gk_s, gv_s, gr_s, la_s, state_ref,
                        *, tiles_per_seq, n_tiles):
    i = pl.program_id(0)
    nt = (((1,), (1,)), ((), ()))

    @pl.when(i == 0)
    def _():
        for ref in (gq_s, gk_s, gv_s, gr_s, la_s, state_ref):
            ref[...] = jnp.zeros_like(ref)

    first_of_seq = (i + tiles_per_seq - 1) % tiles_per_seq == 0
    chunks = _gla_chunks(gq_s, gk_s, gv_s, la_s, gr_s, nw_ref, og_ref, state_ref, first_of_seq)

    @pl.when(i == n_tiles)
    def _():
        for chunk in chunks:
            chunk()

    @pl.when(i < n_tiles)
    def _():
        b = i // tiles_per_seq
        h = _rmsnorm_mod(x_ref[...], n1w_ref[...], scale_ref[pl.ds(b, 1), :], shift_ref[pl.ds(b, 1), :]).astype(BF16)

        def rows_proj(row, width):
            return lax.dot_general(h, wt_ref[row:row + width, :], nt, preferred_element_type=F32)

        def proj(col, width):
            return rows_proj(IN_SHIFTED_START + col, width)

        cos = cos_ref[...]
        sin = sin_ref[...]
        q_scale = ATTN_HEAD_DIM ** -0.5 * LOG2_E
        dilations = [d for _, d in ATTN_GROUPS]

        def rope_piece(ref, col, scale, dilation):
            def run():
                t = proj(col, ATTN_OUT_WIDTH)
                for s in range(0, ATTN_OUT_WIDTH, LANES):
                    rot = _rotate_half_pairs(t[:, s:s + LANES], cos, sin) * scale
                    _store_residue_major(ref, perm_ref, rot, s, dilation)
            return run

        def value_piece(ref, col, dilation):
            def run():
                t = proj(col, ATTN_OUT_WIDTH)
                for s in range(0, ATTN_OUT_WIDTH, LANES):
                    _store_residue_major(ref, perm_ref, t[:, s:s + LANES], s, dilation)
            return run

        def gate_piece(ref, col, s):
            def run():
                z = proj(col + s, 512)
                ref[:, s:s + 512] = (1.0 / (1.0 + jnp.exp(-z))).astype(ref.dtype)
            return run

        pieces = []
        col = 0
        for refs, scale in (((q1_ref, q2_ref, q3_ref), q_scale), ((k1_ref, k2_ref, k3_ref), 1.0)):
            for g, ref in enumerate(refs):
                pieces.append(rope_piece(ref, col, scale, dilations[g]))
                col += ATTN_OUT_WIDTH
        for g, ref in enumerate((v1_ref, v2_ref, v3_ref)):
            pieces.append(value_piece(ref, col, dilations[g]))
            col += ATTN_OUT_WIDTH
        for ref in (ma_ref, mb_ref):
            for s in range(0, D_MODEL, 512):
                pieces.append(gate_piece(ref, col, s))
            col += D_MODEL

        for n in range(max(len(pieces), len(chunks))):
            if n < len(pieces):
                pieces[n]()
            if n < len(chunks):
                chunks[n]()

        col = 0
        for ref, width in ((gq_s, GLA_QK_WIDTH), (gk_s, GLA_QK_WIDTH)):
            ref[...] = rows_proj(col, width).astype(ref.dtype)
            col += width
        for ref in (gv_s, gr_s):
            for s in range(0, GLA_V_WIDTH, 512):
                ref[:, s:s + 512] = rows_proj(col + s, 512).astype(ref.dtype)
            col += GLA_V_WIDTH
        g_lr = rows_proj(IN_ALIGNED, LANES)
        g_hi, g_lo = _split_hi_lo(g_lr)
        w2_hi, w2_lo = _split_hi_lo(w2_ref[...])
        z = (jnp.dot(g_hi, w2_hi, preferred_element_type=F32) + jnp.dot(g_lo, w2_hi, preferred_element_type=F32)
             + jnp.dot(g_hi, w2_lo, preferred_element_type=F32)) + b2_ref[...]
        log_sig = jnp.minimum(z, 0.0) - jnp.log(1.0 + jnp.exp(-jnp.abs(z)))
        la_s[...] = log_sig * (LOG2_E / GLA_TAU)


def _gla_chunks(q_ref, k_ref, v_ref, la_ref, gr_ref, nw_ref, o_ref, state_ref, first_of_seq):
    C = GLA_CHUNK
    H = C // 2
    heads = range(GLA_HEADS)
    rows = lax.broadcasted_iota(jnp.int32, (C, C), 0)
    cols = lax.broadcasted_iota(jnp.int32, (C, C), 1)
    causal = cols <= rows
    tri = causal.astype(BF16)
    same_half = (rows >= H) == (cols >= H)
    upper_rows = lax.broadcasted_iota(jnp.int32, (C, GLA_QK_WIDTH), 0) < H
    q_scale = GLA_DK ** -0.5
    nw = jnp.concatenate([nw_ref[...]] * GLA_HEADS, axis=1)
    nt = (((1,), (1,)), ((), ()))
    ks = lambda a, h: a[:, h * GLA_DK:(h + 1) * GLA_DK]
    vs = lambda a, h: a[:, h * GLA_DV:(h + 1) * GLA_DV]

    def run_chunk(c):
        sl = pl.ds(c * C, C)
        q = q_ref[sl, :].astype(F32) * q_scale
        k = k_ref[sl, :].astype(F32)
        v = v_ref[sl, :]
        la_hi, la_lo = _split_hi_lo(la_ref[sl, :])
        cum = jnp.dot(tri, la_hi, preferred_element_type=F32) + jnp.dot(tri, la_lo, preferred_element_type=F32)
        mid = jnp.where(upper_rows, cum[H // 2 - 1:H // 2, :], cum[H + H // 2 - 1:H + H // 2, :])
        edge = cum[H - 1:H, :]
        last = cum[C - 1:C, :]
        qg = (q * jnp.exp2(cum - mid)).astype(BF16)
        kg = (k * jnp.exp2(mid - cum)).astype(BF16)
        zeros_half = jnp.zeros((H, GLA_QK_WIDTH), BF16)
        q_lo = jnp.concatenate([zeros_half, (q[H:] * jnp.exp2(cum[H:] - edge)).astype(BF16)], axis=0)
        k_up = jnp.concatenate([(k[:H] * jnp.exp2(edge - cum[:H])).astype(BF16), zeros_half], axis=0)
        q_in = (q * jnp.exp2(cum)).astype(BF16)
        k_out = k * jnp.exp2(last - cum)
        decay_row = jnp.broadcast_to(jnp.exp2(last), (8, GLA_QK_WIDTH))
        g = gr_ref[sl, :].astype(F32)
        gate = nw * (g * (1.0 / (1.0 + jnp.exp(-g))))

        within = [lax.dot_general(ks(qg, h), ks(kg, h), nt, preferred_element_type=F32) for h in heads]
        across = [lax.dot_general(ks(q_lo, h), ks(k_up, h), nt, preferred_element_type=F32) for h in heads]
        attn = [jnp.where(same_half, jnp.where(causal, within[h], 0.0), across[h]).astype(BF16) for h in heads]
        if c == 0:
            state = [jnp.where(first_of_seq, 0.0, state_ref[h]) for h in heads]
        else:
            state = [state_ref[h] for h in heads]
        o = [jnp.dot(attn[h], vs(v, h), preferred_element_type=F32)
             + jnp.dot(ks(q_in, h), state[h].astype(BF16), preferred_element_type=F32) for h in heads]
        k_out_t = [jnp.transpose(ks(k_out, h)).astype(BF16) for h in heads]
        decay = [jnp.transpose(ks(decay_row, h))[:, 0:1] for h in heads]
        for h in heads:
            state_ref[h] = decay[h] * state[h] + jnp.dot(k_out_t[h], vs(v, h), preferred_element_type=F32)
        ms = [jnp.mean(o[h] * o[h], axis=-1, keepdims=True) for h in heads]
        for h in heads:
            o_ref[sl, h * GLA_DV:(h + 1) * GLA_DV] = (o[h] * lax.rsqrt(ms[h] + EPS) * vs(gate, h)).astype(o_ref.dtype)

    return [functools.partial(run_chunk, c) for c in range(q_ref.shape[0] // C)]


def _in_proj_gla(x2, n1w, mod, cos, sin, w_in_t, w2, b2, gla_norm_w, B):
    T, D = x2.shape
    tm = TM_IN
    n_tiles = T // tm
    cur = lambda i: (jnp.minimum(i, n_tiles - 1), 0)
    lag = lambda i: (jnp.maximum(i - 1, 0), 0)
    const = lambda i: (0, 0)
    widths = (ATTN_OUT_WIDTH,) * 9 + (D_MODEL, D_MODEL)
    single = dict(pipeline_mode=pl.Buffered(1))
    w2p = jnp.pad(w2, ((0, LANES - GLA_LOWRANK), (0, 0)))
    outs = pl.pallas_call(
        functools.partial(_in_proj_gla_kernel, tiles_per_seq=T // B // tm, n_tiles=n_tiles),
        grid=(n_tiles + 1,),
        in_specs=[pl.BlockSpec((tm, D), cur),
                  pl.BlockSpec((1, D), const),
                  _mod_spec(MOD_SCALE1, D),
                  _mod_spec(MOD_SHIFT1, D),
                  pl.BlockSpec((tm, LANES), cur),
                  pl.BlockSpec((tm, LANES), cur),
                  pl.BlockSpec(w_in_t.shape, const, **single),
                  pl.BlockSpec(w2p.shape, const, **single),
                  pl.BlockSpec(b2.shape, const, **single),
                  pl.BlockSpec(gla_norm_w.shape, const, **single)],
        out_specs=[pl.BlockSpec((tm, GLA_V_WIDTH), lag)] + [pl.BlockSpec((tm, w), cur) for w in widths],
        out_shape=[jax.ShapeDtypeStruct((T, GLA_V_WIDTH), BF16)]
                  + [jax.ShapeDtypeStruct((T, w), BF16) for w in widths],
        scratch_shapes=[pltpu.VMEM((tm, LANES), F32),
                        pltpu.VMEM((tm, GLA_QK_WIDTH), BF16),
                        pltpu.VMEM((tm, GLA_QK_WIDTH), BF16),
                        pltpu.VMEM((tm, GLA_V_WIDTH), BF16),
                        pltpu.VMEM((tm, GLA_V_WIDTH), BF16),
                        pltpu.VMEM((tm, GLA_QK_WIDTH), F32),
                        pltpu.VMEM((GLA_HEADS, GLA_DK, GLA_DV), F32)],
        compiler_params=_params("arbitrary"),
        name="in_proj_gla",
    )(x2, n1w, mod, mod, cos, sin, w_in_t, w2p, b2, gla_norm_w)
    return outs[0], outs[1:]


def _attn_kernel(q1_ref, q2_ref, q3_ref, k1c_ref, k1p_ref, k2c_ref, k2p_ref, k3c_ref, k3p_ref,
                 v1c_ref, v1p_ref, v2c_ref, v2p_ref, v3c_ref, v3p_ref, *refs):
    n_cast = ATTN_N_CAST
    o_ref = refs[n_cast]
    slabs = refs[2 * n_cast + 1:]
    for src, dst in zip(refs[:n_cast], refs[n_cast + 1:2 * n_cast + 1]):
        dst[...] = src[...].astype(dst.dtype)

    blk = ATTN_BLK
    tile = TM_IN
    span = ATTN_SPAN
    tiles = span // tile
    r2, r3 = ATTN_GROUPS[1][1], ATTN_GROUPS[2][1]
    per3 = tile // r3
    not_first_span = pl.program_id(1) > 0
    npair = ATTN_OUT_WIDTH // LANES
    o1_s, o2_s, o3_s, l1_s, l2_s, l3_s = [slabs[n * npair:(n + 1) * npair] for n in range(6)]

    rows = lax.broadcasted_iota(jnp.int32, (blk, 2 * blk), 0)
    cols = lax.broadcasted_iota(jnp.int32, (blk, 2 * blk), 1)
    band = (cols >= rows) & (cols <= rows + blk)
    in_cur = cols >= blk
    left = lax.broadcasted_iota(jnp.int32, (blk, LANES), 1) < ATTN_HEAD_DIM
    ones = jnp.ones((2 * blk, LANES), BF16)
    nt = (((1,), (1,)), ((), ()))

    def pair_attend(q_pair, k_cat, v_cat, has_prev):
        zero = jnp.zeros_like(q_pair)
        q2 = jnp.concatenate([jnp.where(left, q_pair, zero), jnp.where(left, zero, q_pair)], axis=0)
        s = lax.dot_general(q2, k_cat, nt, preferred_element_type=F32)
        valid = band & (in_cur | has_prev)
        s = jnp.where(jnp.concatenate([valid, valid], axis=0), s, MASK_VALUE)
        m = jnp.max(s, axis=-1, keepdims=True)
        p = jnp.exp2(s - m).astype(BF16)
        r = jnp.dot(p, jnp.concatenate([v_cat, ones], axis=1), preferred_element_type=F32)
        acc = jnp.where(left, r[:blk, :LANES], r[blk:, :LANES])
        den = jnp.where(left, r[:blk, LANES:], r[blk:, LANES:])
        m_pair = jnp.where(left, jnp.broadcast_to(m[:blk], (blk, LANES)), jnp.broadcast_to(m[blk:], (blk, LANES)))
        return acc / den, m_pair * LN_2 + jnp.log(den)

    def with_prev(cur_ref, prev_ref, r0, back, prev_rows, first, cs):
        before = cur_ref[pl.ds(pl.multiple_of(jnp.maximum(r0 - back, 0), blk), blk), cs]
        if prev_rows is not None:
            before = jnp.where(first, prev_ref[prev_rows, cs], before)
        return jnp.concatenate([before, cur_ref[pl.ds(r0, blk), cs]], axis=0)

    def tile_body(t, carry):
        for p in range(tiles):
            i = t * tiles + p
            r0 = pl.multiple_of(i * blk, blk)
            for j in range(npair):
                cs = slice(j * LANES, (j + 1) * LANES)
                rows1 = slice(0, blk) if p == 0 else None
                o, lse = pair_attend(q1_ref[pl.ds(r0, blk), cs],
                                     with_prev(k1c_ref, k1p_ref, r0, blk, rows1, t == 0, cs),
                                     with_prev(v1c_ref, v1p_ref, r0, blk, rows1, t == 0, cs),
                                     not_first_span | (i > 0))
                o1_s[j][pl.ds(r0, blk), :] = o
                l1_s[j][pl.ds(r0, blk), :] = lse
                rows2 = slice(p * blk, (p + 1) * blk)
                o, lse = pair_attend(q2_ref[pl.ds(r0, blk), cs],
                                     with_prev(k2c_ref, k2p_ref, r0, tile, rows2, t == 0, cs),
                                     with_prev(v2c_ref, v2p_ref, r0, tile, rows2, t == 0, cs),
                                     not_first_span | (t > 0))
                tok = pl.ds(pl.multiple_of(t * tile, tile) + p, blk, stride=r2)
                o2_s[j][tok, :] = o
                l2_s[j][tok, :] = lse
                rr = pl.multiple_of(i * per3, per3)
                gather = lambda ref: jnp.concatenate(
                    [ref[pl.ds(rr + u * tile, per3), cs] for u in range(tiles)], axis=0)
                k_cat = jnp.concatenate([gather(k3p_ref), gather(k3c_ref)], axis=0)
                v_cat = jnp.concatenate([gather(v3p_ref), gather(v3c_ref)], axis=0)
                o, lse = pair_attend(gather(q3_ref), k_cat, v_cat, not_first_span)
                tok = pl.ds(i, blk, stride=r3)
                o3_s[j][tok, :] = o
                l3_s[j][tok, :] = lse
        return carry

    lax.fori_loop(0, tiles, tile_body, 0)

    for j in range(npair):
        for n in range(span // ATTN_MERGE_ROWS):
            rs = slice(n * ATTN_MERGE_ROWS, (n + 1) * ATTN_MERGE_ROWS)
            l1, l2, l3 = l1_s[j][rs, :], l2_s[j][rs, :], l3_s[j][rs, :]
            m = jnp.maximum(jnp.maximum(l1, l2), l3)
            e1, e2, e3 = jnp.exp(l1 - m), jnp.exp(l2 - m), jnp.exp(l3 - m)
            num = e1 * o1_s[j][rs, :] + e2 * o2_s[j][rs, :] + e3 * o3_s[j][rs, :]
            o_ref[rs, j * LANES:(j + 1) * LANES] = (num / (e1 + e2 + e3)).astype(o_ref.dtype)


def _dilated_attention(qs, ks, vs, weights, B, S):
    assert len(weights) == ATTN_N_CAST
    span = ATTN_SPAN
    nsp = S // span
    gw = ATTN_OUT_WIDTH
    T = B * S
    cur = lambda b, s: (b * nsp + s, 0)
    steps = B * nsp
    slab = lambda w: pl.BlockSpec((w.shape[0] // steps, w.shape[1]), cur)

    def prev(rows):
        per = span // rows
        return pl.BlockSpec((rows, gw), lambda b, s: (jnp.maximum((b * nsp + s) * per - 1, 0), 0))

    halos = (ATTN_BLK, TM_IN, span)
    full = pl.BlockSpec((span, gw), cur)
    kv_specs = []
    for h in halos:
        kv_specs += [full, prev(h)]
    kv_args = lambda arrs: [a for arr in arrs for a in (arr, arr)]
    outs = pl.pallas_call(
        _attn_kernel,
        grid=(B, nsp),
        in_specs=[full] * 3 + kv_specs + kv_specs + [slab(w) for w in weights],
        out_specs=[full] + [slab(w) for w in weights],
        out_shape=[jax.ShapeDtypeStruct((T, gw), BF16)] + [jax.ShapeDtypeStruct(w.shape, BF16) for w in weights],
        scratch_shapes=[pltpu.VMEM((span, LANES), F32)] * (6 * (gw // LANES)),
        compiler_params=_params("parallel", "parallel"),
        name="dilated_attn",
    )(*qs, *kv_args(ks), *kv_args(vs), *weights)
    return outs[0], outs[1:]


def _merge_kernel(x_ref, og_ref, oa_ref, ma_ref, mb_ref,
                  gate_ref, scale_ref, shift_ref, n2w_ref, wg_ref, wa_ref, wo_ref, x1_ref, h2_ref, *, tiles_per_seq):
    b = pl.ds(pl.program_id(0) // tiles_per_seq, 1)
    gate, scale, shift = gate_ref[b, :], scale_ref[b, :], shift_ref[b, :]
    sub = x_ref.shape[0] // MERGE_SPLIT
    for r in range(MERGE_SPLIT):
        rs = slice(r * sub, (r + 1) * sub)
        y_attn = jnp.dot(oa_ref[rs, :], wa_ref[...], preferred_element_type=F32)
        y_gla = jnp.dot(og_ref[rs, :], wg_ref[...], preferred_element_type=F32)
        mixed = (ma_ref[rs, :].astype(F32) * y_gla + mb_ref[rs, :].astype(F32) * y_attn).astype(BF16)
        x1 = x_ref[rs, :] + gate * jnp.dot(mixed, wo_ref[...], preferred_element_type=F32)
        x1_ref[rs, :] = x1
        h2_ref[rs, :] = _rmsnorm_mod(x1, n2w_ref[...], scale, shift).astype(h2_ref.dtype)


def _merge(x2, og, oa, ma, mb, mod, n2w, wg, wa, wo, B):
    T, D = x2.shape
    tm = TM_MERGE
    row = lambda i: (i, 0)
    const = lambda i: (0, 0)
    return pl.pallas_call(
        functools.partial(_merge_kernel, tiles_per_seq=T // B // tm),
        grid=(T // tm,),
        in_specs=[pl.BlockSpec((tm, D), row), pl.BlockSpec((tm, GLA_V_WIDTH), row),
                  pl.BlockSpec((tm, ATTN_OUT_WIDTH), row)]
                 + [pl.BlockSpec((tm, D), row)] * 2
                 + [_mod_spec(MOD_GATE1, D), _mod_spec(MOD_SCALE2, D), _mod_spec(MOD_SHIFT2, D)]
                 + [pl.BlockSpec((1, D), const),
                    pl.BlockSpec(wg.shape, const), pl.BlockSpec(wa.shape, const), pl.BlockSpec(wo.shape, const)],
        out_specs=[pl.BlockSpec((tm, D), row)] * 2,
        out_shape=[jax.ShapeDtypeStruct((T, D), F32), jax.ShapeDtypeStruct((T, D), BF16)],
        compiler_params=_params("parallel"),
        name="merge",
    )(x2, og, oa, ma, mb, mod, mod, mod, n2w, wg, wa, wo)


def _ffn_kernel(h_ref, x1_ref, wup_ref, cw_ref, cb_ref, wd_ref, gate_ref, fw_ref, o_ref,
                u_s, tail_s, hid_s, order_s, *, tiles_per_seq):
    i = pl.program_id(0)
    tm = h_ref.shape[0]
    halo = FFN_HALO
    tf = FFN_CHUNK
    half = tm // 2
    nchunk = D_FF // tf
    branches = (0, D_FF)
    first_of_seq = i % tiles_per_seq == 0

    @pl.when(i == 0)
    def _():
        tail_s[...] = jnp.zeros_like(tail_s)

    def up_project(j):
        for n, off in enumerate(branches):
            u = jnp.dot(h_ref[...], wup_ref[:, off + j * tf:off + (j + 1) * tf], preferred_element_type=F32)
            for c in range(tf // LANES):
                slab = u[:, c * LANES:(c + 1) * LANES]
                u_s[j % 2, n, c, 0:halo, :] = jnp.where(first_of_seq, 0.0, tail_s[j, n, c])
                u_s[j % 2, n, c, halo:, :] = slab
                tail_s[j, n, c] = slab[tm - halo:, :]

    def conv(j, n, c, parity):
        u = u_s.at[j % 2, n, c]
        cs = slice(branches[n] + j * tf + c * LANES, branches[n] + j * tf + (c + 1) * LANES)
        tap = lambda back: u[pl.ds(halo + parity - back, half, stride=2), :]
        k = 2.0 ** -0.5
        return (cb_ref[:, cs] * k + (cw_ref[0:1, cs] * k) * tap(2) + (cw_ref[1:2, cs] * k) * tap(1)
                + (cw_ref[2:3, cs] * k) * tap(0))

    up_project(0)
    for j in range(nchunk):
        if j + 1 < nchunk:
            up_project(j + 1)
        for c in range(tf // LANES):
            for parity in range(2):
                t = conv(j, 1, c, parity)
                hidden = (t * (1.0 + lax.erf(t))) * conv(j, 0, c, parity)
                hid_s[parity * half:(parity + 1) * half, j * tf + c * LANES:j * tf + (c + 1) * LANES] = (
                    hidden.astype(hid_s.dtype))

    gate = gate_ref[pl.ds(i // tiles_per_seq, 1), :]
    d_model = o_ref.shape[1]
    sq = jnp.zeros((tm, 1), F32)
    for n in range(d_model // FFN_DOWN_SLAB):
        ns = slice(n * FFN_DOWN_SLAB, (n + 1) * FFN_DOWN_SLAB)
        down = jnp.dot(hid_s[...], wd_ref[:, ns], preferred_element_type=F32)
        for c in range(FFN_DOWN_SLAB // LANES):
            g = n * (FFN_DOWN_SLAB // LANES) + c
            cs = slice(g * LANES, (g + 1) * LANES)
            for parity in range(2):
                order_s[g, pl.ds(parity, half, stride=2), :] = (
                    down[parity * half:(parity + 1) * half, c * LANES:(c + 1) * LANES])
            x2 = x1_ref[:, cs] + gate[:, cs] * order_s[g]
            sq = sq + jnp.sum(x2 * x2, axis=-1, keepdims=True)
            o_ref[:, cs] = x2
    o_ref[...] = o_ref[...] * lax.rsqrt(sq * (1.0 / d_model) + EPS) * fw_ref[...]


def _ffn(h2, x1, w_up, conv_w, conv_b, w_down, mod, final_w, B):
    T, D = x1.shape
    tm, tf, halo = TM_FFN, FFN_CHUNK, FFN_HALO
    row = lambda i: (i, 0)
    const = lambda i: (0, 0)
    single = dict(pipeline_mode=pl.Buffered(1))
    return pl.pallas_call(
        functools.partial(_ffn_kernel, tiles_per_seq=T // B // tm),
        grid=(T // tm,),
        in_specs=[pl.BlockSpec((tm, D), row),
                  pl.BlockSpec((tm, D), row),
                  pl.BlockSpec(w_up.shape, const, **single),
                  pl.BlockSpec(conv_w.shape, const, **single),
                  pl.BlockSpec(conv_b.shape, const, **single),
                  pl.BlockSpec(w_down.shape, const, **single),
                  _mod_spec(MOD_GATE2, D),
                  pl.BlockSpec((1, D), const)],
        out_specs=pl.BlockSpec((tm, D), row),
        out_shape=jax.ShapeDtypeStruct((T, D), F32),
        scratch_shapes=[pltpu.VMEM((2, 2, tf // LANES, halo + tm, LANES), F32),
                        pltpu.VMEM((D_FF // tf, 2, tf // LANES, halo, LANES), F32),
                        pltpu.VMEM((tm, D_FF), BF16),
                        pltpu.VMEM((D // LANES, tm, LANES), F32)],
        compiler_params=_params("arbitrary"),
        name="ffn",
    )(h2, x1, w_up, conv_w, conv_b, w_down, mod, final_w)


def kernel(x, c, positions, ada_w, ada_b, norm1_w, w_in, gla_gate_w2, gla_gate_b, gla_norm_w, w_gla_branch,
           w_attn_branch, w_out, norm2_w, w_up, conv_w, conv_b, w_down, final_norm_w):
    B, S, D = x.shape
    T = B * S
    depth = ada_w.shape[0]
    assert depth == 1, "the final norm is fused into the (single) layer's ffn"
    assert all(window // dilation == ATTN_BLK for window, dilation in ATTN_GROUPS)
    x2 = x.reshape(T, D)
    for layer in range(depth):
        cos, sin, mod, w_in_bf = _prep(positions, c, ada_w[layer], ada_b[layer], w_in[layer].T)

        og, (q1, q2, q3, k1, k2, k3, v1, v2, v3, ma, mb) = _in_proj_gla(
            x2, norm1_w[layer].reshape(1, D), mod, cos, sin, w_in_bf, gla_gate_w2[layer],
            gla_gate_b[layer].reshape(1, -1), gla_norm_w[layer].reshape(1, -1), B)
        oa, (w_up_bf, w_down_bf, w_gla_bf, w_attn_bf, w_out_bf) = _dilated_attention(
            (q1, q2, q3), (k1, k2, k3), (v1, v2, v3),
            (w_up[layer], w_down[layer], w_gla_branch[layer], w_attn_branch[layer], w_out[layer]), B, S)

        x1, h2 = _merge(x2, og, oa, ma, mb, mod, norm2_w[layer].reshape(1, D), w_gla_bf, w_attn_bf, w_out_bf, B)

        x2 = _ffn(h2, x1, w_up_bf, conv_w[layer], conv_b[layer].reshape(1, -1),
                  w_down_bf, mod, final_norm_w.reshape(1, D), B)
    return x2.reshape(B, S, D)
```

```python
import functools

import jax
import jax.numpy as jnp
from jax import lax
from jax.experimental import pallas as pl
from jax.experimental.pallas import tpu as pltpu

F32 = jnp.float32
BF16 = jnp.bfloat16

D_MODEL = 1024
GLA_HEADS = 4
GLA_DK = 128
GLA_DV = 256
GLA_LOWRANK = 16
GLA_TAU = 16.0
GLA_QK_WIDTH = GLA_HEADS * GLA_DK
GLA_V_WIDTH = GLA_HEADS * GLA_DV
ATTN_GROUPS = ((128, 1), (512, 4), (2048, 16))
ATTN_HEADS_PER_GROUP = 4
ATTN_HEAD_DIM = 64
ATTN_WIDTH = ATTN_HEADS_PER_GROUP * len(ATTN_GROUPS) * ATTN_HEAD_DIM
ATTN_OUT_WIDTH = ATTN_HEADS_PER_GROUP * ATTN_HEAD_DIM
ROPE_THETA = 10000.0
D_FF = 2816
CONV_WIDTH = 3
EPS = 1e-6
IN_WIDTHS = (GLA_QK_WIDTH, GLA_QK_WIDTH, GLA_V_WIDTH, GLA_V_WIDTH, GLA_LOWRANK,
             ATTN_WIDTH, ATTN_WIDTH, ATTN_WIDTH, D_MODEL, D_MODEL)

LANES = 128
VMEM_LIMIT_BYTES = 56 * 1024 * 1024

PREP_ROWS = 512
TM_IN = 512
PERM_STRIDE = 4
GLA_CHUNK = 64
ATTN_BLK = 128
ATTN_SPAN = 2048
ATTN_MERGE_ROWS = 256
ATTN_N_CAST = 5
TM_MERGE = 1024
MERGE_SPLIT = 2
TM_FFN = 512
FFN_CHUNK = 256
FFN_HALO = 16
FFN_DOWN_SLAB = 256
MASK_VALUE = float("-inf")
LOG2_E = 1.4426950408889634
LN_2 = 0.6931471805599453


def _params(*sem):
    return pltpu.CompilerParams(dimension_semantics=sem, vmem_limit_bytes=VMEM_LIMIT_BYTES)


def _split_hi_lo(a):
    hi = a.astype(BF16)
    lo = (a - hi.astype(F32)).astype(BF16)
    return hi, lo


def _mod_spec(which, D):
    return pl.BlockSpec((8, D), lambda i: (0, which))


MOD_SHIFT1, MOD_SCALE1, MOD_GATE1, MOD_SHIFT2, MOD_SCALE2, MOD_GATE2 = range(6)


def _prep_kernel(pos_ref, invf_ref, c_ref, adaw_ref, adab_ref, wt_ref, cos_ref, sin_ref, mod_ref, wt_bf_ref):
    c = c_ref[...]
    s = c * (1.0 / (1.0 + jnp.exp(-c)))
    s_hi, s_lo = _split_hi_lo(s)
    lhs = jnp.concatenate([s_hi, s_lo], axis=0)
    w_hi, w_lo = _split_hi_lo(adaw_ref[...])
    acc = jnp.dot(lhs, w_hi, preferred_element_type=F32) + jnp.dot(lhs, w_lo, preferred_element_type=F32)
    mod_ref[...] = acc[0:8] + acc[8:16] + adab_ref[...]

    wt_bf_ref[...] = wt_ref[...].astype(wt_bf_ref.dtype)

    half = ATTN_HEAD_DIM // 2
    groups = LANES // half
    tr = pos_ref.shape[1]
    pos = jnp.concatenate([pos_ref[...].astype(F32), jnp.zeros((8 - groups, tr), F32)], axis=0)
    pos_t = jnp.transpose(pos)
    lane = lax.broadcasted_iota(jnp.int32, (tr, LANES), 1)
    group = lane // half
    pos_dense = jnp.zeros((tr, LANES), F32)
    for q in range(groups):
        pos_dense = jnp.where(group == q, pos_t[:, q:q + 1], pos_dense)
    ang = pos_dense * invf_ref[...]
    first_half = (lane % ATTN_HEAD_DIM) < half
    for table, out_ref, signed in ((jnp.cos(ang), cos_ref, False), (jnp.sin(ang), sin_ref, True)):
        for q in range(groups):
            spread = jnp.where(group == q, table, 0.0)
            shift = groups // 2
            while shift >= 1:
                spread = spread + pltpu.roll(spread, shift * half, axis=1)
                shift //= 2
            out_ref[q] = jnp.where(first_half, -spread, spread) if signed else spread


def _prep(positions, c, ada_w, ada_b, w_in_t):
    T = positions.size
    B, D = c.shape
    N = ada_w.shape[1]
    half = ATTN_HEAD_DIM // 2
    groups = LANES // half
    inv_freq = ROPE_THETA ** (-jnp.arange(half, dtype=F32) / half)
    invf = jnp.tile(inv_freq, groups).reshape(1, LANES)
    per = T // groups
    tr = PREP_ROWS
    steps = per // tr
    tn = N // steps
    bf16_rows = 16
    wt_slab = pl.BlockSpec((-(-w_in_t.shape[0] // (steps * bf16_rows)) * bf16_rows, w_in_t.shape[1]), lambda i: (i, 0))
    c8 = jnp.pad(c, ((0, 8 - B), (0, 0)))
    cos, sin, mod, wt_bf = pl.pallas_call(
        _prep_kernel,
        grid=(steps,),
        in_specs=[pl.BlockSpec((groups, tr), lambda i: (0, i)),
                  pl.BlockSpec((1, LANES), lambda i: (0, 0)),
                  pl.BlockSpec((8, D), lambda i: (0, 0)),
                  pl.BlockSpec((D, tn), lambda i: (0, i)),
                  pl.BlockSpec((1, tn), lambda i: (0, i)),
                  wt_slab],
        out_specs=[pl.BlockSpec((groups, tr, LANES), lambda i: (0, i, 0))] * 2
                  + [pl.BlockSpec((8, tn), lambda i: (0, i)), wt_slab],
        out_shape=[jax.ShapeDtypeStruct((groups, per, LANES), F32)] * 2
                  + [jax.ShapeDtypeStruct((8, N), F32), jax.ShapeDtypeStruct(w_in_t.shape, BF16)],
        compiler_params=_params("parallel"),
        name="prep",
    )(positions.reshape(groups, per), invf, c8, ada_w, ada_b.reshape(1, N), w_in_t)
    return cos.reshape(T, LANES), sin.reshape(T, LANES), mod, wt_bf


def _rmsnorm_mod(x, w, scale, shift):
    ms = jnp.mean(x * x, axis=-1, keepdims=True)
    return (x * lax.rsqrt(ms + EPS) * w) * (1.0 + scale) + shift


def _rotate_half_pairs(t, cos, sin_signed):
    lane = lax.broadcasted_iota(jnp.int32, t.shape, 1)
    first_half = (lane % ATTN_HEAD_DIM) < (ATTN_HEAD_DIM // 2)
    from_right = pltpu.roll(t, LANES - ATTN_HEAD_DIM // 2, axis=1)
    from_left = pltpu.roll(t, ATTN_HEAD_DIM // 2, axis=1)
    return t * cos + jnp.where(first_half, from_right, from_left) * sin_signed


def _store_residue_major(ref, perm_ref, slab, col, dilation):
    if dilation == 1:
        ref[:, col:col + LANES] = slab.astype(ref.dtype)
        return
    tm = slab.shape[0]
    n = tm // dilation
    perm_ref[0] = slab
    if dilation <= PERM_STRIDE:
        for p in range(dilation):
            ref[p * n:(p + 1) * n, col:col + LANES] = perm_ref[0, pl.ds(p, n, stride=dilation), :].astype(ref.dtype)
        return
    f = PERM_STRIDE
    g = dilation // f
    rows = tm // f
    for a in range(f):
        perm_ref[1, a * rows:(a + 1) * rows, :] = perm_ref[0, pl.ds(a, rows, stride=f), :]
    for a in range(f):
        for q in range(g):
            p = q * f + a
            ref[p * n:(p + 1) * n, col:col + LANES] = (
                perm_ref[1, pl.ds(a * rows + q, n, stride=g), :].astype(ref.dtype))


IN_ALIGNED = sum(IN_WIDTHS[:4])
IN_SHIFTED_START = IN_ALIGNED + GLA_LOWRANK


def _in_proj_gla_kernel(x_ref, n1w_ref, scale_ref, shift_ref, cos_ref, sin_ref, wt_ref, w2_ref, b2_ref, nw_ref,
                        og_ref, q1_ref, q2_ref, q3_ref, k1_ref, k2_ref, k3_ref, v1_ref, v2_ref, v3_ref,
                        ma_ref, mb_ref,
                        perm_ref, gq_s, gk_s, gv_s, gr_s, la_s, state_ref,
                        *, tiles_per_seq, n_tiles):
    i = pl.program_id(0)
    nt = (((1,), (1,)), ((), ()))

    @pl.when(i == 0)
    def _():
        for ref in (gq_s, gk_s, gv_s, gr_s, la_s, state_ref):
            ref[...] = jnp.zeros_like(ref)

    first_of_seq = (i + tiles_per_seq - 1) % tiles_per_seq == 0
    chunks = _gla_chunks(gq_s, gk_s, gv_s, la_s, gr_s, nw_ref, og_ref, state_ref, first_of_seq)

    @pl.when(i == n_tiles)
    def _():
        for chunk in chunks:
            chunk()

    @pl.when(i < n_tiles)
    def _():
        b = i // tiles_per_seq
        h = _rmsnorm_mod(x_ref[...], n1w_ref[...], scale_ref[pl.ds(b, 1), :], shift_ref[pl.ds(b, 1), :]).astype(BF16)

        def rows_proj(row, width):
            return lax.dot_general(h, wt_ref[row:row + width, :], nt, preferred_element_type=F32)

        def proj(col, width):
            return rows_proj(IN_SHIFTED_START + col, width)

        cos = cos_ref[...]
        sin = sin_ref[...]
        q_scale = ATTN_HEAD_DIM ** -0.5 * LOG2_E
        dilations = [d for _, d in ATTN_GROUPS]

        def rope_piece(ref, col, scale, dilation):
            def run():
                t = proj(col, ATTN_OUT_WIDTH)
                for s in range(0, ATTN_OUT_WIDTH, LANES):
                    rot = _rotate_half_pairs(t[:, s:s + LANES], cos, sin) * scale
                    _store_residue_major(ref, perm_ref, rot, s, dilation)
            return run

        def value_piece(ref, col, dilation):
            def run():
                t = proj(col, ATTN_OUT_WIDTH)
                for s in range(0, ATTN_OUT_WIDTH, LANES):
                    _store_residue_major(ref, perm_ref, t[:, s:s + LANES], s, dilation)
            return run

        def gate_piece(ref, col, s):
            def run():
                z = proj(col + s, 512)
                ref[:, s:s + 512] = (1.0 / (1.0 + jnp.exp(-z))).astype(ref.dtype)
            return run

        pieces = []
        col = 0
        for refs, scale in (((q1_ref, q2_ref, q3_ref), q_scale), ((k1_ref, k2_ref, k3_ref), 1.0)):
            for g, ref in enumerate(refs):
                pieces.append(rope_piece(ref, col, scale, dilations[g]))
                col += ATTN_OUT_WIDTH
        for g, ref in enumerate((v1_ref, v2_ref, v3_ref)):
            pieces.append(value_piece(ref, col, dilations[g]))
            col += ATTN_OUT_WIDTH
        for ref in (ma_ref, mb_ref):
            for s in range(0, D_MODEL, 512):
                pieces.append(gate_piece(ref, col, s))
            col += D_MODEL

        for n in range(max(len(pieces), len(chunks))):
            if n < len(pieces):
                pieces[n]()
            if n < len(chunks):
                chunks[n]()

        col = 0
        for ref, width in ((gq_s, GLA_QK_WIDTH), (gk_s, GLA_QK_WIDTH)):
            ref[...] = rows_proj(col, width).astype(ref.dtype)
            col += width
        for ref in (gv_s, gr_s):
            for s in range(0, GLA_V_WIDTH, 512):
                ref[:, s:s + 512] = rows_proj(col + s, 512).astype(ref.dtype)
            col += GLA_V_WIDTH
        g_lr = rows_proj(IN_ALIGNED, LANES)
        g_hi, g_lo = _split_hi_lo(g_lr)
        w2_hi, w2_lo = _split_hi_lo(w2_ref[...])
        z = (jnp.dot(g_hi, w2_hi, preferred_element_type=F32) + jnp.dot(g_lo, w2_hi, preferred_element_type=F32)
             + jnp.dot(g_hi, w2_lo, preferred_element_type=F32)) + b2_ref[...]
        log_sig = jnp.minimum(z, 0.0) - jnp.log(1.0 + jnp.exp(-jnp.abs(z)))
        la_s[...] = log_sig * (LOG2_E / GLA_TAU)


def _gla_chunks(q_ref, k_ref, v_ref, la_ref, gr_ref, nw_ref, o_ref, state_ref, first_of_seq):
    C = GLA_CHUNK
    H = C // 2
    heads = range(GLA_HEADS)
    rows = lax.broadcasted_iota(jnp.int32, (C, C), 0)
    cols = lax.broadcasted_iota(jnp.int32, (C, C), 1)
    causal = cols <= rows
    tri = causal.astype(BF16)
    same_half = (rows >= H) == (cols >= H)
    upper_rows = lax.broadcasted_iota(jnp.int32, (C, GLA_QK_WIDTH), 0) < H
    q_scale = GLA_DK ** -0.5
    nw = jnp.concatenate([nw_ref[...]] * GLA_HEADS, axis=1)
    nt = (((1,), (1,)), ((), ()))
    ks = lambda a, h: a[:, h * GLA_DK:(h + 1) * GLA_DK]
    vs = lambda a, h: a[:, h * GLA_DV:(h + 1) * GLA_DV]

    def run_chunk(c):
        sl = pl.ds(c * C, C)
        q = q_ref[sl, :].astype(F32) * q_scale
        k = k_ref[sl, :].astype(F32)
        v = v_ref[sl, :]
        la_hi, la_lo = _split_hi_lo(la_ref[sl, :])
        cum = jnp.dot(tri, la_hi, preferred_element_type=F32) + jnp.dot(tri, la_lo, preferred_element_type=F32)
        mid = jnp.where(upper_rows, cum[H // 2 - 1:H // 2, :], cum[H + H // 2 - 1:H + H // 2, :])
        edge = cum[H - 1:H, :]
        last = cum[C - 1:C, :]
        qg = (q * jnp.exp2(cum - mid)).astype(BF16)
        kg = (k * jnp.exp2(mid - cum)).astype(BF16)
        zeros_half = jnp.zeros((H, GLA_QK_WIDTH), BF16)
        q_lo = jnp.concatenate([zeros_half, (q[H:] * jnp.exp2(cum[H:] - edge)).astype(BF16)], axis=0)
        k_up = jnp.concatenate([(k[:H] * jnp.exp2(edge - cum[:H])).astype(BF16), zeros_half], axis=0)
        q_in = (q * jnp.exp2(cum)).astype(BF16)
        k_out = k * jnp.exp2(last - cum)
        decay_row = jnp.broadcast_to(jnp.exp2(last), (8, GLA_QK_WIDTH))
        g = gr_ref[sl, :].astype(F32)
        gate = nw * (g * (1.0 / (1.0 + jnp.exp(-g))))

        within = [lax.dot_general(ks(qg, h), ks(kg, h), nt, preferred_element_type=F32) for h in heads]
        across = [lax.dot_general(ks(q_lo, h), ks(k_up, h), nt, preferred_element_type=F32) for h in heads]
        attn = [jnp.where(same_half, jnp.where(causal, within[h], 0.0), across[h]).astype(BF16) for h in heads]
        if c == 0:
            state = [jnp.where(first_of_seq, 0.0, state_ref[h]) for h in heads]
        else:
            state = [state_ref[h] for h in heads]
        o = [jnp.dot(attn[h], vs(v, h), preferred_element_type=F32)
             + jnp.dot(ks(q_in, h), state[h].astype(BF16), preferred_element_type=F32) for h in heads]
        k_out_t = [jnp.transpose(ks(k_out, h)).astype(BF16) for h in heads]
        decay = [jnp.transpose(ks(decay_row, h))[:, 0:1] for h in heads]
        for h in heads:
            state_ref[h] = decay[h] * state[h] + jnp.dot(k_out_t[h], vs(v, h), preferred_element_type=F32)
        ms = [jnp.mean(o[h] * o[h], axis=-1, keepdims=True) for h in heads]
        for h in heads:
            o_ref[sl, h * GLA_DV:(h + 1) * GLA_DV] = (o[h] * lax.rsqrt(ms[h] + EPS) * vs(gate, h)).astype(o_ref.dtype)

    return [functools.partial(run_chunk, c) for c in range(q_ref.shape[0] // C)]


def _in_proj_gla(x2, n1w, mod, cos, sin, w_in_t, w2, b2, gla_norm_w, B):
    T, D = x2.shape
    tm = TM_IN
    n_tiles = T // tm
    cur = lambda i: (jnp.minimum(i, n_tiles - 1), 0)
    lag = lambda i: (jnp.maximum(i - 1, 0), 0)
    const = lambda i: (0, 0)
    widths = (ATTN_OUT_WIDTH,) * 9 + (D_MODEL, D_MODEL)
    single = dict(pipeline_mode=pl.Buffered(1))
    w2p = jnp.pad(w2, ((0, LANES - GLA_LOWRANK), (0, 0)))
    outs = pl.pallas_call(
        functools.partial(_in_proj_gla_kernel, tiles_per_seq=T // B // tm, n_tiles=n_tiles),
        grid=(n_tiles + 1,),
        in_specs=[pl.BlockSpec((tm, D), cur),
                  pl.BlockSpec((1, D), const),
                  _mod_spec(MOD_SCALE1, D),
                  _mod_spec(MOD_SHIFT1, D),
                  pl.BlockSpec((tm, LANES), cur),
                  pl.BlockSpec((tm, LANES), cur),
                  pl.BlockSpec(w_in_t.shape, const, **single),
                  pl.BlockSpec(w2p.shape, const, **single),
                  pl.BlockSpec(b2.shape, const, **single),
                  pl.BlockSpec(gla_norm_w.shape, const, **single)],
        out_specs=[pl.BlockSpec((tm, GLA_V_WIDTH), lag)] + [pl.BlockSpec((tm, w), cur) for w in widths],
        out_shape=[jax.ShapeDtypeStruct((T, GLA_V_WIDTH), BF16)]
                  + [jax.ShapeDtypeStruct((T, w), BF16) for w in widths],
        scratch_shapes=[pltpu.VMEM((2, tm, LANES), F32),
                        pltpu.VMEM((tm, GLA_QK_WIDTH), BF16),
                        pltpu.VMEM((tm, GLA_QK_WIDTH), BF16),
                        pltpu.VMEM((tm, GLA_V_WIDTH), BF16),
                        pltpu.VMEM((tm, GLA_V_WIDTH), BF16),
                        pltpu.VMEM((tm, GLA_QK_WIDTH), F32),
                        pltpu.VMEM((GLA_HEADS, GLA_DK, GLA_DV), F32)],
        compiler_params=_params("arbitrary"),
        name="in_proj_gla",
    )(x2, n1w, mod, mod, cos, sin, w_in_t, w2p, b2, gla_norm_w)
    return outs[0], outs[1:]


def _attn_kernel(q1_ref, q2_ref, q3_ref, k1c_ref, k1p_ref, k2c_ref, k2p_ref, k3c_ref, k3p_ref,
                 v1c_ref, v1p_ref, v2c_ref, v2p_ref, v3c_ref, v3p_ref, *refs):
    n_cast = ATTN_N_CAST
    o_ref = refs[n_cast]
    slabs = refs[2 * n_cast + 1:]
    for src, dst in zip(refs[:n_cast], refs[n_cast + 1:2 * n_cast + 1]):
        dst[...] = src[...].astype(dst.dtype)

    blk = ATTN_BLK
    tile = TM_IN
    span = ATTN_SPAN
    tiles = span // tile
    r2, r3 = ATTN_GROUPS[1][1], ATTN_GROUPS[2][1]
    per3 = tile // r3
    not_first_span = pl.program_id(1) > 0
    npair = ATTN_OUT_WIDTH // LANES
    o1_s, o2_s, o3_s, l1_s, l2_s, l3_s = [slabs[n * npair:(n + 1) * npair] for n in range(6)]

    rows = lax.broadcasted_iota(jnp.int32, (blk, 2 * blk), 0)
    cols = lax.broadcasted_iota(jnp.int32, (blk, 2 * blk), 1)
    band = (cols >= rows) & (cols <= rows + blk)
    in_cur = cols >= blk
    left = lax.broadcasted_iota(jnp.int32, (blk, LANES), 1) < ATTN_HEAD_DIM
    ones = jnp.ones((2 * blk, LANES), BF16)
    nt = (((1,), (1,)), ((), ()))

    def pair_attend(q_pair, k_cat, v_cat, has_prev):
        zero = jnp.zeros_like(q_pair)
        q2 = jnp.concatenate([jnp.where(left, q_pair, zero), jnp.where(left, zero, q_pair)], axis=0)
        s = lax.dot_general(q2, k_cat, nt, preferred_element_type=F32)
        valid = band & (in_cur | has_prev)
        s = jnp.where(jnp.concatenate([valid, valid], axis=0), s, MASK_VALUE)
        m = jnp.max(s, axis=-1, keepdims=True)
        p = jnp.exp2(s - m).astype(BF16)
        r = jnp.dot(p, jnp.concatenate([v_cat, ones], axis=1), preferred_element_type=F32)
        acc = jnp.where(left, r[:blk, :LANES], r[blk:, :LANES])
        den = jnp.where(left, r[:blk, LANES:], r[blk:, LANES:])
        m_pair = jnp.where(left, jnp.broadcast_to(m[:blk], (blk, LANES)), jnp.broadcast_to(m[blk:], (blk, LANES)))
        return acc / den, m_pair * LN_2 + jnp.log(den)

    def with_prev(cur_ref, prev_ref, r0, back, prev_rows, first, cs):
        before = cur_ref[pl.ds(pl.multiple_of(jnp.maximum(r0 - back, 0), blk), blk), cs]
        if prev_rows is not None:
            before = jnp.where(first, prev_ref[prev_rows, cs], before)
        return jnp.concatenate([before, cur_ref[pl.ds(r0, blk), cs]], axis=0)

    def tile_body(t, carry):
        for p in range(tiles):
            i = t * tiles + p
            r0 = pl.multiple_of(i * blk, blk)
            for j in range(npair):
                cs = slice(j * LANES, (j + 1) * LANES)
                rows1 = slice(0, blk) if p == 0 else None
                o, lse = pair_attend(q1_ref[pl.ds(r0, blk), cs],
                                     with_prev(k1c_ref, k1p_ref, r0, blk, rows1, t == 0, cs),
                                     with_prev(v1c_ref, v1p_ref, r0, blk, rows1, t == 0, cs),
                                     not_first_span | (i > 0))
                o1_s[j][pl.ds(r0, blk), :] = o
                l1_s[j][pl.ds(r0, blk), :] = lse
                rows2 = slice(p * blk, (p + 1) * blk)
                o, lse = pair_attend(q2_ref[pl.ds(r0, blk), cs],
                                     with_prev(k2c_ref, k2p_ref, r0, tile, rows2, t == 0, cs),
                                     with_prev(v2c_ref, v2p_ref, r0, tile, rows2, t == 0, cs),
                                     not_first_span | (t > 0))
                tok = pl.ds(pl.multiple_of(t * tile, tile) + p, blk, stride=r2)
                o2_s[j][tok, :] = o
                l2_s[j][tok, :] = lse
                rr = pl.multiple_of(i * per3, per3)
                gather = lambda ref: jnp.concatenate(
                    [ref[pl.ds(rr + u * tile, per3), cs] for u in range(tiles)], axis=0)
                k_cat = jnp.concatenate([gather(k3p_ref), gather(k3c_ref)], axis=0)
                v_cat = jnp.concatenate([gather(v3p_ref), gather(v3c_ref)], axis=0)
                o, lse = pair_attend(gather(q3_ref), k_cat, v_cat, not_first_span)
                tok = pl.ds(i, blk, stride=r3)
                o3_s[j][tok, :] = o
                l3_s[j][tok, :] = lse
        return carry

    lax.fori_loop(0, tiles, tile_body, 0)

    for j in range(npair):
        for n in range(span // ATTN_MERGE_ROWS):
            rs = slice(n * ATTN_MERGE_ROWS, (n + 1) * ATTN_MERGE_ROWS)
            l1, l2, l3 = l1_s[j][rs, :], l2_s[j][rs, :], l3_s[j][rs, :]
            m = jnp.maximum(jnp.maximum(l1, l2), l3)
            e1, e2, e3 = jnp.exp(l1 - m), jnp.exp(l2 - m), jnp.exp(l3 - m)
            num = e1 * o1_s[j][rs, :] + e2 * o2_s[j][rs, :] + e3 * o3_s[j][rs, :]
            o_ref[rs, j * LANES:(j + 1) * LANES] = (num / (e1 + e2 + e3)).astype(o_ref.dtype)


def _dilated_attention(qs, ks, vs, weights, B, S):
    assert len(weights) == ATTN_N_CAST
    span = ATTN_SPAN
    nsp = S // span
    gw = ATTN_OUT_WIDTH
    T = B * S
    cur = lambda b, s: (b * nsp + s, 0)
    steps = B * nsp
    slab = lambda w: pl.BlockSpec((w.shape[0] // steps, w.shape[1]), cur)

    def prev(rows):
        per = span // rows
        return pl.BlockSpec((rows, gw), lambda b, s: (jnp.maximum((b * nsp + s) * per - 1, 0), 0))

    halos = (ATTN_BLK, TM_IN, span)
    full = pl.BlockSpec((span, gw), cur)
    kv_specs = []
    for h in halos:
        kv_specs += [full, prev(h)]
    kv_args = lambda arrs: [a for arr in arrs for a in (arr, arr)]
    outs = pl.pallas_call(
        _attn_kernel,
        grid=(B, nsp),
        in_specs=[full] * 3 + kv_specs + kv_specs + [slab(w) for w in weights],
        out_specs=[full] + [slab(w) for w in weights],
        out_shape=[jax.ShapeDtypeStruct((T, gw), BF16)] + [jax.ShapeDtypeStruct(w.shape, BF16) for w in weights],
        scratch_shapes=[pltpu.VMEM((span, LANES), F32)] * (6 * (gw // LANES)),
        compiler_params=_params("parallel", "parallel"),
        name="dilated_attn",
    )(*qs, *kv_args(ks), *kv_args(vs), *weights)
    return outs[0], outs[1:]


def _merge_kernel(x_ref, og_ref, oa_ref, ma_ref, mb_ref,
                  gate_ref, scale_ref, shift_ref, n2w_ref, wg_ref, wa_ref, wo_ref, x1_ref, h2_ref, *, tiles_per_seq):
    b = pl.ds(pl.program_id(0) // tiles_per_seq, 1)
    gate, scale, shift = gate_ref[b, :], scale_ref[b, :], shift_ref[b, :]
    sub = x_ref.shape[0] // MERGE_SPLIT
    for r in range(MERGE_SPLIT):
        rs = slice(r * sub, (r + 1) * sub)
        y_attn = jnp.dot(oa_ref[rs, :], wa_ref[...], preferred_element_type=F32)
        y_gla = jnp.dot(og_ref[rs, :], wg_ref[...], preferred_element_type=F32)
        mixed = (ma_ref[rs, :].astype(F32) * y_gla + mb_ref[rs, :].astype(F32) * y_attn).astype(BF16)
        x1 = x_ref[rs, :] + gate * jnp.dot(mixed, wo_ref[...], preferred_element_type=F32)
        x1_ref[rs, :] = x1
        h2_ref[rs, :] = _rmsnorm_mod(x1, n2w_ref[...], scale, shift).astype(h2_ref.dtype)


def _merge(x2, og, oa, ma, mb, mod, n2w, wg, wa, wo, B):
    T, D = x2.shape
    tm = TM_MERGE
    row = lambda i: (i, 0)
    const = lambda i: (0, 0)
    return pl.pallas_call(
        functools.partial(_merge_kernel, tiles_per_seq=T // B // tm),
        grid=(T // tm,),
        in_specs=[pl.BlockSpec((tm, D), row), pl.BlockSpec((tm, GLA_V_WIDTH), row),
                  pl.BlockSpec((tm, ATTN_OUT_WIDTH), row)]
                 + [pl.BlockSpec((tm, D), row)] * 2
                 + [_mod_spec(MOD_GATE1, D), _mod_spec(MOD_SCALE2, D), _mod_spec(MOD_SHIFT2, D)]
                 + [pl.BlockSpec((1, D), const),
                    pl.BlockSpec(wg.shape, const), pl.BlockSpec(wa.shape, const), pl.BlockSpec(wo.shape, const)],
        out_specs=[pl.BlockSpec((tm, D), row)] * 2,
        out_shape=[jax.ShapeDtypeStruct((T, D), F32), jax.ShapeDtypeStruct((T, D), BF16)],
        compiler_params=_params("parallel"),
        name="merge",
    )(x2, og, oa, ma, mb, mod, mod, mod, n2w, wg, wa, wo)


def _ffn_kernel(h_ref, hprev_ref, x1_ref, wup_ref, cw_ref, cb_ref, wd_ref, gate_ref, fw_ref, o_ref,
                hcat_s, u_s, hid_s, order_s, *, tiles_per_seq):
    i = pl.program_id(0)
    tm = h_ref.shape[0]
    halo = hprev_ref.shape[0]
    tf = FFN_CHUNK
    half = tm // 2
    hcat_s[0:halo] = jnp.where(i % tiles_per_seq == 0, jnp.zeros_like(hprev_ref[...]), hprev_ref[...])
    hcat_s[halo:] = h_ref[...]
    nchunk = D_FF // tf
    branches = (0, D_FF)

    def up_project(j):
        for n, off in enumerate(branches):
            u = jnp.dot(hcat_s[...], wup_ref[:, off + j * tf:off + (j + 1) * tf], preferred_element_type=F32)
            for c in range(tf // LANES):
                u_s[j % 2, n, c] = u[:, c * LANES:(c + 1) * LANES]

    def conv(j, n, c, parity):
        u = u_s.at[j % 2, n, c]
        cs = slice(branches[n] + j * tf + c * LANES, branches[n] + j * tf + (c + 1) * LANES)
        tap = lambda back: u[pl.ds(halo + parity - back, half, stride=2), :]
        k = 2.0 ** -0.5
        return (cb_ref[:, cs] * k + (cw_ref[0:1, cs] * k) * tap(2) + (cw_ref[1:2, cs] * k) * tap(1)
                + (cw_ref[2:3, cs] * k) * tap(0))

    up_project(0)
    for j in range(nchunk):
        if j + 1 < nchunk:
            up_project(j + 1)
        for c in range(tf // LANES):
            for parity in range(2):
                t = conv(j, 1, c, parity)
                hidden = (t * (1.0 + lax.erf(t))) * conv(j, 0, c, parity)
                hid_s[parity * half:(parity + 1) * half, j * tf + c * LANES:j * tf + (c + 1) * LANES] = (
                    hidden.astype(hid_s.dtype))

    gate = gate_ref[pl.ds(i // tiles_per_seq, 1), :]
    d_model = o_ref.shape[1]
    sq = jnp.zeros((tm, 1), F32)
    for n in range(d_model // FFN_DOWN_SLAB):
        ns = slice(n * FFN_DOWN_SLAB, (n + 1) * FFN_DOWN_SLAB)
        down = jnp.dot(hid_s[...], wd_ref[:, ns], preferred_element_type=F32)
        for c in range(FFN_DOWN_SLAB // LANES):
            g = n * (FFN_DOWN_SLAB // LANES) + c
            cs = slice(g * LANES, (g + 1) * LANES)
            for parity in range(2):
                order_s[g, pl.ds(parity, half, stride=2), :] = (
                    down[parity * half:(parity + 1) * half, c * LANES:(c + 1) * LANES])
            x2 = x1_ref[:, cs] + gate[:, cs] * order_s[g]
            sq = sq + jnp.sum(x2 * x2, axis=-1, keepdims=True)
            o_ref[:, cs] = x2
    o_ref[...] = o_ref[...] * lax.rsqrt(sq * (1.0 / d_model) + EPS) * fw_ref[...]


def _ffn(h2, x1, w_up, conv_w, conv_b, w_down, mod, final_w, B):
    T, D = x1.shape
    tm, tf, halo = TM_FFN, FFN_CHUNK, FFN_HALO
    row = lambda i: (i, 0)
    const = lambda i: (0, 0)
    single = dict(pipeline_mode=pl.Buffered(1))
    return pl.pallas_call(
        functools.partial(_ffn_kernel, tiles_per_seq=T // B // tm),
        grid=(T // tm,),
        in_specs=[pl.BlockSpec((tm, D), row),
                  pl.BlockSpec((halo, D), lambda i: (jnp.maximum(i * (tm // halo) - 1, 0), 0)),
                  pl.BlockSpec((tm, D), row),
                  pl.BlockSpec(w_up.shape, const, **single),
                  pl.BlockSpec(conv_w.shape, const, **single),
                  pl.BlockSpec(conv_b.shape, const, **single),
                  pl.BlockSpec(w_down.shape, const, **single),
                  _mod_spec(MOD_GATE2, D),
                  pl.BlockSpec((1, D), const)],
        out_specs=pl.BlockSpec((tm, D), row),
        out_shape=jax.ShapeDtypeStruct((T, D), F32),
        scratch_shapes=[pltpu.VMEM((halo + tm, D), BF16),
                        pltpu.VMEM((2, 2, tf // LANES, halo + tm, LANES), F32),
                        pltpu.VMEM((tm, D_FF), BF16),
                        pltpu.VMEM((D // LANES, tm, LANES), F32)],
        compiler_params=_params("parallel"),
        name="ffn",
    )(h2, h2, x1, w_up, conv_w, conv_b, w_down, mod, final_w)


def kernel(x, c, positions, ada_w, ada_b, norm1_w, w_in, gla_gate_w2, gla_gate_b, gla_norm_w, w_gla_branch,
           w_attn_branch, w_out, norm2_w, w_up, conv_w, conv_b, w_down, final_norm_w):
    B, S, D = x.shape
    T = B * S
    depth = ada_w.shape[0]
    assert depth == 1, "the final norm is fused into the (single) layer's ffn"
    assert all(window // dilation == ATTN_BLK for window, dilation in ATTN_GROUPS)
    x2 = x.reshape(T, D)
    for layer in range(depth):
        cos, sin, mod, w_in_bf = _prep(positions, c, ada_w[layer], ada_b[layer], w_in[layer].T)

        og, (q1, q2, q3, k1, k2, k3, v1, v2, v3, ma, mb) = _in_proj_gla(
            x2, norm1_w[layer].reshape(1, D), mod, cos, sin, w_in_bf, gla_gate_w2[layer],
            gla_gate_b[layer].reshape(1, -1), gla_norm_w[layer].reshape(1, -1), B)
        oa, (w_up_bf, w_down_bf, w_gla_bf, w_attn_bf, w_out_bf) = _dilated_attention(
            (q1, q2, q3), (k1, k2, k3), (v1, v2, v3),
            (w_up[layer], w_down[layer], w_gla_branch[layer], w_attn_branch[layer], w_out[layer]), B, S)

        x1, h2 = _merge(x2, og, oa, ma, mb, mod, norm2_w[layer].reshape(1, D), w_gla_bf, w_attn_bf, w_out_bf, B)

        x2 = _ffn(h2, x1, w_up_bf, conv_w[layer], conv_b[layer].reshape(1, -1),
                  w_down_bf, mod, final_norm_w.reshape(1, D), B)
    return x2.reshape(B, S, D)
```

```python
import functools

import jax
import jax.numpy as jnp
from jax import lax
from jax.experimental import pallas as pl
from jax.experimental.pallas import tpu as pltpu

F32 = jnp.float32
BF16 = jnp.bfloat16

D_MODEL = 1024
GLA_HEADS = 4
GLA_DK = 128
GLA_DV = 256
GLA_LOWRANK = 16
GLA_TAU = 16.0
GLA_QK_WIDTH = GLA_HEADS * GLA_DK
GLA_V_WIDTH = GLA_HEADS * GLA_DV
ATTN_GROUPS = ((128, 1), (512, 4), (2048, 16))
ATTN_HEADS_PER_GROUP = 4
ATTN_HEAD_DIM = 64
ATTN_WIDTH = ATTN_HEADS_PER_GROUP * len(ATTN_GROUPS) * ATTN_HEAD_DIM
ATTN_OUT_WIDTH = ATTN_HEADS_PER_GROUP * ATTN_HEAD_DIM
ROPE_THETA = 10000.0
D_FF = 2816
CONV_WIDTH = 3
EPS = 1e-6
IN_WIDTHS = (GLA_QK_WIDTH, GLA_QK_WIDTH, GLA_V_WIDTH, GLA_V_WIDTH, GLA_LOWRANK,
             ATTN_WIDTH, ATTN_WIDTH, ATTN_WIDTH, D_MODEL, D_MODEL)

LANES = 128
VMEM_LIMIT_BYTES = 56 * 1024 * 1024

PREP_ROWS = 512
TM_IN = 512
PERM_STRIDE = 4
GLA_CHUNK = 64
ATTN_BLK = 128
ATTN_SPAN = 2048
ATTN_MERGE_ROWS = 256
ATTN_N_CAST = 5
TM_MERGE = 1024
MERGE_SPLIT = 2
TM_FFN = 512
FFN_CHUNK = 256
FFN_HALO = 16
FFN_DOWN_SLAB = 256
MASK_VALUE = float("-inf")
LOG2_E = 1.4426950408889634
LN_2 = 0.6931471805599453


def _params(*sem):
    return pltpu.CompilerParams(dimension_semantics=sem, vmem_limit_bytes=VMEM_LIMIT_BYTES)


def _split_hi_lo(a):
    hi = a.astype(BF16)
    lo = (a - hi.astype(F32)).astype(BF16)
    return hi, lo


def _mod_spec(which, D):
    return pl.BlockSpec((8, D), lambda i: (0, which))


MOD_SHIFT1, MOD_SCALE1, MOD_GATE1, MOD_SHIFT2, MOD_SCALE2, MOD_GATE2 = range(6)


def _prep_kernel(pos_ref, invf_ref, c_ref, adaw_ref, adab_ref, wt_ref, cos_ref, sin_ref, mod_ref, wt_bf_ref):
    c = c_ref[...]
    s = c * (1.0 / (1.0 + jnp.exp(-c)))
    s_hi, s_lo = _split_hi_lo(s)
    lhs = jnp.concatenate([s_hi, s_lo], axis=0)
    w_hi, w_lo = _split_hi_lo(adaw_ref[...])
    acc = jnp.dot(lhs, w_hi, preferred_element_type=F32) + jnp.dot(lhs, w_lo, preferred_element_type=F32)
    mod_ref[...] = acc[0:8] + acc[8:16] + adab_ref[...]

    wt_bf_ref[...] = wt_ref[...].astype(wt_bf_ref.dtype)

    half = ATTN_HEAD_DIM // 2
    groups = LANES // half
    tr = pos_ref.shape[1]
    pos = jnp.concatenate([pos_ref[...].astype(F32), jnp.zeros((8 - groups, tr), F32)], axis=0)
    pos_t = jnp.transpose(pos)
    lane = lax.broadcasted_iota(jnp.int32, (tr, LANES), 1)
    group = lane // half
    pos_dense = jnp.zeros((tr, LANES), F32)
    for q in range(groups):
        pos_dense = jnp.where(group == q, pos_t[:, q:q + 1], pos_dense)
    ang = pos_dense * invf_ref[...]
    first_half = (lane % ATTN_HEAD_DIM) < half
    for table, out_ref, signed in ((jnp.cos(ang), cos_ref, False), (jnp.sin(ang), sin_ref, True)):
        for q in range(groups):
            spread = jnp.where(group == q, table, 0.0)
            shift = groups // 2
            while shift >= 1:
                spread = spread + pltpu.roll(spread, shift * half, axis=1)
                shift //= 2
            out_ref[q] = jnp.where(first_half, -spread, spread) if signed else spread


def _prep(positions, c, ada_w, ada_b, w_in_t):
    T = positions.size
    B, D = c.shape
    N = ada_w.shape[1]
    half = ATTN_HEAD_DIM // 2
    groups = LANES // half
    inv_freq = ROPE_THETA ** (-jnp.arange(half, dtype=F32) / half)
    invf = jnp.tile(inv_freq, groups).reshape(1, LANES)
    per = T // groups
    tr = PREP_ROWS
    steps = per // tr
    tn = N // steps
    bf16_rows = 16
    wt_slab = pl.BlockSpec((-(-w_in_t.shape[0] // (steps * bf16_rows)) * bf16_rows, w_in_t.shape[1]), lambda i: (i, 0))
    c8 = jnp.pad(c, ((0, 8 - B), (0, 0)))
    cos, sin, mod, wt_bf = pl.pallas_call(
        _prep_kernel,
        grid=(steps,),
        in_specs=[pl.BlockSpec((groups, tr), lambda i: (0, i)),
                  pl.BlockSpec((1, LANES), lambda i: (0, 0)),
                  pl.BlockSpec((8, D), lambda i: (0, 0)),
                  pl.BlockSpec((D, tn), lambda i: (0, i)),
                  pl.BlockSpec((1, tn), lambda i: (0, i)),
                  wt_slab],
        out_specs=[pl.BlockSpec((groups, tr, LANES), lambda i: (0, i, 0))] * 2
                  + [pl.BlockSpec((8, tn), lambda i: (0, i)), wt_slab],
        out_shape=[jax.ShapeDtypeStruct((groups, per, LANES), F32)] * 2
                  + [jax.ShapeDtypeStruct((8, N), F32), jax.ShapeDtypeStruct(w_in_t.shape, BF16)],
        compiler_params=_params("parallel"),
        name="prep",
    )(positions.reshape(groups, per), invf, c8, ada_w, ada_b.reshape(1, N), w_in_t)
    return cos.reshape(T, LANES), sin.reshape(T, LANES), mod, wt_bf


def _rmsnorm_mod(x, w, scale, shift):
    ms = jnp.mean(x * x, axis=-1, keepdims=True)
    return (x * lax.rsqrt(ms + EPS)) * (w * (1.0 + scale)) + shift


def _rotate_half_pairs(t, cos, sin_signed):
    lane = lax.broadcasted_iota(jnp.int32, t.shape, 1)
    first_half = (lane % ATTN_HEAD_DIM) < (ATTN_HEAD_DIM // 2)
    from_right = pltpu.roll(t, LANES - ATTN_HEAD_DIM // 2, axis=1)
    from_left = pltpu.roll(t, ATTN_HEAD_DIM // 2, axis=1)
    return t * cos + jnp.where(first_half, from_right, from_left) * sin_signed


def _store_residue_major(ref, perm_ref, slab, col, dilation):
    if dilation == 1:
        ref[:, col:col + LANES] = slab.astype(ref.dtype)
        return
    tm = slab.shape[0]
    n = tm // dilation
    perm_ref[0] = slab
    if dilation <= PERM_STRIDE:
        for p in range(dilation):
            ref[p * n:(p + 1) * n, col:col + LANES] = perm_ref[0, pl.ds(p, n, stride=dilation), :].astype(ref.dtype)
        return
    f = PERM_STRIDE
    g = dilation // f
    rows = tm // f
    for a in range(f):
        perm_ref[1, a * rows:(a + 1) * rows, :] = perm_ref[0, pl.ds(a, rows, stride=f), :]
    for a in range(f):
        for q in range(g):
            p = q * f + a
            ref[p * n:(p + 1) * n, col:col + LANES] = (
                perm_ref[1, pl.ds(a * rows + q, n, stride=g), :].astype(ref.dtype))


IN_ALIGNED = sum(IN_WIDTHS[:4])
IN_SHIFTED_START = IN_ALIGNED + GLA_LOWRANK


def _in_proj_gla_kernel(x_ref, n1w_ref, scale_ref, shift_ref, cos_ref, sin_ref, wt_ref, w2_ref, b2_ref, nw_ref,
                        og_ref, q1_ref, q2_ref, q3_ref, k1_ref, k2_ref, k3_ref, v1_ref, v2_ref, v3_ref,
                        ma_ref, mb_ref,
                        perm_ref, gq_s, gk_s, gv_s, gr_s, la_s, state_ref,
                        *, tiles_per_seq, n_tiles):
    i = pl.program_id(0)
    nt = (((1,), (1,)), ((), ()))

    @pl.when(i == 0)
    def _():
        for ref in (gq_s, gk_s, gv_s, gr_s, la_s, state_ref):
            ref[...] = jnp.zeros_like(ref)

    first_of_seq = (i + tiles_per_seq - 1) % tiles_per_seq == 0
    chunks = _gla_chunks(gq_s, gk_s, gv_s, la_s, gr_s, nw_ref, og_ref, state_ref, first_of_seq)

    @pl.when(i == n_tiles)
    def _():
        for chunk in chunks:
            chunk()

    @pl.when(i < n_tiles)
    def _():
        b = i // tiles_per_seq
        h = _rmsnorm_mod(x_ref[...], n1w_ref[...], scale_ref[pl.ds(b, 1), :], shift_ref[pl.ds(b, 1), :]).astype(BF16)

        def rows_proj(row, width):
            return lax.dot_general(h, wt_ref[row:row + width, :], nt, preferred_element_type=F32)

        def proj(col, width):
            return rows_proj(IN_SHIFTED_START + col, width)

        cos = cos_ref[...]
        sin = sin_ref[...]
        q_scale = ATTN_HEAD_DIM ** -0.5 * LOG2_E
        dilations = [d for _, d in ATTN_GROUPS]

        def rope_piece(ref, col, scale, dilation):
            def run():
                t = proj(col, ATTN_OUT_WIDTH)
                for s in range(0, ATTN_OUT_WIDTH, LANES):
                    rot = _rotate_half_pairs(t[:, s:s + LANES], cos, sin) * scale
                    _store_residue_major(ref, perm_ref, rot, s, dilation)
            return run

        def value_piece(ref, col, dilation):
            def run():
                t = proj(col, ATTN_OUT_WIDTH)
                for s in range(0, ATTN_OUT_WIDTH, LANES):
                    _store_residue_major(ref, perm_ref, t[:, s:s + LANES], s, dilation)
            return run

        def gate_piece(ref, col, s):
            def run():
                z = proj(col + s, 512)
                ref[:, s:s + 512] = (1.0 / (1.0 + jnp.exp(-z))).astype(ref.dtype)
            return run

        pieces = []
        col = 0
        for refs, scale in (((q1_ref, q2_ref, q3_ref), q_scale), ((k1_ref, k2_ref, k3_ref), 1.0)):
            for g, ref in enumerate(refs):
                pieces.append(rope_piece(ref, col, scale, dilations[g]))
                col += ATTN_OUT_WIDTH
        for g, ref in enumerate((v1_ref, v2_ref, v3_ref)):
            pieces.append(value_piece(ref, col, dilations[g]))
            col += ATTN_OUT_WIDTH
        for ref in (ma_ref, mb_ref):
            for s in range(0, D_MODEL, 512):
                pieces.append(gate_piece(ref, col, s))
            col += D_MODEL

        for n in range(max(len(pieces), len(chunks))):
            if n < len(pieces):
                pieces[n]()
            if n < len(chunks):
                chunks[n]()

        col = 0
        for ref, width in ((gq_s, GLA_QK_WIDTH), (gk_s, GLA_QK_WIDTH)):
            ref[...] = rows_proj(col, width).astype(ref.dtype)
            col += width
        for ref in (gv_s, gr_s):
            for s in range(0, GLA_V_WIDTH, 512):
                ref[:, s:s + 512] = rows_proj(col + s, 512).astype(ref.dtype)
            col += GLA_V_WIDTH
        g_lr = rows_proj(IN_ALIGNED, LANES)
        g_hi, g_lo = _split_hi_lo(g_lr)
        w2_hi, w2_lo = _split_hi_lo(w2_ref[...])
        z = (jnp.dot(g_hi, w2_hi, preferred_element_type=F32) + jnp.dot(g_lo, w2_hi, preferred_element_type=F32)
             + jnp.dot(g_hi, w2_lo, preferred_element_type=F32)) + b2_ref[...]
        log_sig = jnp.minimum(z, 0.0) - jnp.log(1.0 + jnp.exp(-jnp.abs(z)))
        la_s[...] = log_sig * (LOG2_E / GLA_TAU)


def _gla_chunks(q_ref, k_ref, v_ref, la_ref, gr_ref, nw_ref, o_ref, state_ref, first_of_seq):
    C = GLA_CHUNK
    H = C // 2
    heads = range(GLA_HEADS)
    rows = lax.broadcasted_iota(jnp.int32, (C, C), 0)
    cols = lax.broadcasted_iota(jnp.int32, (C, C), 1)
    causal = cols <= rows
    tri = causal.astype(BF16)
    same_half = (rows >= H) == (cols >= H)
    upper_rows = lax.broadcasted_iota(jnp.int32, (C, GLA_QK_WIDTH), 0) < H
    q_scale = GLA_DK ** -0.5
    nw = jnp.concatenate([nw_ref[...]] * GLA_HEADS, axis=1)
    nt = (((1,), (1,)), ((), ()))
    ks = lambda a, h: a[:, h * GLA_DK:(h + 1) * GLA_DK]
    vs = lambda a, h: a[:, h * GLA_DV:(h + 1) * GLA_DV]

    def run_chunk(c):
        sl = pl.ds(c * C, C)
        q = q_ref[sl, :].astype(F32) * q_scale
        k = k_ref[sl, :].astype(F32)
        v = v_ref[sl, :]
        la_hi, la_lo = _split_hi_lo(la_ref[sl, :])
        cum = jnp.dot(tri, la_hi, preferred_element_type=F32) + jnp.dot(tri, la_lo, preferred_element_type=F32)
        mid = jnp.where(upper_rows, cum[H // 2 - 1:H // 2, :], cum[H + H // 2 - 1:H + H // 2, :])
        edge = cum[H - 1:H, :]
        last = cum[C - 1:C, :]
        qg = (q * jnp.exp2(cum - mid)).astype(BF16)
        kg = (k * jnp.exp2(mid - cum)).astype(BF16)
        zeros_half = jnp.zeros((H, GLA_QK_WIDTH), BF16)
        q_lo = jnp.concatenate([zeros_half, (q[H:] * jnp.exp2(cum[H:] - edge)).astype(BF16)], axis=0)
        k_up = jnp.concatenate([(k[:H] * jnp.exp2(edge - cum[:H])).astype(BF16), zeros_half], axis=0)
        q_in = (q * jnp.exp2(cum)).astype(BF16)
        k_out = k * jnp.exp2(last - cum)
        decay_row = jnp.broadcast_to(jnp.exp2(last), (8, GLA_QK_WIDTH))
        g = gr_ref[sl, :].astype(F32)
        gate = nw * (g * (1.0 / (1.0 + jnp.exp(-g))))

        within = [lax.dot_general(ks(qg, h), ks(kg, h), nt, preferred_element_type=F32) for h in heads]
        across = [lax.dot_general(ks(q_lo, h), ks(k_up, h), nt, preferred_element_type=F32) for h in heads]
        attn = [jnp.where(same_half, jnp.where(causal, within[h], 0.0), across[h]).astype(BF16) for h in heads]
        if c == 0:
            state = [jnp.where(first_of_seq, 0.0, state_ref[h]) for h in heads]
        else:
            state = [state_ref[h] for h in heads]
        o = [jnp.dot(attn[h], vs(v, h), preferred_element_type=F32)
             + jnp.dot(ks(q_in, h), state[h].astype(BF16), preferred_element_type=F32) for h in heads]
        k_out_t = [jnp.transpose(ks(k_out, h)).astype(BF16) for h in heads]
        decay = [jnp.transpose(ks(decay_row, h))[:, 0:1] for h in heads]
        for h in heads:
            state_ref[h] = decay[h] * state[h] + jnp.dot(k_out_t[h], vs(v, h), preferred_element_type=F32)
        ms = [jnp.mean(o[h] * o[h], axis=-1, keepdims=True) for h in heads]
        for h in heads:
            o_ref[sl, h * GLA_DV:(h + 1) * GLA_DV] = (o[h] * lax.rsqrt(ms[h] + EPS) * vs(gate, h)).astype(o_ref.dtype)

    return [functools.partial(run_chunk, c) for c in range(q_ref.shape[0] // C)]


def _in_proj_gla(x2, n1w, mod, cos, sin, w_in_t, w2, b2, gla_norm_w, B):
    T, D = x2.shape
    tm = TM_IN
    n_tiles = T // tm
    cur = lambda i: (jnp.minimum(i, n_tiles - 1), 0)
    lag = lambda i: (jnp.maximum(i - 1, 0), 0)
    const = lambda i: (0, 0)
    widths = (ATTN_OUT_WIDTH,) * 9 + (D_MODEL, D_MODEL)
    single = dict(pipeline_mode=pl.Buffered(1))
    w2p = jnp.pad(w2, ((0, LANES - GLA_LOWRANK), (0, 0)))
    outs = pl.pallas_call(
        functools.partial(_in_proj_gla_kernel, tiles_per_seq=T // B // tm, n_tiles=n_tiles),
        grid=(n_tiles + 1,),
        in_specs=[pl.BlockSpec((tm, D), cur),
                  pl.BlockSpec((1, D), const),
                  _mod_spec(MOD_SCALE1, D),
                  _mod_spec(MOD_SHIFT1, D),
                  pl.BlockSpec((tm, LANES), cur),
                  pl.BlockSpec((tm, LANES), cur),
                  pl.BlockSpec(w_in_t.shape, const, **single),
                  pl.BlockSpec(w2p.shape, const, **single),
                  pl.BlockSpec(b2.shape, const, **single),
                  pl.BlockSpec(gla_norm_w.shape, const, **single)],
        out_specs=[pl.BlockSpec((tm, GLA_V_WIDTH), lag)] + [pl.BlockSpec((tm, w), cur) for w in widths],
        out_shape=[jax.ShapeDtypeStruct((T, GLA_V_WIDTH), BF16)]
                  + [jax.ShapeDtypeStruct((T, w), BF16) for w in widths],
        scratch_shapes=[pltpu.VMEM((2, tm, LANES), F32),
                        pltpu.VMEM((tm, GLA_QK_WIDTH), BF16),
                        pltpu.VMEM((tm, GLA_QK_WIDTH), BF16),
                        pltpu.VMEM((tm, GLA_V_WIDTH), BF16),
                        pltpu.VMEM((tm, GLA_V_WIDTH), BF16),
                        pltpu.VMEM((tm, GLA_QK_WIDTH), F32),
                        pltpu.VMEM((GLA_HEADS, GLA_DK, GLA_DV), F32)],
        compiler_params=_params("arbitrary"),
        name="in_proj_gla",
    )(x2, n1w, mod, mod, cos, sin, w_in_t, w2p, b2, gla_norm_w)
    return outs[0], outs[1:]


def _attn_kernel(q1_ref, q2_ref, q3_ref, k1c_ref, k1p_ref, k2c_ref, k2p_ref, k3c_ref, k3p_ref,
                 v1c_ref, v1p_ref, v2c_ref, v2p_ref, v3c_ref, v3p_ref, *refs):
    n_cast = ATTN_N_CAST
    o_ref = refs[n_cast]
    slabs = refs[2 * n_cast + 1:]
    for src, dst in zip(refs[:n_cast], refs[n_cast + 1:2 * n_cast + 1]):
        dst[...] = src[...].astype(dst.dtype)

    blk = ATTN_BLK
    tile = TM_IN
    span = ATTN_SPAN
    tiles = span // tile
    r2, r3 = ATTN_GROUPS[1][1], ATTN_GROUPS[2][1]
    per3 = tile // r3
    not_first_span = pl.program_id(1) > 0
    npair = ATTN_OUT_WIDTH // LANES
    o1_s, o2_s, o3_s, l1_s, l2_s, l3_s = [slabs[n * npair:(n + 1) * npair] for n in range(6)]

    rows = lax.broadcasted_iota(jnp.int32, (blk, 2 * blk), 0)
    cols = lax.broadcasted_iota(jnp.int32, (blk, 2 * blk), 1)
    band = (cols >= rows) & (cols <= rows + blk)
    in_cur = cols >= blk
    left = lax.broadcasted_iota(jnp.int32, (blk, LANES), 1) < ATTN_HEAD_DIM
    ones = jnp.ones((2 * blk, LANES), BF16)
    nt = (((1,), (1,)), ((), ()))

    def pair_attend(q_pair, k_cat, v_cat, has_prev):
        zero = jnp.zeros_like(q_pair)
        q2 = jnp.concatenate([jnp.where(left, q_pair, zero), jnp.where(left, zero, q_pair)], axis=0)
        s = lax.dot_general(q2, k_cat, nt, preferred_element_type=F32)
        valid = band & (in_cur | has_prev)
        s = jnp.where(jnp.concatenate([valid, valid], axis=0), s, MASK_VALUE)
        m = jnp.max(s, axis=-1, keepdims=True)
        p = jnp.exp2(s - m).astype(BF16)
        r = jnp.dot(p, jnp.concatenate([v_cat, ones], axis=1), preferred_element_type=F32)
        acc = jnp.where(left, r[:blk, :LANES], r[blk:, :LANES])
        den = jnp.where(left, r[:blk, LANES:], r[blk:, LANES:])
        m_pair = jnp.where(left, jnp.broadcast_to(m[:blk], (blk, LANES)), jnp.broadcast_to(m[blk:], (blk, LANES)))
        return acc / den, m_pair * LN_2 + jnp.log(den)

    def with_prev(cur_ref, prev_ref, r0, back, prev_rows, first, cs):
        before = cur_ref[pl.ds(pl.multiple_of(jnp.maximum(r0 - back, 0), blk), blk), cs]
        if prev_rows is not None:
            before = jnp.where(first, prev_ref[prev_rows, cs], before)
        return jnp.concatenate([before, cur_ref[pl.ds(r0, blk), cs]], axis=0)

    def tile_body(t, carry):
        for p in range(tiles):
            i = t * tiles + p
            r0 = pl.multiple_of(i * blk, blk)
            for j in range(npair):
                cs = slice(j * LANES, (j + 1) * LANES)
                rows1 = slice(0, blk) if p == 0 else None
                o, lse = pair_attend(q1_ref[pl.ds(r0, blk), cs],
                                     with_prev(k1c_ref, k1p_ref, r0, blk, rows1, t == 0, cs),
                                     with_prev(v1c_ref, v1p_ref, r0, blk, rows1, t == 0, cs),
                                     not_first_span | (i > 0))
                o1_s[j][pl.ds(r0, blk), :] = o
                l1_s[j][pl.ds(r0, blk), :] = lse
                rows2 = slice(p * blk, (p + 1) * blk)
                o, lse = pair_attend(q2_ref[pl.ds(r0, blk), cs],
                                     with_prev(k2c_ref, k2p_ref, r0, tile, rows2, t == 0, cs),
                                     with_prev(v2c_ref, v2p_ref, r0, tile, rows2, t == 0, cs),
                                     not_first_span | (t > 0))
                tok = pl.ds(pl.multiple_of(t * tile, tile) + p, blk, stride=r2)
                o2_s[j][tok, :] = o
                l2_s[j][tok, :] = lse
                rr = pl.multiple_of(i * per3, per3)
                gather = lambda ref: jnp.concatenate(
                    [ref[pl.ds(rr + u * tile, per3), cs] for u in range(tiles)], axis=0)
                k_cat = jnp.concatenate([gather(k3p_ref), gather(k3c_ref)], axis=0)
                v_cat = jnp.concatenate([gather(v3p_ref), gather(v3c_ref)], axis=0)
                o, lse = pair_attend(gather(q3_ref), k_cat, v_cat, not_first_span)
                tok = pl.ds(i, blk, stride=r3)
                o3_s[j][tok, :] = o
                l3_s[j][tok, :] = lse
        return carry

    lax.fori_loop(0, tiles, tile_body, 0)

    for j in range(npair):
        for n in range(span // ATTN_MERGE_ROWS):
            rs = slice(n * ATTN_MERGE_ROWS, (n + 1) * ATTN_MERGE_ROWS)
            l1, l2, l3 = l1_s[j][rs, :], l2_s[j][rs, :], l3_s[j][rs, :]
            m = jnp.maximum(jnp.maximum(l1, l2), l3)
            e1, e2, e3 = jnp.exp(l1 - m), jnp.exp(l2 - m), jnp.exp(l3 - m)
            num = e1 * o1_s[j][rs, :] + e2 * o2_s[j][rs, :] + e3 * o3_s[j][rs, :]
            o_ref[rs, j * LANES:(j + 1) * LANES] = (num / (e1 + e2 + e3)).astype(o_ref.dtype)


def _dilated_attention(qs, ks, vs, weights, B, S):
    assert len(weights) == ATTN_N_CAST
    span = ATTN_SPAN
    nsp = S // span
    gw = ATTN_OUT_WIDTH
    T = B * S
    cur = lambda b, s: (b * nsp + s, 0)
    steps = B * nsp
    slab = lambda w: pl.BlockSpec((w.shape[0] // steps, w.shape[1]), cur)

    def prev(rows):
        per = span // rows
        return pl.BlockSpec((rows, gw), lambda b, s: (jnp.maximum((b * nsp + s) * per - 1, 0), 0))

    halos = (ATTN_BLK, TM_IN, span)
    full = pl.BlockSpec((span, gw), cur)
    kv_specs = []
    for h in halos:
        kv_specs += [full, prev(h)]
    kv_args = lambda arrs: [a for arr in arrs for a in (arr, arr)]
    outs = pl.pallas_call(
        _attn_kernel,
        grid=(B, nsp),
        in_specs=[full] * 3 + kv_specs + kv_specs + [slab(w) for w in weights],
        out_specs=[full] + [slab(w) for w in weights],
        out_shape=[jax.ShapeDtypeStruct((T, gw), BF16)] + [jax.ShapeDtypeStruct(w.shape, BF16) for w in weights],
        scratch_shapes=[pltpu.VMEM((span, LANES), F32)] * (6 * (gw // LANES)),
        compiler_params=_params("parallel", "parallel"),
        name="dilated_attn",
    )(*qs, *kv_args(ks), *kv_args(vs), *weights)
    return outs[0], outs[1:]


def _merge_kernel(x_ref, og_ref, oa_ref, ma_ref, mb_ref,
                  gate_ref, scale_ref, shift_ref, n2w_ref, wg_ref, wa_ref, wo_ref, x1_ref, h2_ref, *, tiles_per_seq):
    b = pl.ds(pl.program_id(0) // tiles_per_seq, 1)
    gate, scale, shift = gate_ref[b, :], scale_ref[b, :], shift_ref[b, :]
    sub = x_ref.shape[0] // MERGE_SPLIT
    for r in range(MERGE_SPLIT):
        rs = slice(r * sub, (r + 1) * sub)
        y_attn = jnp.dot(oa_ref[rs, :], wa_ref[...], preferred_element_type=F32)
        y_gla = jnp.dot(og_ref[rs, :], wg_ref[...], preferred_element_type=F32)
        mixed = (ma_ref[rs, :].astype(F32) * y_gla + mb_ref[rs, :].astype(F32) * y_attn).astype(BF16)
        x1 = x_ref[rs, :] + gate * jnp.dot(mixed, wo_ref[...], preferred_element_type=F32)
        x1_ref[rs, :] = x1
        h2_ref[rs, :] = _rmsnorm_mod(x1, n2w_ref[...], scale, shift).astype(h2_ref.dtype)


def _merge(x2, og, oa, ma, mb, mod, n2w, wg, wa, wo, B):
    T, D = x2.shape
    tm = TM_MERGE
    row = lambda i: (i, 0)
    const = lambda i: (0, 0)
    return pl.pallas_call(
        functools.partial(_merge_kernel, tiles_per_seq=T // B // tm),
        grid=(T // tm,),
        in_specs=[pl.BlockSpec((tm, D), row), pl.BlockSpec((tm, GLA_V_WIDTH), row),
                  pl.BlockSpec((tm, ATTN_OUT_WIDTH), row)]
                 + [pl.BlockSpec((tm, D), row)] * 2
                 + [_mod_spec(MOD_GATE1, D), _mod_spec(MOD_SCALE2, D), _mod_spec(MOD_SHIFT2, D)]
                 + [pl.BlockSpec((1, D), const),
                    pl.BlockSpec(wg.shape, const), pl.BlockSpec(wa.shape, const), pl.BlockSpec(wo.shape, const)],
        out_specs=[pl.BlockSpec((tm, D), row)] * 2,
        out_shape=[jax.ShapeDtypeStruct((T, D), F32), jax.ShapeDtypeStruct((T, D), BF16)],
        compiler_params=_params("parallel"),
        name="merge",
    )(x2, og, oa, ma, mb, mod, mod, mod, n2w, wg, wa, wo)


def _ffn_kernel(h_ref, hprev_ref, x1_ref, wup_ref, cw_ref, cb_ref, wd_ref, gate_ref, fw_ref, o_ref,
                hcat_s, u_s, hid_s, order_s, *, tiles_per_seq):
    i = pl.program_id(0)
    tm = h_ref.shape[0]
    halo = hprev_ref.shape[0]
    tf = FFN_CHUNK
    half = tm // 2
    hcat_s[0:halo] = jnp.where(i % tiles_per_seq == 0, jnp.zeros_like(hprev_ref[...]), hprev_ref[...])
    hcat_s[halo:] = h_ref[...]
    nchunk = D_FF // tf
    branches = (0, D_FF)

    def up_project(j):
        for n, off in enumerate(branches):
            u = jnp.dot(hcat_s[...], wup_ref[:, off + j * tf:off + (j + 1) * tf], preferred_element_type=F32)
            for c in range(tf // LANES):
                u_s[j % 2, n, c] = u[:, c * LANES:(c + 1) * LANES]

    def conv(j, n, c, parity):
        u = u_s.at[j % 2, n, c]
        cs = slice(branches[n] + j * tf + c * LANES, branches[n] + j * tf + (c + 1) * LANES)
        tap = lambda back: u[pl.ds(halo + parity - back, half, stride=2), :]
        k = 2.0 ** -0.5
        return (cb_ref[:, cs] * k + (cw_ref[0:1, cs] * k) * tap(2) + (cw_ref[1:2, cs] * k) * tap(1)
                + (cw_ref[2:3, cs] * k) * tap(0))

    up_project(0)
    for j in range(nchunk):
        if j + 1 < nchunk:
            up_project(j + 1)
        for c in range(tf // LANES):
            for parity in range(2):
                t = conv(j, 1, c, parity)
                hidden = (t * (1.0 + lax.erf(t))) * conv(j, 0, c, parity)
                hid_s[parity * half:(parity + 1) * half, j * tf + c * LANES:j * tf + (c + 1) * LANES] = (
                    hidden.astype(hid_s.dtype))

    gate = gate_ref[pl.ds(i // tiles_per_seq, 1), :]
    d_model = o_ref.shape[1]
    sq = jnp.zeros((tm, 1), F32)
    for n in range(d_model // FFN_DOWN_SLAB):
        ns = slice(n * FFN_DOWN_SLAB, (n + 1) * FFN_DOWN_SLAB)
        down = jnp.dot(hid_s[...], wd_ref[:, ns], preferred_element_type=F32)
        for c in range(FFN_DOWN_SLAB // LANES):
            g = n * (FFN_DOWN_SLAB // LANES) + c
            cs = slice(g * LANES, (g + 1) * LANES)
            for parity in range(2):
                order_s[g, pl.ds(parity, half, stride=2), :] = (
                    down[parity * half:(parity + 1) * half, c * LANES:(c + 1) * LANES])
            x2 = x1_ref[:, cs] + gate[:, cs] * order_s[g]
            sq = sq + jnp.sum(x2 * x2, axis=-1, keepdims=True)
            o_ref[:, cs] = x2
    o_ref[...] = o_ref[...] * lax.rsqrt(sq * (1.0 / d_model) + EPS) * fw_ref[...]


def _ffn(h2, x1, w_up, conv_w, conv_b, w_down, mod, final_w, B):
    T, D = x1.shape
    tm, tf, halo = TM_FFN, FFN_CHUNK, FFN_HALO
    row = lambda i: (i, 0)
    const = lambda i: (0, 0)
    single = dict(pipeline_mode=pl.Buffered(1))
    return pl.pallas_call(
        functools.partial(_ffn_kernel, tiles_per_seq=T // B // tm),
        grid=(T // tm,),
        in_specs=[pl.BlockSpec((tm, D), row),
                  pl.BlockSpec((halo, D), lambda i: (jnp.maximum(i * (tm // halo) - 1, 0), 0)),
                  pl.BlockSpec((tm, D), row),
                  pl.BlockSpec(w_up.shape, const, **single),
                  pl.BlockSpec(conv_w.shape, const, **single),
                  pl.BlockSpec(conv_b.shape, const, **single),
                  pl.BlockSpec(w_down.shape, const, **single),
                  _mod_spec(MOD_GATE2, D),
                  pl.BlockSpec((1, D), const)],
        out_specs=pl.BlockSpec((tm, D), row),
        out_shape=jax.ShapeDtypeStruct((T, D), F32),
        scratch_shapes=[pltpu.VMEM((halo + tm, D), BF16),
                        pltpu.VMEM((2, 2, tf // LANES, halo + tm, LANES), F32),
                        pltpu.VMEM((tm, D_FF), BF16),
                        pltpu.VMEM((D // LANES, tm, LANES), F32)],
        compiler_params=_params("parallel"),
        name="ffn",
    )(h2, h2, x1, w_up, conv_w, conv_b, w_down, mod, final_w)


def kernel(x, c, positions, ada_w, ada_b, norm1_w, w_in, gla_gate_w2, gla_gate_b, gla_norm_w, w_gla_branch,
           w_attn_branch, w_out, norm2_w, w_up, conv_w, conv_b, w_down, final_norm_w):
    B, S, D = x.shape
    T = B * S
    depth = ada_w.shape[0]
    assert depth == 1, "the final norm is fused into the (single) layer's ffn"
    assert all(window // dilation == ATTN_BLK for window, dilation in ATTN_GROUPS)
    assert conv_w.shape[1] == CONV_WIDTH == 3, "the ffn conv reads exactly three taps"
    assert S % ATTN_SPAN == 0 and S % max(TM_IN, TM_MERGE, TM_FFN) == 0 and T % (LANES // (ATTN_HEAD_DIM // 2)) == 0
    x2 = x.reshape(T, D)
    for layer in range(depth):
        cos, sin, mod, w_in_bf = _prep(positions, c, ada_w[layer], ada_b[layer], w_in[layer].T)

        og, (q1, q2, q3, k1, k2, k3, v1, v2, v3, ma, mb) = _in_proj_gla(
            x2, norm1_w[layer].reshape(1, D), mod, cos, sin, w_in_bf, gla_gate_w2[layer],
            gla_gate_b[layer].reshape(1, -1), gla_norm_w[layer].reshape(1, -1), B)
        oa, (w_up_bf, w_down_bf, w_gla_bf, w_attn_bf, w_out_bf) = _dilated_attention(
            (q1, q2, q3), (k1, k2, k3), (v1, v2, v3),
            (w_up[layer], w_down[layer], w_gla_branch[layer], w_attn_branch[layer], w_out[layer]), B, S)

        x1, h2 = _merge(x2, og, oa, ma, mb, mod, norm2_w[layer].reshape(1, D), w_gla_bf, w_attn_bf, w_out_bf, B)

        x2 = _ffn(h2, x1, w_up_bf, conv_w[layer], conv_b[layer].reshape(1, -1),
                  w_down_bf, mod, final_norm_w.reshape(1, D), B)
    return x2.reshape(B, S, D)
```

```python
import functools

import jax
import jax.numpy as jnp
from jax import lax
from jax.experimental import pallas as pl
from jax.experimental.pallas import tpu as pltpu

F32 = jnp.float32
BF16 = jnp.bfloat16

D_MODEL = 1024
GLA_HEADS = 4
GLA_DK = 128
GLA_DV = 256
GLA_LOWRANK = 16
GLA_TAU = 16.0
GLA_QK_WIDTH = GLA_HEADS * GLA_DK
GLA_V_WIDTH = GLA_HEADS * GLA_DV
ATTN_GROUPS = ((128, 1), (512, 4), (2048, 16))
ATTN_HEADS_PER_GROUP = 4
ATTN_HEAD_DIM = 64
ATTN_WIDTH = ATTN_HEADS_PER_GROUP * len(ATTN_GROUPS) * ATTN_HEAD_DIM
ATTN_OUT_WIDTH = ATTN_HEADS_PER_GROUP * ATTN_HEAD_DIM
ROPE_THETA = 10000.0
D_FF = 2816
CONV_WIDTH = 3
EPS = 1e-6
IN_WIDTHS = (GLA_QK_WIDTH, GLA_QK_WIDTH, GLA_V_WIDTH, GLA_V_WIDTH, GLA_LOWRANK,
             ATTN_WIDTH, ATTN_WIDTH, ATTN_WIDTH, D_MODEL, D_MODEL)

LANES = 128
VMEM_LIMIT_BYTES = 56 * 1024 * 1024

PREP_ROWS = 512
TM_IN = 512
PERM_STRIDE = 4
GLA_CHUNK = 64
ATTN_BLK = 128
ATTN_SPAN = 2048
ATTN_MERGE_ROWS = 256
ATTN_N_CAST = 5
TM_MERGE = 1024
MERGE_SPLIT = 2
TM_FFN = 512
FFN_CHUNK = 256
FFN_HALO = 16
FFN_DOWN_SLAB = 256
MASK_VALUE = float("-inf")
LOG2_E = 1.4426950408889634
LN_2 = 0.6931471805599453


def _params(*sem):
    return pltpu.CompilerParams(dimension_semantics=sem, vmem_limit_bytes=VMEM_LIMIT_BYTES)


def _split_hi_lo(a):
    hi = a.astype(BF16)
    lo = (a - hi.astype(F32)).astype(BF16)
    return hi, lo


def _mod_spec(which, D):
    return pl.BlockSpec((8, D), lambda i: (0, which))


MOD_SHIFT1, MOD_SCALE1, MOD_GATE1, MOD_SHIFT2, MOD_SCALE2, MOD_GATE2 = range(6)


def _prep_kernel(pos_ref, invf_ref, c_ref, adaw_ref, adab_ref, wt_ref, cos_ref, sin_ref, mod_ref, wt_bf_ref):
    c = c_ref[...]
    s = c * (1.0 / (1.0 + jnp.exp(-c)))
    s_hi, s_lo = _split_hi_lo(s)
    lhs = jnp.concatenate([s_hi, s_lo], axis=0)
    w_hi, w_lo = _split_hi_lo(adaw_ref[...])
    acc = jnp.dot(lhs, w_hi, preferred_element_type=F32) + jnp.dot(lhs, w_lo, preferred_element_type=F32)
    mod_ref[...] = acc[0:8] + acc[8:16] + adab_ref[...]

    wt_bf_ref[...] = wt_ref[...].astype(wt_bf_ref.dtype)

    half = ATTN_HEAD_DIM // 2
    groups = LANES // half
    tr = pos_ref.shape[1]
    pos = jnp.concatenate([pos_ref[...].astype(F32), jnp.zeros((8 - groups, tr), F32)], axis=0)
    pos_t = jnp.transpose(pos)
    lane = lax.broadcasted_iota(jnp.int32, (tr, LANES), 1)
    group = lane // half
    pos_dense = jnp.zeros((tr, LANES), F32)
    for q in range(groups):
        pos_dense = jnp.where(group == q, pos_t[:, q:q + 1], pos_dense)
    ang = pos_dense * invf_ref[...]
    first_half = (lane % ATTN_HEAD_DIM) < half
    for table, out_ref, signed in ((jnp.cos(ang), cos_ref, False), (jnp.sin(ang), sin_ref, True)):
        for q in range(groups):
            spread = jnp.where(group == q, table, 0.0)
            shift = groups // 2
            while shift >= 1:
                spread = spread + pltpu.roll(spread, shift * half, axis=1)
                shift //= 2
            out_ref[q] = jnp.where(first_half, -spread, spread) if signed else spread


def _prep(positions, c, ada_w, ada_b, w_in_t):
    T = positions.size
    B, D = c.shape
    N = ada_w.shape[1]
    half = ATTN_HEAD_DIM // 2
    groups = LANES // half
    inv_freq = ROPE_THETA ** (-jnp.arange(half, dtype=F32) / half)
    invf = jnp.tile(inv_freq, groups).reshape(1, LANES)
    per = T // groups
    tr = PREP_ROWS
    steps = per // tr
    tn = N // steps
    bf16_rows = 16
    wt_slab = pl.BlockSpec((-(-w_in_t.shape[0] // (steps * bf16_rows)) * bf16_rows, w_in_t.shape[1]), lambda i: (i, 0))
    c8 = jnp.pad(c, ((0, 8 - B), (0, 0)))
    cos, sin, mod, wt_bf = pl.pallas_call(
        _prep_kernel,
        grid=(steps,),
        in_specs=[pl.BlockSpec((groups, tr), lambda i: (0, i)),
                  pl.BlockSpec((1, LANES), lambda i: (0, 0)),
                  pl.BlockSpec((8, D), lambda i: (0, 0)),
                  pl.BlockSpec((D, tn), lambda i: (0, i)),
                  pl.BlockSpec((1, tn), lambda i: (0, i)),
                  wt_slab],
        out_specs=[pl.BlockSpec((groups, tr, LANES), lambda i: (0, i, 0))] * 2
                  + [pl.BlockSpec((8, tn), lambda i: (0, i)), wt_slab],
        out_shape=[jax.ShapeDtypeStruct((groups, per, LANES), F32)] * 2
                  + [jax.ShapeDtypeStruct((8, N), F32), jax.ShapeDtypeStruct(w_in_t.shape, BF16)],
        compiler_params=_params("parallel"),
        name="prep",
    )(positions.reshape(groups, per), invf, c8, ada_w, ada_b.reshape(1, N), w_in_t)
    return cos.reshape(T, LANES), sin.reshape(T, LANES), mod, wt_bf


def _rmsnorm_mod(x, w, scale, shift):
    ms = jnp.mean(x * x, axis=-1, keepdims=True)
    return (x * lax.rsqrt(ms + EPS)) * (w * (1.0 + scale)) + shift


def _rotate_half_pairs(t, cos, sin_signed):
    lane = lax.broadcasted_iota(jnp.int32, t.shape, 1)
    first_half = (lane % ATTN_HEAD_DIM) < (ATTN_HEAD_DIM // 2)
    from_right = pltpu.roll(t, LANES - ATTN_HEAD_DIM // 2, axis=1)
    from_left = pltpu.roll(t, ATTN_HEAD_DIM // 2, axis=1)
    return t * cos + jnp.where(first_half, from_right, from_left) * sin_signed


def _store_residue_major(ref, perm_ref, slab, col, dilation):
    if dilation == 1:
        ref[:, col:col + LANES] = slab.astype(ref.dtype)
        return
    tm = slab.shape[0]
    n = tm // dilation
    perm_ref[0] = slab
    if dilation <= PERM_STRIDE:
        for p in range(dilation):
            ref[p * n:(p + 1) * n, col:col + LANES] = perm_ref[0, pl.ds(p, n, stride=dilation), :].astype(ref.dtype)
        return
    f = PERM_STRIDE
    g = dilation // f
    rows = tm // f
    for a in range(f):
        perm_ref[1, a * rows:(a + 1) * rows, :] = perm_ref[0, pl.ds(a, rows, stride=f), :]
    for a in range(f):
        for q in range(g):
            p = q * f + a
            ref[p * n:(p + 1) * n, col:col + LANES] = (
                perm_ref[1, pl.ds(a * rows + q, n, stride=g), :].astype(ref.dtype))


IN_ALIGNED = sum(IN_WIDTHS[:4])
IN_SHIFTED_START = IN_ALIGNED + GLA_LOWRANK


def _in_proj_gla_kernel(x_ref, n1w_ref, scale_ref, shift_ref, cos_ref, sin_ref, wt_ref, w2_ref, b2_ref, nw_ref,
                        og_ref, qkv1_ref, qkv2_ref, qkv3_ref, mab_ref,
                        perm_ref, gq_s, gk_s, gv_s, gr_s, la_s, state_ref,
                        *, tiles_per_seq, n_tiles):
    i = pl.program_id(0)
    nt = (((1,), (1,)), ((), ()))
    gw = ATTN_OUT_WIDTH
    k1_ref, k2_ref, k3_ref = (r.at[:, 0:gw] for r in (qkv1_ref, qkv2_ref, qkv3_ref))
    v1_ref, v2_ref, v3_ref = (r.at[:, gw:2 * gw] for r in (qkv1_ref, qkv2_ref, qkv3_ref))
    q1_ref, q2_ref, q3_ref = (r.at[:, 2 * gw:3 * gw] for r in (qkv1_ref, qkv2_ref, qkv3_ref))
    ma_ref, mb_ref = mab_ref.at[:, 0:D_MODEL], mab_ref.at[:, D_MODEL:2 * D_MODEL]

    @pl.when(i == 0)
    def _():
        for ref in (gq_s, gk_s, gv_s, gr_s, la_s, state_ref):
            ref[...] = jnp.zeros_like(ref)

    first_of_seq = (i + tiles_per_seq - 1) % tiles_per_seq == 0
    chunks = _gla_chunks(gq_s, gk_s, gv_s, la_s, gr_s, nw_ref, og_ref, state_ref, first_of_seq)

    @pl.when(i == n_tiles)
    def _():
        for chunk in chunks:
            chunk()

    @pl.when(i < n_tiles)
    def _():
        b = i // tiles_per_seq
        h = _rmsnorm_mod(x_ref[...], n1w_ref[...], scale_ref[pl.ds(b, 1), :], shift_ref[pl.ds(b, 1), :]).astype(BF16)

        def rows_proj(row, width):
            return lax.dot_general(h, wt_ref[row:row + width, :], nt, preferred_element_type=F32)

        def proj(col, width):
            return rows_proj(IN_SHIFTED_START + col, width)

        cos = cos_ref[...]
        sin = sin_ref[...]
        q_scale = ATTN_HEAD_DIM ** -0.5 * LOG2_E
        dilations = [d for _, d in ATTN_GROUPS]

        def rope_piece(ref, col, scale, dilation):
            def run():
                t = proj(col, ATTN_OUT_WIDTH)
                for s in range(0, ATTN_OUT_WIDTH, LANES):
                    rot = _rotate_half_pairs(t[:, s:s + LANES], cos, sin) * scale
                    _store_residue_major(ref, perm_ref, rot, s, dilation)
            return run

        def value_piece(ref, col, dilation):
            def run():
                t = proj(col, ATTN_OUT_WIDTH)
                for s in range(0, ATTN_OUT_WIDTH, LANES):
                    _store_residue_major(ref, perm_ref, t[:, s:s + LANES], s, dilation)
            return run

        def gate_piece(ref, col, s):
            def run():
                z = proj(col + s, 512)
                ref[:, s:s + 512] = (1.0 / (1.0 + jnp.exp(-z))).astype(ref.dtype)
            return run

        pieces = []
        col = 0
        for refs, scale in (((q1_ref, q2_ref, q3_ref), q_scale), ((k1_ref, k2_ref, k3_ref), 1.0)):
            for g, ref in enumerate(refs):
                pieces.append(rope_piece(ref, col, scale, dilations[g]))
                col += ATTN_OUT_WIDTH
        for g, ref in enumerate((v1_ref, v2_ref, v3_ref)):
            pieces.append(value_piece(ref, col, dilations[g]))
            col += ATTN_OUT_WIDTH
        for ref in (ma_ref, mb_ref):
            for s in range(0, D_MODEL, 512):
                pieces.append(gate_piece(ref, col, s))
            col += D_MODEL

        for n in range(max(len(pieces), len(chunks))):
            if n < len(pieces):
                pieces[n]()
            if n < len(chunks):
                chunks[n]()

        col = 0
        for ref, width in ((gq_s, GLA_QK_WIDTH), (gk_s, GLA_QK_WIDTH)):
            ref[...] = rows_proj(col, width).astype(ref.dtype)
            col += width
        for ref in (gv_s, gr_s):
            for s in range(0, GLA_V_WIDTH, 512):
                ref[:, s:s + 512] = rows_proj(col + s, 512).astype(ref.dtype)
            col += GLA_V_WIDTH
        g_lr = rows_proj(IN_ALIGNED, LANES)
        g_hi, g_lo = _split_hi_lo(g_lr)
        w2_hi, w2_lo = _split_hi_lo(w2_ref[...])
        z = (jnp.dot(g_hi, w2_hi, preferred_element_type=F32) + jnp.dot(g_lo, w2_hi, preferred_element_type=F32)
             + jnp.dot(g_hi, w2_lo, preferred_element_type=F32)) + b2_ref[...]
        log_sig = jnp.minimum(z, 0.0) - jnp.log(1.0 + jnp.exp(-jnp.abs(z)))
        la_s[...] = log_sig * (LOG2_E / GLA_TAU)


def _gla_chunks(q_ref, k_ref, v_ref, la_ref, gr_ref, nw_ref, o_ref, state_ref, first_of_seq):
    C = GLA_CHUNK
    H = C // 2
    heads = range(GLA_HEADS)
    rows = lax.broadcasted_iota(jnp.int32, (C, C), 0)
    cols = lax.broadcasted_iota(jnp.int32, (C, C), 1)
    causal = cols <= rows
    tri = causal.astype(BF16)
    same_half = (rows >= H) == (cols >= H)
    upper_rows = lax.broadcasted_iota(jnp.int32, (C, GLA_QK_WIDTH), 0) < H
    q_scale = GLA_DK ** -0.5
    nw = jnp.concatenate([nw_ref[...]] * GLA_HEADS, axis=1)
    nt = (((1,), (1,)), ((), ()))
    ks = lambda a, h: a[:, h * GLA_DK:(h + 1) * GLA_DK]
    vs = lambda a, h: a[:, h * GLA_DV:(h + 1) * GLA_DV]

    def run_chunk(c):
        sl = pl.ds(c * C, C)
        q = q_ref[sl, :].astype(F32) * q_scale
        k = k_ref[sl, :].astype(F32)
        v = v_ref[sl, :]
        la_hi, la_lo = _split_hi_lo(la_ref[sl, :])
        cum = jnp.dot(tri, la_hi, preferred_element_type=F32) + jnp.dot(tri, la_lo, preferred_element_type=F32)
        mid = jnp.where(upper_rows, cum[H // 2 - 1:H // 2, :], cum[H + H // 2 - 1:H + H // 2, :])
        edge = cum[H - 1:H, :]
        last = cum[C - 1:C, :]
        qg = (q * jnp.exp2(cum - mid)).astype(BF16)
        kg = (k * jnp.exp2(mid - cum)).astype(BF16)
        zeros_half = jnp.zeros((H, GLA_QK_WIDTH), BF16)
        q_lo = jnp.concatenate([zeros_half, (q[H:] * jnp.exp2(cum[H:] - edge)).astype(BF16)], axis=0)
        k_up = jnp.concatenate([(k[:H] * jnp.exp2(edge - cum[:H])).astype(BF16), zeros_half], axis=0)
        q_in = (q * jnp.exp2(cum)).astype(BF16)
        k_out = k * jnp.exp2(last - cum)
        decay_row = jnp.broadcast_to(jnp.exp2(last), (8, GLA_QK_WIDTH))
        g = gr_ref[sl, :].astype(F32)
        gate = nw * (g * (1.0 / (1.0 + jnp.exp(-g))))

        within = [lax.dot_general(ks(qg, h), ks(kg, h), nt, preferred_element_type=F32) for h in heads]
        across = [lax.dot_general(ks(q_lo, h), ks(k_up, h), nt, preferred_element_type=F32) for h in heads]
        attn = [jnp.where(same_half, jnp.where(causal, within[h], 0.0), across[h]).astype(BF16) for h in heads]
        if c == 0:
            state = [jnp.where(first_of_seq, 0.0, state_ref[h]) for h in heads]
        else:
            state = [state_ref[h] for h in heads]
        o = [jnp.dot(attn[h], vs(v, h), preferred_element_type=F32)
             + jnp.dot(ks(q_in, h), state[h].astype(BF16), preferred_element_type=F32) for h in heads]
        k_out_t = [jnp.transpose(ks(k_out, h)).astype(BF16) for h in heads]
        decay = [jnp.transpose(ks(decay_row, h))[:, 0:1] for h in heads]
        for h in heads:
            state_ref[h] = decay[h] * state[h] + jnp.dot(k_out_t[h], vs(v, h), preferred_element_type=F32)
        ms = [jnp.mean(o[h] * o[h], axis=-1, keepdims=True) for h in heads]
        for h in heads:
            o_ref[sl, h * GLA_DV:(h + 1) * GLA_DV] = (o[h] * lax.rsqrt(ms[h] + EPS) * vs(gate, h)).astype(o_ref.dtype)

    return [functools.partial(run_chunk, c) for c in range(q_ref.shape[0] // C)]


def _in_proj_gla(x2, n1w, mod, cos, sin, w_in_t, w2, b2, gla_norm_w, B):
    T, D = x2.shape
    tm = TM_IN
    n_tiles = T // tm
    cur = lambda i: (jnp.minimum(i, n_tiles - 1), 0)
    lag = lambda i: (jnp.maximum(i - 1, 0), 0)
    const = lambda i: (0, 0)
    widths = (3 * ATTN_OUT_WIDTH,) * len(ATTN_GROUPS) + (2 * D_MODEL,)
    single = dict(pipeline_mode=pl.Buffered(1))
    w2p = jnp.pad(w2, ((0, LANES - GLA_LOWRANK), (0, 0)))
    outs = pl.pallas_call(
        functools.partial(_in_proj_gla_kernel, tiles_per_seq=T // B // tm, n_tiles=n_tiles),
        grid=(n_tiles + 1,),
        in_specs=[pl.BlockSpec((tm, D), cur),
                  pl.BlockSpec((1, D), const),
                  _mod_spec(MOD_SCALE1, D),
                  _mod_spec(MOD_SHIFT1, D),
                  pl.BlockSpec((tm, LANES), cur),
                  pl.BlockSpec((tm, LANES), cur),
                  pl.BlockSpec(w_in_t.shape, const, **single),
                  pl.BlockSpec(w2p.shape, const, **single),
                  pl.BlockSpec(b2.shape, const, **single),
                  pl.BlockSpec(gla_norm_w.shape, const, **single)],
        out_specs=[pl.BlockSpec((tm, GLA_V_WIDTH), lag)] + [pl.BlockSpec((tm, w), cur) for w in widths],
        out_shape=[jax.ShapeDtypeStruct((T, GLA_V_WIDTH), BF16)]
                  + [jax.ShapeDtypeStruct((T, w), BF16) for w in widths],
        scratch_shapes=[pltpu.VMEM((2, tm, LANES), F32),
                        pltpu.VMEM((tm, GLA_QK_WIDTH), BF16),
                        pltpu.VMEM((tm, GLA_QK_WIDTH), BF16),
                        pltpu.VMEM((tm, GLA_V_WIDTH), BF16),
                        pltpu.VMEM((tm, GLA_V_WIDTH), BF16),
                        pltpu.VMEM((tm, GLA_QK_WIDTH), F32),
                        pltpu.VMEM((GLA_HEADS, GLA_DK, GLA_DV), F32)],
        compiler_params=_params("arbitrary"),
        name="in_proj_gla",
    )(x2, n1w, mod, mod, cos, sin, w_in_t, w2p, b2, gla_norm_w)
    return outs[0], outs[1:]


def _attn_kernel(g1c_ref, g1p_ref, g2c_ref, g2p_ref, g3c_ref, g3p_ref, *refs):
    n_cast = ATTN_N_CAST
    o_ref = refs[n_cast]
    slabs = refs[2 * n_cast + 1:]
    for src, dst in zip(refs[:n_cast], refs[n_cast + 1:2 * n_cast + 1]):
        dst[...] = src[...].astype(dst.dtype)

    gw = ATTN_OUT_WIDTH
    k1c_ref, k2c_ref, k3c_ref = (r.at[:, 0:gw] for r in (g1c_ref, g2c_ref, g3c_ref))
    v1c_ref, v2c_ref, v3c_ref = (r.at[:, gw:2 * gw] for r in (g1c_ref, g2c_ref, g3c_ref))
    q1_ref, q2_ref, q3_ref = (r.at[:, 2 * gw:3 * gw] for r in (g1c_ref, g2c_ref, g3c_ref))
    k1p_ref, k2p_ref, k3p_ref = (r.at[:, 0:gw] for r in (g1p_ref, g2p_ref, g3p_ref))
    v1p_ref, v2p_ref, v3p_ref = (r.at[:, gw:2 * gw] for r in (g1p_ref, g2p_ref, g3p_ref))

    blk = ATTN_BLK
    tile = TM_IN
    span = ATTN_SPAN
    tiles = span // tile
    r2, r3 = ATTN_GROUPS[1][1], ATTN_GROUPS[2][1]
    per3 = tile // r3
    not_first_span = pl.program_id(1) > 0
    npair = ATTN_OUT_WIDTH // LANES
    o1_s, o2_s, o3_s, l1_s, l2_s, l3_s = [slabs[n * npair:(n + 1) * npair] for n in range(6)]

    rows = lax.broadcasted_iota(jnp.int32, (blk, 2 * blk), 0)
    cols = lax.broadcasted_iota(jnp.int32, (blk, 2 * blk), 1)
    band = (cols >= rows) & (cols <= rows + blk)
    in_cur = cols >= blk
    left = lax.broadcasted_iota(jnp.int32, (blk, LANES), 1) < ATTN_HEAD_DIM
    ones = jnp.ones((2 * blk, LANES), BF16)
    nt = (((1,), (1,)), ((), ()))

    def pair_attend(q_pair, k_cat, v_cat, has_prev):
        zero = jnp.zeros_like(q_pair)
        q2 = jnp.concatenate([jnp.where(left, q_pair, zero), jnp.where(left, zero, q_pair)], axis=0)
        s = lax.dot_general(q2, k_cat, nt, preferred_element_type=F32)
        valid = band & (in_cur | has_prev)
        s = jnp.where(jnp.concatenate([valid, valid], axis=0), s, MASK_VALUE)
        m = jnp.max(s, axis=-1, keepdims=True)
        p = jnp.exp2(s - m).astype(BF16)
        r = jnp.dot(p, jnp.concatenate([v_cat, ones], axis=1), preferred_element_type=F32)
        acc = jnp.where(left, r[:blk, :LANES], r[blk:, :LANES])
        den = jnp.where(left, r[:blk, LANES:], r[blk:, LANES:])
        m_pair = jnp.where(left, jnp.broadcast_to(m[:blk], (blk, LANES)), jnp.broadcast_to(m[blk:], (blk, LANES)))
        return acc / den, m_pair * LN_2 + jnp.log(den)

    def with_prev(cur_ref, prev_ref, r0, back, prev_rows, first, cs):
        before = cur_ref[pl.ds(pl.multiple_of(jnp.maximum(r0 - back, 0), blk), blk), cs]
        if prev_rows is not None:
            before = jnp.where(first, prev_ref[prev_rows, cs], before)
        return jnp.concatenate([before, cur_ref[pl.ds(r0, blk), cs]], axis=0)

    def tile_body(t, carry):
        for p in range(tiles):
            i = t * tiles + p
            r0 = pl.multiple_of(i * blk, blk)
            for j in range(npair):
                cs = slice(j * LANES, (j + 1) * LANES)
                rows1 = slice(0, blk) if p == 0 else None
                o, lse = pair_attend(q1_ref[pl.ds(r0, blk), cs],
                                     with_prev(k1c_ref, k1p_ref, r0, blk, rows1, t == 0, cs),
                                     with_prev(v1c_ref, v1p_ref, r0, blk, rows1, t == 0, cs),
                                     not_first_span | (i > 0))
                o1_s[j][pl.ds(r0, blk), :] = o
                l1_s[j][pl.ds(r0, blk), :] = lse
                rows2 = slice(p * blk, (p + 1) * blk)
                o, lse = pair_attend(q2_ref[pl.ds(r0, blk), cs],
                                     with_prev(k2c_ref, k2p_ref, r0, tile, rows2, t == 0, cs),
                                     with_prev(v2c_ref, v2p_ref, r0, tile, rows2, t == 0, cs),
                                     not_first_span | (t > 0))
                tok = pl.ds(pl.multiple_of(t * tile, tile) + p, blk, stride=r2)
                o2_s[j][tok, :] = o
                l2_s[j][tok, :] = lse
                rr = pl.multiple_of(i * per3, per3)
                gather = lambda ref: jnp.concatenate(
                    [ref[pl.ds(rr + u * tile, per3), cs] for u in range(tiles)], axis=0)
                k_cat = jnp.concatenate([gather(k3p_ref), gather(k3c_ref)], axis=0)
                v_cat = jnp.concatenate([gather(v3p_ref), gather(v3c_ref)], axis=0)
                o, lse = pair_attend(gather(q3_ref), k_cat, v_cat, not_first_span)
                tok = pl.ds(i, blk, stride=r3)
                o3_s[j][tok, :] = o
                l3_s[j][tok, :] = lse
        return carry

    lax.fori_loop(0, tiles, tile_body, 0)

    for j in range(npair):
        for n in range(span // ATTN_MERGE_ROWS):
            rs = slice(n * ATTN_MERGE_ROWS, (n + 1) * ATTN_MERGE_ROWS)
            l1, l2, l3 = l1_s[j][rs, :], l2_s[j][rs, :], l3_s[j][rs, :]
            m = jnp.maximum(jnp.maximum(l1, l2), l3)
            e1, e2, e3 = jnp.exp(l1 - m), jnp.exp(l2 - m), jnp.exp(l3 - m)
            num = e1 * o1_s[j][rs, :] + e2 * o2_s[j][rs, :] + e3 * o3_s[j][rs, :]
            o_ref[rs, j * LANES:(j + 1) * LANES] = (num / (e1 + e2 + e3)).astype(o_ref.dtype)


def _dilated_attention(qkvs, weights, B, S):
    assert len(weights) == ATTN_N_CAST
    span = ATTN_SPAN
    nsp = S // span
    gw = ATTN_OUT_WIDTH
    T = B * S
    cur = lambda b, s: (b * nsp + s, 0)
    steps = B * nsp
    slab = lambda w: pl.BlockSpec((w.shape[0] // steps, w.shape[1]), cur)

    def prev(rows):
        per = span // rows
        return pl.BlockSpec((rows, 2 * gw), lambda b, s: (jnp.maximum((b * nsp + s) * per - 1, 0), 0))

    halos = (ATTN_BLK, TM_IN, span)
    full = pl.BlockSpec((span, gw), cur)
    group_specs = []
    for h in halos:
        group_specs += [pl.BlockSpec((span, 3 * gw), cur), prev(h)]
    outs = pl.pallas_call(
        _attn_kernel,
        grid=(B, nsp),
        in_specs=group_specs + [slab(w) for w in weights],
        out_specs=[full] + [slab(w) for w in weights],
        out_shape=[jax.ShapeDtypeStruct((T, gw), BF16)] + [jax.ShapeDtypeStruct(w.shape, BF16) for w in weights],
        scratch_shapes=[pltpu.VMEM((span, LANES), F32)] * (6 * (gw // LANES)),
        compiler_params=_params("parallel", "parallel"),
        name="dilated_attn",
    )(*[a for g in qkvs for a in (g, g)], *weights)
    return outs[0], outs[1:]


def _merge_kernel(x_ref, og_ref, oa_ref, mab_ref,
                  gate_ref, scale_ref, shift_ref, n2w_ref, wg_ref, wa_ref, wo_ref, x1_ref, h2_ref, *, tiles_per_seq):
    ma_ref, mb_ref = mab_ref.at[:, 0:D_MODEL], mab_ref.at[:, D_MODEL:2 * D_MODEL]
    b = pl.ds(pl.program_id(0) // tiles_per_seq, 1)
    gate, scale, shift = gate_ref[b, :], scale_ref[b, :], shift_ref[b, :]
    sub = x_ref.shape[0] // MERGE_SPLIT
    for r in range(MERGE_SPLIT):
        rs = slice(r * sub, (r + 1) * sub)
        y_attn = jnp.dot(oa_ref[rs, :], wa_ref[...], preferred_element_type=F32)
        y_gla = jnp.dot(og_ref[rs, :], wg_ref[...], preferred_element_type=F32)
        mixed = (ma_ref[rs, :].astype(F32) * y_gla + mb_ref[rs, :].astype(F32) * y_attn).astype(BF16)
        x1 = x_ref[rs, :] + gate * jnp.dot(mixed, wo_ref[...], preferred_element_type=F32)
        x1_ref[rs, :] = x1
        h2_ref[rs, :] = _rmsnorm_mod(x1, n2w_ref[...], scale, shift).astype(h2_ref.dtype)


def _merge(x2, og, oa, mab, mod, n2w, wg, wa, wo, B):
    T, D = x2.shape
    tm = TM_MERGE
    row = lambda i: (i, 0)
    const = lambda i: (0, 0)
    return pl.pallas_call(
        functools.partial(_merge_kernel, tiles_per_seq=T // B // tm),
        grid=(T // tm,),
        in_specs=[pl.BlockSpec((tm, D), row), pl.BlockSpec((tm, GLA_V_WIDTH), row),
                  pl.BlockSpec((tm, ATTN_OUT_WIDTH), row), pl.BlockSpec((tm, 2 * D), row)]
                 + [_mod_spec(MOD_GATE1, D), _mod_spec(MOD_SCALE2, D), _mod_spec(MOD_SHIFT2, D)]
                 + [pl.BlockSpec((1, D), const),
                    pl.BlockSpec(wg.shape, const), pl.BlockSpec(wa.shape, const), pl.BlockSpec(wo.shape, const)],
        out_specs=[pl.BlockSpec((tm, D), row)] * 2,
        out_shape=[jax.ShapeDtypeStruct((T, D), F32), jax.ShapeDtypeStruct((T, D), BF16)],
        compiler_params=_params("parallel"),
        name="merge",
    )(x2, og, oa, mab, mod, mod, mod, n2w, wg, wa, wo)


def _ffn_kernel(h_ref, hprev_ref, x1_ref, wup_ref, cw_ref, cb_ref, wd_ref, gate_ref, fw_ref, o_ref,
                hcat_s, u_s, hid_s, order_s, *, tiles_per_seq):
    i = pl.program_id(0)
    tm = h_ref.shape[0]
    halo = hprev_ref.shape[0]
    tf = FFN_CHUNK
    half = tm // 2
    hcat_s[0:halo] = jnp.where(i % tiles_per_seq == 0, jnp.zeros_like(hprev_ref[...]), hprev_ref[...])
    hcat_s[halo:] = h_ref[...]
    nchunk = D_FF // tf
    branches = (0, D_FF)

    def up_project(j):
        for n, off in enumerate(branches):
            u = jnp.dot(hcat_s[...], wup_ref[:, off + j * tf:off + (j + 1) * tf], preferred_element_type=F32)
            for c in range(tf // LANES):
                u_s[j % 2, n, c] = u[:, c * LANES:(c + 1) * LANES]

    def conv(j, n, c, parity):
        u = u_s.at[j % 2, n, c]
        cs = slice(branches[n] + j * tf + c * LANES, branches[n] + j * tf + (c + 1) * LANES)
        tap = lambda back: u[pl.ds(halo + parity - back, half, stride=2), :]
        k = 2.0 ** -0.5
        return (cb_ref[:, cs] * k + (cw_ref[0:1, cs] * k) * tap(2) + (cw_ref[1:2, cs] * k) * tap(1)
                + (cw_ref[2:3, cs] * k) * tap(0))

    up_project(0)
    for j in range(nchunk):
        if j + 1 < nchunk:
            up_project(j + 1)
        for c in range(tf // LANES):
            for parity in range(2):
                t = conv(j, 1, c, parity)
                hidden = (t * (1.0 + lax.erf(t))) * conv(j, 0, c, parity)
                hid_s[parity * half:(parity + 1) * half, j * tf + c * LANES:j * tf + (c + 1) * LANES] = (
                    hidden.astype(hid_s.dtype))

    gate = gate_ref[pl.ds(i // tiles_per_seq, 1), :]
    d_model = o_ref.shape[1]
    sq = jnp.zeros((tm, 1), F32)
    for n in range(d_model // FFN_DOWN_SLAB):
        ns = slice(n * FFN_DOWN_SLAB, (n + 1) * FFN_DOWN_SLAB)
        down = jnp.dot(hid_s[...], wd_ref[:, ns], preferred_element_type=F32)
        for c in range(FFN_DOWN_SLAB // LANES):
            g = n * (FFN_DOWN_SLAB // LANES) + c
            cs = slice(g * LANES, (g + 1) * LANES)
            for parity in range(2):
                order_s[g, pl.ds(parity, half, stride=2), :] = (
                    down[parity * half:(parity + 1) * half, c * LANES:(c + 1) * LANES])
            x2 = x1_ref[:, cs] + gate[:, cs] * order_s[g]
            sq = sq + jnp.sum(x2 * x2, axis=-1, keepdims=True)
            o_ref[:, cs] = x2
    o_ref[...] = o_ref[...] * lax.rsqrt(sq * (1.0 / d_model) + EPS) * fw_ref[...]


def _ffn(h2, x1, w_up, conv_w, conv_b, w_down, mod, final_w, B):
    T, D = x1.shape
    tm, tf, halo = TM_FFN, FFN_CHUNK, FFN_HALO
    row = lambda i: (i, 0)
    const = lambda i: (0, 0)
    single = dict(pipeline_mode=pl.Buffered(1))
    return pl.pallas_call(
        functools.partial(_ffn_kernel, tiles_per_seq=T // B // tm),
        grid=(T // tm,),
        in_specs=[pl.BlockSpec((tm, D), row),
                  pl.BlockSpec((halo, D), lambda i: (jnp.maximum(i * (tm // halo) - 1, 0), 0)),
                  pl.BlockSpec((tm, D), row),
                  pl.BlockSpec(w_up.shape, const, **single),
                  pl.BlockSpec(conv_w.shape, const, **single),
                  pl.BlockSpec(conv_b.shape, const, **single),
                  pl.BlockSpec(w_down.shape, const, **single),
                  _mod_spec(MOD_GATE2, D),
                  pl.BlockSpec((1, D), const)],
        out_specs=pl.BlockSpec((tm, D), row),
        out_shape=jax.ShapeDtypeStruct((T, D), F32),
        scratch_shapes=[pltpu.VMEM((halo + tm, D), BF16),
                        pltpu.VMEM((2, 2, tf // LANES, halo + tm, LANES), F32),
                        pltpu.VMEM((tm, D_FF), BF16),
                        pltpu.VMEM((D // LANES, tm, LANES), F32)],
        compiler_params=_params("parallel"),
        name="ffn",
    )(h2, h2, x1, w_up, conv_w, conv_b, w_down, mod, final_w)


def kernel(x, c, positions, ada_w, ada_b, norm1_w, w_in, gla_gate_w2, gla_gate_b, gla_norm_w, w_gla_branch,
           w_attn_branch, w_out, norm2_w, w_up, conv_w, conv_b, w_down, final_norm_w):
    B, S, D = x.shape
    T = B * S
    depth = ada_w.shape[0]
    assert depth == 1, "the final norm is fused into the (single) layer's ffn"
    assert all(window // dilation == ATTN_BLK for window, dilation in ATTN_GROUPS)
    assert conv_w.shape[1] == CONV_WIDTH == 3, "the ffn conv reads exactly three taps"
    assert S % ATTN_SPAN == 0 and S % max(TM_IN, TM_MERGE, TM_FFN) == 0 and T % (LANES // (ATTN_HEAD_DIM // 2)) == 0
    x2 = x.reshape(T, D)
    for layer in range(depth):
        cos, sin, mod, w_in_bf = _prep(positions, c, ada_w[layer], ada_b[layer], w_in[layer].T)

        og, (qkv1, qkv2, qkv3, mab) = _in_proj_gla(
            x2, norm1_w[layer].reshape(1, D), mod, cos, sin, w_in_bf, gla_gate_w2[layer],
            gla_gate_b[layer].reshape(1, -1), gla_norm_w[layer].reshape(1, -1), B)
        oa, (w_up_bf, w_down_bf, w_gla_bf, w_attn_bf, w_out_bf) = _dilated_attention(
            (qkv1, qkv2, qkv3),
            (w_up[layer], w_down[layer], w_gla_branch[layer], w_attn_branch[layer], w_out[layer]), B, S)

        x1, h2 = _merge(x2, og, oa, mab, mod, norm2_w[layer].reshape(1, D), w_gla_bf, w_attn_bf, w_out_bf, B)

        x2 = _ffn(h2, x1, w_up_bf, conv_w[layer], conv_b[layer].reshape(1, -1),
                  w_down_bf, mod, final_norm_w.reshape(1, D), B)
    return x2.reshape(B, S, D)
```

```python
import functools

import jax
import jax.numpy as jnp
from jax import lax
from jax.experimental import pallas as pl
from jax.experimental.pallas import tpu as pltpu

F32 = jnp.float32
BF16 = jnp.bfloat16

D_MODEL = 1024
GLA_HEADS = 4
GLA_DK = 128
GLA_DV = 256
GLA_LOWRANK = 16
GLA_TAU = 16.0
GLA_QK_WIDTH = GLA_HEADS * GLA_DK
GLA_V_WIDTH = GLA_HEADS * GLA_DV
ATTN_GROUPS = ((128, 1), (512, 4), (2048, 16))
ATTN_HEADS_PER_GROUP = 4
ATTN_HEAD_DIM = 64
ATTN_WIDTH = ATTN_HEADS_PER_GROUP * len(ATTN_GROUPS) * ATTN_HEAD_DIM
ATTN_OUT_WIDTH = ATTN_HEADS_PER_GROUP * ATTN_HEAD_DIM
ROPE_THETA = 10000.0
D_FF = 2816
CONV_WIDTH = 3
EPS = 1e-6
IN_WIDTHS = (GLA_QK_WIDTH, GLA_QK_WIDTH, GLA_V_WIDTH, GLA_V_WIDTH, GLA_LOWRANK,
             ATTN_WIDTH, ATTN_WIDTH, ATTN_WIDTH, D_MODEL, D_MODEL)

LANES = 128
VMEM_LIMIT_BYTES = 56 * 1024 * 1024

PREP_ROWS = 512
TM_IN = 512
PERM_STRIDE = 4
GLA_CHUNK = 64
ATTN_BLK = 128
ATTN_SPAN = 2048
ATTN_MERGE_ROWS = 256
ATTN_N_CAST = 5
TM_MERGE = 1024
MERGE_SPLIT = 2
TM_FFN = 512
FFN_CHUNK = 256
FFN_HALO = 16
FFN_DOWN_SLAB = 256
MASK_VALUE = float("-inf")
LOG2_E = 1.4426950408889634


def _params(*sem):
    return pltpu.CompilerParams(dimension_semantics=sem, vmem_limit_bytes=VMEM_LIMIT_BYTES)


def _split_hi_lo(a):
    hi = a.astype(BF16)
    lo = (a - hi.astype(F32)).astype(BF16)
    return hi, lo


def _mod_spec(which, D):
    return pl.BlockSpec((8, D), lambda i: (0, which))


MOD_SHIFT1, MOD_SCALE1, MOD_GATE1, MOD_SHIFT2, MOD_SCALE2, MOD_GATE2 = range(6)


def _prep_kernel(pos_ref, invf_ref, c_ref, adaw_ref, adab_ref, wt_ref, cos_ref, sin_ref, mod_ref, wt_bf_ref):
    c = c_ref[...]
    s = c * (1.0 / (1.0 + jnp.exp(-c)))
    s_hi, s_lo = _split_hi_lo(s)
    lhs = jnp.concatenate([s_hi, s_lo], axis=0)
    w_hi, w_lo = _split_hi_lo(adaw_ref[...])
    acc = jnp.dot(lhs, w_hi, preferred_element_type=F32) + jnp.dot(lhs, w_lo, preferred_element_type=F32)
    mod_ref[...] = acc[0:8] + acc[8:16] + adab_ref[...]

    wt_bf_ref[...] = wt_ref[...].astype(wt_bf_ref.dtype)

    half = ATTN_HEAD_DIM // 2
    groups = LANES // half
    tr = pos_ref.shape[1]
    pos = jnp.concatenate([pos_ref[...].astype(F32), jnp.zeros((8 - groups, tr), F32)], axis=0)
    pos_t = jnp.transpose(pos)
    lane = lax.broadcasted_iota(jnp.int32, (tr, LANES), 1)
    group = lane // half
    pos_dense = jnp.zeros((tr, LANES), F32)
    for q in range(groups):
        pos_dense = jnp.where(group == q, pos_t[:, q:q + 1], pos_dense)
    ang = pos_dense * invf_ref[...]
    first_half = (lane % ATTN_HEAD_DIM) < half
    for table, out_ref, signed in ((jnp.cos(ang), cos_ref, False), (jnp.sin(ang), sin_ref, True)):
        for q in range(groups):
            spread = jnp.where(group == q, table, 0.0)
            shift = groups // 2
            while shift >= 1:
                spread = spread + pltpu.roll(spread, shift * half, axis=1)
                shift //= 2
            out_ref[q] = jnp.where(first_half, -spread, spread) if signed else spread


def _prep(positions, c, ada_w, ada_b, w_in_t):
    T = positions.size
    B, D = c.shape
    N = ada_w.shape[1]
    half = ATTN_HEAD_DIM // 2
    groups = LANES // half
    inv_freq = ROPE_THETA ** (-jnp.arange(half, dtype=F32) / half)
    invf = jnp.tile(inv_freq, groups).reshape(1, LANES)
    per = T // groups
    tr = PREP_ROWS
    steps = per // tr
    tn = N // steps
    bf16_rows = 16
    wt_slab = pl.BlockSpec((-(-w_in_t.shape[0] // (steps * bf16_rows)) * bf16_rows, w_in_t.shape[1]), lambda i: (i, 0))
    c8 = jnp.pad(c, ((0, 8 - B), (0, 0)))
    cos, sin, mod, wt_bf = pl.pallas_call(
        _prep_kernel,
        grid=(steps,),
        in_specs=[pl.BlockSpec((groups, tr), lambda i: (0, i)),
                  pl.BlockSpec((1, LANES), lambda i: (0, 0)),
                  pl.BlockSpec((8, D), lambda i: (0, 0)),
                  pl.BlockSpec((D, tn), lambda i: (0, i)),
                  pl.BlockSpec((1, tn), lambda i: (0, i)),
                  wt_slab],
        out_specs=[pl.BlockSpec((groups, tr, LANES), lambda i: (0, i, 0))] * 2
                  + [pl.BlockSpec((8, tn), lambda i: (0, i)), wt_slab],
        out_shape=[jax.ShapeDtypeStruct((groups, per, LANES), F32)] * 2
                  + [jax.ShapeDtypeStruct((8, N), F32), jax.ShapeDtypeStruct(w_in_t.shape, BF16)],
        compiler_params=_params("parallel"),
        name="prep",
    )(positions.reshape(groups, per), invf, c8, ada_w, ada_b.reshape(1, N), w_in_t)
    return cos.reshape(T, LANES), sin.reshape(T, LANES), mod, wt_bf


def _rmsnorm_mod(x, w, scale, shift):
    ms = jnp.mean(x * x, axis=-1, keepdims=True)
    return (x * lax.rsqrt(ms + EPS)) * (w * (1.0 + scale)) + shift


def _rotate_half_pairs(t, cos, sin_signed):
    lane = lax.broadcasted_iota(jnp.int32, t.shape, 1)
    first_half = (lane % ATTN_HEAD_DIM) < (ATTN_HEAD_DIM // 2)
    from_right = pltpu.roll(t, LANES - ATTN_HEAD_DIM // 2, axis=1)
    from_left = pltpu.roll(t, ATTN_HEAD_DIM // 2, axis=1)
    return t * cos + jnp.where(first_half, from_right, from_left) * sin_signed


def _store_residue_major(ref, perm_ref, slab, col, dilation):
    if dilation == 1:
        ref[:, col:col + LANES] = slab.astype(ref.dtype)
        return
    tm = slab.shape[0]
    n = tm // dilation
    perm_ref[0] = slab
    if dilation <= PERM_STRIDE:
        for p in range(dilation):
            ref[p * n:(p + 1) * n, col:col + LANES] = perm_ref[0, pl.ds(p, n, stride=dilation), :].astype(ref.dtype)
        return
    f = PERM_STRIDE
    g = dilation // f
    rows = tm // f
    for a in range(f):
        perm_ref[1, a * rows:(a + 1) * rows, :] = perm_ref[0, pl.ds(a, rows, stride=f), :]
    for a in range(f):
        for q in range(g):
            p = q * f + a
            ref[p * n:(p + 1) * n, col:col + LANES] = (
                perm_ref[1, pl.ds(a * rows + q, n, stride=g), :].astype(ref.dtype))


IN_ALIGNED = sum(IN_WIDTHS[:4])
IN_SHIFTED_START = IN_ALIGNED + GLA_LOWRANK


def _in_proj_gla_kernel(x_ref, n1w_ref, scale_ref, shift_ref, cos_ref, sin_ref, wt_ref, w2_ref, b2_ref, nw_ref,
                        og_ref, qkv1_ref, qkv2_ref, qkv3_ref, mab_ref,
                        perm_ref, gq_s, gk_s, gv_s, gr_s, la_s, state_ref,
                        *, tiles_per_seq, n_tiles):
    i = pl.program_id(0)
    nt = (((1,), (1,)), ((), ()))
    gw = ATTN_OUT_WIDTH
    k1_ref, k2_ref, k3_ref = (r.at[:, 0:gw] for r in (qkv1_ref, qkv2_ref, qkv3_ref))
    v1_ref, v2_ref, v3_ref = (r.at[:, gw:2 * gw] for r in (qkv1_ref, qkv2_ref, qkv3_ref))
    q1_ref, q2_ref, q3_ref = (r.at[:, 2 * gw:3 * gw] for r in (qkv1_ref, qkv2_ref, qkv3_ref))
    ma_ref, mb_ref = mab_ref.at[:, 0:D_MODEL], mab_ref.at[:, D_MODEL:2 * D_MODEL]

    @pl.when(i == 0)
    def _():
        for ref in (gq_s, gk_s, gv_s, gr_s, la_s, state_ref):
            ref[...] = jnp.zeros_like(ref)

    first_of_seq = (i + tiles_per_seq - 1) % tiles_per_seq == 0
    chunks = _gla_chunks(gq_s, gk_s, gv_s, la_s, gr_s, nw_ref, og_ref, state_ref, first_of_seq)

    @pl.when(i == n_tiles)
    def _():
        for chunk in chunks:
            chunk()

    @pl.when(i < n_tiles)
    def _():
        b = i // tiles_per_seq
        h = _rmsnorm_mod(x_ref[...], n1w_ref[...], scale_ref[pl.ds(b, 1), :], shift_ref[pl.ds(b, 1), :]).astype(BF16)

        def rows_proj(row, width):
            return lax.dot_general(h, wt_ref[row:row + width, :], nt, preferred_element_type=F32)

        def proj(col, width):
            return rows_proj(IN_SHIFTED_START + col, width)

        cos = cos_ref[...]
        sin = sin_ref[...]
        q_scale = ATTN_HEAD_DIM ** -0.5 * LOG2_E
        dilations = [d for _, d in ATTN_GROUPS]

        def rope_piece(ref, col, tables, dilation):
            cos_s, sin_s = tables

            def run():
                t = proj(col, ATTN_OUT_WIDTH)
                for s in range(0, ATTN_OUT_WIDTH, LANES):
                    rot = _rotate_half_pairs(t[:, s:s + LANES], cos_s, sin_s)
                    _store_residue_major(ref, perm_ref, rot, s, dilation)
            return run

        def value_piece(ref, col, dilation):
            def run():
                t = proj(col, ATTN_OUT_WIDTH)
                for s in range(0, ATTN_OUT_WIDTH, LANES):
                    _store_residue_major(ref, perm_ref, t[:, s:s + LANES], s, dilation)
            return run

        def gate_piece(ref, col, s):
            def run():
                z = proj(col + s, 512)
                ref[:, s:s + 512] = (1.0 / (1.0 + jnp.exp(-z))).astype(ref.dtype)
            return run

        pieces = []
        col = 0
        for refs, tables in (((q1_ref, q2_ref, q3_ref), (cos * q_scale, sin * q_scale)),
                             ((k1_ref, k2_ref, k3_ref), (cos, sin))):
            for g, ref in enumerate(refs):
                pieces.append(rope_piece(ref, col, tables, dilations[g]))
                col += ATTN_OUT_WIDTH
        for g, ref in enumerate((v1_ref, v2_ref, v3_ref)):
            pieces.append(value_piece(ref, col, dilations[g]))
            col += ATTN_OUT_WIDTH
        for ref in (ma_ref, mb_ref):
            for s in range(0, D_MODEL, 512):
                pieces.append(gate_piece(ref, col, s))
            col += D_MODEL

        for n in range(max(len(pieces), len(chunks))):
            if n < len(pieces):
                pieces[n]()
            if n < len(chunks):
                chunks[n]()

        col = 0
        for ref, width in ((gq_s, GLA_QK_WIDTH), (gk_s, GLA_QK_WIDTH)):
            ref[...] = rows_proj(col, width).astype(ref.dtype)
            col += width
        for ref in (gv_s, gr_s):
            for s in range(0, GLA_V_WIDTH, 512):
                ref[:, s:s + 512] = rows_proj(col + s, 512).astype(ref.dtype)
            col += GLA_V_WIDTH
        g_lr = rows_proj(IN_ALIGNED, LANES)
        g_hi, g_lo = _split_hi_lo(g_lr)
        w2_hi, w2_lo = _split_hi_lo(w2_ref[...])
        z = (jnp.dot(g_hi, w2_hi, preferred_element_type=F32) + jnp.dot(g_lo, w2_hi, preferred_element_type=F32)
             + jnp.dot(g_hi, w2_lo, preferred_element_type=F32)) + b2_ref[...]
        log_sig = jnp.minimum(z, 0.0) - jnp.log(1.0 + jnp.exp(-jnp.abs(z)))
        la_s[...] = log_sig * (LOG2_E / GLA_TAU)


def _gla_chunks(q_ref, k_ref, v_ref, la_ref, gr_ref, nw_ref, o_ref, state_ref, first_of_seq):
    C = GLA_CHUNK
    H = C // 2
    heads = range(GLA_HEADS)
    rows = lax.broadcasted_iota(jnp.int32, (C, C), 0)
    cols = lax.broadcasted_iota(jnp.int32, (C, C), 1)
    causal = cols <= rows
    tri = causal.astype(BF16)
    same_half = (rows >= H) == (cols >= H)
    upper_rows = lax.broadcasted_iota(jnp.int32, (C, GLA_QK_WIDTH), 0) < H
    q_scale = GLA_DK ** -0.5
    nw = jnp.concatenate([nw_ref[...]] * GLA_HEADS, axis=1)
    nt = (((1,), (1,)), ((), ()))
    ks = lambda a, h: a[:, h * GLA_DK:(h + 1) * GLA_DK]
    vs = lambda a, h: a[:, h * GLA_DV:(h + 1) * GLA_DV]

    def run_chunk(c):
        sl = pl.ds(c * C, C)
        q = q_ref[sl, :].astype(F32) * q_scale
        k = k_ref[sl, :].astype(F32)
        v = v_ref[sl, :]
        la_hi, la_lo = _split_hi_lo(la_ref[sl, :])
        cum = jnp.dot(tri, la_hi, preferred_element_type=F32) + jnp.dot(tri, la_lo, preferred_element_type=F32)
        mid = jnp.where(upper_rows, cum[H // 2 - 1:H // 2, :], cum[H + H // 2 - 1:H + H // 2, :])
        edge = cum[H - 1:H, :]
        last = cum[C - 1:C, :]
        qg = (q * jnp.exp2(cum - mid)).astype(BF16)
        kg = (k * jnp.exp2(mid - cum)).astype(BF16)
        zeros_half = jnp.zeros((H, GLA_QK_WIDTH), BF16)
        q_lo = jnp.concatenate([zeros_half, (q[H:] * jnp.exp2(cum[H:] - edge)).astype(BF16)], axis=0)
        k_up = jnp.concatenate([(k[:H] * jnp.exp2(edge - cum[:H])).astype(BF16), zeros_half], axis=0)
        q_in = (q * jnp.exp2(cum)).astype(BF16)
        k_out = k * jnp.exp2(last - cum)
        decay_row = jnp.broadcast_to(jnp.exp2(last), (8, GLA_QK_WIDTH))
        g = gr_ref[sl, :].astype(F32)
        gate = nw * (g * (1.0 / (1.0 + jnp.exp(-g))))

        within = [lax.dot_general(ks(qg, h), ks(kg, h), nt, preferred_element_type=F32) for h in heads]
        across = [lax.dot_general(ks(q_lo, h), ks(k_up, h), nt, preferred_element_type=F32) for h in heads]
        attn = [jnp.where(same_half, jnp.where(causal, within[h], 0.0), across[h]).astype(BF16) for h in heads]
        if c == 0:
            state = [jnp.where(first_of_seq, 0.0, state_ref[h]) for h in heads]
        else:
            state = [state_ref[h] for h in heads]
        o = [jnp.dot(attn[h], vs(v, h), preferred_element_type=F32)
             + jnp.dot(ks(q_in, h), state[h].astype(BF16), preferred_element_type=F32) for h in heads]
        k_out_t = [jnp.transpose(ks(k_out, h)).astype(BF16) for h in heads]
        decay = [jnp.transpose(ks(decay_row, h))[:, 0:1] for h in heads]
        for h in heads:
            state_ref[h] = decay[h] * state[h] + jnp.dot(k_out_t[h], vs(v, h), preferred_element_type=F32)
        ms = [jnp.mean(o[h] * o[h], axis=-1, keepdims=True) for h in heads]
        for h in heads:
            o_ref[sl, h * GLA_DV:(h + 1) * GLA_DV] = (o[h] * lax.rsqrt(ms[h] + EPS) * vs(gate, h)).astype(o_ref.dtype)

    return [functools.partial(run_chunk, c) for c in range(q_ref.shape[0] // C)]


def _in_proj_gla(x2, n1w, mod, cos, sin, w_in_t, w2, b2, gla_norm_w, B):
    T, D = x2.shape
    tm = TM_IN
    n_tiles = T // tm
    cur = lambda i: (jnp.minimum(i, n_tiles - 1), 0)
    lag = lambda i: (jnp.maximum(i - 1, 0), 0)
    const = lambda i: (0, 0)
    widths = (3 * ATTN_OUT_WIDTH,) * len(ATTN_GROUPS) + (2 * D_MODEL,)
    single = dict(pipeline_mode=pl.Buffered(1))
    w2p = jnp.pad(w2, ((0, LANES - GLA_LOWRANK), (0, 0)))
    outs = pl.pallas_call(
        functools.partial(_in_proj_gla_kernel, tiles_per_seq=T // B // tm, n_tiles=n_tiles),
        grid=(n_tiles + 1,),
        in_specs=[pl.BlockSpec((tm, D), cur),
                  pl.BlockSpec((1, D), const),
                  _mod_spec(MOD_SCALE1, D),
                  _mod_spec(MOD_SHIFT1, D),
                  pl.BlockSpec((tm, LANES), cur),
                  pl.BlockSpec((tm, LANES), cur),
                  pl.BlockSpec(w_in_t.shape, const, **single),
                  pl.BlockSpec(w2p.shape, const, **single),
                  pl.BlockSpec(b2.shape, const, **single),
                  pl.BlockSpec(gla_norm_w.shape, const, **single)],
        out_specs=[pl.BlockSpec((tm, GLA_V_WIDTH), lag)] + [pl.BlockSpec((tm, w), cur) for w in widths],
        out_shape=[jax.ShapeDtypeStruct((T, GLA_V_WIDTH), BF16)]
                  + [jax.ShapeDtypeStruct((T, w), BF16) for w in widths],
        scratch_shapes=[pltpu.VMEM((2, tm, LANES), F32),
                        pltpu.VMEM((tm, GLA_QK_WIDTH), BF16),
                        pltpu.VMEM((tm, GLA_QK_WIDTH), BF16),
                        pltpu.VMEM((tm, GLA_V_WIDTH), BF16),
                        pltpu.VMEM((tm, GLA_V_WIDTH), BF16),
                        pltpu.VMEM((tm, GLA_QK_WIDTH), F32),
                        pltpu.VMEM((GLA_HEADS, GLA_DK, GLA_DV), F32)],
        compiler_params=_params("arbitrary"),
        name="in_proj_gla",
    )(x2, n1w, mod, mod, cos, sin, w_in_t, w2p, b2, gla_norm_w)
    return outs[0], outs[1:]


def _attn_kernel(g1c_ref, g1p_ref, g2c_ref, g2p_ref, g3c_ref, g3p_ref, *refs):
    n_cast = ATTN_N_CAST
    o_ref = refs[n_cast]
    slabs = refs[2 * n_cast + 1:]
    for src, dst in zip(refs[:n_cast], refs[n_cast + 1:2 * n_cast + 1]):
        dst[...] = src[...].astype(dst.dtype)

    gw = ATTN_OUT_WIDTH
    k1c_ref, k2c_ref, k3c_ref = (r.at[:, 0:gw] for r in (g1c_ref, g2c_ref, g3c_ref))
    v1c_ref, v2c_ref, v3c_ref = (r.at[:, gw:2 * gw] for r in (g1c_ref, g2c_ref, g3c_ref))
    q1_ref, q2_ref, q3_ref = (r.at[:, 2 * gw:3 * gw] for r in (g1c_ref, g2c_ref, g3c_ref))
    k1p_ref, k2p_ref, k3p_ref = (r.at[:, 0:gw] for r in (g1p_ref, g2p_ref, g3p_ref))
    v1p_ref, v2p_ref, v3p_ref = (r.at[:, gw:2 * gw] for r in (g1p_ref, g2p_ref, g3p_ref))

    blk = ATTN_BLK
    tile = TM_IN
    span = ATTN_SPAN
    tiles = span // tile
    r2, r3 = ATTN_GROUPS[1][1], ATTN_GROUPS[2][1]
    per3 = tile // r3
    not_first_span = pl.program_id(1) > 0
    npair = ATTN_OUT_WIDTH // LANES
    o1_s, o2_s, o3_s, l1_s, l2_s, l3_s = [slabs[n * npair:(n + 1) * npair] for n in range(6)]

    rows = lax.broadcasted_iota(jnp.int32, (blk, 2 * blk), 0)
    cols = lax.broadcasted_iota(jnp.int32, (blk, 2 * blk), 1)
    band = (cols >= rows) & (cols <= rows + blk)
    in_cur = cols >= blk
    left = lax.broadcasted_iota(jnp.int32, (blk, LANES), 1) < ATTN_HEAD_DIM
    ones = jnp.ones((2 * blk, LANES), BF16)
    nt = (((1,), (1,)), ((), ()))

    def pair_attend(q_pair, k_cat, v_cat, has_prev):
        zero = jnp.zeros_like(q_pair)
        q2 = jnp.concatenate([jnp.where(left, q_pair, zero), jnp.where(left, zero, q_pair)], axis=0)
        s = lax.dot_general(q2, k_cat, nt, preferred_element_type=F32)
        valid = band & (in_cur | has_prev)
        s = jnp.where(jnp.concatenate([valid, valid], axis=0), s, MASK_VALUE)
        m = jnp.max(s, axis=-1, keepdims=True)
        p = jnp.exp2(s - m).astype(BF16)
        r = jnp.dot(p, jnp.concatenate([v_cat, ones], axis=1), preferred_element_type=F32)
        acc = jnp.where(left, r[:blk, :LANES], r[blk:, :LANES])
        den = jnp.where(left, r[:blk, LANES:], r[blk:, LANES:])
        m_pair = jnp.where(left, jnp.broadcast_to(m[:blk], (blk, LANES)), jnp.broadcast_to(m[blk:], (blk, LANES)))
        return acc / den, m_pair + jnp.log2(den)

    def with_prev(cur_ref, prev_ref, r0, back, prev_rows, first, cs):
        before = cur_ref[pl.ds(pl.multiple_of(jnp.maximum(r0 - back, 0), blk), blk), cs]
        if prev_rows is not None:
            before = jnp.where(first, prev_ref[prev_rows, cs], before)
        return jnp.concatenate([before, cur_ref[pl.ds(r0, blk), cs]], axis=0)

    def tile_body(t, carry):
        for p in range(tiles):
            i = t * tiles + p
            r0 = pl.multiple_of(i * blk, blk)
            for j in range(npair):
                cs = slice(j * LANES, (j + 1) * LANES)
                rows1 = slice(0, blk) if p == 0 else None
                o, lse = pair_attend(q1_ref[pl.ds(r0, blk), cs],
                                     with_prev(k1c_ref, k1p_ref, r0, blk, rows1, t == 0, cs),
                                     with_prev(v1c_ref, v1p_ref, r0, blk, rows1, t == 0, cs),
                                     not_first_span | (i > 0))
                o1_s[j][pl.ds(r0, blk), :] = o
                l1_s[j][pl.ds(r0, blk), :] = lse
                rows2 = slice(p * blk, (p + 1) * blk)
                o, lse = pair_attend(q2_ref[pl.ds(r0, blk), cs],
                                     with_prev(k2c_ref, k2p_ref, r0, tile, rows2, t == 0, cs),
                                     with_prev(v2c_ref, v2p_ref, r0, tile, rows2, t == 0, cs),
                                     not_first_span | (t > 0))
                tok = pl.ds(pl.multiple_of(t * tile, tile) + p, blk, stride=r2)
                o2_s[j][tok, :] = o
                l2_s[j][tok, :] = lse
                rr = pl.multiple_of(i * per3, per3)
                gather = lambda ref: jnp.concatenate(
                    [ref[pl.ds(rr + u * tile, per3), cs] for u in range(tiles)], axis=0)
                k_cat = jnp.concatenate([gather(k3p_ref), gather(k3c_ref)], axis=0)
                v_cat = jnp.concatenate([gather(v3p_ref), gather(v3c_ref)], axis=0)
                o, lse = pair_attend(gather(q3_ref), k_cat, v_cat, not_first_span)
                tok = pl.ds(i, blk, stride=r3)
                o3_s[j][tok, :] = o
                l3_s[j][tok, :] = lse
        return carry

    lax.fori_loop(0, tiles, tile_body, 0)

    for j in range(npair):
        for n in range(span // ATTN_MERGE_ROWS):
            rs = slice(n * ATTN_MERGE_ROWS, (n + 1) * ATTN_MERGE_ROWS)
            l1, l2, l3 = l1_s[j][rs, :], l2_s[j][rs, :], l3_s[j][rs, :]
            m = jnp.maximum(jnp.maximum(l1, l2), l3)
            e1, e2, e3 = jnp.exp2(l1 - m), jnp.exp2(l2 - m), jnp.exp2(l3 - m)
            num = e1 * o1_s[j][rs, :] + e2 * o2_s[j][rs, :] + e3 * o3_s[j][rs, :]
            o_ref[rs, j * LANES:(j + 1) * LANES] = (num / (e1 + e2 + e3)).astype(o_ref.dtype)


def _dilated_attention(qkvs, weights, B, S):
    assert len(weights) == ATTN_N_CAST
    span = ATTN_SPAN
    nsp = S // span
    gw = ATTN_OUT_WIDTH
    T = B * S
    cur = lambda b, s: (b * nsp + s, 0)
    steps = B * nsp
    slab = lambda w: pl.BlockSpec((w.shape[0] // steps, w.shape[1]), cur)

    def prev(rows):
        per = span // rows
        return pl.BlockSpec((rows, 2 * gw), lambda b, s: (jnp.maximum((b * nsp + s) * per - 1, 0), 0))

    halos = (ATTN_BLK, TM_IN, span)
    full = pl.BlockSpec((span, gw), cur)
    group_specs = []
    for h in halos:
        group_specs += [pl.BlockSpec((span, 3 * gw), cur), prev(h)]
    outs = pl.pallas_call(
        _attn_kernel,
        grid=(B, nsp),
        in_specs=group_specs + [slab(w) for w in weights],
        out_specs=[full] + [slab(w) for w in weights],
        out_shape=[jax.ShapeDtypeStruct((T, gw), BF16)] + [jax.ShapeDtypeStruct(w.shape, BF16) for w in weights],
        scratch_shapes=[pltpu.VMEM((span, LANES), F32)] * (6 * (gw // LANES)),
        compiler_params=_params("parallel", "parallel"),
        name="dilated_attn",
    )(*[a for g in qkvs for a in (g, g)], *weights)
    return outs[0], outs[1:]


def _merge_kernel(x_ref, og_ref, oa_ref, mab_ref,
                  gate_ref, scale_ref, shift_ref, n2w_ref, wg_ref, wa_ref, wo_ref, x1_ref, h2_ref, *, tiles_per_seq):
    ma_ref, mb_ref = mab_ref.at[:, 0:D_MODEL], mab_ref.at[:, D_MODEL:2 * D_MODEL]
    b = pl.ds(pl.program_id(0) // tiles_per_seq, 1)
    gate, scale, shift = gate_ref[b, :], scale_ref[b, :], shift_ref[b, :]
    sub = x_ref.shape[0] // MERGE_SPLIT
    for r in range(MERGE_SPLIT):
        rs = slice(r * sub, (r + 1) * sub)
        y_attn = jnp.dot(oa_ref[rs, :], wa_ref[...], preferred_element_type=F32)
        y_gla = jnp.dot(og_ref[rs, :], wg_ref[...], preferred_element_type=F32)
        mixed = (ma_ref[rs, :].astype(F32) * y_gla + mb_ref[rs, :].astype(F32) * y_attn).astype(BF16)
        x1 = x_ref[rs, :] + gate * jnp.dot(mixed, wo_ref[...], preferred_element_type=F32)
        x1_ref[rs, :] = x1
        h2_ref[rs, :] = _rmsnorm_mod(x1, n2w_ref[...], scale, shift).astype(h2_ref.dtype)


def _merge(x2, og, oa, mab, mod, n2w, wg, wa, wo, B):
    T, D = x2.shape
    tm = TM_MERGE
    row = lambda i: (i, 0)
    const = lambda i: (0, 0)
    return pl.pallas_call(
        functools.partial(_merge_kernel, tiles_per_seq=T // B // tm),
        grid=(T // tm,),
        in_specs=[pl.BlockSpec((tm, D), row), pl.BlockSpec((tm, GLA_V_WIDTH), row),
                  pl.BlockSpec((tm, ATTN_OUT_WIDTH), row), pl.BlockSpec((tm, 2 * D), row)]
                 + [_mod_spec(MOD_GATE1, D), _mod_spec(MOD_SCALE2, D), _mod_spec(MOD_SHIFT2, D)]
                 + [pl.BlockSpec((1, D), const),
                    pl.BlockSpec(wg.shape, const), pl.BlockSpec(wa.shape, const), pl.BlockSpec(wo.shape, const)],
        out_specs=[pl.BlockSpec((tm, D), row)] * 2,
        out_shape=[jax.ShapeDtypeStruct((T, D), F32), jax.ShapeDtypeStruct((T, D), BF16)],
        compiler_params=_params("parallel"),
        name="merge",
    )(x2, og, oa, mab, mod, mod, mod, n2w, wg, wa, wo)


def _ffn_kernel(h_ref, hprev_ref, x1_ref, wup_ref, cw_ref, cb_ref, wd_ref, gate_ref, fw_ref, o_ref,
                hcat_s, u_s, hid_s, order_s, *, tiles_per_seq):
    i = pl.program_id(0)
    tm = h_ref.shape[0]
    halo = hprev_ref.shape[0]
    tf = FFN_CHUNK
    half = tm // 2
    hcat_s[0:halo] = jnp.where(i % tiles_per_seq == 0, jnp.zeros_like(hprev_ref[...]), hprev_ref[...])
    hcat_s[halo:] = h_ref[...]
    nchunk = D_FF // tf
    branches = (0, D_FF)

    def up_project(j):
        for n, off in enumerate(branches):
            u = jnp.dot(hcat_s[...], wup_ref[:, off + j * tf:off + (j + 1) * tf], preferred_element_type=F32)
            for c in range(tf // LANES):
                u_s[j % 2, n, c] = u[:, c * LANES:(c + 1) * LANES]

    def conv(j, n, c, parity):
        u = u_s.at[j % 2, n, c]
        cs = slice(branches[n] + j * tf + c * LANES, branches[n] + j * tf + (c + 1) * LANES)
        tap = lambda back: u[pl.ds(halo + parity - back, half, stride=2), :]
        k = 2.0 ** -0.5
        return (cb_ref[:, cs] * k + (cw_ref[0:1, cs] * k) * tap(2) + (cw_ref[1:2, cs] * k) * tap(1)
                + (cw_ref[2:3, cs] * k) * tap(0))

    up_project(0)
    for j in range(nchunk):
        if j + 1 < nchunk:
            up_project(j + 1)
        for c in range(tf // LANES):
            for parity in range(2):
                t = conv(j, 1, c, parity)
                hidden = (t * (1.0 + lax.erf(t))) * conv(j, 0, c, parity)
                hid_s[parity * half:(parity + 1) * half, j * tf + c * LANES:j * tf + (c + 1) * LANES] = (
                    hidden.astype(hid_s.dtype))

    gate = gate_ref[pl.ds(i // tiles_per_seq, 1), :]
    d_model = o_ref.shape[1]
    sq = jnp.zeros((tm, 1), F32)
    for n in range(d_model // FFN_DOWN_SLAB):
        ns = slice(n * FFN_DOWN_SLAB, (n + 1) * FFN_DOWN_SLAB)
        down = jnp.dot(hid_s[...], wd_ref[:, ns], preferred_element_type=F32)
        for c in range(FFN_DOWN_SLAB // LANES):
            g = n * (FFN_DOWN_SLAB // LANES) + c
            cs = slice(g * LANES, (g + 1) * LANES)
            for parity in range(2):
                order_s[g, pl.ds(parity, half, stride=2), :] = (
                    down[parity * half:(parity + 1) * half, c * LANES:(c + 1) * LANES])
            x2 = x1_ref[:, cs] + gate[:, cs] * order_s[g]
            sq = sq + jnp.sum(x2 * x2, axis=-1, keepdims=True)
            o_ref[:, cs] = x2
    o_ref[...] = o_ref[...] * lax.rsqrt(sq * (1.0 / d_model) + EPS) * fw_ref[...]


def _ffn(h2, x1, w_up, conv_w, conv_b, w_down, mod, final_w, B):
    T, D = x1.shape
    tm, tf, halo = TM_FFN, FFN_CHUNK, FFN_HALO
    row = lambda i: (i, 0)
    const = lambda i: (0, 0)
    single = dict(pipeline_mode=pl.Buffered(1))
    return pl.pallas_call(
        functools.partial(_ffn_kernel, tiles_per_seq=T // B // tm),
        grid=(T // tm,),
        in_specs=[pl.BlockSpec((tm, D), row),
                  pl.BlockSpec((halo, D), lambda i: (jnp.maximum(i * (tm // halo) - 1, 0), 0)),
                  pl.BlockSpec((tm, D), row),
                  pl.BlockSpec(w_up.shape, const, **single),
                  pl.BlockSpec(conv_w.shape, const, **single),
                  pl.BlockSpec(conv_b.shape, const, **single),
                  pl.BlockSpec(w_down.shape, const, **single),
                  _mod_spec(MOD_GATE2, D),
                  pl.BlockSpec((1, D), const)],
        out_specs=pl.BlockSpec((tm, D), row),
        out_shape=jax.ShapeDtypeStruct((T, D), F32),
        scratch_shapes=[pltpu.VMEM((halo + tm, D), BF16),
                        pltpu.VMEM((2, 2, tf // LANES, halo + tm, LANES), F32),
                        pltpu.VMEM((tm, D_FF), BF16),
                        pltpu.VMEM((D // LANES, tm, LANES), F32)],
        compiler_params=_params("parallel"),
        name="ffn",
    )(h2, h2, x1, w_up, conv_w, conv_b, w_down, mod, final_w)


def kernel(x, c, positions, ada_w, ada_b, norm1_w, w_in, gla_gate_w2, gla_gate_b, gla_norm_w, w_gla_branch,
           w_attn_branch, w_out, norm2_w, w_up, conv_w, conv_b, w_down, final_norm_w):
    B, S, D = x.shape
    T = B * S
    depth = ada_w.shape[0]
    assert depth == 1, "the final norm is fused into the (single) layer's ffn"
    assert all(window // dilation == ATTN_BLK for window, dilation in ATTN_GROUPS)
    assert conv_w.shape[1] == CONV_WIDTH == 3, "the ffn conv reads exactly three taps"
    assert S % ATTN_SPAN == 0 and S % max(TM_IN, TM_MERGE, TM_FFN) == 0 and T % (LANES // (ATTN_HEAD_DIM // 2)) == 0
    x2 = x.reshape(T, D)
    for layer in range(depth):
        cos, sin, mod, w_in_bf = _prep(positions, c, ada_w[layer], ada_b[layer], w_in[layer].T)

        og, (qkv1, qkv2, qkv3, mab) = _in_proj_gla(
            x2, norm1_w[layer].reshape(1, D), mod, cos, sin, w_in_bf, gla_gate_w2[layer],
            gla_gate_b[layer].reshape(1, -1), gla_norm_w[layer].reshape(1, -1), B)
        oa, (w_up_bf, w_down_bf, w_gla_bf, w_attn_bf, w_out_bf) = _dilated_attention(
            (qkv1, qkv2, qkv3),
            (w_up[layer], w_down[layer], w_gla_branch[layer], w_attn_branch[layer], w_out[layer]), B, S)

        x1, h2 = _merge(x2, og, oa, mab, mod, norm2_w[layer].reshape(1, D), w_gla_bf, w_attn_bf, w_out_bf, B)

        x2 = _ffn(h2, x1, w_up_bf, conv_w[layer], conv_b[layer].reshape(1, -1),
                  w_down_bf, mod, final_norm_w.reshape(1, D), B)
    return x2.reshape(B, S, D)
```

```python
import functools

import jax
import jax.numpy as jnp
from jax import lax
from jax.experimental import pallas as pl
from jax.experimental.pallas import tpu as pltpu

F32 = jnp.float32
BF16 = jnp.bfloat16

D_MODEL = 1024
GLA_HEADS = 4
GLA_DK = 128
GLA_DV = 256
GLA_LOWRANK = 16
GLA_TAU = 16.0
GLA_QK_WIDTH = GLA_HEADS * GLA_DK
GLA_V_WIDTH = GLA_HEADS * GLA_DV
ATTN_GROUPS = ((128, 1), (512, 4), (2048, 16))
ATTN_HEADS_PER_GROUP = 4
ATTN_HEAD_DIM = 64
ATTN_WIDTH = ATTN_HEADS_PER_GROUP * len(ATTN_GROUPS) * ATTN_HEAD_DIM
ATTN_OUT_WIDTH = ATTN_HEADS_PER_GROUP * ATTN_HEAD_DIM
ROPE_THETA = 10000.0
D_FF = 2816
CONV_WIDTH = 3
EPS = 1e-6
IN_WIDTHS = (GLA_QK_WIDTH, GLA_QK_WIDTH, GLA_V_WIDTH, GLA_V_WIDTH, GLA_LOWRANK,
             ATTN_WIDTH, ATTN_WIDTH, ATTN_WIDTH, D_MODEL, D_MODEL)

LANES = 128
VMEM_LIMIT_BYTES = 56 * 1024 * 1024

PREP_ROWS = 512
TM_IN = 512
PERM_STRIDE = 4
GLA_CHUNK = 64
ATTN_BLK = 128
ATTN_SPAN = 2048
ATTN_MERGE_ROWS = 256
ATTN_N_CAST = 5
TM_MERGE = 1024
MERGE_SPLIT = 2
MERGE_XBUF = 3
TM_FFN = 512
FFN_CHUNK = 256
FFN_HALO = 16
FFN_DOWN_SLAB = 256
MASK_VALUE = float("-inf")
LOG2_E = 1.4426950408889634


def _params(*sem):
    return pltpu.CompilerParams(dimension_semantics=sem, vmem_limit_bytes=VMEM_LIMIT_BYTES)


def _split_hi_lo(a):
    hi = a.astype(BF16)
    lo = (a - hi.astype(F32)).astype(BF16)
    return hi, lo


def _mod_spec(which, D):
    return pl.BlockSpec((8, D), lambda i: (0, which))


MOD_SHIFT1, MOD_SCALE1, MOD_GATE1, MOD_SHIFT2, MOD_SCALE2, MOD_GATE2 = range(6)


def _prep_kernel(pos_ref, invf_ref, c_ref, adaw_ref, adab_ref, wt_ref, cos_ref, sin_ref, mod_ref, wt_bf_ref):
    c = c_ref[...]
    s = c * (1.0 / (1.0 + jnp.exp(-c)))
    s_hi, s_lo = _split_hi_lo(s)
    lhs = jnp.concatenate([s_hi, s_lo], axis=0)
    w_hi, w_lo = _split_hi_lo(adaw_ref[...])
    acc = jnp.dot(lhs, w_hi, preferred_element_type=F32) + jnp.dot(lhs, w_lo, preferred_element_type=F32)
    mod_ref[...] = acc[0:8] + acc[8:16] + adab_ref[...]

    wt_bf_ref[...] = wt_ref[...].astype(wt_bf_ref.dtype)

    half = ATTN_HEAD_DIM // 2
    groups = LANES // half
    tr = pos_ref.shape[1]
    pos = jnp.concatenate([pos_ref[...].astype(F32), jnp.zeros((8 - groups, tr), F32)], axis=0)
    pos_t = jnp.transpose(pos)
    lane = lax.broadcasted_iota(jnp.int32, (tr, LANES), 1)
    group = lane // half
    pos_dense = jnp.zeros((tr, LANES), F32)
    for q in range(groups):
        pos_dense = jnp.where(group == q, pos_t[:, q:q + 1], pos_dense)
    ang = pos_dense * invf_ref[...]
    first_half = (lane % ATTN_HEAD_DIM) < half
    for table, out_ref, signed in ((jnp.cos(ang), cos_ref, False), (jnp.sin(ang), sin_ref, True)):
        for q in range(groups):
            spread = jnp.where(group == q, table, 0.0)
            shift = groups // 2
            while shift >= 1:
                spread = spread + pltpu.roll(spread, shift * half, axis=1)
                shift //= 2
            out_ref[q] = jnp.where(first_half, -spread, spread) if signed else spread


def _prep(positions, c, ada_w, ada_b, w_in_t):
    T = positions.size
    B, D = c.shape
    N = ada_w.shape[1]
    half = ATTN_HEAD_DIM // 2
    groups = LANES // half
    inv_freq = ROPE_THETA ** (-jnp.arange(half, dtype=F32) / half)
    invf = jnp.tile(inv_freq, groups).reshape(1, LANES)
    per = T // groups
    tr = PREP_ROWS
    steps = per // tr
    tn = N // steps
    bf16_rows = 16
    wt_slab = pl.BlockSpec((-(-w_in_t.shape[0] // (steps * bf16_rows)) * bf16_rows, w_in_t.shape[1]), lambda i: (i, 0))
    c8 = jnp.pad(c, ((0, 8 - B), (0, 0)))
    cos, sin, mod, wt_bf = pl.pallas_call(
        _prep_kernel,
        grid=(steps,),
        in_specs=[pl.BlockSpec((groups, tr), lambda i: (0, i)),
                  pl.BlockSpec((1, LANES), lambda i: (0, 0)),
                  pl.BlockSpec((8, D), lambda i: (0, 0)),
                  pl.BlockSpec((D, tn), lambda i: (0, i)),
                  pl.BlockSpec((1, tn), lambda i: (0, i)),
                  wt_slab],
        out_specs=[pl.BlockSpec((groups, tr, LANES), lambda i: (0, i, 0))] * 2
                  + [pl.BlockSpec((8, tn), lambda i: (0, i)), wt_slab],
        out_shape=[jax.ShapeDtypeStruct((groups, per, LANES), F32)] * 2
                  + [jax.ShapeDtypeStruct((8, N), F32), jax.ShapeDtypeStruct(w_in_t.shape, BF16)],
        compiler_params=_params("parallel"),
        name="prep",
    )(positions.reshape(groups, per), invf, c8, ada_w, ada_b.reshape(1, N), w_in_t)
    return cos.reshape(T, LANES), sin.reshape(T, LANES), mod, wt_bf


def _rmsnorm_mod(x, w, scale, shift):
    ms = jnp.mean(x * x, axis=-1, keepdims=True)
    return (x * lax.rsqrt(ms + EPS)) * (w * (1.0 + scale)) + shift


def _rotate_half_pairs(t, cos, sin_signed):
    lane = lax.broadcasted_iota(jnp.int32, t.shape, 1)
    first_half = (lane % ATTN_HEAD_DIM) < (ATTN_HEAD_DIM // 2)
    from_right = pltpu.roll(t, LANES - ATTN_HEAD_DIM // 2, axis=1)
    from_left = pltpu.roll(t, ATTN_HEAD_DIM // 2, axis=1)
    return t * cos + jnp.where(first_half, from_right, from_left) * sin_signed


def _store_residue_major(ref, perm_ref, slab, col, dilation):
    if dilation == 1:
        ref[:, col:col + LANES] = slab.astype(ref.dtype)
        return
    tm = slab.shape[0]
    n = tm // dilation
    perm_ref[0] = slab
    if dilation <= PERM_STRIDE:
        for p in range(dilation):
            ref[p * n:(p + 1) * n, col:col + LANES] = perm_ref[0, pl.ds(p, n, stride=dilation), :].astype(ref.dtype)
        return
    f = PERM_STRIDE
    g = dilation // f
    rows = tm // f
    for a in range(f):
        perm_ref[1, a * rows:(a + 1) * rows, :] = perm_ref[0, pl.ds(a, rows, stride=f), :]
    for a in range(f):
        for q in range(g):
            p = q * f + a
            ref[p * n:(p + 1) * n, col:col + LANES] = (
                perm_ref[1, pl.ds(a * rows + q, n, stride=g), :].astype(ref.dtype))


IN_ALIGNED = sum(IN_WIDTHS[:4])
IN_SHIFTED_START = IN_ALIGNED + GLA_LOWRANK


def _in_proj_gla_kernel(x_ref, n1w_ref, scale_ref, shift_ref, cos_ref, sin_ref, wt_ref, w2_ref, b2_ref, nw_ref,
                        og_ref, qkv1_ref, qkv2_ref, qkv3_ref, mab_ref,
                        perm_ref, gq_s, gk_s, gv_s, gr_s, la_s, state_ref,
                        *, tiles_per_seq, n_tiles):
    i = pl.program_id(0)
    nt = (((1,), (1,)), ((), ()))
    gw = ATTN_OUT_WIDTH
    k1_ref, k2_ref, k3_ref = (r.at[:, 0:gw] for r in (qkv1_ref, qkv2_ref, qkv3_ref))
    v1_ref, v2_ref, v3_ref = (r.at[:, gw:2 * gw] for r in (qkv1_ref, qkv2_ref, qkv3_ref))
    q1_ref, q2_ref, q3_ref = (r.at[:, 2 * gw:3 * gw] for r in (qkv1_ref, qkv2_ref, qkv3_ref))
    ma_ref, mb_ref = mab_ref.at[:, 0:D_MODEL], mab_ref.at[:, D_MODEL:2 * D_MODEL]

    @pl.when(i == 0)
    def _():
        for ref in (gq_s, gk_s, gv_s, gr_s, la_s, state_ref):
            ref[...] = jnp.zeros_like(ref)

    first_of_seq = (i + tiles_per_seq - 1) % tiles_per_seq == 0
    chunks = _gla_chunks(gq_s, gk_s, gv_s, la_s, gr_s, nw_ref, og_ref, state_ref, first_of_seq)

    @pl.when(i == n_tiles)
    def _():
        for chunk in chunks:
            chunk()

    @pl.when(i < n_tiles)
    def _():
        b = i // tiles_per_seq
        h = _rmsnorm_mod(x_ref[...], n1w_ref[...], scale_ref[pl.ds(b, 1), :], shift_ref[pl.ds(b, 1), :]).astype(BF16)

        def rows_proj(row, width):
            return lax.dot_general(h, wt_ref[row:row + width, :], nt, preferred_element_type=F32)

        def proj(col, width):
            return rows_proj(IN_SHIFTED_START + col, width)

        cos = cos_ref[...]
        sin = sin_ref[...]
        q_scale = ATTN_HEAD_DIM ** -0.5 * LOG2_E
        dilations = [d for _, d in ATTN_GROUPS]

        def rope_piece(ref, col, tables, dilation):
            cos_s, sin_s = tables

            def run():
                t = proj(col, ATTN_OUT_WIDTH)
                for s in range(0, ATTN_OUT_WIDTH, LANES):
                    rot = _rotate_half_pairs(t[:, s:s + LANES], cos_s, sin_s)
                    _store_residue_major(ref, perm_ref, rot, s, dilation)
            return run

        def value_piece(ref, col, dilation):
            def run():
                t = proj(col, ATTN_OUT_WIDTH)
                for s in range(0, ATTN_OUT_WIDTH, LANES):
                    _store_residue_major(ref, perm_ref, t[:, s:s + LANES], s, dilation)
            return run

        def gate_piece(ref, col, s):
            def run():
                z = proj(col + s, 512)
                ref[:, s:s + 512] = (1.0 / (1.0 + jnp.exp(-z))).astype(ref.dtype)
            return run

        pieces = []
        col = 0
        for refs, tables in (((q1_ref, q2_ref, q3_ref), (cos * q_scale, sin * q_scale)),
                             ((k1_ref, k2_ref, k3_ref), (cos, sin))):
            for g, ref in enumerate(refs):
                pieces.append(rope_piece(ref, col, tables, dilations[g]))
                col += ATTN_OUT_WIDTH
        for g, ref in enumerate((v1_ref, v2_ref, v3_ref)):
            pieces.append(value_piece(ref, col, dilations[g]))
            col += ATTN_OUT_WIDTH
        for ref in (ma_ref, mb_ref):
            for s in range(0, D_MODEL, 512):
                pieces.append(gate_piece(ref, col, s))
            col += D_MODEL

        for n in range(max(len(pieces), len(chunks))):
            if n < len(pieces):
                pieces[n]()
            if n < len(chunks):
                chunks[n]()

        col = 0
        for ref, width in ((gq_s, GLA_QK_WIDTH), (gk_s, GLA_QK_WIDTH)):
            ref[...] = rows_proj(col, width).astype(ref.dtype)
            col += width
        for ref in (gv_s, gr_s):
            for s in range(0, GLA_V_WIDTH, 512):
                ref[:, s:s + 512] = rows_proj(col + s, 512).astype(ref.dtype)
            col += GLA_V_WIDTH
        g_lr = rows_proj(IN_ALIGNED, LANES)
        g_hi, g_lo = _split_hi_lo(g_lr)
        w2_hi, w2_lo = _split_hi_lo(w2_ref[...])
        z = (jnp.dot(g_hi, w2_hi, preferred_element_type=F32) + jnp.dot(g_lo, w2_hi, preferred_element_type=F32)
             + jnp.dot(g_hi, w2_lo, preferred_element_type=F32)) + b2_ref[...]
        log_sig = jnp.minimum(z, 0.0) - jnp.log(1.0 + jnp.exp(-jnp.abs(z)))
        la_s[...] = log_sig * (LOG2_E / GLA_TAU)


def _gla_chunks(q_ref, k_ref, v_ref, la_ref, gr_ref, nw_ref, o_ref, state_ref, first_of_seq):
    C = GLA_CHUNK
    H = C // 2
    heads = range(GLA_HEADS)
    rows = lax.broadcasted_iota(jnp.int32, (C, C), 0)
    cols = lax.broadcasted_iota(jnp.int32, (C, C), 1)
    causal = cols <= rows
    tri = causal.astype(BF16)
    same_half = (rows >= H) == (cols >= H)
    upper_rows = lax.broadcasted_iota(jnp.int32, (C, GLA_QK_WIDTH), 0) < H
    q_scale = GLA_DK ** -0.5
    nw = jnp.concatenate([nw_ref[...]] * GLA_HEADS, axis=1)
    nt = (((1,), (1,)), ((), ()))
    ks = lambda a, h: a[:, h * GLA_DK:(h + 1) * GLA_DK]
    vs = lambda a, h: a[:, h * GLA_DV:(h + 1) * GLA_DV]

    def run_chunk(c):
        sl = pl.ds(c * C, C)
        q = q_ref[sl, :].astype(F32) * q_scale
        k = k_ref[sl, :].astype(F32)
        v = v_ref[sl, :]
        la_hi, la_lo = _split_hi_lo(la_ref[sl, :])
        cum = jnp.dot(tri, la_hi, preferred_element_type=F32) + jnp.dot(tri, la_lo, preferred_element_type=F32)
        mid = jnp.where(upper_rows, cum[H // 2 - 1:H // 2, :], cum[H + H // 2 - 1:H + H // 2, :])
        edge = cum[H - 1:H, :]
        last = cum[C - 1:C, :]
        qg = (q * jnp.exp2(cum - mid)).astype(BF16)
        kg = (k * jnp.exp2(mid - cum)).astype(BF16)
        zeros_half = jnp.zeros((H, GLA_QK_WIDTH), BF16)
        q_lo = jnp.concatenate([zeros_half, (q[H:] * jnp.exp2(cum[H:] - edge)).astype(BF16)], axis=0)
        k_up = jnp.concatenate([(k[:H] * jnp.exp2(edge - cum[:H])).astype(BF16), zeros_half], axis=0)
        q_in = (q * jnp.exp2(cum)).astype(BF16)
        k_out = k * jnp.exp2(last - cum)
        decay_row = jnp.broadcast_to(jnp.exp2(last), (8, GLA_QK_WIDTH))
        g = gr_ref[sl, :].astype(F32)
        gate = nw * (g * (1.0 / (1.0 + jnp.exp(-g))))

        within = [lax.dot_general(ks(qg, h), ks(kg, h), nt, preferred_element_type=F32) for h in heads]
        across = [lax.dot_general(ks(q_lo, h), ks(k_up, h), nt, preferred_element_type=F32) for h in heads]
        attn = [jnp.where(same_half, jnp.where(causal, within[h], 0.0), across[h]).astype(BF16) for h in heads]
        if c == 0:
            state = [jnp.where(first_of_seq, 0.0, state_ref[h]) for h in heads]
        else:
            state = [state_ref[h] for h in heads]
        o = [jnp.dot(attn[h], vs(v, h), preferred_element_type=F32)
             + jnp.dot(ks(q_in, h), state[h].astype(BF16), preferred_element_type=F32) for h in heads]
        k_out_t = [jnp.transpose(ks(k_out, h)).astype(BF16) for h in heads]
        decay = [jnp.transpose(ks(decay_row, h))[:, 0:1] for h in heads]
        for h in heads:
            state_ref[h] = decay[h] * state[h] + jnp.dot(k_out_t[h], vs(v, h), preferred_element_type=F32)
        ms = [jnp.mean(o[h] * o[h], axis=-1, keepdims=True) for h in heads]
        for h in heads:
            o_ref[sl, h * GLA_DV:(h + 1) * GLA_DV] = (o[h] * lax.rsqrt(ms[h] + EPS) * vs(gate, h)).astype(o_ref.dtype)

    return [functools.partial(run_chunk, c) for c in range(q_ref.shape[0] // C)]


def _in_proj_gla(x2, n1w, mod, cos, sin, w_in_t, w2, b2, gla_norm_w, B):
    T, D = x2.shape
    tm = TM_IN
    n_tiles = T // tm
    cur = lambda i: (jnp.minimum(i, n_tiles - 1), 0)
    lag = lambda i: (jnp.maximum(i - 1, 0), 0)
    const = lambda i: (0, 0)
    widths = (3 * ATTN_OUT_WIDTH,) * len(ATTN_GROUPS) + (2 * D_MODEL,)
    single = dict(pipeline_mode=pl.Buffered(1))
    w2p = jnp.pad(w2, ((0, LANES - GLA_LOWRANK), (0, 0)))
    outs = pl.pallas_call(
        functools.partial(_in_proj_gla_kernel, tiles_per_seq=T // B // tm, n_tiles=n_tiles),
        grid=(n_tiles + 1,),
        in_specs=[pl.BlockSpec((tm, D), cur),
                  pl.BlockSpec((1, D), const),
                  _mod_spec(MOD_SCALE1, D),
                  _mod_spec(MOD_SHIFT1, D),
                  pl.BlockSpec((tm, LANES), cur),
                  pl.BlockSpec((tm, LANES), cur),
                  pl.BlockSpec(w_in_t.shape, const, **single),
                  pl.BlockSpec(w2p.shape, const, **single),
                  pl.BlockSpec(b2.shape, const, **single),
                  pl.BlockSpec(gla_norm_w.shape, const, **single)],
        out_specs=[pl.BlockSpec((tm, GLA_V_WIDTH), lag)] + [pl.BlockSpec((tm, w), cur) for w in widths],
        out_shape=[jax.ShapeDtypeStruct((T, GLA_V_WIDTH), BF16)]
                  + [jax.ShapeDtypeStruct((T, w), BF16) for w in widths],
        scratch_shapes=[pltpu.VMEM((2, tm, LANES), F32),
                        pltpu.VMEM((tm, GLA_QK_WIDTH), BF16),
                        pltpu.VMEM((tm, GLA_QK_WIDTH), BF16),
                        pltpu.VMEM((tm, GLA_V_WIDTH), BF16),
                        pltpu.VMEM((tm, GLA_V_WIDTH), BF16),
                        pltpu.VMEM((tm, GLA_QK_WIDTH), F32),
                        pltpu.VMEM((GLA_HEADS, GLA_DK, GLA_DV), F32)],
        compiler_params=_params("arbitrary"),
        name="in_proj_gla",
    )(x2, n1w, mod, mod, cos, sin, w_in_t, w2p, b2, gla_norm_w)
    return outs[0], outs[1:]


def _attn_kernel(g1c_ref, g1p_ref, g2c_ref, g2p_ref, g3c_ref, g3p_ref, *refs):
    n_cast = ATTN_N_CAST
    o_ref = refs[n_cast]
    slabs = refs[2 * n_cast + 1:]
    for src, dst in zip(refs[:n_cast], refs[n_cast + 1:2 * n_cast + 1]):
        dst[...] = src[...].astype(dst.dtype)

    gw = ATTN_OUT_WIDTH
    k1c_ref, k2c_ref, k3c_ref = (r.at[:, 0:gw] for r in (g1c_ref, g2c_ref, g3c_ref))
    v1c_ref, v2c_ref, v3c_ref = (r.at[:, gw:2 * gw] for r in (g1c_ref, g2c_ref, g3c_ref))
    q1_ref, q2_ref, q3_ref = (r.at[:, 2 * gw:3 * gw] for r in (g1c_ref, g2c_ref, g3c_ref))
    k1p_ref, k2p_ref, k3p_ref = (r.at[:, 0:gw] for r in (g1p_ref, g2p_ref, g3p_ref))
    v1p_ref, v2p_ref, v3p_ref = (r.at[:, gw:2 * gw] for r in (g1p_ref, g2p_ref, g3p_ref))

    blk = ATTN_BLK
    tile = TM_IN
    span = ATTN_SPAN
    tiles = span // tile
    r2, r3 = ATTN_GROUPS[1][1], ATTN_GROUPS[2][1]
    per3 = tile // r3
    not_first_span = pl.program_id(1) > 0
    npair = ATTN_OUT_WIDTH // LANES
    o1_s, o2_s, o3_s, l1_s, l2_s, l3_s = [slabs[n * npair:(n + 1) * npair] for n in range(6)]

    rows = lax.broadcasted_iota(jnp.int32, (blk, 2 * blk), 0)
    cols = lax.broadcasted_iota(jnp.int32, (blk, 2 * blk), 1)
    band = (cols >= rows) & (cols <= rows + blk)
    in_cur = cols >= blk
    left = lax.broadcasted_iota(jnp.int32, (blk, LANES), 1) < ATTN_HEAD_DIM
    ones = jnp.ones((2 * blk, LANES), BF16)
    nt = (((1,), (1,)), ((), ()))

    def pair_attend(q_pair, k_cat, v_cat, has_prev):
        zero = jnp.zeros_like(q_pair)
        q2 = jnp.concatenate([jnp.where(left, q_pair, zero), jnp.where(left, zero, q_pair)], axis=0)
        s = lax.dot_general(q2, k_cat, nt, preferred_element_type=F32)
        valid = band & (in_cur | has_prev)
        s = jnp.where(jnp.concatenate([valid, valid], axis=0), s, MASK_VALUE)
        m = jnp.max(s, axis=-1, keepdims=True)
        p = jnp.exp2(s - m).astype(BF16)
        r = jnp.dot(p, jnp.concatenate([v_cat, ones], axis=1), preferred_element_type=F32)
        acc = jnp.where(left, r[:blk, :LANES], r[blk:, :LANES])
        den = jnp.where(left, r[:blk, LANES:], r[blk:, LANES:])
        m_pair = jnp.where(left, jnp.broadcast_to(m[:blk], (blk, LANES)), jnp.broadcast_to(m[blk:], (blk, LANES)))
        return acc / den, m_pair + jnp.log2(den)

    def with_prev(cur_ref, prev_ref, r0, back, prev_rows, first, cs):
        before = cur_ref[pl.ds(pl.multiple_of(jnp.maximum(r0 - back, 0), blk), blk), cs]
        if prev_rows is not None:
            before = jnp.where(first, prev_ref[prev_rows, cs], before)
        return jnp.concatenate([before, cur_ref[pl.ds(r0, blk), cs]], axis=0)

    def tile_body(t, carry):
        for p in range(tiles):
            i = t * tiles + p
            r0 = pl.multiple_of(i * blk, blk)
            for j in range(npair):
                cs = slice(j * LANES, (j + 1) * LANES)
                rows1 = slice(0, blk) if p == 0 else None
                o, lse = pair_attend(q1_ref[pl.ds(r0, blk), cs],
                                     with_prev(k1c_ref, k1p_ref, r0, blk, rows1, t == 0, cs),
                                     with_prev(v1c_ref, v1p_ref, r0, blk, rows1, t == 0, cs),
                                     not_first_span | (i > 0))
                o1_s[j][pl.ds(r0, blk), :] = o
                l1_s[j][pl.ds(r0, blk), :] = lse
                rows2 = slice(p * blk, (p + 1) * blk)
                o, lse = pair_attend(q2_ref[pl.ds(r0, blk), cs],
                                     with_prev(k2c_ref, k2p_ref, r0, tile, rows2, t == 0, cs),
                                     with_prev(v2c_ref, v2p_ref, r0, tile, rows2, t == 0, cs),
                                     not_first_span | (t > 0))
                tok = pl.ds(pl.multiple_of(t * tile, tile) + p, blk, stride=r2)
                o2_s[j][tok, :] = o
                l2_s[j][tok, :] = lse
                rr = pl.multiple_of(i * per3, per3)
                gather = lambda ref: jnp.concatenate(
                    [ref[pl.ds(rr + u * tile, per3), cs] for u in range(tiles)], axis=0)
                k_cat = jnp.concatenate([gather(k3p_ref), gather(k3c_ref)], axis=0)
                v_cat = jnp.concatenate([gather(v3p_ref), gather(v3c_ref)], axis=0)
                o, lse = pair_attend(gather(q3_ref), k_cat, v_cat, not_first_span)
                tok = pl.ds(i, blk, stride=r3)
                o3_s[j][tok, :] = o
                l3_s[j][tok, :] = lse
        return carry

    lax.fori_loop(0, tiles, tile_body, 0)

    for j in range(npair):
        for n in range(span // ATTN_MERGE_ROWS):
            rs = slice(n * ATTN_MERGE_ROWS, (n + 1) * ATTN_MERGE_ROWS)
            l1, l2, l3 = l1_s[j][rs, :], l2_s[j][rs, :], l3_s[j][rs, :]
            m = jnp.maximum(jnp.maximum(l1, l2), l3)
            e1, e2, e3 = jnp.exp2(l1 - m), jnp.exp2(l2 - m), jnp.exp2(l3 - m)
            num = e1 * o1_s[j][rs, :] + e2 * o2_s[j][rs, :] + e3 * o3_s[j][rs, :]
            o_ref[rs, j * LANES:(j + 1) * LANES] = (num / (e1 + e2 + e3)).astype(o_ref.dtype)


def _dilated_attention(qkvs, weights, B, S):
    assert len(weights) == ATTN_N_CAST
    span = ATTN_SPAN
    nsp = S // span
    gw = ATTN_OUT_WIDTH
    T = B * S
    cur = lambda b, s: (b * nsp + s, 0)
    steps = B * nsp
    slab = lambda w: pl.BlockSpec((w.shape[0] // steps, w.shape[1]), cur)

    def prev(rows):
        per = span // rows
        return pl.BlockSpec((rows, 2 * gw), lambda b, s: (jnp.maximum((b * nsp + s) * per - 1, 0), 0))

    halos = (ATTN_BLK, TM_IN, span)
    full = pl.BlockSpec((span, gw), cur)
    group_specs = []
    for h in halos:
        group_specs += [pl.BlockSpec((span, 3 * gw), cur), prev(h)]
    outs = pl.pallas_call(
        _attn_kernel,
        grid=(B, nsp),
        in_specs=group_specs + [slab(w) for w in weights],
        out_specs=[full] + [slab(w) for w in weights],
        out_shape=[jax.ShapeDtypeStruct((T, gw), BF16)] + [jax.ShapeDtypeStruct(w.shape, BF16) for w in weights],
        scratch_shapes=[pltpu.VMEM((span, LANES), F32)] * (6 * (gw // LANES)),
        compiler_params=_params("parallel", "parallel"),
        name="dilated_attn",
    )(*[a for g in qkvs for a in (g, g)], *weights)
    return outs[0], outs[1:]


def _merge_kernel(x_hbm, og_ref, oa_ref, mab_ref,
                  gate_ref, scale_ref, shift_ref, n2w_ref, wg_ref, wa_ref, wo_ref, x1_ref, h2_ref,
                  xbuf, xsem, *, tiles_per_seq, n_tiles):
    i = pl.program_id(0)
    tm = og_ref.shape[0]

    def x_copy(t):
        slot = t % MERGE_XBUF
        return pltpu.make_async_copy(x_hbm.at[pl.ds(pl.multiple_of(t * tm, tm), tm), :], xbuf.at[slot], xsem.at[slot])

    @pl.when(i == 0)
    def _():
        for t in range(MERGE_XBUF - 1):
            x_copy(t).start()

    @pl.when(i + MERGE_XBUF - 1 < n_tiles)
    def _():
        x_copy(i + MERGE_XBUF - 1).start()

    x_copy(i).wait()
    x_ref = xbuf.at[i % MERGE_XBUF]

    ma_ref, mb_ref = mab_ref.at[:, 0:D_MODEL], mab_ref.at[:, D_MODEL:2 * D_MODEL]
    b = pl.ds(i // tiles_per_seq, 1)
    gate, scale, shift = gate_ref[b, :], scale_ref[b, :], shift_ref[b, :]
    sub = tm // MERGE_SPLIT
    for r in range(MERGE_SPLIT):
        rs = slice(r * sub, (r + 1) * sub)
        y_attn = jnp.dot(oa_ref[rs, :], wa_ref[...], preferred_element_type=F32)
        y_gla = jnp.dot(og_ref[rs, :], wg_ref[...], preferred_element_type=F32)
        mixed = (ma_ref[rs, :].astype(F32) * y_gla + mb_ref[rs, :].astype(F32) * y_attn).astype(BF16)
        x1 = x_ref[rs, :] + gate * jnp.dot(mixed, wo_ref[...], preferred_element_type=F32)
        x1_ref[rs, :] = x1
        h2_ref[rs, :] = _rmsnorm_mod(x1, n2w_ref[...], scale, shift).astype(h2_ref.dtype)


def _merge(x2, og, oa, mab, mod, n2w, wg, wa, wo, B):
    T, D = x2.shape
    tm = TM_MERGE
    row = lambda i: (i, 0)
    const = lambda i: (0, 0)
    assert T // tm >= MERGE_XBUF
    return pl.pallas_call(
        functools.partial(_merge_kernel, tiles_per_seq=T // B // tm, n_tiles=T // tm),
        grid=(T // tm,),
        in_specs=[pl.BlockSpec(memory_space=pl.ANY), pl.BlockSpec((tm, GLA_V_WIDTH), row),
                  pl.BlockSpec((tm, ATTN_OUT_WIDTH), row), pl.BlockSpec((tm, 2 * D), row)]
                 + [_mod_spec(MOD_GATE1, D), _mod_spec(MOD_SCALE2, D), _mod_spec(MOD_SHIFT2, D)]
                 + [pl.BlockSpec((1, D), const),
                    pl.BlockSpec(wg.shape, const), pl.BlockSpec(wa.shape, const), pl.BlockSpec(wo.shape, const)],
        out_specs=[pl.BlockSpec((tm, D), row)] * 2,
        out_shape=[jax.ShapeDtypeStruct((T, D), F32), jax.ShapeDtypeStruct((T, D), BF16)],
        scratch_shapes=[pltpu.VMEM((MERGE_XBUF, tm, D), F32), pltpu.SemaphoreType.DMA((MERGE_XBUF,))],
        compiler_params=_params("arbitrary"),
        name="merge",
    )(x2, og, oa, mab, mod, mod, mod, n2w, wg, wa, wo)


def _ffn_kernel(h_ref, hprev_ref, x1_ref, wup_ref, cw_ref, cb_ref, wd_ref, gate_ref, fw_ref, o_ref,
                hcat_s, u_s, hid_s, order_s, *, tiles_per_seq):
    i = pl.program_id(0)
    tm = h_ref.shape[0]
    halo = hprev_ref.shape[0]
    tf = FFN_CHUNK
    half = tm // 2
    hcat_s[0:halo] = jnp.where(i % tiles_per_seq == 0, jnp.zeros_like(hprev_ref[...]), hprev_ref[...])
    hcat_s[halo:] = h_ref[...]
    nchunk = D_FF // tf
    branches = (0, D_FF)

    def up_project(j):
        for n, off in enumerate(branches):
            u = jnp.dot(hcat_s[...], wup_ref[:, off + j * tf:off + (j + 1) * tf], preferred_element_type=F32)
            for c in range(tf // LANES):
                u_s[j % 2, n, c] = u[:, c * LANES:(c + 1) * LANES]

    def conv(j, n, c, parity):
        u = u_s.at[j % 2, n, c]
        cs = slice(branches[n] + j * tf + c * LANES, branches[n] + j * tf + (c + 1) * LANES)
        tap = lambda back: u[pl.ds(halo + parity - back, half, stride=2), :]
        k = 2.0 ** -0.5
        return (cb_ref[:, cs] * k + (cw_ref[0:1, cs] * k) * tap(2) + (cw_ref[1:2, cs] * k) * tap(1)
                + (cw_ref[2:3, cs] * k) * tap(0))

    up_project(0)
    for j in range(nchunk):
        if j + 1 < nchunk:
            up_project(j + 1)
        for c in range(tf // LANES):
            for parity in range(2):
                t = conv(j, 1, c, parity)
                hidden = (t * (1.0 + lax.erf(t))) * conv(j, 0, c, parity)
                hid_s[parity * half:(parity + 1) * half, j * tf + c * LANES:j * tf + (c + 1) * LANES] = (
                    hidden.astype(hid_s.dtype))

    gate = gate_ref[pl.ds(i // tiles_per_seq, 1), :]
    d_model = o_ref.shape[1]
    sq = jnp.zeros((tm, 1), F32)
    for n in range(d_model // FFN_DOWN_SLAB):
        ns = slice(n * FFN_DOWN_SLAB, (n + 1) * FFN_DOWN_SLAB)
        down = jnp.dot(hid_s[...], wd_ref[:, ns], preferred_element_type=F32)
        for c in range(FFN_DOWN_SLAB // LANES):
            g = n * (FFN_DOWN_SLAB // LANES) + c
            cs = slice(g * LANES, (g + 1) * LANES)
            for parity in range(2):
                order_s[g, pl.ds(parity, half, stride=2), :] = (
                    down[parity * half:(parity + 1) * half, c * LANES:(c + 1) * LANES])
            x2 = x1_ref[:, cs] + gate[:, cs] * order_s[g]
            sq = sq + jnp.sum(x2 * x2, axis=-1, keepdims=True)
            o_ref[:, cs] = x2
    o_ref[...] = o_ref[...] * lax.rsqrt(sq * (1.0 / d_model) + EPS) * fw_ref[...]


def _ffn(h2, x1, w_up, conv_w, conv_b, w_down, mod, final_w, B):
    T, D = x1.shape
    tm, tf, halo = TM_FFN, FFN_CHUNK, FFN_HALO
    row = lambda i: (i, 0)
    const = lambda i: (0, 0)
    single = dict(pipeline_mode=pl.Buffered(1))
    return pl.pallas_call(
        functools.partial(_ffn_kernel, tiles_per_seq=T // B // tm),
        grid=(T // tm,),
        in_specs=[pl.BlockSpec((tm, D), row),
                  pl.BlockSpec((halo, D), lambda i: (jnp.maximum(i * (tm // halo) - 1, 0), 0)),
                  pl.BlockSpec((tm, D), row),
                  pl.BlockSpec(w_up.shape, const, **single),
                  pl.BlockSpec(conv_w.shape, const, **single),
                  pl.BlockSpec(conv_b.shape, const, **single),
                  pl.BlockSpec(w_down.shape, const, **single),
                  _mod_spec(MOD_GATE2, D),
                  pl.BlockSpec((1, D), const)],
        out_specs=pl.BlockSpec((tm, D), row),
        out_shape=jax.ShapeDtypeStruct((T, D), F32),
        scratch_shapes=[pltpu.VMEM((halo + tm, D), BF16),
                        pltpu.VMEM((2, 2, tf // LANES, halo + tm, LANES), F32),
                        pltpu.VMEM((tm, D_FF), BF16),
                        pltpu.VMEM((D // LANES, tm, LANES), F32)],
        compiler_params=_params("parallel"),
        name="ffn",
    )(h2, h2, x1, w_up, conv_w, conv_b, w_down, mod, final_w)


def kernel(x, c, positions, ada_w, ada_b, norm1_w, w_in, gla_gate_w2, gla_gate_b, gla_norm_w, w_gla_branch,
           w_attn_branch, w_out, norm2_w, w_up, conv_w, conv_b, w_down, final_norm_w):
    B, S, D = x.shape
    T = B * S
    depth = ada_w.shape[0]
    assert depth == 1, "the final norm is fused into the (single) layer's ffn"
    assert all(window // dilation == ATTN_BLK for window, dilation in ATTN_GROUPS)
    assert conv_w.shape[1] == CONV_WIDTH == 3, "the ffn conv reads exactly three taps"
    assert S % ATTN_SPAN == 0 and S % max(TM_IN, TM_MERGE, TM_FFN) == 0 and T % (LANES // (ATTN_HEAD_DIM // 2)) == 0
    x2 = x.reshape(T, D)
    for layer in range(depth):
        cos, sin, mod, w_in_bf = _prep(positions, c, ada_w[layer], ada_b[layer], w_in[layer].T)

        og, (qkv1, qkv2, qkv3, mab) = _in_proj_gla(
            x2, norm1_w[layer].reshape(1, D), mod, cos, sin, w_in_bf, gla_gate_w2[layer],
            gla_gate_b[layer].reshape(1, -1), gla_norm_w[layer].reshape(1, -1), B)
        oa, (w_up_bf, w_down_bf, w_gla_bf, w_attn_bf, w_out_bf) = _dilated_attention(
            (qkv1, qkv2, qkv3),
            (w_up[layer], w_down[layer], w_gla_branch[layer], w_attn_branch[layer], w_out[layer]), B, S)

        x1, h2 = _merge(x2, og, oa, mab, mod, norm2_w[layer].reshape(1, D), w_gla_bf, w_attn_bf, w_out_bf, B)

        x2 = _ffn(h2, x1, w_up_bf, conv_w[layer], conv_b[layer].reshape(1, -1),
                  w_down_bf, mod, final_norm_w.reshape(1, D), B)
    return x2.reshape(B, S, D)
```
